```python
import jax, jax.numpy as jnp
from jax import lax
import numpy as np

D_MODEL = 1024
BATCH = 8
SEQ = 2048
DEPTH = 2

HEAD_DIM = 64
NSA_HEADS = 8
NSA_KV_GROUPS = 2
NSA_REP = NSA_HEADS // NSA_KV_GROUPS
CMP_LEN = 32
CMP_STRIDE = 16
CMP_HIDDEN = 128
SEL_BLOCK = 64
SEL_TOPK = 16
WINDOW = 512
MOBA_HEADS = 8
MOBA_BLOCK = 256
MOBA_TOPK = 3
Q_BLOCK = 128
D_FF = 2816
ROPE_THETA = 10000.0
EPS = 1e-6
NEG = -1e30
TINY = 1e-30
FORCE_BONUS = 1e4
NSA_WIDTH = NSA_HEADS * HEAD_DIM
NSA_KV_WIDTH = NSA_KV_GROUPS * HEAD_DIM
MOBA_WIDTH = MOBA_HEADS * HEAD_DIM
SPLIT_SIZES = (NSA_WIDTH, 3 * NSA_HEADS, NSA_KV_WIDTH, NSA_KV_WIDTH, NSA_KV_WIDTH, NSA_KV_WIDTH, NSA_KV_WIDTH, NSA_KV_WIDTH, MOBA_WIDTH, MOBA_WIDTH, MOBA_WIDTH, D_MODEL, D_MODEL)
IN_COLS = NSA_WIDTH + 3 * NSA_HEADS + 6 * NSA_KV_WIDTH + 3 * MOBA_WIDTH + 2 * D_MODEL

kernel_name = "hybrid_nsa_moba_macaron"


def _rms(x, g):
    xf = x.astype(jnp.float32)
    y = xf * lax.rsqrt(jnp.mean(xf * xf, axis=-1, keepdims=True) + EPS)
    return (y * g.astype(jnp.float32)).astype(x.dtype)


def _swiglu(h, wg, wu, wd):
    return (jax.nn.silu(h @ wg) * (h @ wu)) @ wd


def _rope_tables(pos):
    inv = ROPE_THETA ** (-jnp.arange(0, HEAD_DIM, 2, dtype=jnp.float32) / HEAD_DIM)
    ang = pos.astype(jnp.float32)[:, None] * inv[None, :]
    return jnp.cos(ang), jnp.sin(ang)


def _rope(x, cos, sin):
    x1, x2 = jnp.split(x.astype(jnp.float32), 2, axis=-1)
    return jnp.concatenate([x1 * cos - x2 * sin, x2 * cos + x1 * sin], axis=-1).astype(x.dtype)


def _masked_softmax(s, mask, axis):
    s = jnp.where(mask, s.astype(jnp.float32), NEG)
    m = jnp.max(s, axis=axis, keepdims=True)
    p = jnp.exp(s - m) * mask
    return p / jnp.maximum(jnp.sum(p, axis=axis, keepdims=True), TINY)


def _heads(z, n):
    b, s, _ = z.shape
    return z.reshape(b, s, n, HEAD_DIM).transpose(0, 2, 1, 3)


def _split_cols(z):
    idx = []
    acc = 0
    for w in SPLIT_SIZES[:-1]:
        acc += w
        idx.append(acc)
    return jnp.split(z, idx, axis=-1)


def _chunk_queries(a):
    b = a.shape[0]
    s = a.shape[-2]
    nq = s // Q_BLOCK
    mid = a.shape[1:-2]
    a = a.reshape((b,) + mid + (nq, Q_BLOCK, a.shape[-1]))
    nd = a.ndim
    perm = (0, nd - 3) + tuple(range(1, nd - 3)) + (nd - 2, nd - 1)
    return a.transpose(perm).reshape((b * nq,) + mid + (Q_BLOCK, a.shape[-1]))


def _unchunk_queries(a, b):
    nq = a.shape[0] // b
    mid = a.shape[1:-2]
    a = a.reshape((b, nq) + mid + a.shape[-2:])
    nd = a.ndim
    perm = (0,) + tuple(range(2, nd - 2)) + (1, nd - 2, nd - 1)
    return a.transpose(perm).reshape((b,) + mid + (nq * Q_BLOCK, a.shape[-1]))


def _compress(k, pos_emb, w1, w2):
    b, g, s, d = k.shape
    ch = k.reshape(b, g, s // CMP_STRIDE, CMP_STRIDE, d)
    blocks = jnp.concatenate([ch[:, :, :-1], ch[:, :, 1:]], axis=3) + pos_emb
    flat = blocks.reshape(b, g, blocks.shape[2], CMP_LEN * d)
    return jax.nn.gelu(flat @ w1) @ w2


def _nsa(q, g, kc, vc, ks, vs, kw, vw, ck_pos, ck_w1, ck_w2, cv_pos, cv_w1, cv_w2):
    B, S, _ = q.shape
    G, R, D = NSA_KV_GROUPS, NSA_REP, HEAD_DIM
    scale = HEAD_DIM ** -0.5
    t = jnp.arange(S)
    cos, sin = _rope_tables(t)
    qh = _rope(_heads(q, NSA_HEADS), cos, sin).reshape(B, G, R, S, D)

    kcb = _compress(_heads(kc, G), ck_pos, ck_w1, ck_w2)
    vcb = _compress(_heads(vc, G), cv_pos, cv_w1, cv_w2)
    nc = kcb.shape[2]
    end_pos = jnp.arange(nc) * CMP_STRIDE + (CMP_LEN - 1)
    cc, sc = _rope_tables(end_pos)
    kcb = _rope(kcb, cc, sc)
    s_cmp = jnp.einsum('bgrtd,bgcd->bgrtc', qh, kcb) * scale
    p_cmp = _masked_softmax(s_cmp, end_pos[None, :] <= t[:, None], -1)
    o_cmp = jnp.einsum('bgrtc,bgcd->bgrtd', p_cmp.astype(vcb.dtype), vcb)

    ns = S // SEL_BLOCK
    ci = jnp.arange(nc)[:, None] * CMP_STRIDE
    sj = jnp.arange(ns)[None, :] * SEL_BLOCK
    overlap = ((ci < sj + SEL_BLOCK) & (ci + CMP_LEN > sj)).astype(jnp.float32)
    imp = jnp.einsum('bgrtc,cj->bgtj', p_cmp, overlap)
    tblk = (t // SEL_BLOCK)[:, None]
    jj = jnp.arange(ns)[None, :]
    forced = (jj == 0) | (jj == tblk) | (jj == tblk - 1)
    imp = jnp.where(jj <= tblk, imp + jnp.where(forced, FORCE_BONUS, 0.0), NEG)
    k_sel = min(SEL_TOPK, ns)
    _, sel_idx = lax.top_k(imp, k_sel)

    ksb = _rope(_heads(ks, G), cos, sin).reshape(B, G, ns, SEL_BLOCK, D)
    vsb = _heads(vs, G).reshape(B, G, ns, SEL_BLOCK, D)
    nq = S // Q_BLOCK
    q_ch = _chunk_queries(qh)
    idx_ch = _chunk_queries(sel_idx)
    b_ids = jnp.repeat(jnp.arange(B), nq)
    n_ids = jnp.tile(jnp.arange(nq), B)

    def sel_body(args):
        qc, ic, bi, ni = args
        kg = jax.vmap(lambda kk, ii: kk[ii])(ksb[bi], ic)
        vg = jax.vmap(lambda vv, ii: vv[ii])(vsb[bi], ic)
        s = jnp.einsum('grqd,gqnkd->grqnk', qc, kg) * scale
        tq = ni * Q_BLOCK + jnp.arange(Q_BLOCK)
        kpos = ic[..., None] * SEL_BLOCK + jnp.arange(SEL_BLOCK)
        mask = (kpos <= tq[None, :, None, None])[:, None]
        p = _masked_softmax(s, mask, (-2, -1))
        return jnp.einsum('grqnk,gqnkd->grqd', p.astype(vg.dtype), vg)

    o_sel = _unchunk_queries(lax.map(sel_body, (q_ch, idx_ch, b_ids, n_ids)), B)

    kwh = _rope(_heads(kw, G), cos, sin)
    vwh = _heads(vw, G)
    pad = ((0, 0), (0, 0), (WINDOW, 0), (0, 0))
    band = jnp.arange(nq)[:, None] * Q_BLOCK + jnp.arange(WINDOW + Q_BLOCK)[None, :]
    kband = jnp.take(jnp.pad(kwh, pad), band, axis=2)
    vband = jnp.take(jnp.pad(vwh, pad), band, axis=2)
    qb = qh.reshape(B, G, R, nq, Q_BLOCK, D)
    s_win = jnp.einsum('bgrnqd,bgnkd->bgrnqk', qb, kband) * scale
    kpos = (band - WINDOW)[:, None, :]
    tq = (jnp.arange(nq)[:, None] * Q_BLOCK + jnp.arange(Q_BLOCK)[None, :])[:, :, None]
    m_win = (kpos >= 0) & (kpos <= tq) & (tq - kpos < WINDOW)
    p_win = _masked_softmax(s_win, m_win, -1)
    o_win = jnp.einsum('bgrnqk,bgnkd->bgrnqd', p_win.astype(vband.dtype), vband).reshape(B, G, R, S, D)

    gate = jax.nn.sigmoid(g).reshape(B, S, G, R, 3).transpose(0, 2, 3, 1, 4)
    o = gate[..., 0:1] * o_cmp + gate[..., 1:2] * o_sel + gate[..., 2:3] * o_win
    return o.transpose(0, 3, 1, 2, 4).reshape(B, S, NSA_WIDTH)


def _moba(q, k, v):
    B, S, _ = q.shape
    H, D = MOBA_HEADS, HEAD_DIM
    scale = HEAD_DIM ** -0.5
    t = jnp.arange(S)
    cos, sin = _rope_tables(t)
    qh = _rope(_heads(q, H), cos, sin)
    kh = _rope(_heads(k, H), cos, sin)
    vh = _heads(v, H)
    nb = -(-S // MOBA_BLOCK)
    sp = nb * MOBA_BLOCK
    kp = jnp.pad(kh, ((0, 0), (0, 0), (0, sp - S), (0, 0)))
    vp = jnp.pad(vh, ((0, 0), (0, 0), (0, sp - S), (0, 0)))
    kblk = kp.reshape(B, H, nb, MOBA_BLOCK, D)
    vblk = vp.reshape(B, H, nb, MOBA_BLOCK, D)
    n_top = min(MOBA_TOPK, nb - 1)
    nq = S // Q_BLOCK
    if n_top > 0:
        kmean = jnp.mean(kblk.astype(jnp.float32), axis=3).astype(kh.dtype)
        gsc = jnp.einsum('bhtd,bhjd->bhtj', qh, kmean)
        past = jnp.arange(nb)[None, :] < (t // MOBA_BLOCK)[:, None]
        gsc = jnp.where(past, gsc.astype(jnp.float32), NEG)
        _, gidx = lax.top_k(gsc, n_top)
        idx_ch = _chunk_queries(gidx)
    else:
        idx_ch = jnp.zeros((B * nq, H, Q_BLOCK, 1), jnp.int32)
    q_ch = _chunk_queries(qh)
    b_ids = jnp.repeat(jnp.arange(B), nq)
    n_ids = jnp.tile(jnp.arange(nq), B)

    def body(args):
        qc, ic, bi, ni = args
        tq = ni * Q_BLOCK + jnp.arange(Q_BLOCK)
        c = (ni * Q_BLOCK) // MOBA_BLOCK
        kown = lax.dynamic_slice_in_dim(kp[bi], c * MOBA_BLOCK, MOBA_BLOCK, axis=1)
        vown = lax.dynamic_slice_in_dim(vp[bi], c * MOBA_BLOCK, MOBA_BLOCK, axis=1)
        s_own = jnp.einsum('hqd,hkd->hqk', qc, kown) * scale
        own_pos = c * MOBA_BLOCK + jnp.arange(MOBA_BLOCK)
        m_own = jnp.broadcast_to((own_pos[None, :] <= tq[:, None])[None], s_own.shape)
        if n_top > 0:
            kg = jax.vmap(lambda kk, ii: kk[ii])(kblk[bi], ic)
            vg = jax.vmap(lambda vv, ii: vv[ii])(vblk[bi], ic)
            s_sel = jnp.einsum('hqd,hqnkd->hqnk', qc, kg) * scale
            m_sel = jnp.broadcast_to((ic < c)[..., None], s_sel.shape)
            s_all = jnp.concatenate([s_sel.reshape(H, Q_BLOCK, -1), s_own], axis=-1)
            m_all = jnp.concatenate([m_sel.reshape(H, Q_BLOCK, -1), m_own], axis=-1)
            p = _masked_softmax(s_all, m_all, -1)
            p_sel = p[..., :n_top * MOBA_BLOCK].reshape(H, Q_BLOCK, n_top, MOBA_BLOCK)
            p_own = p[..., n_top * MOBA_BLOCK:]
            return (jnp.einsum('hqnk,hqnkd->hqd', p_sel.astype(vg.dtype), vg)
                    + jnp.einsum('hqk,hkd->hqd', p_own.astype(vown.dtype), vown))
        p_own = _masked_softmax(s_own, m_own, -1)
        return jnp.einsum('hqk,hkd->hqd', p_own.astype(vown.dtype), vown)

    o = _unchunk_queries(lax.map(body, (q_ch, idx_ch, b_ids, n_ids)), B)
    return o.transpose(0, 2, 1, 3).reshape(B, S, MOBA_WIDTH)


def setup_inputs(seed: int = 0) -> dict:
    key = jax.random.key(seed)
    ks = jax.random.split(key, 24)

    def w(k, shape, fan_in):
        return jax.random.normal(k, shape, jnp.float32) * (fan_in ** -0.5)

    def gain(k, shape):
        return 1.0 + 0.01 * jax.random.normal(k, shape, jnp.float32)

    L, Dm, F = DEPTH, D_MODEL, D_FF
    return {
        "x": jax.random.normal(ks[0], (BATCH, SEQ, Dm), jnp.float32),
        "ffn1_norm": gain(ks[1], (L, Dm)),
        "ffn1_wg": w(ks[2], (L, Dm, F), Dm),
        "ffn1_wu": w(ks[3], (L, Dm, F), Dm),
        "ffn1_wd": w(ks[4], (L, F, Dm), F),
        "mix_norm": gain(ks[5], (L, Dm)),
        "w_in": w(ks[6], (L, Dm, IN_COLS), Dm),
        "cmpk_pos": 0.1 * jax.random.normal(ks[7], (L, CMP_LEN, HEAD_DIM), jnp.float32),
        "cmpk_w1": w(ks[8], (L, CMP_LEN * HEAD_DIM, CMP_HIDDEN), CMP_LEN * HEAD_DIM),
        "cmpk_w2": w(ks[9], (L, CMP_HIDDEN, HEAD_DIM), CMP_HIDDEN),
        "cmpv_pos": 0.1 * jax.random.normal(ks[10], (L, CMP_LEN, HEAD_DIM), jnp.float32),
        "cmpv_w1": w(ks[11], (L, CMP_LEN * HEAD_DIM, CMP_HIDDEN), CMP_LEN * HEAD_DIM),
        "cmpv_w2": w(ks[12], (L, CMP_HIDDEN, HEAD_DIM), CMP_HIDDEN),
        "w_branch_nsa": w(ks[13], (L, NSA_WIDTH, Dm), NSA_WIDTH),
        "w_branch_moba": w(ks[14], (L, MOBA_WIDTH, Dm), MOBA_WIDTH),
        "w_out": w(ks[15], (L, Dm, Dm), Dm),
        "ffn2_norm": gain(ks[16], (L, Dm)),
        "ffn2_wg": w(ks[17], (L, Dm, F), Dm),
        "ffn2_wu": w(ks[18], (L, Dm, F), Dm),
        "ffn2_wd": w(ks[19], (L, F, Dm), F),
        "final_norm": gain(ks[20], (Dm,)),
    }


def reference(x, ffn1_norm, ffn1_wg, ffn1_wu, ffn1_wd, mix_norm, w_in, cmpk_pos, cmpk_w1, cmpk_w2, cmpv_pos, cmpv_w1, cmpv_w2, w_branch_nsa, w_branch_moba, w_out, ffn2_norm, ffn2_wg, ffn2_wu, ffn2_wd, final_norm):
    for l in range(DEPTH):
        x = x + 0.5 * _swiglu(_rms(x, ffn1_norm[l]), ffn1_wg[l], ffn1_wu[l], ffn1_wd[l])
        h = _rms(x, mix_norm[l])
        (q_a, g_a, kc, vc, ks_, vs_, kw, vw, q_b, k_b, v_b, gate_a, gate_b) = _split_cols(h @ w_in[l])
        y_a = _nsa(q_a, g_a, kc, vc, ks_, vs_, kw, vw, cmpk_pos[l], cmpk_w1[l], cmpk_w2[l], cmpv_pos[l], cmpv_w1[l], cmpv_w2[l]) @ w_branch_nsa[l]
        y_b = _moba(q_b, k_b, v_b) @ w_branch_moba[l]
        merged = jax.nn.sigmoid(gate_a) * y_a + jax.nn.sigmoid(gate_b) * y_b
        x = x + merged @ w_out[l]
        x = x + 0.5 * _swiglu(_rms(x, ffn2_norm[l]), ffn2_wg[l], ffn2_wu[l], ffn2_wd[l])
    return _rms(x, final_norm)
```

```python
import functools

import jax
import jax.numpy as jnp
import numpy as np
from jax import lax
from jax.experimental import pallas as pl
from jax.experimental.pallas import tpu as pltpu

F32 = jnp.float32
BF16 = jnp.bfloat16

D_MODEL = 1024
HEAD_DIM = 64
HALF = HEAD_DIM // 2
NSA_HEADS = 8
NSA_GROUPS = 2
NSA_REP = NSA_HEADS // NSA_GROUPS
CMP_LEN = 32
CMP_STRIDE = 16
CMP_HIDDEN = 128
SEL_BLOCK = 64
SEL_TOPK = 16
WINDOW = 512
MOBA_HEADS = 8
MOBA_BLOCK = 256
MOBA_TOPK = 3
D_FF = 2816
ROPE_THETA = 10000.0
EPS = 1e-6
NEG = -1e30
TINY = 1e-30
FORCE_BONUS = 1e4
SCALE = HEAD_DIM ** -0.5

NSA_WIDTH = NSA_HEADS * HEAD_DIM
KV_WIDTH = NSA_GROUPS * HEAD_DIM
MOBA_WIDTH = MOBA_HEADS * HEAD_DIM
N_GATE_COLS = 3 * NSA_HEADS

LANES = 128
TOKEN_TILE = 512
FF_CHUNK = 256
NSA_Q = 128
SEL_KEYS = 512
VMEM_LIMIT = 56 * 1024 * 1024

COL_QA = 0
COL_KC = COL_QA + NSA_WIDTH
COL_VC = COL_KC + KV_WIDTH
COL_KV4 = COL_VC + KV_WIDTH
COL_QB = COL_KV4 + 4 * KV_WIDTH
COL_KB = COL_QB + MOBA_WIDTH
COL_VB = COL_KB + MOBA_WIDTH
COL_GATES = COL_VB + MOBA_WIDTH
COL_GS = COL_GATES + 2 * D_MODEL
IN_COLS_PADDED = COL_GS + LANES


def _rms(x, g):
    return x * lax.rsqrt(jnp.mean(x * x, axis=-1, keepdims=True) + EPS) * g


def _dot(a, b):
    return jnp.dot(a, b, preferred_element_type=F32)


def _dot_nt(a, b):
    return lax.dot_general(a, b, (((1,), (1,)), ((), ())), preferred_element_type=F32)


def _swap_halves(z):
    lane = lax.broadcasted_iota(jnp.int32, z.shape, 1)
    lo = (lane & (HEAD_DIM - 1)) < HALF
    return jnp.where(lo, pltpu.roll(z, LANES - HALF, 1), pltpu.roll(z, HALF, 1))


def _rope_slab(z, cos, sin):
    parts = []
    for c in range(z.shape[1] // LANES):
        zc = z[:, c * LANES:(c + 1) * LANES]
        parts.append(zc * cos + _swap_halves(zc) * sin)
    return parts[0] if len(parts) == 1 else jnp.concatenate(parts, axis=1)


def _masked_softmax_rows(s, mask):
    s = jnp.where(mask, s, NEG)
    m = jnp.max(s, axis=-1, keepdims=True)
    p = jnp.where(mask, jnp.exp(s - m), 0.0)
    return p / jnp.maximum(jnp.sum(p, axis=-1, keepdims=True), TINY)


def _online_update(carry, s, mask, v):
    m, l, acc = carry
    s = jnp.where(mask, s, NEG)
    m_new = jnp.maximum(m, jnp.max(s, axis=-1, keepdims=True))
    alpha = jnp.exp(m - m_new)
    p = jnp.where(mask, jnp.exp(s - m_new), 0.0)
    l = alpha * l + jnp.sum(p, axis=-1, keepdims=True)
    acc = alpha * acc + _dot(p.astype(BF16), v)
    return m_new, l, acc


def _rank_select(v, n_candidates, k):
    lane = lax.broadcasted_iota(jnp.int32, v.shape, 1)
    cnt = jnp.zeros(v.shape, F32)
    for j in range(n_candidates):
        col = v[:, j:j + 1]
        beats = jnp.where(col > v, 1.0, jnp.where(col == v, jnp.where(lane > j, 1.0, 0.0), 0.0))
        cnt = cnt + beats
    return cnt < float(k)


def _ffn_kernel(x_ref, g_ref, wg_ref, wu_ref, wd_ref, fin_ref, o_ref, *, final_norm):
    x = x_ref[...]
    h = _rms(x, g_ref[...]).astype(BF16)

    def body(f, acc):
        a = _dot(h, wg_ref[f])
        b = _dot(h, wu_ref[f])
        t = (a * jax.nn.sigmoid(a) * b).astype(BF16)
        return acc + _dot(t, wd_ref[f])

    acc = lax.fori_loop(0, D_FF // FF_CHUNK, body, jnp.zeros(x.shape, F32))
    y = x + 0.5 * acc
    if final_norm:
        y = _rms(y, fin_ref[...])
    o_ref[...] = y


def _resident(shape):
    nd = len(shape)
    return pl.BlockSpec(shape, lambda *_: (0,) * nd, pipeline_mode=pl.Buffered(1))


def _ffn(x, g, wg, wu, wd, fin, final_norm):
    t = x.shape[0]
    nf = D_FF // FF_CHUNK
    wg3 = wg.astype(BF16).reshape(D_MODEL, nf, FF_CHUNK).transpose(1, 0, 2)
    wu3 = wu.astype(BF16).reshape(D_MODEL, nf, FF_CHUNK).transpose(1, 0, 2)
    wd3 = wd.astype(BF16).reshape(nf, FF_CHUNK, D_MODEL)
    row = pl.BlockSpec((TOKEN_TILE, D_MODEL), lambda i: (i, 0))
    return pl.pallas_call(
        functools.partial(_ffn_kernel, final_norm=final_norm),
        grid=(t // TOKEN_TILE,),
        in_specs=[row, _resident((1, D_MODEL)), _resident(wg3.shape), _resident(wu3.shape),
                  _resident(wd3.shape), _resident((1, D_MODEL))],
        out_specs=row,
        out_shape=jax.ShapeDtypeStruct(x.shape, F32),
        compiler_params=pltpu.CompilerParams(dimension_semantics=("parallel",),
                                             vmem_limit_bytes=VMEM_LIMIT),
        name="ffn",
    )(x, g.reshape(1, D_MODEL), wg3, wu3, wd3, fin.reshape(1, D_MODEL))


def _proj_kernel(x_ref, g_ref, w_ref, cos_ref, sin_ref,
                 qa_ref, kc_ref, vc_ref, kv4_ref, qkvb_ref, gates_ref, gs_ref, kmean_ref):
    h = _rms(x_ref[...], g_ref[...]).astype(BF16)
    cos = cos_ref[...]
    sin = sin_ref[...]

    def mm(c0, n):
        return _dot(h, w_ref[:, c0:c0 + n])

    qa_ref[...] = (_rope_slab(mm(COL_QA, NSA_WIDTH), cos, sin) * SCALE).astype(BF16)
    kcvc = mm(COL_KC, 2 * KV_WIDTH)
    kc_ref[...] = kcvc[:, :KV_WIDTH].astype(BF16)
    vc_ref[...] = kcvc[:, KV_WIDTH:].astype(BF16)
    kv4 = mm(COL_KV4, 4 * KV_WIDTH)
    kv4_ref[...] = jnp.concatenate(
        [_rope_slab(kv4[:, 0:KV_WIDTH], cos, sin), kv4[:, KV_WIDTH:2 * KV_WIDTH],
         _rope_slab(kv4[:, 2 * KV_WIDTH:3 * KV_WIDTH], cos, sin), kv4[:, 3 * KV_WIDTH:]],
        axis=1).astype(BF16)
    qkvb_ref[:, 0:MOBA_WIDTH] = (_rope_slab(mm(COL_QB, MOBA_WIDTH), cos, sin) * SCALE).astype(BF16)
    kb = _rope_slab(mm(COL_KB, MOBA_WIDTH), cos, sin)
    qkvb_ref[:, MOBA_WIDTH:2 * MOBA_WIDTH] = kb.astype(BF16)
    means = [jnp.mean(kb[i * MOBA_BLOCK:(i + 1) * MOBA_BLOCK], axis=0, keepdims=True)
             for i in range(TOKEN_TILE // MOBA_BLOCK)]
    kmean_ref[0] = jnp.concatenate(means, axis=0)
    qkvb_ref[:, 2 * MOBA_WIDTH:] = mm(COL_VB, MOBA_WIDTH).astype(BF16)
    gates_ref[:, :D_MODEL] = jax.nn.sigmoid(mm(COL_GATES, D_MODEL))
    gates_ref[:, D_MODEL:] = jax.nn.sigmoid(mm(COL_GATES + D_MODEL, D_MODEL))
    gs_ref[...] = jax.nn.sigmoid(mm(COL_GS, LANES))


def _reorder_w_in(w_in):
    sizes = (NSA_WIDTH, N_GATE_COLS) + (KV_WIDTH,) * 6 + (MOBA_WIDTH,) * 3 + (D_MODEL, D_MODEL)
    offs = np.concatenate([[0], np.cumsum(sizes)])
    parts = [w_in[:, offs[i]:offs[i + 1]] for i in range(len(sizes))]
    q_a, g_a = parts[0], parts[1]
    rest = parts[2:]
    pad = jnp.zeros((D_MODEL, LANES - N_GATE_COLS), w_in.dtype)
    return jnp.concatenate([q_a] + rest + [g_a, pad], axis=1).astype(BF16)


def _proj(x, g, w_in, cos_t, sin_t, seq):
    t = x.shape[0]
    w = _reorder_w_in(w_in)
    tiles_per_seq = seq // TOKEN_TILE
    blocks_per_tile = TOKEN_TILE // MOBA_BLOCK

    def row(n):
        return pl.BlockSpec((TOKEN_TILE, n), lambda i: (i, 0))

    tab = pl.BlockSpec((TOKEN_TILE, LANES), lambda i: (i % tiles_per_seq, 0))
    out_shape = (
        jax.ShapeDtypeStruct((t, NSA_WIDTH), BF16),
        jax.ShapeDtypeStruct((t, KV_WIDTH), BF16),
        jax.ShapeDtypeStruct((t, KV_WIDTH), BF16),
        jax.ShapeDtypeStruct((t, 4 * KV_WIDTH), BF16),
        jax.ShapeDtypeStruct((t, 3 * MOBA_WIDTH), BF16),
        jax.ShapeDtypeStruct((t, 2 * D_MODEL), F32),
        jax.ShapeDtypeStruct((t, LANES), F32),
        jax.ShapeDtypeStruct((t // TOKEN_TILE, blocks_per_tile, MOBA_WIDTH), F32),
    )
    out_specs = (row(NSA_WIDTH), row(KV_WIDTH), row(KV_WIDTH), row(4 * KV_WIDTH), row(3 * MOBA_WIDTH),
                 row(2 * D_MODEL), row(LANES),
                 pl.BlockSpec((1, blocks_per_tile, MOBA_WIDTH), lambda i: (i, 0, 0)))
    return pl.pallas_call(
        _proj_kernel,
        grid=(t // TOKEN_TILE,),
        in_specs=[row(D_MODEL), _resident((1, D_MODEL)), _resident(w.shape), tab, tab],
        out_specs=out_specs,
        out_shape=out_shape,
        compiler_params=pltpu.CompilerParams(dimension_semantics=("parallel",),
                                             vmem_limit_bytes=VMEM_LIMIT),
        name="proj",
    )(x, g.reshape(1, D_MODEL), w, cos_t, sin_t)


def _gelu_tanh(x):
    return 0.5 * x * (1.0 + jnp.tanh(np.sqrt(2.0 / np.pi).astype(np.float32) * (x + 0.044715 * (x * x * x))))


def _compress_kernel(kc_ref, vc_ref, kw1_ref, kw1f_ref, kpos_ref, kw2_ref,
                     vw1_ref, vw1f_ref, vpos_ref, vw2_ref, cos_ref, sin_ref, kcb_ref, vcb_ref):
    n_chunks = kc_ref.shape[1]

    def compress(x_ref, w1_ref, w1f_ref, pos_ref, w2_ref):
        u = _dot(x_ref[0], w1_ref[...])
        first = u[:, :2 * CMP_HIDDEN]
        second = pltpu.roll(u[:, 2 * CMP_HIDDEN:], n_chunks - 1, 0)
        pos = jnp.broadcast_to(pos_ref[...], (8, CMP_LEN * HEAD_DIM)).astype(BF16)
        bias = _dot(pos, w1f_ref[...])[0:1]
        pre = first + second + jnp.concatenate([bias, bias], axis=1)
        return _dot(_gelu_tanh(pre).astype(BF16), w2_ref[...])

    k = compress(kc_ref, kw1_ref, kw1f_ref, kpos_ref, kw2_ref)
    kcb_ref[0] = _rope_slab(k, cos_ref[...], sin_ref[...]).astype(BF16)
    vcb_ref[0] = compress(vc_ref, vw1_ref, vw1f_ref, vpos_ref, vw2_ref).astype(BF16)


def _compress_weights(pos, w1, w2):
    w1r = w1.reshape(2, CMP_STRIDE, HEAD_DIM, CMP_HIDDEN)
    eye = jnp.eye(NSA_GROUPS, dtype=w1.dtype)
    big = jnp.einsum('hpdj,ge->pgdhej', w1r, eye)
    big = big.reshape(CMP_STRIDE * KV_WIDTH, 2 * NSA_GROUPS * CMP_HIDDEN).astype(BF16)
    w2big = jnp.einsum('jd,ge->gjed', w2, eye).reshape(NSA_GROUPS * CMP_HIDDEN, KV_WIDTH).astype(BF16)
    return big, w1.astype(BF16), pos.reshape(1, CMP_LEN * HEAD_DIM), w2big


def _compress(kc, vc, kparams, vparams, cos_c, sin_c):
    b, n_chunks, width = kc.shape
    args = (kc, vc) + _compress_weights(*kparams) + _compress_weights(*vparams) + (cos_c, sin_c)
    x_spec = pl.BlockSpec((1, n_chunks, width), lambda i: (i, 0, 0))
    in_specs = [x_spec, x_spec] + [_resident(a.shape) for a in args[2:]]
    o_spec = pl.BlockSpec((1, n_chunks, KV_WIDTH), lambda i: (i, 0, 0))
    o_shape = jax.ShapeDtypeStruct((b, n_chunks, KV_WIDTH), BF16)
    return pl.pallas_call(
        _compress_kernel,
        grid=(b,),
        in_specs=in_specs,
        out_specs=(o_spec, o_spec),
        out_shape=(o_shape, o_shape),
        compiler_params=pltpu.CompilerParams(dimension_semantics=("parallel",),
                                             vmem_limit_bytes=VMEM_LIMIT),
        name="compress",
    )(*args)


def _nsa_kernel(qa_ref, gs_ref, kcb_ref, vcb_ref, kv4_ref, ov_ref, o_ref, *, seq):
    qi = pl.program_id(1)
    t0 = qi * NSA_Q
    rows = NSA_REP * NSA_Q
    n_cmp = seq // CMP_STRIDE - 1
    gs = gs_ref[0]
    ov = ov_ref[...]

    def t_of(shape):
        return t0 + (lax.broadcasted_iota(jnp.int32, shape, 0) & (NSA_Q - 1))

    outs = []
    for g in range(NSA_GROUPS):
        lo, hi = g * HEAD_DIM, (g + 1) * HEAD_DIM
        q = jnp.concatenate(
            [qa_ref[0, :, (g * NSA_REP + r) * HEAD_DIM:(g * NSA_REP + r + 1) * HEAD_DIM]
             for r in range(NSA_REP)], axis=0)

        s = _dot_nt(q, kcb_ref[0, :, lo:hi])
        n_idx = lax.broadcasted_iota(jnp.int32, s.shape, 1)
        cmask = (n_idx * CMP_STRIDE + (CMP_LEN - 1) <= t_of(s.shape)) & (n_idx < n_cmp)
        p_cmp = _masked_softmax_rows(s, cmask).astype(BF16)
        o_cmp = _dot(p_cmp, vcb_ref[0, :, lo:hi])

        pi = _dot(p_cmp, ov)
        imp = pi[0:NSA_Q]
        for r in range(1, NSA_REP):
            imp = imp + pi[r * NSA_Q:(r + 1) * NSA_Q]
        jj = lax.broadcasted_iota(jnp.int32, imp.shape, 1)
        tblk = (t0 + lax.broadcasted_iota(jnp.int32, imp.shape, 0)) >> 6
        forced = (jj == 0) | (jj == tblk) | (jj == tblk - 1)
        imp = jnp.where(jj <= tblk, imp + jnp.where(forced, FORCE_BONUS, 0.0), NEG)
        sel = jnp.where(_rank_select(imp, seq // SEL_BLOCK, SEL_TOPK), 1.0, 0.0).astype(BF16)

        def sel_step(c, carry):
            k0 = pl.multiple_of(c * SEL_KEYS, SEL_KEYS)
            k = kv4_ref[0, pl.ds(k0, SEL_KEYS), lo:hi]
            v = kv4_ref[0, pl.ds(k0, SEL_KEYS), KV_WIDTH + lo:KV_WIDTH + hi]
            s = _dot_nt(q, k)
            blk = (k0 + lax.broadcasted_iota(jnp.int32, (LANES, SEL_KEYS), 1)) >> 6
            expand = jnp.where(blk == lax.broadcasted_iota(jnp.int32, (LANES, SEL_KEYS), 0), 1.0, 0.0)
            chosen = _dot(sel, expand.astype(BF16))
            chosen = jnp.concatenate([chosen] * NSA_REP, axis=0)
            kpos = k0 + lax.broadcasted_iota(jnp.int32, s.shape, 1)
            mask = (chosen > 0.5) & (kpos <= t_of(s.shape))
            return _online_update(carry, s, mask, v)

        init = (jnp.full((rows, 1), NEG, F32), jnp.zeros((rows, 1), F32), jnp.zeros((rows, HEAD_DIM), F32))
        _, l, acc = lax.fori_loop(0, qi // (SEL_KEYS // NSA_Q) + 1, sel_step, init)
        o_sel = acc / jnp.maximum(l, TINY)

        n_win = WINDOW + NSA_Q
        w0 = pl.multiple_of(jnp.maximum(t0 - WINDOW, 0), NSA_Q)
        k = kv4_ref[0, pl.ds(w0, n_win), 2 * KV_WIDTH + lo:2 * KV_WIDTH + hi]
        v = kv4_ref[0, pl.ds(w0, n_win), 3 * KV_WIDTH + lo:3 * KV_WIDTH + hi]
        s = _dot_nt(q, k)
        kpos = w0 + lax.broadcasted_iota(jnp.int32, s.shape, 1)
        tq = t_of(s.shape)
        wmask = (kpos <= tq) & (tq - kpos < WINDOW)
        o_win = _dot(_masked_softmax_rows(s, wmask).astype(BF16), v)

        for r in range(NSA_REP):
            c0 = (g * NSA_REP + r) * 3
            sl = slice(r * NSA_Q, (r + 1) * NSA_Q)
            outs.append(gs[:, c0:c0 + 1] * o_cmp[sl] + gs[:, c0 + 1:c0 + 2] * o_sel[sl]
                        + gs[:, c0 + 2:c0 + 3] * o_win[sl])
    o_ref[0] = jnp.concatenate(outs, axis=1).astype(BF16)


def _nsa(qa, gs, kcb, vcb, kv4, ov):
    b, seq, _ = qa.shape
    n_chunks = kcb.shape[1]

    def qspec(n):
        return pl.BlockSpec((1, NSA_Q, n), lambda i, j: (i, j, 0))

    def bspec(rows, n):
        return pl.BlockSpec((1, rows, n), lambda i, j: (i, 0, 0))

    return pl.pallas_call(
        functools.partial(_nsa_kernel, seq=seq),
        grid=(b, seq // NSA_Q),
        in_specs=[qspec(NSA_WIDTH), qspec(LANES), bspec(n_chunks, KV_WIDTH), bspec(n_chunks, KV_WIDTH),
                  bspec(seq, 4 * KV_WIDTH), pl.BlockSpec(ov.shape, lambda i, j: (0, 0))],
        out_specs=qspec(NSA_WIDTH),
        out_shape=jax.ShapeDtypeStruct((b, seq, NSA_WIDTH), BF16),
        compiler_params=pltpu.CompilerParams(dimension_semantics=("parallel", "parallel"),
                                             vmem_limit_bytes=VMEM_LIMIT),
        name="nsa",
    )(qa, gs, kcb, vcb, kv4, ov)


def _moba_kernel(q_ref, k_ref, v_ref, kmean_ref, o_ref):
    c = pl.program_id(1)
    nb = kmean_ref.shape[1]
    outs = []
    for h in range(MOBA_HEADS):
        lo, hi = h * HEAD_DIM, (h + 1) * HEAD_DIM
        q = q_ref[0, :, lo:hi]

        km = jnp.concatenate([kmean_ref[0, :, lo:hi], jnp.zeros((LANES - nb, HEAD_DIM), F32)], axis=0)
        gsc = _dot_nt(q, km.astype(BF16))
        jj = lax.broadcasted_iota(jnp.int32, gsc.shape, 1)
        past = jj < c
        gsc = jnp.where(past, gsc, NEG)
        chosen = jnp.where(_rank_select(gsc, nb, MOBA_TOPK) & past, 1.0, 0.0)

        def past_step(j, carry):
            k0 = pl.multiple_of(j * MOBA_BLOCK, MOBA_BLOCK)
            s = _dot_nt(q, k_ref[0, pl.ds(k0, MOBA_BLOCK), lo:hi])
            pick = jnp.sum(jnp.where(jj == j, chosen, 0.0), axis=-1, keepdims=True)
            mask = jnp.broadcast_to(pick > 0.5, s.shape)
            return _online_update(carry, s, mask, v_ref[0, pl.ds(k0, MOBA_BLOCK), lo:hi])

        init = (jnp.full((MOBA_BLOCK, 1), NEG, F32), jnp.zeros((MOBA_BLOCK, 1), F32),
                jnp.zeros((MOBA_BLOCK, HEAD_DIM), F32))
        carry = lax.fori_loop(0, c, past_step, init)

        k0 = pl.multiple_of(c * MOBA_BLOCK, MOBA_BLOCK)
        s = _dot_nt(q, k_ref[0, pl.ds(k0, MOBA_BLOCK), lo:hi])
        mask = lax.broadcasted_iota(jnp.int32, s.shape, 1) <= lax.broadcasted_iota(jnp.int32, s.shape, 0)
        _, l, acc = _online_update(carry, s, mask, v_ref[0, pl.ds(k0, MOBA_BLOCK), lo:hi])
        outs.append(acc / jnp.maximum(l, TINY))
    o_ref[0] = jnp.concatenate(outs, axis=1).astype(BF16)


def _moba(qkvb, kmean):
    b, seq, _ = qkvb.shape
    nb = seq // MOBA_BLOCK
    return pl.pallas_call(
        _moba_kernel,
        grid=(b, nb),
        in_specs=[pl.BlockSpec((1, MOBA_BLOCK, MOBA_WIDTH), lambda i, j: (i, j, 0)),
                  pl.BlockSpec((1, seq, MOBA_WIDTH), lambda i, j: (i, 0, 1)),
                  pl.BlockSpec((1, seq, MOBA_WIDTH), lambda i, j: (i, 0, 2)),
                  pl.BlockSpec((1, nb, MOBA_WIDTH), lambda i, j: (i, 0, 0))],
        out_specs=pl.BlockSpec((1, MOBA_BLOCK, MOBA_WIDTH), lambda i, j: (i, j, 0)),
        out_shape=jax.ShapeDtypeStruct((b, seq, MOBA_WIDTH), BF16),
        compiler_params=pltpu.CompilerParams(dimension_semantics=("parallel", "parallel"),
                                             vmem_limit_bytes=VMEM_LIMIT),
        name="moba",
    )(qkvb, qkvb, qkvb, kmean)


def _merge_kernel(x_ref, a_ref, b_ref, gates_ref, pa_ref, pb_ref, wo_ref, o_ref):
    ya = _dot(a_ref[...], pa_ref[...])
    yb = _dot(b_ref[...], pb_ref[...])
    merged = gates_ref[:, :D_MODEL] * ya + gates_ref[:, D_MODEL:] * yb
    o_ref[...] = x_ref[...] + _dot(merged.astype(BF16), wo_ref[...])


def _merge(x, a, bm, gates, pa, pb, wo):
    t = x.shape[0]

    def row(n):
        return pl.BlockSpec((TOKEN_TILE, n), lambda i: (i, 0))

    pa, pb, wo = pa.astype(BF16), pb.astype(BF16), wo.astype(BF16)
    return pl.pallas_call(
        _merge_kernel,
        grid=(t // TOKEN_TILE,),
        in_specs=[row(D_MODEL), row(NSA_WIDTH), row(MOBA_WIDTH), row(2 * D_MODEL),
                  _resident(pa.shape), _resident(pb.shape), _resident(wo.shape)],
        out_specs=row(D_MODEL),
        out_shape=jax.ShapeDtypeStruct(x.shape, F32),
        compiler_params=pltpu.CompilerParams(dimension_semantics=("parallel",),
                                             vmem_limit_bytes=VMEM_LIMIT),
        name="merge",
    )(x, a, bm, gates, pa, pb, wo)


def _rope_tables(pos):
    inv = ROPE_THETA ** (-jnp.arange(0, HEAD_DIM, 2, dtype=F32) / HEAD_DIM)
    ang = pos.astype(F32)[:, None] * inv[None, :]
    cos, sin = jnp.cos(ang), jnp.sin(ang)
    reps = LANES // HEAD_DIM
    return (jnp.tile(jnp.concatenate([cos, cos], axis=1), (1, reps)),
            jnp.tile(jnp.concatenate([-sin, sin], axis=1), (1, reps)))


def _overlap_matrix(seq):
    n_chunks = seq // CMP_STRIDE
    ci = np.arange(n_chunks)[:, None] * CMP_STRIDE
    sj = np.arange(LANES)[None, :] * SEL_BLOCK
    ov = (ci < sj + SEL_BLOCK) & (ci + CMP_LEN > sj) & (np.arange(LANES)[None, :] < seq // SEL_BLOCK)
    return jnp.asarray(ov, dtype=BF16)


def kernel(x, ffn1_norm, ffn1_wg, ffn1_wu, ffn1_wd, mix_norm, w_in, cmpk_pos, cmpk_w1, cmpk_w2, cmpv_pos, cmpv_w1, cmpv_w2, w_branch_nsa, w_branch_moba, w_out, ffn2_norm, ffn2_wg, ffn2_wu, ffn2_wd, final_norm):
    b, seq, d = x.shape
    depth = w_in.shape[0]
    assert d == D_MODEL and seq % TOKEN_TILE == 0 and seq // CMP_STRIDE == LANES
    t = b * seq
    n_chunks = seq // CMP_STRIDE
    cos_t, sin_t = _rope_tables(jnp.arange(seq))
    cos_c, sin_c = _rope_tables(jnp.arange(n_chunks) * CMP_STRIDE + (CMP_LEN - 1))
    ov = _overlap_matrix(seq)

    xf = x.reshape(t, d)
    for l in range(depth):
        xf = _ffn(xf, ffn1_norm[l], ffn1_wg[l], ffn1_wu[l], ffn1_wd[l], final_norm, False)
        qa, kc, vc, kv4, qkvb, gates, gs, kmean = _proj(xf, mix_norm[l], w_in[l], cos_t, sin_t, seq)
        kcb, vcb = _compress(kc.reshape(b, n_chunks, CMP_STRIDE * KV_WIDTH),
                             vc.reshape(b, n_chunks, CMP_STRIDE * KV_WIDTH),
                             (cmpk_pos[l], cmpk_w1[l], cmpk_w2[l]),
                             (cmpv_pos[l], cmpv_w1[l], cmpv_w2[l]), cos_c, sin_c)
        a = _nsa(qa.reshape(b, seq, NSA_WIDTH), gs.reshape(b, seq, LANES), kcb, vcb,
                 kv4.reshape(b, seq, 4 * KV_WIDTH), ov)
        bm = _moba(qkvb.reshape(b, seq, 3 * MOBA_WIDTH),
                   kmean.reshape(b, seq // MOBA_BLOCK, MOBA_WIDTH))
        xf = _merge(xf, a.reshape(t, NSA_WIDTH), bm.reshape(t, MOBA_WIDTH), gates,
                    w_branch_nsa[l], w_branch_moba[l], w_out[l])
        xf = _ffn(xf, ffn2_norm[l], ffn2_wg[l], ffn2_wu[l], ffn2_wd[l], final_norm, l == depth - 1)
    return xf.reshape(b, seq, d)
```

```python
import functools

import jax
import jax.numpy as jnp
import numpy as np
from jax import lax
from jax.experimental import pallas as pl
from jax.experimental.pallas import tpu as pltpu

F32 = jnp.float32
BF16 = jnp.bfloat16

D_MODEL = 1024
HEAD_DIM = 64
HALF = HEAD_DIM // 2
NSA_HEADS = 8
NSA_GROUPS = 2
NSA_REP = NSA_HEADS // NSA_GROUPS
CMP_LEN = 32
CMP_STRIDE = 16
CMP_HIDDEN = 128
SEL_BLOCK = 64
SEL_TOPK = 16
WINDOW = 512
MOBA_HEADS = 8
MOBA_BLOCK = 256
MOBA_TOPK = 3
D_FF = 2816
ROPE_THETA = 10000.0
EPS = 1e-6
NEG = -1e30
TINY = 1e-30
FORCE_BONUS = 1e4
SCALE = HEAD_DIM ** -0.5

NSA_WIDTH = NSA_HEADS * HEAD_DIM
KV_WIDTH = NSA_GROUPS * HEAD_DIM
MOBA_WIDTH = MOBA_HEADS * HEAD_DIM
N_GATE_COLS = 3 * NSA_HEADS

LANES = 128
BF16_ROWS = 16
TOKEN_TILE = 512
FF_CHUNK = 256
NSA_Q = 128
SEL_KEYS = 512
VMEM_LIMIT = 56 * 1024 * 1024

COL_QA = 0
COL_KC = COL_QA + NSA_WIDTH
COL_VC = COL_KC + KV_WIDTH
COL_KV4 = COL_VC + KV_WIDTH
COL_QB = COL_KV4 + 4 * KV_WIDTH
COL_KB = COL_QB + MOBA_WIDTH
COL_VB = COL_KB + MOBA_WIDTH
COL_GATES = COL_VB + MOBA_WIDTH
COL_GS = COL_GATES + 2 * D_MODEL
IN_COLS_PADDED = COL_GS + LANES


def _rms(x, g):
    return x * lax.rsqrt(jnp.mean(x * x, axis=-1, keepdims=True) + EPS) * g


def _dot(a, b):
    return jnp.dot(a, b, preferred_element_type=F32)


def _dot_nt(a, b):
    return lax.dot_general(a, b, (((1,), (1,)), ((), ())), preferred_element_type=F32)


def _swap_halves(z):
    lane = lax.broadcasted_iota(jnp.int32, z.shape, 1)
    lo = (lane & (HEAD_DIM - 1)) < HALF
    return jnp.where(lo, pltpu.roll(z, LANES - HALF, 1), pltpu.roll(z, HALF, 1))


def _rope_slab(z, cos, sin):
    parts = []
    for c in range(z.shape[1] // LANES):
        zc = z[:, c * LANES:(c + 1) * LANES]
        parts.append(zc * cos + _swap_halves(zc) * sin)
    return parts[0] if len(parts) == 1 else jnp.concatenate(parts, axis=1)


def _slabs(s, n):
    return [s[:, i * n:(i + 1) * n] for i in range(s.shape[1] // n)]


def _masked_softmax_keys(s, mask):
    out = []
    for x in _slabs(s, mask.shape[1]):
        x = jnp.where(mask, x, NEG)
        m = jnp.max(x, axis=0, keepdims=True)
        p = jnp.where(mask, jnp.exp(x - m), 0.0)
        out.append(p / jnp.maximum(jnp.sum(p, axis=0, keepdims=True), TINY))
    return jnp.concatenate(out, axis=1)


def _online_update(carry, s, mask, vt):
    m, l, acc = carry
    q = mask.shape[1]
    xs = [jnp.where(mask, x, NEG) for x in _slabs(s, q)]
    m_new = jnp.maximum(m, jnp.concatenate([jnp.max(x, axis=0, keepdims=True) for x in xs], axis=1))
    alpha = jnp.exp(m - m_new)
    p = jnp.concatenate([jnp.where(mask, jnp.exp(x - mn), 0.0) for x, mn in zip(xs, _slabs(m_new, q))], axis=1)
    l = alpha * l + jnp.sum(p, axis=0, keepdims=True)
    acc = alpha * acc + _dot(vt, p.astype(BF16))
    return m_new, l, acc


def _online_init(n):
    return jnp.full((1, n), NEG, F32), jnp.zeros((1, n), F32), jnp.zeros((HEAD_DIM, n), F32)


def _rank_select(v, n_candidates, k):
    row = lax.broadcasted_iota(jnp.int32, v.shape, 0)
    cnt = jnp.zeros(v.shape, F32)
    for j in range(n_candidates):
        r = v[j:j + 1, :]
        cnt = cnt + jnp.where(r > v, 1.0, jnp.where(r == v, jnp.where(row > j, 1.0, 0.0), 0.0))
    return cnt < float(k)


def _ffn_kernel(x_ref, g_ref, wg_ref, wu_ref, wd_ref, fin_ref, o_ref, *, final_norm):
    x = x_ref[...]
    h = _rms(x, g_ref[...]).astype(BF16)

    def body(f, acc):
        a = _dot(h, wg_ref[f])
        b = _dot(h, wu_ref[f])
        t = (a * jax.nn.sigmoid(a) * b).astype(BF16)
        return acc + _dot(t, wd_ref[f])

    acc = lax.fori_loop(0, D_FF // FF_CHUNK, body, jnp.zeros(x.shape, F32))
    y = x + 0.5 * acc
    if final_norm:
        y = _rms(y, fin_ref[...])
    o_ref[...] = y


def _resident(shape):
    nd = len(shape)
    return pl.BlockSpec(shape, lambda *_: (0,) * nd, pipeline_mode=pl.Buffered(1))


def _ffn(x, g, wg, wu, wd, fin, final_norm):
    t = x.shape[0]
    nf = D_FF // FF_CHUNK
    wg3 = wg.astype(BF16).reshape(D_MODEL, nf, FF_CHUNK).transpose(1, 0, 2)
    wu3 = wu.astype(BF16).reshape(D_MODEL, nf, FF_CHUNK).transpose(1, 0, 2)
    wd3 = wd.astype(BF16).reshape(nf, FF_CHUNK, D_MODEL)
    row = pl.BlockSpec((TOKEN_TILE, D_MODEL), lambda i: (i, 0))
    return pl.pallas_call(
        functools.partial(_ffn_kernel, final_norm=final_norm),
        grid=(t // TOKEN_TILE,),
        in_specs=[row, _resident((1, D_MODEL)), _resident(wg3.shape), _resident(wu3.shape),
                  _resident(wd3.shape), _resident((1, D_MODEL))],
        out_specs=row,
        out_shape=jax.ShapeDtypeStruct(x.shape, F32),
        compiler_params=pltpu.CompilerParams(dimension_semantics=("parallel",),
                                             vmem_limit_bytes=VMEM_LIMIT),
        name="ffn",
    )(x, g.reshape(1, D_MODEL), wg3, wu3, wd3, fin.reshape(1, D_MODEL))


def _proj_kernel(x_ref, g_ref, w_ref, cos_ref, sin_ref,
                 qa_ref, kc_ref, vc_ref, kk_ref, vv_ref, qb_ref, kb_ref, vb_ref, gates_ref, gs_ref,
                 kmean_ref):
    h = _rms(x_ref[...], g_ref[...]).astype(BF16)
    cos = cos_ref[...]
    sin = sin_ref[...]

    def mm(c0, n):
        return _dot(h, w_ref[:, c0:c0 + n])

    qa_ref[...] = (_rope_slab(mm(COL_QA, NSA_WIDTH), cos, sin) * SCALE).astype(BF16)
    kcvc = mm(COL_KC, 2 * KV_WIDTH)
    kc_ref[...] = kcvc[:, :KV_WIDTH].astype(BF16)
    vc_ref[...] = kcvc[:, KV_WIDTH:].astype(BF16)
    kv4 = mm(COL_KV4, 4 * KV_WIDTH)
    kk_ref[...] = jnp.concatenate(
        [_rope_slab(kv4[:, 0:KV_WIDTH], cos, sin), _rope_slab(kv4[:, 2 * KV_WIDTH:3 * KV_WIDTH], cos, sin)],
        axis=1).astype(BF16)
    vv_ref[...] = jnp.concatenate([kv4[:, KV_WIDTH:2 * KV_WIDTH], kv4[:, 3 * KV_WIDTH:]], axis=1).astype(BF16)
    qb_ref[...] = (_rope_slab(mm(COL_QB, MOBA_WIDTH), cos, sin) * SCALE).astype(BF16)
    kb = _rope_slab(mm(COL_KB, MOBA_WIDTH), cos, sin)
    kb_ref[...] = kb.astype(BF16)
    means = [jnp.mean(kb[i * MOBA_BLOCK:(i + 1) * MOBA_BLOCK], axis=0, keepdims=True)
             for i in range(TOKEN_TILE // MOBA_BLOCK)]
    kmean_ref[0] = jnp.concatenate(means, axis=0)
    vb_ref[...] = mm(COL_VB, MOBA_WIDTH).astype(BF16)
    gates_ref[:, :D_MODEL] = jax.nn.sigmoid(mm(COL_GATES, D_MODEL))
    gates_ref[:, D_MODEL:] = jax.nn.sigmoid(mm(COL_GATES + D_MODEL, D_MODEL))
    gs_ref[...] = jax.nn.sigmoid(mm(COL_GS, LANES))


def _reorder_w_in(w_in):
    sizes = (NSA_WIDTH, N_GATE_COLS) + (KV_WIDTH,) * 6 + (MOBA_WIDTH,) * 3 + (D_MODEL, D_MODEL)
    offs = np.concatenate([[0], np.cumsum(sizes)])
    parts = [w_in[:, offs[i]:offs[i + 1]] for i in range(len(sizes))]
    q_a, g_a = parts[0], parts[1]
    rest = parts[2:]
    pad = jnp.zeros((D_MODEL, LANES - N_GATE_COLS), w_in.dtype)
    return jnp.concatenate([q_a] + rest + [g_a, pad], axis=1).astype(BF16)


def _proj(x, g, w_in, cos_t, sin_t, seq):
    t = x.shape[0]
    w = _reorder_w_in(w_in)
    tiles_per_seq = seq // TOKEN_TILE
    blocks_per_tile = TOKEN_TILE // MOBA_BLOCK

    def row(n):
        return pl.BlockSpec((TOKEN_TILE, n), lambda i: (i, 0))

    def act(n, dtype=BF16):
        return jax.ShapeDtypeStruct((t, n), dtype)

    tab = pl.BlockSpec((TOKEN_TILE, LANES), lambda i: (i % tiles_per_seq, 0))
    widths = (NSA_WIDTH, KV_WIDTH, KV_WIDTH, 2 * KV_WIDTH, 2 * KV_WIDTH, MOBA_WIDTH, MOBA_WIDTH, MOBA_WIDTH)
    out_shape = tuple(act(n) for n in widths) + (
        act(2 * D_MODEL, F32), act(LANES, F32),
        jax.ShapeDtypeStruct((t // TOKEN_TILE, blocks_per_tile, MOBA_WIDTH), F32))
    out_specs = tuple(row(n) for n in widths) + (
        row(2 * D_MODEL), row(LANES),
        pl.BlockSpec((1, blocks_per_tile, MOBA_WIDTH), lambda i: (i, 0, 0)))
    return pl.pallas_call(
        _proj_kernel,
        grid=(t // TOKEN_TILE,),
        in_specs=[row(D_MODEL), _resident((1, D_MODEL)), _resident(w.shape), tab, tab],
        out_specs=out_specs,
        out_shape=out_shape,
        compiler_params=pltpu.CompilerParams(dimension_semantics=("parallel",),
                                             vmem_limit_bytes=VMEM_LIMIT),
        name="proj",
    )(x, g.reshape(1, D_MODEL), w, cos_t, sin_t)


def _gelu_tanh(x):
    return 0.5 * x * (1.0 + jnp.tanh(np.sqrt(2.0 / np.pi).astype(np.float32) * (x + 0.044715 * (x * x * x))))


def _compress_kernel(kc_ref, vc_ref, kw1_ref, kw1f_ref, kpos_ref, kw2_ref,
                     vw1_ref, vw1f_ref, vpos_ref, vw2_ref, cos_ref, sin_ref, kcb_ref, vcb_ref):
    n_chunks = kc_ref.shape[1]

    def compress(x_ref, w1_ref, w1f_ref, pos_ref, w2_ref):
        u = _dot(x_ref[0], w1_ref[...])
        first = u[:, :2 * CMP_HIDDEN]
        second = pltpu.roll(u[:, 2 * CMP_HIDDEN:], n_chunks - 1, 0)
        pos = jnp.broadcast_to(pos_ref[...], (8, CMP_LEN * HEAD_DIM)).astype(BF16)
        bias = _dot(pos, w1f_ref[...])[0:1]
        pre = first + second + jnp.concatenate([bias, bias], axis=1)
        return _dot(_gelu_tanh(pre).astype(BF16), w2_ref[...])

    k = compress(kc_ref, kw1_ref, kw1f_ref, kpos_ref, kw2_ref)
    kcb_ref[0] = _rope_slab(k, cos_ref[...], sin_ref[...]).astype(BF16)
    vcb_ref[0] = compress(vc_ref, vw1_ref, vw1f_ref, vpos_ref, vw2_ref).astype(BF16)


def _compress_weights(pos, w1, w2):
    w1r = w1.reshape(2, CMP_STRIDE, HEAD_DIM, CMP_HIDDEN)
    eye = jnp.eye(NSA_GROUPS, dtype=w1.dtype)
    big = jnp.einsum('hpdj,ge->pgdhej', w1r, eye)
    big = big.reshape(CMP_STRIDE * KV_WIDTH, 2 * NSA_GROUPS * CMP_HIDDEN).astype(BF16)
    w2big = jnp.einsum('jd,ge->gjed', w2, eye).reshape(NSA_GROUPS * CMP_HIDDEN, KV_WIDTH).astype(BF16)
    return big, w1.astype(BF16), pos.reshape(1, CMP_LEN * HEAD_DIM), w2big


def _compress(kc, vc, kparams, vparams, cos_c, sin_c):
    b, n_chunks, width = kc.shape
    args = (kc, vc) + _compress_weights(*kparams) + _compress_weights(*vparams) + (cos_c, sin_c)
    x_spec = pl.BlockSpec((1, n_chunks, width), lambda i: (i, 0, 0))
    in_specs = [x_spec, x_spec] + [_resident(a.shape) for a in args[2:]]
    o_spec = pl.BlockSpec((1, n_chunks, KV_WIDTH), lambda i: (i, 0, 0))
    o_shape = jax.ShapeDtypeStruct((b, n_chunks, KV_WIDTH), BF16)
    return pl.pallas_call(
        _compress_kernel,
        grid=(b,),
        in_specs=in_specs,
        out_specs=(o_spec, o_spec),
        out_shape=(o_shape, o_shape),
        compiler_params=pltpu.CompilerParams(dimension_semantics=("parallel",),
                                             vmem_limit_bytes=VMEM_LIMIT),
        name="compress",
    )(*args)


def _nsa_kernel(qa_ref, gst_ref, kcb_ref, vcbt_ref, kk_ref, vst_ref, vwt_ref, ovt_ref, o_ref, *, seq):
    qi = pl.program_id(1)
    t0 = qi * NSA_Q
    cols = NSA_REP * NSA_Q
    n_cmp = seq // CMP_STRIDE - 1
    n_sel = seq // SEL_BLOCK
    gst = gst_ref[0]
    ovt = ovt_ref[...]

    lane = lax.broadcasted_iota(jnp.int32, (NSA_Q, LANES), 1)

    def key_and_query_pos(n, k0):
        shape = (n, NSA_Q)
        return k0 + lax.broadcasted_iota(jnp.int32, shape, 0), t0 + lax.broadcasted_iota(jnp.int32, shape, 1)

    groups = range(NSA_GROUPS)
    d_rows = [slice(g * HEAD_DIM, (g + 1) * HEAD_DIM) for g in groups]
    g_cols = [slice(g * cols, (g + 1) * cols) for g in groups]

    heads = []
    for n in range(NSA_HEADS):
        g = n // NSA_REP
        x = qa_ref[0, :, (n // 2) * LANES:(n // 2 + 1) * LANES].astype(F32)
        if (n % 2) != g:
            x = pltpu.roll(x, HEAD_DIM, 1)
        heads.append(jnp.where((lane >= g * HEAD_DIM) & (lane < (g + 1) * HEAD_DIM), x, 0.0).astype(BF16))
    q = jnp.concatenate(heads, axis=0)

    s = _dot_nt(kcb_ref[0], q)
    n_idx, tq = key_and_query_pos(s.shape[0], 0)
    cmask = (n_idx * CMP_STRIDE + (CMP_LEN - 1) <= tq) & (n_idx < n_cmp)
    p_cmp = _masked_softmax_keys(s, cmask).astype(BF16)
    o_cmp = [_dot(vcbt_ref[0, d_rows[g], :], p_cmp[:, g_cols[g]]) for g in groups]

    pi = _dot(ovt, p_cmp)
    jj, tsel = key_and_query_pos(n_sel, 0)
    tblk = tsel >> 6
    bonus = jnp.where((jj == 0) | (jj == tblk) | (jj == tblk - 1), FORCE_BONUS, 0.0)
    sels = []
    for g in groups:
        imp = pi[:, g * cols:g * cols + NSA_Q]
        for r in range(1, NSA_REP):
            imp = imp + pi[:, g * cols + r * NSA_Q:g * cols + (r + 1) * NSA_Q]
        imp = jnp.where(jj <= tblk, imp + bonus, NEG)
        sels.append(jnp.where(_rank_select(imp, n_sel, SEL_TOPK), 1.0, 0.0).astype(BF16))
    sel = jnp.concatenate(sels, axis=1)

    def sel_step(c, carry):
        k0 = pl.multiple_of(c * SEL_KEYS, SEL_KEYS)
        s = _dot_nt(kk_ref[0, pl.ds(k0, SEL_KEYS), 0:KV_WIDTH], q)
        blk = (k0 + lax.broadcasted_iota(jnp.int32, (SEL_KEYS, n_sel), 0)) >> 6
        expand = jnp.where(blk == lax.broadcasted_iota(jnp.int32, (SEL_KEYS, n_sel), 1), 1.0, 0.0)
        chosen = _dot(expand.astype(BF16), sel)
        kpos, tq = key_and_query_pos(SEL_KEYS, k0)
        out = []
        for g in groups:
            mask = jnp.where(kpos <= tq, chosen[:, g * NSA_Q:(g + 1) * NSA_Q], 0.0) > 0.5
            out.append(_online_update(carry[g], s[:, g_cols[g]], mask,
                                      vst_ref[0, d_rows[g], pl.ds(k0, SEL_KEYS)]))
        return tuple(out)

    state = lax.fori_loop(0, qi // (SEL_KEYS // NSA_Q) + 1, sel_step,
                          tuple(_online_init(cols) for _ in groups))
    o_sel = [acc / jnp.maximum(l, TINY) for _, l, acc in state]

    n_win = WINDOW + NSA_Q
    w0 = pl.multiple_of(jnp.maximum(t0 - WINDOW, 0), NSA_Q)
    s = _dot_nt(kk_ref[0, pl.ds(w0, n_win), KV_WIDTH:2 * KV_WIDTH], q)
    kpos, tq = key_and_query_pos(n_win, w0)
    wmask = (kpos <= tq) & (tq - kpos < WINDOW)
    p_win = _masked_softmax_keys(s, wmask).astype(BF16)
    o_win = [_dot(vwt_ref[0, d_rows[g], pl.ds(w0, n_win)], p_win[:, g_cols[g]]) for g in groups]

    outs = []
    for n in range(NSA_HEADS):
        g, r = divmod(n, NSA_REP)
        sl = slice(r * NSA_Q, (r + 1) * NSA_Q)
        outs.append(gst[3 * n:3 * n + 1, :] * o_cmp[g][:, sl] + gst[3 * n + 1:3 * n + 2, :] * o_sel[g][:, sl]
                    + gst[3 * n + 2:3 * n + 3, :] * o_win[g][:, sl])
    o_ref[0] = jnp.concatenate(outs, axis=0).astype(BF16)


def _nsa(qa, gst, kcb, vcbt, kk, vvt, ovt):
    b, seq, _ = qa.shape
    n_chunks = kcb.shape[1]
    return pl.pallas_call(
        functools.partial(_nsa_kernel, seq=seq),
        grid=(b, seq // NSA_Q),
        in_specs=[pl.BlockSpec((1, NSA_Q, NSA_WIDTH), lambda i, j: (i, j, 0)),
                  pl.BlockSpec((1, LANES, NSA_Q), lambda i, j: (i, 0, j)),
                  pl.BlockSpec((1, n_chunks, KV_WIDTH), lambda i, j: (i, 0, 0)),
                  pl.BlockSpec((1, KV_WIDTH, n_chunks), lambda i, j: (i, 0, 0)),
                  pl.BlockSpec((1, seq, 2 * KV_WIDTH), lambda i, j: (i, 0, 0)),
                  pl.BlockSpec((1, KV_WIDTH, seq), lambda i, j: (i, 0, 0)),
                  pl.BlockSpec((1, KV_WIDTH, seq), lambda i, j: (i, 1, 0)),
                  pl.BlockSpec(ovt.shape, lambda i, j: (0, 0))],
        out_specs=pl.BlockSpec((1, NSA_WIDTH, NSA_Q), lambda i, j: (i, 0, j)),
        out_shape=jax.ShapeDtypeStruct((b, NSA_WIDTH, seq), BF16),
        compiler_params=pltpu.CompilerParams(dimension_semantics=("parallel", "parallel"),
                                             vmem_limit_bytes=VMEM_LIMIT),
        name="nsa",
    )(qa, gst, kcb, vcbt, kk, vvt, vvt, ovt)


def _moba_kernel(q_ref, k_ref, vt_ref, kmean_ref, o_ref, acc_ref):
    c = pl.program_id(1)
    nb = kmean_ref.shape[1]
    k_own = pl.multiple_of(c * MOBA_BLOCK, MOBA_BLOCK)
    lane = lax.broadcasted_iota(jnp.int32, (MOBA_BLOCK, LANES), 1)
    jj = lax.broadcasted_iota(jnp.int32, (BF16_ROWS, MOBA_BLOCK), 0)
    past = jj < c
    pad = jnp.zeros((BF16_ROWS - nb, LANES), F32)

    def pair(h):
        return slice((h // 2) * LANES, (h // 2 + 1) * LANES)

    qs, gscs = [], []
    for h in range(MOBA_HEADS):
        mine = (lane >= (h % 2) * HEAD_DIM) & (lane < (h % 2 + 1) * HEAD_DIM)
        q = jnp.where(mine, q_ref[0, :, pair(h)], jnp.zeros((), BF16))
        km = jnp.concatenate([kmean_ref[0, :, pair(h)], pad], axis=0).astype(BF16)
        qs.append(q)
        gscs.append(_dot_nt(km, q))
    chosen = [jnp.where(_rank_select(jnp.where(past, g, NEG), nb, MOBA_TOPK) & past, 1.0, 0.0) for g in gscs]

    acc_ref[...] = jnp.zeros(acc_ref.shape, F32)

    def past_step(j, ml):
        k0 = pl.multiple_of(j * MOBA_BLOCK, MOBA_BLOCK)
        out = []

        ss = [_dot_nt(k_ref[0, pl.ds(k0, MOBA_BLOCK), pair(h)], qs[h]) for h in range(MOBA_HEADS)]
        for h in range(MOBA_HEADS):
            rows = slice(h * HEAD_DIM, (h + 1) * HEAD_DIM)
            s = ss[h]
            pick = jnp.sum(jnp.where(jj == j, chosen[h], 0.0), axis=0, keepdims=True) > 0.5
            m, l = ml[h]
            m_new = jnp.maximum(m, jnp.where(pick, jnp.max(s, axis=0, keepdims=True), NEG))
            alpha = jnp.exp(m - m_new)
            p = jnp.exp(s - jnp.where(pick, m_new, jnp.inf))
            out.append((m_new, alpha * l + jnp.sum(p, axis=0, keepdims=True)))
            acc_ref[rows, :] = alpha * acc_ref[rows, :] + _dot(
                vt_ref[0, rows, pl.ds(k0, MOBA_BLOCK)], p.astype(BF16))
        return tuple(out)

    init = tuple((jnp.full((1, MOBA_BLOCK), NEG, F32), jnp.zeros((1, MOBA_BLOCK), F32))
                 for _ in range(MOBA_HEADS))
    ml = lax.fori_loop(0, c, past_step, init)

    kq = (MOBA_BLOCK, MOBA_BLOCK)
    causal = lax.broadcasted_iota(jnp.int32, kq, 0) <= lax.broadcasted_iota(jnp.int32, kq, 1)
    ss = [_dot_nt(k_ref[0, pl.ds(k_own, MOBA_BLOCK), pair(h)], qs[h]) for h in range(MOBA_HEADS)]
    for h in range(MOBA_HEADS):
        rows = slice(h * HEAD_DIM, (h + 1) * HEAD_DIM)
        carry = (ml[h][0], ml[h][1], acc_ref[rows, :])
        _, l, acc = _online_update(carry, ss[h], causal, vt_ref[0, rows, pl.ds(k_own, MOBA_BLOCK)])
        o_ref[0, rows, :] = (acc / jnp.maximum(l, TINY)).astype(BF16)


def _moba(qb, kb, vbt, kmean):
    b, seq, _ = qb.shape
    nb = seq // MOBA_BLOCK
    return pl.pallas_call(
        _moba_kernel,
        grid=(b, nb),
        in_specs=[pl.BlockSpec((1, MOBA_BLOCK, MOBA_WIDTH), lambda i, j: (i, j, 0)),
                  pl.BlockSpec((1, seq, MOBA_WIDTH), lambda i, j: (i, 0, 0)),
                  pl.BlockSpec((1, MOBA_WIDTH, seq), lambda i, j: (i, 0, 0)),
                  pl.BlockSpec((1, nb, MOBA_WIDTH), lambda i, j: (i, 0, 0))],
        out_specs=pl.BlockSpec((1, MOBA_WIDTH, MOBA_BLOCK), lambda i, j: (i, 0, j)),
        out_shape=jax.ShapeDtypeStruct((b, MOBA_WIDTH, seq), BF16),
        scratch_shapes=[pltpu.VMEM((MOBA_WIDTH, MOBA_BLOCK), F32)],
        compiler_params=pltpu.CompilerParams(dimension_semantics=("parallel", "parallel"),
                                             vmem_limit_bytes=VMEM_LIMIT),
        name="moba",
    )(qb, kb, vbt, kmean)


def _merge_kernel(x_ref, a_ref, b_ref, gates_ref, pa_ref, pb_ref, wo_ref, o_ref):
    ya = _dot(a_ref[...], pa_ref[...])
    yb = _dot(b_ref[...], pb_ref[...])
    merged = gates_ref[:, :D_MODEL] * ya + gates_ref[:, D_MODEL:] * yb
    o_ref[...] = x_ref[...] + _dot(merged.astype(BF16), wo_ref[...])


def _merge(x, a, bm, gates, pa, pb, wo):
    t = x.shape[0]

    def row(n):
        return pl.BlockSpec((TOKEN_TILE, n), lambda i: (i, 0))

    pa, pb, wo = pa.astype(BF16), pb.astype(BF16), wo.astype(BF16)
    return pl.pallas_call(
        _merge_kernel,
        grid=(t // TOKEN_TILE,),
        in_specs=[row(D_MODEL), row(NSA_WIDTH), row(MOBA_WIDTH), row(2 * D_MODEL),
                  _resident(pa.shape), _resident(pb.shape), _resident(wo.shape)],
        out_specs=row(D_MODEL),
        out_shape=jax.ShapeDtypeStruct(x.shape, F32),
        compiler_params=pltpu.CompilerParams(dimension_semantics=("parallel",),
                                             vmem_limit_bytes=VMEM_LIMIT),
        name="merge",
    )(x, a, bm, gates, pa, pb, wo)


def _rope_tables(pos):
    inv = ROPE_THETA ** (-jnp.arange(0, HEAD_DIM, 2, dtype=F32) / HEAD_DIM)
    ang = pos.astype(F32)[:, None] * inv[None, :]
    cos, sin = jnp.cos(ang), jnp.sin(ang)
    reps = LANES // HEAD_DIM
    return (jnp.tile(jnp.concatenate([cos, cos], axis=1), (1, reps)),
            jnp.tile(jnp.concatenate([-sin, sin], axis=1), (1, reps)))


def _overlap_matrix_t(seq):
    n_chunks = seq // CMP_STRIDE
    ci = np.arange(n_chunks)[None, :] * CMP_STRIDE
    sj = np.arange(seq // SEL_BLOCK)[:, None] * SEL_BLOCK
    ov = (ci < sj + SEL_BLOCK) & (ci + CMP_LEN > sj) & (np.arange(n_chunks)[None, :] < n_chunks - 1)
    return jnp.asarray(ov, dtype=BF16)


def kernel(x, ffn1_norm, ffn1_wg, ffn1_wu, ffn1_wd, mix_norm, w_in, cmpk_pos, cmpk_w1, cmpk_w2, cmpv_pos, cmpv_w1, cmpv_w2, w_branch_nsa, w_branch_moba, w_out, ffn2_norm, ffn2_wg, ffn2_wu, ffn2_wd, final_norm):
    b, seq, d = x.shape
    depth = w_in.shape[0]
    assert d == D_MODEL and seq % TOKEN_TILE == 0 and seq // CMP_STRIDE == LANES
    t = b * seq
    n_chunks = seq // CMP_STRIDE
    cos_t, sin_t = _rope_tables(jnp.arange(seq))
    cos_c, sin_c = _rope_tables(jnp.arange(n_chunks) * CMP_STRIDE + (CMP_LEN - 1))
    ovt = _overlap_matrix_t(seq)

    def seq_major(a):
        return a.reshape(b, seq, a.shape[-1]).transpose(0, 2, 1)

    xf = x.reshape(t, d)
    for l in range(depth):
        xf = _ffn(xf, ffn1_norm[l], ffn1_wg[l], ffn1_wu[l], ffn1_wd[l], final_norm, False)
        qa, kc, vc, kk, vv, qb, kb, vb, gates, gs, kmean = _proj(xf, mix_norm[l], w_in[l], cos_t, sin_t, seq)
        kcb, vcb = _compress(kc.reshape(b, n_chunks, CMP_STRIDE * KV_WIDTH),
                             vc.reshape(b, n_chunks, CMP_STRIDE * KV_WIDTH),
                             (cmpk_pos[l], cmpk_w1[l], cmpk_w2[l]),
                             (cmpv_pos[l], cmpv_w1[l], cmpv_w2[l]), cos_c, sin_c)
        at = _nsa(qa.reshape(b, seq, NSA_WIDTH), seq_major(gs), kcb, vcb.transpose(0, 2, 1),
                  kk.reshape(b, seq, 2 * KV_WIDTH), seq_major(vv), ovt)
        bt = _moba(qb.reshape(b, seq, MOBA_WIDTH), kb.reshape(b, seq, MOBA_WIDTH), seq_major(vb),
                   kmean.reshape(b, seq // MOBA_BLOCK, MOBA_WIDTH))
        xf = _merge(xf, at.transpose(0, 2, 1).reshape(t, NSA_WIDTH), bt.transpose(0, 2, 1).reshape(t, MOBA_WIDTH),
                    gates, w_branch_nsa[l], w_branch_moba[l], w_out[l])
        xf = _ffn(xf, ffn2_norm[l], ffn2_wg[l], ffn2_wu[l], ffn2_wd[l], final_norm, l == depth - 1)
    return xf.reshape(b, seq, d)
```

```python
import functools

import jax
import jax.numpy as jnp
import numpy as np
from jax import lax
from jax.experimental import pallas as pl
from jax.experimental.pallas import tpu as pltpu

F32 = jnp.float32
BF16 = jnp.bfloat16

D_MODEL = 1024
HEAD_DIM = 64
HALF = HEAD_DIM // 2
NSA_HEADS = 8
NSA_GROUPS = 2
NSA_REP = NSA_HEADS // NSA_GROUPS
CMP_LEN = 32
CMP_STRIDE = 16
CMP_HIDDEN = 128
SEL_BLOCK = 64
SEL_TOPK = 16
WINDOW = 512
MOBA_HEADS = 8
MOBA_BLOCK = 256
MOBA_TOPK = 3
D_FF = 2816
ROPE_THETA = 10000.0
EPS = 1e-6
NEG = -1e30
TINY = 1e-30
FORCE_BONUS = 1e4
SCALE = HEAD_DIM ** -0.5

NSA_WIDTH = NSA_HEADS * HEAD_DIM
KV_WIDTH = NSA_GROUPS * HEAD_DIM
MOBA_WIDTH = MOBA_HEADS * HEAD_DIM
N_GATE_COLS = 3 * NSA_HEADS

LANES = 128
BF16_ROWS = 16
TOKEN_TILE = 512
FF_CHUNK = 256
NSA_Q = 128
SEL_KEYS = 512
VMEM_LIMIT = 56 * 1024 * 1024

COL_QA = 0
COL_KC = COL_QA + NSA_WIDTH
COL_VC = COL_KC + KV_WIDTH
COL_KV4 = COL_VC + KV_WIDTH
COL_QB = COL_KV4 + 4 * KV_WIDTH
COL_KB = COL_QB + MOBA_WIDTH
COL_VB = COL_KB + MOBA_WIDTH
COL_GATES = COL_VB + MOBA_WIDTH
COL_GS = COL_GATES + 2 * D_MODEL
IN_COLS_PADDED = COL_GS + LANES


def _rms(x, g):
    return x * lax.rsqrt(jnp.mean(x * x, axis=-1, keepdims=True) + EPS) * g


def _dot(a, b):
    return jnp.dot(a, b, preferred_element_type=F32)


def _dot_nt(a, b):
    return lax.dot_general(a, b, (((1,), (1,)), ((), ())), preferred_element_type=F32)


def _swap_halves(z):
    lane = lax.broadcasted_iota(jnp.int32, z.shape, 1)
    lo = (lane & (HEAD_DIM - 1)) < HALF
    return jnp.where(lo, pltpu.roll(z, LANES - HALF, 1), pltpu.roll(z, HALF, 1))


def _rope_slab(z, cos, sin):
    parts = []
    for c in range(z.shape[1] // LANES):
        zc = z[:, c * LANES:(c + 1) * LANES]
        parts.append(zc * cos + _swap_halves(zc) * sin)
    return parts[0] if len(parts) == 1 else jnp.concatenate(parts, axis=1)


def _slabs(s, n):
    return [s[:, i * n:(i + 1) * n] for i in range(s.shape[1] // n)]


def _masked_softmax_keys(s, mask):
    out = []
    for x in _slabs(s, mask.shape[1]):
        x = jnp.where(mask, x, NEG)
        m = jnp.max(x, axis=0, keepdims=True)
        p = jnp.where(mask, jnp.exp(x - m), 0.0)
        out.append(p / jnp.maximum(jnp.sum(p, axis=0, keepdims=True), TINY))
    return jnp.concatenate(out, axis=1)


def _online_update(carry, s, mask, vt):
    m, l, acc = carry
    q = mask.shape[1]
    xs = [jnp.where(mask, x, NEG) for x in _slabs(s, q)]
    m_new = jnp.maximum(m, jnp.concatenate([jnp.max(x, axis=0, keepdims=True) for x in xs], axis=1))
    alpha = jnp.exp(m - m_new)
    p = jnp.concatenate([jnp.where(mask, jnp.exp(x - mn), 0.0) for x, mn in zip(xs, _slabs(m_new, q))], axis=1)
    l = alpha * l + jnp.sum(p, axis=0, keepdims=True)
    acc = alpha * acc + _dot(vt, p.astype(BF16))
    return m_new, l, acc


def _online_init(n):
    return jnp.full((1, n), NEG, F32), jnp.zeros((1, n), F32), jnp.zeros((HEAD_DIM, n), F32)


def _rank_select(v, n_candidates, k):
    row = lax.broadcasted_iota(jnp.int32, v.shape, 0)
    cnt = jnp.zeros(v.shape, F32)
    for j in range(n_candidates):
        r = v[j:j + 1, :]
        cnt = cnt + jnp.where(r > v, 1.0, jnp.where(r == v, jnp.where(row > j, 1.0, 0.0), 0.0))
    return cnt < float(k)


def _ffn_kernel(x_ref, g_ref, wg_ref, wu_ref, wd_ref, fin_ref, o_ref, t_ref, *, final_norm):
    x = x_ref[...]
    h = _rms(x, g_ref[...]).astype(BF16)
    for f in range(D_FF // FF_CHUNK):
        c = slice(f * FF_CHUNK, (f + 1) * FF_CHUNK)
        a = _dot(h, wg_ref[:, c])
        b = _dot(h, wu_ref[:, c])
        t_ref[:, c] = (a * jax.nn.sigmoid(a) * b).astype(BF16)
    y = x + 0.5 * _dot(t_ref[...], wd_ref[...])
    if final_norm:
        y = _rms(y, fin_ref[...])
    o_ref[...] = y


def _resident(shape):
    nd = len(shape)
    return pl.BlockSpec(shape, lambda *_: (0,) * nd, pipeline_mode=pl.Buffered(1))


def _ffn(x, g, wg, wu, wd, fin, final_norm):
    t = x.shape[0]
    row = pl.BlockSpec((TOKEN_TILE, D_MODEL), lambda i: (i, 0))
    return pl.pallas_call(
        functools.partial(_ffn_kernel, final_norm=final_norm),
        grid=(t // TOKEN_TILE,),
        in_specs=[row, _resident((1, D_MODEL)), _resident(wg.shape), _resident(wu.shape),
                  _resident(wd.shape), _resident((1, D_MODEL))],
        out_specs=row,
        out_shape=jax.ShapeDtypeStruct(x.shape, F32),
        scratch_shapes=[pltpu.VMEM((TOKEN_TILE, D_FF), BF16)],
        compiler_params=pltpu.CompilerParams(dimension_semantics=("parallel",),
                                             vmem_limit_bytes=VMEM_LIMIT),
        name="ffn",
    )(x, g.reshape(1, D_MODEL), wg.astype(BF16), wu.astype(BF16), wd.astype(BF16), fin.reshape(1, D_MODEL))


def _proj_kernel(x_ref, g_ref, w_ref, cos_ref, sin_ref,
                 qa_ref, kc_ref, vc_ref, kk_ref, vv_ref, qb_ref, kb_ref, vb_ref, gates_ref, gs_ref,
                 kmean_ref):
    h = _rms(x_ref[...], g_ref[...]).astype(BF16)
    cos = cos_ref[...]
    sin = sin_ref[...]

    def mm(c0, n):
        return _dot(h, w_ref[:, c0:c0 + n])

    qa_ref[...] = (_rope_slab(mm(COL_QA, NSA_WIDTH), cos, sin) * SCALE).astype(BF16)
    kcvc = mm(COL_KC, 2 * KV_WIDTH)
    kc_ref[...] = kcvc[:, :KV_WIDTH].astype(BF16)
    vc_ref[...] = kcvc[:, KV_WIDTH:].astype(BF16)
    kv4 = mm(COL_KV4, 4 * KV_WIDTH)
    kk_ref[...] = jnp.concatenate(
        [_rope_slab(kv4[:, 0:KV_WIDTH], cos, sin), _rope_slab(kv4[:, 2 * KV_WIDTH:3 * KV_WIDTH], cos, sin)],
        axis=1).astype(BF16)
    vv_ref[...] = jnp.concatenate([kv4[:, KV_WIDTH:2 * KV_WIDTH], kv4[:, 3 * KV_WIDTH:]], axis=1).astype(BF16)
    qb_ref[...] = (_rope_slab(mm(COL_QB, MOBA_WIDTH), cos, sin) * SCALE).astype(BF16)
    kb = _rope_slab(mm(COL_KB, MOBA_WIDTH), cos, sin)
    kb_ref[...] = kb.astype(BF16)
    means = [jnp.mean(kb[i * MOBA_BLOCK:(i + 1) * MOBA_BLOCK], axis=0, keepdims=True)
             for i in range(TOKEN_TILE // MOBA_BLOCK)]
    kmean_ref[0] = jnp.concatenate(means, axis=0)
    vb_ref[...] = mm(COL_VB, MOBA_WIDTH).astype(BF16)
    gates_ref[:, :D_MODEL] = jax.nn.sigmoid(mm(COL_GATES, D_MODEL))
    gates_ref[:, D_MODEL:] = jax.nn.sigmoid(mm(COL_GATES + D_MODEL, D_MODEL))
    gs_ref[...] = jax.nn.sigmoid(mm(COL_GS, LANES))


def _reorder_w_in(w_in):
    sizes = (NSA_WIDTH, N_GATE_COLS) + (KV_WIDTH,) * 6 + (MOBA_WIDTH,) * 3 + (D_MODEL, D_MODEL)
    offs = np.concatenate([[0], np.cumsum(sizes)])
    parts = [w_in[:, offs[i]:offs[i + 1]] for i in range(len(sizes))]
    q_a, g_a = parts[0], parts[1]
    rest = parts[2:]
    pad = jnp.zeros((D_MODEL, LANES - N_GATE_COLS), w_in.dtype)
    return jnp.concatenate([q_a] + rest + [g_a, pad], axis=1).astype(BF16)


def _proj(x, g, w_in, cos_t, sin_t, seq):
    t = x.shape[0]
    w = _reorder_w_in(w_in)
    tiles_per_seq = seq // TOKEN_TILE
    blocks_per_tile = TOKEN_TILE // MOBA_BLOCK

    def row(n):
        return pl.BlockSpec((TOKEN_TILE, n), lambda i: (i, 0))

    def act(n, dtype=BF16):
        return jax.ShapeDtypeStruct((t, n), dtype)

    tab = pl.BlockSpec((TOKEN_TILE, LANES), lambda i: (i % tiles_per_seq, 0))
    widths = (NSA_WIDTH, KV_WIDTH, KV_WIDTH, 2 * KV_WIDTH, 2 * KV_WIDTH, MOBA_WIDTH, MOBA_WIDTH, MOBA_WIDTH)
    out_shape = tuple(act(n) for n in widths) + (
        act(2 * D_MODEL, F32), act(LANES, F32),
        jax.ShapeDtypeStruct((t // TOKEN_TILE, blocks_per_tile, MOBA_WIDTH), F32))
    out_specs = tuple(row(n) for n in widths) + (
        row(2 * D_MODEL), row(LANES),
        pl.BlockSpec((1, blocks_per_tile, MOBA_WIDTH), lambda i: (i, 0, 0)))
    return pl.pallas_call(
        _proj_kernel,
        grid=(t // TOKEN_TILE,),
        in_specs=[row(D_MODEL), _resident((1, D_MODEL)), _resident(w.shape), tab, tab],
        out_specs=out_specs,
        out_shape=out_shape,
        compiler_params=pltpu.CompilerParams(dimension_semantics=("parallel",),
                                             vmem_limit_bytes=VMEM_LIMIT),
        name="proj",
    )(x, g.reshape(1, D_MODEL), w, cos_t, sin_t)


def _gelu_tanh(x):
    return 0.5 * x * (1.0 + jnp.tanh(np.sqrt(2.0 / np.pi).astype(np.float32) * (x + 0.044715 * (x * x * x))))


def _compress_kernel(kc_ref, vc_ref, kw1_ref, kw1f_ref, kpos_ref, kw2_ref,
                     vw1_ref, vw1f_ref, vpos_ref, vw2_ref, cos_ref, sin_ref, kcb_ref, vcb_ref):
    n_chunks = kc_ref.shape[1]

    def compress(x_ref, w1_ref, w1f_ref, pos_ref, w2_ref):
        u = _dot(x_ref[0], w1_ref[...])
        first = u[:, :2 * CMP_HIDDEN]
        second = pltpu.roll(u[:, 2 * CMP_HIDDEN:], n_chunks - 1, 0)
        pos = jnp.broadcast_to(pos_ref[...], (8, CMP_LEN * HEAD_DIM)).astype(BF16)
        bias = _dot(pos, w1f_ref[...])[0:1]
        pre = first + second + jnp.concatenate([bias, bias], axis=1)
        return _dot(_gelu_tanh(pre).astype(BF16), w2_ref[...])

    k = compress(kc_ref, kw1_ref, kw1f_ref, kpos_ref, kw2_ref)
    kcb_ref[0] = _rope_slab(k, cos_ref[...], sin_ref[...]).astype(BF16)
    vcb_ref[0] = compress(vc_ref, vw1_ref, vw1f_ref, vpos_ref, vw2_ref).astype(BF16)


def _compress_weights(pos, w1, w2):
    w1r = w1.reshape(2, CMP_STRIDE, HEAD_DIM, CMP_HIDDEN)
    eye = jnp.eye(NSA_GROUPS, dtype=w1.dtype)
    big = jnp.einsum('hpdj,ge->pgdhej', w1r, eye)
    big = big.reshape(CMP_STRIDE * KV_WIDTH, 2 * NSA_GROUPS * CMP_HIDDEN).astype(BF16)
    w2big = jnp.einsum('jd,ge->gjed', w2, eye).reshape(NSA_GROUPS * CMP_HIDDEN, KV_WIDTH).astype(BF16)
    return big, w1.astype(BF16), pos.reshape(1, CMP_LEN * HEAD_DIM), w2big


def _compress(kc, vc, kparams, vparams, cos_c, sin_c):
    b, n_chunks, width = kc.shape
    args = (kc, vc) + _compress_weights(*kparams) + _compress_weights(*vparams) + (cos_c, sin_c)
    x_spec = pl.BlockSpec((1, n_chunks, width), lambda i: (i, 0, 0))
    in_specs = [x_spec, x_spec] + [_resident(a.shape) for a in args[2:]]
    o_spec = pl.BlockSpec((1, n_chunks, KV_WIDTH), lambda i: (i, 0, 0))
    o_shape = jax.ShapeDtypeStruct((b, n_chunks, KV_WIDTH), BF16)
    return pl.pallas_call(
        _compress_kernel,
        grid=(b,),
        in_specs=in_specs,
        out_specs=(o_spec, o_spec),
        out_shape=(o_shape, o_shape),
        compiler_params=pltpu.CompilerParams(dimension_semantics=("parallel",),
                                             vmem_limit_bytes=VMEM_LIMIT),
        name="compress",
    )(*args)


def _nsa_kernel(qa_ref, gst_ref, kcb_ref, vcbt_ref, kk_ref, vst_ref, vwt_ref, ovt_ref, o_ref, *, seq):
    qi = pl.program_id(1)
    t0 = qi * NSA_Q
    cols = NSA_REP * NSA_Q
    n_cmp = seq // CMP_STRIDE - 1
    n_sel = seq // SEL_BLOCK
    gst = gst_ref[0]
    ovt = ovt_ref[...]

    lane = lax.broadcasted_iota(jnp.int32, (NSA_Q, LANES), 1)

    def key_and_query_pos(n, k0):
        shape = (n, NSA_Q)
        return k0 + lax.broadcasted_iota(jnp.int32, shape, 0), t0 + lax.broadcasted_iota(jnp.int32, shape, 1)

    groups = range(NSA_GROUPS)
    d_rows = [slice(g * HEAD_DIM, (g + 1) * HEAD_DIM) for g in groups]
    g_cols = [slice(g * cols, (g + 1) * cols) for g in groups]

    heads = []
    for n in range(NSA_HEADS):
        g = n // NSA_REP
        x = qa_ref[0, :, (n // 2) * LANES:(n // 2 + 1) * LANES].astype(F32)
        if (n % 2) != g:
            x = pltpu.roll(x, HEAD_DIM, 1)
        heads.append(jnp.where((lane >= g * HEAD_DIM) & (lane < (g + 1) * HEAD_DIM), x, 0.0).astype(BF16))
    q = jnp.concatenate(heads, axis=0)

    s = _dot_nt(kcb_ref[0], q)
    n_idx, tq = key_and_query_pos(s.shape[0], 0)
    cmask = (n_idx * CMP_STRIDE + (CMP_LEN - 1) <= tq) & (n_idx < n_cmp)
    p_cmp = _masked_softmax_keys(s, cmask).astype(BF16)
    o_cmp = [_dot(vcbt_ref[0, d_rows[g], :], p_cmp[:, g_cols[g]]) for g in groups]

    pi = _dot(ovt, p_cmp)
    jj, tsel = key_and_query_pos(n_sel, 0)
    tblk = tsel >> 6
    bonus = jnp.where((jj == 0) | (jj == tblk) | (jj == tblk - 1), FORCE_BONUS, 0.0)
    sels = []
    for g in groups:
        imp = pi[:, g * cols:g * cols + NSA_Q]
        for r in range(1, NSA_REP):
            imp = imp + pi[:, g * cols + r * NSA_Q:g * cols + (r + 1) * NSA_Q]
        imp = jnp.where(jj <= tblk, imp + bonus, NEG)
        sels.append(jnp.where(_rank_select(imp, n_sel, SEL_TOPK), 1.0, 0.0).astype(BF16))
    sel = jnp.concatenate(sels, axis=1)

    def sel_step(c, carry):
        k0 = pl.multiple_of(c * SEL_KEYS, SEL_KEYS)
        s = _dot_nt(kk_ref[0, pl.ds(k0, SEL_KEYS), 0:KV_WIDTH], q)
        blk = (k0 + lax.broadcasted_iota(jnp.int32, (SEL_KEYS, n_sel), 0)) >> 6
        expand = jnp.where(blk == lax.broadcasted_iota(jnp.int32, (SEL_KEYS, n_sel), 1), 1.0, 0.0)
        chosen = _dot(expand.astype(BF16), sel)
        kpos, tq = key_and_query_pos(SEL_KEYS, k0)
        out = []
        for g in groups:
            mask = jnp.where(kpos <= tq, chosen[:, g * NSA_Q:(g + 1) * NSA_Q], 0.0) > 0.5
            out.append(_online_update(carry[g], s[:, g_cols[g]], mask,
                                      vst_ref[0, d_rows[g], pl.ds(k0, SEL_KEYS)]))
        return tuple(out)

    state = lax.fori_loop(0, qi // (SEL_KEYS // NSA_Q) + 1, sel_step,
                          tuple(_online_init(cols) for _ in groups))
    o_sel = [acc / jnp.maximum(l, TINY) for _, l, acc in state]

    n_win = WINDOW + NSA_Q
    w0 = pl.multiple_of(jnp.maximum(t0 - WINDOW, 0), NSA_Q)
    s = _dot_nt(kk_ref[0, pl.ds(w0, n_win), KV_WIDTH:2 * KV_WIDTH], q)
    kpos, tq = key_and_query_pos(n_win, w0)
    wmask = (kpos <= tq) & (tq - kpos < WINDOW)
    p_win = _masked_softmax_keys(s, wmask).astype(BF16)
    o_win = [_dot(vwt_ref[0, d_rows[g], pl.ds(w0, n_win)], p_win[:, g_cols[g]]) for g in groups]

    outs = []
    for n in range(NSA_HEADS):
        g, r = divmod(n, NSA_REP)
        sl = slice(r * NSA_Q, (r + 1) * NSA_Q)
        outs.append(gst[3 * n:3 * n + 1, :] * o_cmp[g][:, sl] + gst[3 * n + 1:3 * n + 2, :] * o_sel[g][:, sl]
                    + gst[3 * n + 2:3 * n + 3, :] * o_win[g][:, sl])
    o_ref[0] = jnp.concatenate(outs, axis=0).astype(BF16)


def _nsa(qa, gst, kcb, vcbt, kk, vvt, ovt):
    b, seq, _ = qa.shape
    n_chunks = kcb.shape[1]
    return pl.pallas_call(
        functools.partial(_nsa_kernel, seq=seq),
        grid=(b, seq // NSA_Q),
        in_specs=[pl.BlockSpec((1, NSA_Q, NSA_WIDTH), lambda i, j: (i, j, 0)),
                  pl.BlockSpec((1, LANES, NSA_Q), lambda i, j: (i, 0, j)),
                  pl.BlockSpec((1, n_chunks, KV_WIDTH), lambda i, j: (i, 0, 0)),
                  pl.BlockSpec((1, KV_WIDTH, n_chunks), lambda i, j: (i, 0, 0)),
                  pl.BlockSpec((1, seq, 2 * KV_WIDTH), lambda i, j: (i, 0, 0)),
                  pl.BlockSpec((1, KV_WIDTH, seq), lambda i, j: (i, 0, 0)),
                  pl.BlockSpec((1, KV_WIDTH, seq), lambda i, j: (i, 1, 0)),
                  pl.BlockSpec(ovt.shape, lambda i, j: (0, 0))],
        out_specs=pl.BlockSpec((1, NSA_WIDTH, NSA_Q), lambda i, j: (i, 0, j)),
        out_shape=jax.ShapeDtypeStruct((b, NSA_WIDTH, seq), BF16),
        compiler_params=pltpu.CompilerParams(dimension_semantics=("parallel", "parallel"),
                                             vmem_limit_bytes=VMEM_LIMIT),
        name="nsa",
    )(qa, gst, kcb, vcbt, kk, vvt, vvt, ovt)


def _moba_kernel(q_ref, k_ref, vt_ref, kmean_ref, o_ref, acc_ref):
    c = pl.program_id(1)
    nb = kmean_ref.shape[1]
    k_own = pl.multiple_of(c * MOBA_BLOCK, MOBA_BLOCK)
    lane = lax.broadcasted_iota(jnp.int32, (MOBA_BLOCK, LANES), 1)
    jj = lax.broadcasted_iota(jnp.int32, (BF16_ROWS, MOBA_BLOCK), 0)
    past = jj < c
    pad = jnp.zeros((BF16_ROWS - nb, LANES), F32)

    def pair(h):
        return slice((h // 2) * LANES, (h // 2 + 1) * LANES)

    qs, gscs = [], []
    for h in range(MOBA_HEADS):
        mine = (lane >= (h % 2) * HEAD_DIM) & (lane < (h % 2 + 1) * HEAD_DIM)
        q = jnp.where(mine, q_ref[0, :, pair(h)], jnp.zeros((), BF16))
        km = jnp.concatenate([kmean_ref[0, :, pair(h)], pad], axis=0).astype(BF16)
        qs.append(q)
        gscs.append(_dot_nt(km, q))
    chosen = [jnp.where(_rank_select(jnp.where(past, g, NEG), nb, MOBA_TOPK) & past, 1.0, 0.0) for g in gscs]

    acc_ref[...] = jnp.zeros(acc_ref.shape, F32)

    def past_step(j, ml):
        k0 = pl.multiple_of(j * MOBA_BLOCK, MOBA_BLOCK)
        out = []

        ss = [_dot_nt(k_ref[0, pl.ds(k0, MOBA_BLOCK), pair(h)], qs[h]) for h in range(MOBA_HEADS)]
        for h in range(MOBA_HEADS):
            rows = slice(h * HEAD_DIM, (h + 1) * HEAD_DIM)
            s = ss[h]
            pick = jnp.sum(jnp.where(jj == j, chosen[h], 0.0), axis=0, keepdims=True) > 0.5
            m, l = ml[h]
            m_new = jnp.maximum(m, jnp.where(pick, jnp.max(s, axis=0, keepdims=True), NEG))
            alpha = jnp.exp(m - m_new)
            p = jnp.exp(s - jnp.where(pick, m_new, jnp.inf))
            out.append((m_new, alpha * l + jnp.sum(p, axis=0, keepdims=True)))
            acc_ref[rows, :] = alpha * acc_ref[rows, :] + _dot(
                vt_ref[0, rows, pl.ds(k0, MOBA_BLOCK)], p.astype(BF16))
        return tuple(out)

    init = tuple((jnp.full((1, MOBA_BLOCK), NEG, F32), jnp.zeros((1, MOBA_BLOCK), F32))
                 for _ in range(MOBA_HEADS))
    ml = lax.fori_loop(0, c, past_step, init)

    kq = (MOBA_BLOCK, MOBA_BLOCK)
    causal = lax.broadcasted_iota(jnp.int32, kq, 0) <= lax.broadcasted_iota(jnp.int32, kq, 1)
    ss = [_dot_nt(k_ref[0, pl.ds(k_own, MOBA_BLOCK), pair(h)], qs[h]) for h in range(MOBA_HEADS)]
    for h in range(MOBA_HEADS):
        rows = slice(h * HEAD_DIM, (h + 1) * HEAD_DIM)
        carry = (ml[h][0], ml[h][1], acc_ref[rows, :])
        _, l, acc = _online_update(carry, ss[h], causal, vt_ref[0, rows, pl.ds(k_own, MOBA_BLOCK)])
        o_ref[0, rows, :] = (acc / jnp.maximum(l, TINY)).astype(BF16)


def _moba(qb, kb, vbt, kmean):
    b, seq, _ = qb.shape
    nb = seq // MOBA_BLOCK
    return pl.pallas_call(
        _moba_kernel,
        grid=(b, nb),
        in_specs=[pl.BlockSpec((1, MOBA_BLOCK, MOBA_WIDTH), lambda i, j: (i, j, 0)),
                  pl.BlockSpec((1, seq, MOBA_WIDTH), lambda i, j: (i, 0, 0)),
                  pl.BlockSpec((1, MOBA_WIDTH, seq), lambda i, j: (i, 0, 0)),
                  pl.BlockSpec((1, nb, MOBA_WIDTH), lambda i, j: (i, 0, 0))],
        out_specs=pl.BlockSpec((1, MOBA_WIDTH, MOBA_BLOCK), lambda i, j: (i, 0, j)),
        out_shape=jax.ShapeDtypeStruct((b, MOBA_WIDTH, seq), BF16),
        scratch_shapes=[pltpu.VMEM((MOBA_WIDTH, MOBA_BLOCK), F32)],
        compiler_params=pltpu.CompilerParams(dimension_semantics=("parallel", "parallel"),
                                             vmem_limit_bytes=VMEM_LIMIT),
        name="moba",
    )(qb, kb, vbt, kmean)


def _merge_kernel(x_ref, a_ref, b_ref, gates_ref, pa_ref, pb_ref, wo_ref, o_ref):
    ya = _dot(a_ref[...], pa_ref[...])
    yb = _dot(b_ref[...], pb_ref[...])
    merged = gates_ref[:, :D_MODEL] * ya + gates_ref[:, D_MODEL:] * yb
    o_ref[...] = x_ref[...] + _dot(merged.astype(BF16), wo_ref[...])


def _merge(x, a, bm, gates, pa, pb, wo):
    t = x.shape[0]

    def row(n):
        return pl.BlockSpec((TOKEN_TILE, n), lambda i: (i, 0))

    pa, pb, wo = pa.astype(BF16), pb.astype(BF16), wo.astype(BF16)
    return pl.pallas_call(
        _merge_kernel,
        grid=(t // TOKEN_TILE,),
        in_specs=[row(D_MODEL), row(NSA_WIDTH), row(MOBA_WIDTH), row(2 * D_MODEL),
                  _resident(pa.shape), _resident(pb.shape), _resident(wo.shape)],
        out_specs=row(D_MODEL),
        out_shape=jax.ShapeDtypeStruct(x.shape, F32),
        compiler_params=pltpu.CompilerParams(dimension_semantics=("parallel",),
                                             vmem_limit_bytes=VMEM_LIMIT),
        name="merge",
    )(x, a, bm, gates, pa, pb, wo)


def _rope_tables(pos):
    inv = ROPE_THETA ** (-jnp.arange(0, HEAD_DIM, 2, dtype=F32) / HEAD_DIM)
    ang = pos.astype(F32)[:, None] * inv[None, :]
    cos, sin = jnp.cos(ang), jnp.sin(ang)
    reps = LANES // HEAD_DIM
    return (jnp.tile(jnp.concatenate([cos, cos], axis=1), (1, reps)),
            jnp.tile(jnp.concatenate([-sin, sin], axis=1), (1, reps)))


def _overlap_matrix_t(seq):
    n_chunks = seq // CMP_STRIDE
    ci = np.arange(n_chunks)[None, :] * CMP_STRIDE
    sj = np.arange(seq // SEL_BLOCK)[:, None] * SEL_BLOCK
    ov = (ci < sj + SEL_BLOCK) & (ci + CMP_LEN > sj) & (np.arange(n_chunks)[None, :] < n_chunks - 1)
    return jnp.asarray(ov, dtype=BF16)


def kernel(x, ffn1_norm, ffn1_wg, ffn1_wu, ffn1_wd, mix_norm, w_in, cmpk_pos, cmpk_w1, cmpk_w2, cmpv_pos, cmpv_w1, cmpv_w2, w_branch_nsa, w_branch_moba, w_out, ffn2_norm, ffn2_wg, ffn2_wu, ffn2_wd, final_norm):
    b, seq, d = x.shape
    depth = w_in.shape[0]
    assert d == D_MODEL and seq % TOKEN_TILE == 0 and seq // CMP_STRIDE == LANES
    t = b * seq
    n_chunks = seq // CMP_STRIDE
    cos_t, sin_t = _rope_tables(jnp.arange(seq))
    cos_c, sin_c = _rope_tables(jnp.arange(n_chunks) * CMP_STRIDE + (CMP_LEN - 1))
    ovt = _overlap_matrix_t(seq)

    def seq_major(a):
        return a.reshape(b, seq, a.shape[-1]).transpose(0, 2, 1)

    xf = x.reshape(t, d)
    for l in range(depth):
        xf = _ffn(xf, ffn1_norm[l], ffn1_wg[l], ffn1_wu[l], ffn1_wd[l], final_norm, False)
        qa, kc, vc, kk, vv, qb, kb, vb, gates, gs, kmean = _proj(xf, mix_norm[l], w_in[l], cos_t, sin_t, seq)
        kcb, vcb = _compress(kc.reshape(b, n_chunks, CMP_STRIDE * KV_WIDTH),
                             vc.reshape(b, n_chunks, CMP_STRIDE * KV_WIDTH),
                             (cmpk_pos[l], cmpk_w1[l], cmpk_w2[l]),
                             (cmpv_pos[l], cmpv_w1[l], cmpv_w2[l]), cos_c, sin_c)
        at = _nsa(qa.reshape(b, seq, NSA_WIDTH), seq_major(gs), kcb, vcb.transpose(0, 2, 1),
                  kk.reshape(b, seq, 2 * KV_WIDTH), seq_major(vv), ovt)
        bt = _moba(qb.reshape(b, seq, MOBA_WIDTH), kb.reshape(b, seq, MOBA_WIDTH), seq_major(vb),
                   kmean.reshape(b, seq // MOBA_BLOCK, MOBA_WIDTH))
        xf = _merge(xf, at.transpose(0, 2, 1).reshape(t, NSA_WIDTH), bt.transpose(0, 2, 1).reshape(t, MOBA_WIDTH),
                    gates, w_branch_nsa[l], w_branch_moba[l], w_out[l])
        xf = _ffn(xf, ffn2_norm[l], ffn2_wg[l], ffn2_wu[l], ffn2_wd[l], final_norm, l == depth - 1)
    return xf.reshape(b, seq, d)
```

```python
import functools

import jax
import jax.numpy as jnp
import numpy as np
from jax import lax
from jax.experimental import pallas as pl
from jax.experimental.pallas import tpu as pltpu

F32 = jnp.float32
BF16 = jnp.bfloat16

D_MODEL = 1024
HEAD_DIM = 64
HALF = HEAD_DIM // 2
NSA_HEADS = 8
NSA_GROUPS = 2
NSA_REP = NSA_HEADS // NSA_GROUPS
CMP_LEN = 32
CMP_STRIDE = 16
CMP_HIDDEN = 128
SEL_BLOCK = 64
SEL_TOPK = 16
WINDOW = 512
MOBA_HEADS = 8
MOBA_BLOCK = 256
MOBA_TOPK = 3
D_FF = 2816
ROPE_THETA = 10000.0
EPS = 1e-6
NEG = -1e30
TINY = 1e-30
FORCE_BONUS = 1e4
Q_SCALE = HEAD_DIM ** -0.5 * float(np.log2(np.e))

NSA_WIDTH = NSA_HEADS * HEAD_DIM
KV_WIDTH = NSA_GROUPS * HEAD_DIM
MOBA_WIDTH = MOBA_HEADS * HEAD_DIM
N_GATE_COLS = 3 * NSA_HEADS

LANES = 128
BF16_ROWS = 16
TOKEN_TILE = 512
FF_CHUNK = 256
NSA_Q = 128
SEL_KEYS = 512
VMEM_LIMIT = 56 * 1024 * 1024

COL_QA = 0
COL_KC = COL_QA + NSA_WIDTH
COL_VC = COL_KC + KV_WIDTH
COL_KV4 = COL_VC + KV_WIDTH
COL_QB = COL_KV4 + 4 * KV_WIDTH
COL_KB = COL_QB + MOBA_WIDTH
COL_VB = COL_KB + MOBA_WIDTH
COL_GATES = COL_VB + MOBA_WIDTH
COL_GS = COL_GATES + 2 * D_MODEL
IN_COLS_PADDED = COL_GS + LANES


def _rms(x, g):
    return x * lax.rsqrt(jnp.mean(x * x, axis=-1, keepdims=True) + EPS) * g


def _dot(a, b):
    return jnp.dot(a, b, preferred_element_type=F32)


def _dot_nt(a, b):
    return lax.dot_general(a, b, (((1,), (1,)), ((), ())), preferred_element_type=F32)


def _swap_halves(z):
    lane = lax.broadcasted_iota(jnp.int32, z.shape, 1)
    lo = (lane & (HEAD_DIM - 1)) < HALF
    return jnp.where(lo, pltpu.roll(z, LANES - HALF, 1), pltpu.roll(z, HALF, 1))


def _rope_slab(z, cos, sin):
    parts = []
    for c in range(z.shape[1] // LANES):
        zc = z[:, c * LANES:(c + 1) * LANES]
        parts.append(zc * cos + _swap_halves(zc) * sin)
    return parts[0] if len(parts) == 1 else jnp.concatenate(parts, axis=1)


def _slabs(s, n):
    return [s[:, i * n:(i + 1) * n] for i in range(s.shape[1] // n)]


def _masked_softmax_keys(s, mask):
    out = []
    for x in _slabs(s, mask.shape[1]):
        x = jnp.where(mask, x, NEG)
        m = jnp.max(x, axis=0, keepdims=True)
        p = jnp.where(mask, jnp.exp2(x - m), 0.0)
        out.append(p / jnp.maximum(jnp.sum(p, axis=0, keepdims=True), TINY))
    return jnp.concatenate(out, axis=1)


def _mask_keys(s, mask):
    return jnp.concatenate([jnp.where(mask, x, NEG) for x in _slabs(s, mask.shape[1])], axis=1)


def _online_update(carry, s, vt):
    m, l, acc = carry
    m_new = jnp.maximum(m, jnp.max(s, axis=0, keepdims=True))
    alpha = jnp.exp2(m - m_new)
    p = jnp.exp2(s - m_new)
    l = alpha * l + jnp.sum(p, axis=0, keepdims=True)
    acc = alpha * acc + _dot(vt, p.astype(BF16))
    return m_new, l, acc


def _online_init(n):
    return jnp.full((1, n), NEG, F32), jnp.zeros((1, n), F32), jnp.zeros((HEAD_DIM, n), F32)


def _rank_select(v, n_candidates, k):
    row = lax.broadcasted_iota(jnp.int32, v.shape, 0)
    cnt = jnp.zeros(v.shape, F32)
    for j in range(n_candidates):
        r = v[j:j + 1, :]
        cnt = cnt + jnp.where(r > v, 1.0, jnp.where(r == v, jnp.where(row > j, 1.0, 0.0), 0.0))
    return cnt < float(k)


def _ffn_kernel(x_ref, g_ref, wg_ref, wu_ref, wd_ref, fin_ref, o_ref, t_ref, *, final_norm):
    x = x_ref[...]
    h = _rms(x, g_ref[...]).astype(BF16)
    for f in range(D_FF // FF_CHUNK):
        c = slice(f * FF_CHUNK, (f + 1) * FF_CHUNK)
        a = _dot(h, wg_ref[:, c])
        b = _dot(h, wu_ref[:, c])
        t_ref[:, c] = (a * jax.nn.sigmoid(a) * b).astype(BF16)
    y = x + 0.5 * _dot(t_ref[...], wd_ref[...])
    if final_norm:
        y = _rms(y, fin_ref[...])
    o_ref[...] = y


def _resident(shape):
    nd = len(shape)
    return pl.BlockSpec(shape, lambda *_: (0,) * nd, pipeline_mode=pl.Buffered(1))


def _ffn(x, g, wg, wu, wd, fin, final_norm):
    t = x.shape[0]
    row = pl.BlockSpec((TOKEN_TILE, D_MODEL), lambda i: (i, 0))
    return pl.pallas_call(
        functools.partial(_ffn_kernel, final_norm=final_norm),
        grid=(t // TOKEN_TILE,),
        in_specs=[row, _resident((1, D_MODEL)), _resident(wg.shape), _resident(wu.shape),
                  _resident(wd.shape), _resident((1, D_MODEL))],
        out_specs=row,
        out_shape=jax.ShapeDtypeStruct(x.shape, F32),
        scratch_shapes=[pltpu.VMEM((TOKEN_TILE, D_FF), BF16)],
        compiler_params=pltpu.CompilerParams(dimension_semantics=("parallel",),
                                             vmem_limit_bytes=VMEM_LIMIT),
        name="ffn",
    )(x, g.reshape(1, D_MODEL), wg.astype(BF16), wu.astype(BF16), wd.astype(BF16), fin.reshape(1, D_MODEL))


def _proj_kernel(x_ref, g_ref, w_ref, cos_ref, sin_ref,
                 qa_ref, kc_ref, vc_ref, kk_ref, vv_ref, qb_ref, kb_ref, vb_ref, gates_ref, gs_ref,
                 kmean_ref):
    h = _rms(x_ref[...], g_ref[...]).astype(BF16)
    cos = cos_ref[...]
    sin = sin_ref[...]

    def mm(c0, n):
        return _dot(h, w_ref[:, c0:c0 + n])

    qa_ref[...] = (_rope_slab(mm(COL_QA, NSA_WIDTH), cos, sin) * Q_SCALE).astype(BF16)
    kcvc = mm(COL_KC, 2 * KV_WIDTH)
    kc_ref[...] = kcvc[:, :KV_WIDTH].astype(BF16)
    vc_ref[...] = kcvc[:, KV_WIDTH:].astype(BF16)
    kv4 = mm(COL_KV4, 4 * KV_WIDTH)
    kk_ref[...] = jnp.concatenate(
        [_rope_slab(kv4[:, 0:KV_WIDTH], cos, sin), _rope_slab(kv4[:, 2 * KV_WIDTH:3 * KV_WIDTH], cos, sin)],
        axis=1).astype(BF16)
    vv_ref[...] = jnp.concatenate([kv4[:, KV_WIDTH:2 * KV_WIDTH], kv4[:, 3 * KV_WIDTH:]], axis=1).astype(BF16)
    qb_ref[...] = (_rope_slab(mm(COL_QB, MOBA_WIDTH), cos, sin) * Q_SCALE).astype(BF16)
    kb = _rope_slab(mm(COL_KB, MOBA_WIDTH), cos, sin)
    kb_ref[...] = kb.astype(BF16)
    means = [jnp.mean(kb[i * MOBA_BLOCK:(i + 1) * MOBA_BLOCK], axis=0, keepdims=True)
             for i in range(TOKEN_TILE // MOBA_BLOCK)]
    kmean_ref[0] = jnp.concatenate(means, axis=0)
    vb_ref[...] = mm(COL_VB, MOBA_WIDTH).astype(BF16)
    gates_ref[:, :D_MODEL] = jax.nn.sigmoid(mm(COL_GATES, D_MODEL))
    gates_ref[:, D_MODEL:] = jax.nn.sigmoid(mm(COL_GATES + D_MODEL, D_MODEL))
    gs_ref[...] = jax.nn.sigmoid(mm(COL_GS, LANES))


def _reorder_w_in(w_in):
    sizes = (NSA_WIDTH, N_GATE_COLS) + (KV_WIDTH,) * 6 + (MOBA_WIDTH,) * 3 + (D_MODEL, D_MODEL)
    offs = np.concatenate([[0], np.cumsum(sizes)])
    parts = [w_in[:, offs[i]:offs[i + 1]] for i in range(len(sizes))]
    q_a, g_a = parts[0], parts[1]
    rest = parts[2:]
    pad = jnp.zeros((D_MODEL, LANES - N_GATE_COLS), w_in.dtype)
    return jnp.concatenate([q_a] + rest + [g_a, pad], axis=1).astype(BF16)


def _proj(x, g, w_in, cos_t, sin_t, seq):
    t = x.shape[0]
    w = _reorder_w_in(w_in)
    tiles_per_seq = seq // TOKEN_TILE
    blocks_per_tile = TOKEN_TILE // MOBA_BLOCK

    def row(n):
        return pl.BlockSpec((TOKEN_TILE, n), lambda i: (i, 0))

    def act(n, dtype=BF16):
        return jax.ShapeDtypeStruct((t, n), dtype)

    tab = pl.BlockSpec((TOKEN_TILE, LANES), lambda i: (i % tiles_per_seq, 0))
    widths = (NSA_WIDTH, KV_WIDTH, KV_WIDTH, 2 * KV_WIDTH, 2 * KV_WIDTH, MOBA_WIDTH, MOBA_WIDTH, MOBA_WIDTH)
    out_shape = tuple(act(n) for n in widths) + (
        act(2 * D_MODEL, F32), act(LANES, F32),
        jax.ShapeDtypeStruct((t // TOKEN_TILE, blocks_per_tile, MOBA_WIDTH), F32))
    out_specs = tuple(row(n) for n in widths) + (
        row(2 * D_MODEL), row(LANES),
        pl.BlockSpec((1, blocks_per_tile, MOBA_WIDTH), lambda i: (i, 0, 0)))
    return pl.pallas_call(
        _proj_kernel,
        grid=(t // TOKEN_TILE,),
        in_specs=[row(D_MODEL), _resident((1, D_MODEL)), _resident(w.shape), tab, tab],
        out_specs=out_specs,
        out_shape=out_shape,
        compiler_params=pltpu.CompilerParams(dimension_semantics=("parallel",),
                                             vmem_limit_bytes=VMEM_LIMIT),
        name="proj",
    )(x, g.reshape(1, D_MODEL), w, cos_t, sin_t)


def _gelu_tanh(x):
    return 0.5 * x * (1.0 + jnp.tanh(np.sqrt(2.0 / np.pi).astype(np.float32) * (x + 0.044715 * (x * x * x))))


def _compress_kernel(kc_ref, vc_ref, kw1_ref, kw1f_ref, kpos_ref, kw2_ref,
                     vw1_ref, vw1f_ref, vpos_ref, vw2_ref, cos_ref, sin_ref, kcb_ref, vcb_ref):
    n_chunks = kc_ref.shape[1]

    def compress(x_ref, w1_ref, w1f_ref, pos_ref, w2_ref):
        u = _dot(x_ref[0], w1_ref[...])
        first = u[:, :2 * CMP_HIDDEN]
        second = pltpu.roll(u[:, 2 * CMP_HIDDEN:], n_chunks - 1, 0)
        pos = jnp.broadcast_to(pos_ref[...], (8, CMP_LEN * HEAD_DIM)).astype(BF16)
        bias = _dot(pos, w1f_ref[...])[0:1]
        pre = first + second + jnp.concatenate([bias, bias], axis=1)
        return _dot(_gelu_tanh(pre).astype(BF16), w2_ref[...])

    k = compress(kc_ref, kw1_ref, kw1f_ref, kpos_ref, kw2_ref)
    kcb_ref[0] = _rope_slab(k, cos_ref[...], sin_ref[...]).astype(BF16)
    vcb_ref[0] = compress(vc_ref, vw1_ref, vw1f_ref, vpos_ref, vw2_ref).astype(BF16)


def _compress_weights(pos, w1, w2):
    w1r = w1.reshape(2, CMP_STRIDE, HEAD_DIM, CMP_HIDDEN)
    eye = jnp.eye(NSA_GROUPS, dtype=w1.dtype)
    big = jnp.einsum('hpdj,ge->pgdhej', w1r, eye)
    big = big.reshape(CMP_STRIDE * KV_WIDTH, 2 * NSA_GROUPS * CMP_HIDDEN).astype(BF16)
    w2big = jnp.einsum('jd,ge->gjed', w2, eye).reshape(NSA_GROUPS * CMP_HIDDEN, KV_WIDTH).astype(BF16)
    return big, w1.astype(BF16), pos.reshape(1, CMP_LEN * HEAD_DIM), w2big


def _compress(kc, vc, kparams, vparams, cos_c, sin_c):
    b, n_chunks, width = kc.shape
    args = (kc, vc) + _compress_weights(*kparams) + _compress_weights(*vparams) + (cos_c, sin_c)
    x_spec = pl.BlockSpec((1, n_chunks, width), lambda i: (i, 0, 0))
    in_specs = [x_spec, x_spec] + [_resident(a.shape) for a in args[2:]]
    o_spec = pl.BlockSpec((1, n_chunks, KV_WIDTH), lambda i: (i, 0, 0))
    o_shape = jax.ShapeDtypeStruct((b, n_chunks, KV_WIDTH), BF16)
    return pl.pallas_call(
        _compress_kernel,
        grid=(b,),
        in_specs=in_specs,
        out_specs=(o_spec, o_spec),
        out_shape=(o_shape, o_shape),
        compiler_params=pltpu.CompilerParams(dimension_semantics=("parallel",),
                                             vmem_limit_bytes=VMEM_LIMIT),
        name="compress",
    )(*args)


def _nsa_kernel(qa_ref, gst_ref, kcb_ref, vcbt_ref, kk_ref, vst_ref, vwt_ref, ovt_ref, blk_ref, o_ref, *, seq):
    qi = pl.program_id(1)
    t0 = qi * NSA_Q
    cols = NSA_REP * NSA_Q
    n_cmp = seq // CMP_STRIDE - 1
    n_sel = seq // SEL_BLOCK
    gst = gst_ref[0]
    ovt = ovt_ref[...]

    lane = lax.broadcasted_iota(jnp.int32, (NSA_Q, LANES), 1)

    def key_and_query_pos(n, k0):
        shape = (n, NSA_Q)
        return k0 + lax.broadcasted_iota(jnp.int32, shape, 0), t0 + lax.broadcasted_iota(jnp.int32, shape, 1)

    groups = range(NSA_GROUPS)
    d_rows = [slice(g * HEAD_DIM, (g + 1) * HEAD_DIM) for g in groups]
    g_cols = [slice(g * cols, (g + 1) * cols) for g in groups]

    heads = []
    for n in range(NSA_HEADS):
        g = n // NSA_REP
        x = qa_ref[0, :, (n // 2) * LANES:(n // 2 + 1) * LANES].astype(F32)
        if (n % 2) != g:
            x = pltpu.roll(x, HEAD_DIM, 1)
        heads.append(jnp.where((lane >= g * HEAD_DIM) & (lane < (g + 1) * HEAD_DIM), x, 0.0).astype(BF16))
    q = jnp.concatenate(heads, axis=0)

    s = _dot_nt(kcb_ref[0], q)
    n_idx, tq = key_and_query_pos(s.shape[0], 0)
    cmask = (n_idx * CMP_STRIDE + (CMP_LEN - 1) <= tq) & (n_idx < n_cmp)
    p_cmp = _masked_softmax_keys(s, cmask).astype(BF16)
    o_cmp = [_dot(vcbt_ref[0, d_rows[g], :], p_cmp[:, g_cols[g]]) for g in groups]

    pi = _dot(ovt, p_cmp)
    jj, tsel = key_and_query_pos(n_sel, 0)
    tblk = tsel >> 6
    bonus = jnp.where((jj == 0) | (jj == tblk) | (jj == tblk - 1), FORCE_BONUS, 0.0)
    pad = jnp.zeros((LANES - n_sel, NSA_Q), F32)
    bias_t = []
    for g in groups:
        imp = pi[:, g * cols:g * cols + NSA_Q]
        for r in range(1, NSA_REP):
            imp = imp + pi[:, g * cols + r * NSA_Q:g * cols + (r + 1) * NSA_Q]
        imp = jnp.where(jj <= tblk, imp + bonus, NEG)
        bias = jnp.where(_rank_select(imp, n_sel, SEL_TOPK) & (jj <= tblk), 0.0, NEG)
        bias_t.append(jnp.transpose(jnp.concatenate([bias, pad], axis=0)).astype(BF16))
    q_sel = jnp.concatenate(
        [jnp.concatenate([heads[n], bias_t[n // NSA_REP]], axis=1) for n in range(NSA_HEADS)], axis=0)

    def sel_tile(c, carry, diagonal):
        k0 = pl.multiple_of(c * SEL_KEYS, SEL_KEYS)
        keys = jnp.concatenate([kk_ref[0, pl.ds(k0, SEL_KEYS), 0:KV_WIDTH], blk_ref[pl.ds(k0, SEL_KEYS), :]],
                               axis=1)
        s = _dot_nt(keys, q_sel)
        if diagonal:
            kpos, tq = key_and_query_pos(SEL_KEYS, k0)
            s = _mask_keys(s, kpos <= tq)
        return tuple(_online_update(carry[g], s[:, g_cols[g]], vst_ref[0, d_rows[g], pl.ds(k0, SEL_KEYS)])
                     for g in groups)

    n_past = qi // (SEL_KEYS // NSA_Q)
    state = lax.fori_loop(0, n_past, functools.partial(sel_tile, diagonal=False),
                          tuple(_online_init(cols) for _ in groups))
    state = sel_tile(n_past, state, True)
    o_sel = [acc / jnp.maximum(l, TINY) for _, l, acc in state]

    n_win = WINDOW + NSA_Q
    w0 = pl.multiple_of(jnp.maximum(t0 - WINDOW, 0), NSA_Q)
    s = _dot_nt(kk_ref[0, pl.ds(w0, n_win), KV_WIDTH:2 * KV_WIDTH], q)
    kpos, tq = key_and_query_pos(n_win, w0)
    s = _mask_keys(s, (kpos <= tq) & (tq - kpos < WINDOW))
    p_win = jnp.exp2(s - jnp.max(s, axis=0, keepdims=True))
    l_win = jnp.maximum(jnp.sum(p_win, axis=0, keepdims=True), TINY)
    p_win = p_win.astype(BF16)
    o_win = [_dot(vwt_ref[0, d_rows[g], pl.ds(w0, n_win)], p_win[:, g_cols[g]]) / l_win[:, g_cols[g]]
             for g in groups]

    outs = []
    for n in range(NSA_HEADS):
        g, r = divmod(n, NSA_REP)
        sl = slice(r * NSA_Q, (r + 1) * NSA_Q)
        outs.append(gst[3 * n:3 * n + 1, :] * o_cmp[g][:, sl] + gst[3 * n + 1:3 * n + 2, :] * o_sel[g][:, sl]
                    + gst[3 * n + 2:3 * n + 3, :] * o_win[g][:, sl])
    o_ref[0] = jnp.concatenate(outs, axis=0).astype(BF16)


def _nsa(qa, gst, kcb, vcbt, kk, vvt, ovt, blk):
    b, seq, _ = qa.shape
    n_chunks = kcb.shape[1]
    return pl.pallas_call(
        functools.partial(_nsa_kernel, seq=seq),
        grid=(b, seq // NSA_Q),
        in_specs=[pl.BlockSpec((1, NSA_Q, NSA_WIDTH), lambda i, j: (i, j, 0)),
                  pl.BlockSpec((1, LANES, NSA_Q), lambda i, j: (i, 0, j)),
                  pl.BlockSpec((1, n_chunks, KV_WIDTH), lambda i, j: (i, 0, 0)),
                  pl.BlockSpec((1, KV_WIDTH, n_chunks), lambda i, j: (i, 0, 0)),
                  pl.BlockSpec((1, seq, 2 * KV_WIDTH), lambda i, j: (i, 0, 0)),
                  pl.BlockSpec((1, KV_WIDTH, seq), lambda i, j: (i, 0, 0)),
                  pl.BlockSpec((1, KV_WIDTH, seq), lambda i, j: (i, 1, 0)),
                  pl.BlockSpec(ovt.shape, lambda i, j: (0, 0)),
                  pl.BlockSpec(blk.shape, lambda i, j: (0, 0))],
        out_specs=pl.BlockSpec((1, NSA_WIDTH, NSA_Q), lambda i, j: (i, 0, j)),
        out_shape=jax.ShapeDtypeStruct((b, NSA_WIDTH, seq), BF16),
        compiler_params=pltpu.CompilerParams(dimension_semantics=("parallel", "parallel"),
                                             vmem_limit_bytes=VMEM_LIMIT),
        name="nsa",
    )(qa, gst, kcb, vcbt, kk, vvt, vvt, ovt, blk)


def _moba_kernel(q_ref, k_ref, vt_ref, kmean_ref, o_ref, acc_ref):
    c = pl.program_id(1)
    nb = kmean_ref.shape[1]
    k_own = pl.multiple_of(c * MOBA_BLOCK, MOBA_BLOCK)
    lane = lax.broadcasted_iota(jnp.int32, (MOBA_BLOCK, LANES), 1)
    jj = lax.broadcasted_iota(jnp.int32, (BF16_ROWS, MOBA_BLOCK), 0)
    past = jj < c
    pad = jnp.zeros((BF16_ROWS - nb, LANES), F32)

    def pair(h):
        return slice((h // 2) * LANES, (h // 2 + 1) * LANES)

    qs, gscs = [], []
    for h in range(MOBA_HEADS):
        mine = (lane >= (h % 2) * HEAD_DIM) & (lane < (h % 2 + 1) * HEAD_DIM)
        q = jnp.where(mine, q_ref[0, :, pair(h)], jnp.zeros((), BF16))
        km = jnp.concatenate([kmean_ref[0, :, pair(h)], pad], axis=0).astype(BF16)
        qs.append(q)
        gscs.append(_dot_nt(km, q))
    chosen = [jnp.where(_rank_select(jnp.where(past, g, NEG), nb, MOBA_TOPK) & past, 1.0, 0.0) for g in gscs]

    acc_ref[...] = jnp.zeros(acc_ref.shape, F32)

    def past_step(j, ml):
        k0 = pl.multiple_of(j * MOBA_BLOCK, MOBA_BLOCK)
        out = []

        ss = [_dot_nt(k_ref[0, pl.ds(k0, MOBA_BLOCK), pair(h)], qs[h]) for h in range(MOBA_HEADS)]
        for h in range(MOBA_HEADS):
            rows = slice(h * HEAD_DIM, (h + 1) * HEAD_DIM)
            s = ss[h]
            pick = jnp.sum(jnp.where(jj == j, chosen[h], 0.0), axis=0, keepdims=True) > 0.5
            m, l = ml[h]
            m_new = jnp.maximum(m, jnp.where(pick, jnp.max(s, axis=0, keepdims=True), NEG))
            alpha = jnp.exp2(m - m_new)
            p = jnp.exp2(s - jnp.where(pick, m_new, jnp.inf))
            out.append((m_new, alpha * l + jnp.sum(p, axis=0, keepdims=True)))
            acc_ref[rows, :] = alpha * acc_ref[rows, :] + _dot(
                vt_ref[0, rows, pl.ds(k0, MOBA_BLOCK)], p.astype(BF16))
        return tuple(out)

    init = tuple((jnp.full((1, MOBA_BLOCK), NEG, F32), jnp.zeros((1, MOBA_BLOCK), F32))
                 for _ in range(MOBA_HEADS))
    ml = lax.fori_loop(0, c, past_step, init)

    kq = (MOBA_BLOCK, MOBA_BLOCK)
    causal = lax.broadcasted_iota(jnp.int32, kq, 0) <= lax.broadcasted_iota(jnp.int32, kq, 1)
    ss = [_dot_nt(k_ref[0, pl.ds(k_own, MOBA_BLOCK), pair(h)], qs[h]) for h in range(MOBA_HEADS)]
    for h in range(MOBA_HEADS):
        rows = slice(h * HEAD_DIM, (h + 1) * HEAD_DIM)
        carry = (ml[h][0], ml[h][1], acc_ref[rows, :])
        _, l, acc = _online_update(carry, jnp.where(causal, ss[h], NEG), vt_ref[0, rows, pl.ds(k_own, MOBA_BLOCK)])
        o_ref[0, rows, :] = (acc / jnp.maximum(l, TINY)).astype(BF16)


def _moba(qb, kb, vbt, kmean):
    b, seq, _ = qb.shape
    nb = seq // MOBA_BLOCK
    return pl.pallas_call(
        _moba_kernel,
        grid=(b, nb),
        in_specs=[pl.BlockSpec((1, MOBA_BLOCK, MOBA_WIDTH), lambda i, j: (i, j, 0)),
                  pl.BlockSpec((1, seq, MOBA_WIDTH), lambda i, j: (i, 0, 0)),
                  pl.BlockSpec((1, MOBA_WIDTH, seq), lambda i, j: (i, 0, 0)),
                  pl.BlockSpec((1, nb, MOBA_WIDTH), lambda i, j: (i, 0, 0))],
        out_specs=pl.BlockSpec((1, MOBA_WIDTH, MOBA_BLOCK), lambda i, j: (i, 0, j)),
        out_shape=jax.ShapeDtypeStruct((b, MOBA_WIDTH, seq), BF16),
        scratch_shapes=[pltpu.VMEM((MOBA_WIDTH, MOBA_BLOCK), F32)],
        compiler_params=pltpu.CompilerParams(dimension_semantics=("parallel", "parallel"),
                                             vmem_limit_bytes=VMEM_LIMIT),
        name="moba",
    )(qb, kb, vbt, kmean)


def _merge_kernel(x_ref, a_ref, b_ref, gates_ref, pa_ref, pb_ref, wo_ref, o_ref):
    ya = _dot(a_ref[...], pa_ref[...])
    yb = _dot(b_ref[...], pb_ref[...])
    merged = gates_ref[:, :D_MODEL] * ya + gates_ref[:, D_MODEL:] * yb
    o_ref[...] = x_ref[...] + _dot(merged.astype(BF16), wo_ref[...])


def _merge(x, a, bm, gates, pa, pb, wo):
    t = x.shape[0]

    def row(n):
        return pl.BlockSpec((TOKEN_TILE, n), lambda i: (i, 0))

    pa, pb, wo = pa.astype(BF16), pb.astype(BF16), wo.astype(BF16)
    return pl.pallas_call(
        _merge_kernel,
        grid=(t // TOKEN_TILE,),
        in_specs=[row(D_MODEL), row(NSA_WIDTH), row(MOBA_WIDTH), row(2 * D_MODEL),
                  _resident(pa.shape), _resident(pb.shape), _resident(wo.shape)],
        out_specs=row(D_MODEL),
        out_shape=jax.ShapeDtypeStruct(x.shape, F32),
        compiler_params=pltpu.CompilerParams(dimension_semantics=("parallel",),
                                             vmem_limit_bytes=VMEM_LIMIT),
        name="merge",
    )(x, a, bm, gates, pa, pb, wo)


def _rope_tables(pos):
    inv = ROPE_THETA ** (-jnp.arange(0, HEAD_DIM, 2, dtype=F32) / HEAD_DIM)
    ang = pos.astype(F32)[:, None] * inv[None, :]
    cos, sin = jnp.cos(ang), jnp.sin(ang)
    reps = LANES // HEAD_DIM
    return (jnp.tile(jnp.concatenate([cos, cos], axis=1), (1, reps)),
            jnp.tile(jnp.concatenate([-sin, sin], axis=1), (1, reps)))


def _overlap_matrix_t(seq):
    n_chunks = seq // CMP_STRIDE
    ci = np.arange(n_chunks)[None, :] * CMP_STRIDE
    sj = np.arange(seq // SEL_BLOCK)[:, None] * SEL_BLOCK
    ov = (ci < sj + SEL_BLOCK) & (ci + CMP_LEN > sj) & (np.arange(n_chunks)[None, :] < n_chunks - 1)
    return jnp.asarray(ov, dtype=BF16)


def _block_onehot(seq):
    return jnp.asarray(np.arange(seq)[:, None] // SEL_BLOCK == np.arange(LANES)[None, :], dtype=BF16)


def kernel(x, ffn1_norm, ffn1_wg, ffn1_wu, ffn1_wd, mix_norm, w_in, cmpk_pos, cmpk_w1, cmpk_w2, cmpv_pos, cmpv_w1, cmpv_w2, w_branch_nsa, w_branch_moba, w_out, ffn2_norm, ffn2_wg, ffn2_wu, ffn2_wd, final_norm):
    b, seq, d = x.shape
    depth = w_in.shape[0]
    assert d == D_MODEL and seq % TOKEN_TILE == 0 and seq // CMP_STRIDE == LANES
    t = b * seq
    n_chunks = seq // CMP_STRIDE
    cos_t, sin_t = _rope_tables(jnp.arange(seq))
    cos_c, sin_c = _rope_tables(jnp.arange(n_chunks) * CMP_STRIDE + (CMP_LEN - 1))
    ovt = _overlap_matrix_t(seq)
    blk = _block_onehot(seq)

    def seq_major(a):
        return a.reshape(b, seq, a.shape[-1]).transpose(0, 2, 1)

    xf = x.reshape(t, d)
    for l in range(depth):
        xf = _ffn(xf, ffn1_norm[l], ffn1_wg[l], ffn1_wu[l], ffn1_wd[l], final_norm, False)
        qa, kc, vc, kk, vv, qb, kb, vb, gates, gs, kmean = _proj(xf, mix_norm[l], w_in[l], cos_t, sin_t, seq)
        kcb, vcb = _compress(kc.reshape(b, n_chunks, CMP_STRIDE * KV_WIDTH),
                             vc.reshape(b, n_chunks, CMP_STRIDE * KV_WIDTH),
                             (cmpk_pos[l], cmpk_w1[l], cmpk_w2[l]),
                             (cmpv_pos[l], cmpv_w1[l], cmpv_w2[l]), cos_c, sin_c)
        at = _nsa(qa.reshape(b, seq, NSA_WIDTH), seq_major(gs), kcb, vcb.transpose(0, 2, 1),
                  kk.reshape(b, seq, 2 * KV_WIDTH), seq_major(vv), ovt, blk)
        bt = _moba(qb.reshape(b, seq, MOBA_WIDTH), kb.reshape(b, seq, MOBA_WIDTH), seq_major(vb),
                   kmean.reshape(b, seq // MOBA_BLOCK, MOBA_WIDTH))
        xf = _merge(xf, at.transpose(0, 2, 1).reshape(t, NSA_WIDTH), bt.transpose(0, 2, 1).reshape(t, MOBA_WIDTH),
                    gates, w_branch_nsa[l], w_branch_moba[l], w_out[l])
        xf = _ffn(xf, ffn2_norm[l], ffn2_wg[l], ffn2_wu[l], ffn2_wd[l], final_norm, l == depth - 1)
    return xf.reshape(b, seq, d)
```

```python
import functools

import jax
import jax.numpy as jnp
import numpy as np
from jax import lax
from jax.experimental import pallas as pl
from jax.experimental.pallas import tpu as pltpu

F32 = jnp.float32
BF16 = jnp.bfloat16

D_MODEL = 1024
HEAD_DIM = 64
HALF = HEAD_DIM // 2
NSA_HEADS = 8
NSA_GROUPS = 2
NSA_REP = NSA_HEADS // NSA_GROUPS
CMP_LEN = 32
CMP_STRIDE = 16
CMP_HIDDEN = 128
SEL_BLOCK = 64
SEL_TOPK = 16
WINDOW = 512
MOBA_HEADS = 8
MOBA_BLOCK = 256
MOBA_TOPK = 3
D_FF = 2816
ROPE_THETA = 10000.0
EPS = 1e-6
NEG = -1e30
TINY = 1e-30
FORCE_BONUS = 1e4
Q_SCALE = HEAD_DIM ** -0.5 * float(np.log2(np.e))

NSA_WIDTH = NSA_HEADS * HEAD_DIM
KV_WIDTH = NSA_GROUPS * HEAD_DIM
MOBA_WIDTH = MOBA_HEADS * HEAD_DIM
N_GATE_COLS = 3 * NSA_HEADS

LANES = 128
BF16_ROWS = 16
TOKEN_TILE = 512
FF_CHUNK = 256
NSA_Q = 128
SEL_KEYS = 512
VMEM_LIMIT = 56 * 1024 * 1024

COL_QA = 0
COL_KC = COL_QA + NSA_WIDTH
COL_VC = COL_KC + KV_WIDTH
COL_KV4 = COL_VC + KV_WIDTH
COL_QB = COL_KV4 + 4 * KV_WIDTH
COL_KB = COL_QB + MOBA_WIDTH
COL_VB = COL_KB + MOBA_WIDTH
COL_GS = COL_VB + MOBA_WIDTH


def _rms(x, g):
    return x * lax.rsqrt(jnp.mean(x * x, axis=-1, keepdims=True) + EPS) * g


def _dot(a, b):
    return jnp.dot(a, b, preferred_element_type=F32)


def _dot_nt(a, b):
    return lax.dot_general(a, b, (((1,), (1,)), ((), ())), preferred_element_type=F32)


def _swap_halves(z):
    lane = lax.broadcasted_iota(jnp.int32, z.shape, 1)
    lo = (lane & (HEAD_DIM - 1)) < HALF
    return jnp.where(lo, pltpu.roll(z, LANES - HALF, 1), pltpu.roll(z, HALF, 1))


def _rope_slab(z, cos, sin):
    parts = []
    for c in range(z.shape[1] // LANES):
        zc = z[:, c * LANES:(c + 1) * LANES]
        parts.append(zc * cos + _swap_halves(zc) * sin)
    return parts[0] if len(parts) == 1 else jnp.concatenate(parts, axis=1)


def _slabs(s, n):
    return [s[:, i * n:(i + 1) * n] for i in range(s.shape[1] // n)]


def _masked_softmax_keys(s, mask):
    out = []
    for x in _slabs(s, mask.shape[1]):
        x = jnp.where(mask, x, NEG)
        m = jnp.max(x, axis=0, keepdims=True)
        p = jnp.where(mask, jnp.exp2(x - m), 0.0)
        out.append(p / jnp.maximum(jnp.sum(p, axis=0, keepdims=True), TINY))
    return jnp.concatenate(out, axis=1)


def _mask_keys(s, mask):
    return jnp.concatenate([jnp.where(mask, x, NEG) for x in _slabs(s, mask.shape[1])], axis=1)


def _online_update(carry, s, vt):
    m, l, acc = carry
    m_new = jnp.maximum(m, jnp.max(s, axis=0, keepdims=True))
    alpha = jnp.exp2(m - m_new)
    p = jnp.exp2(s - m_new)
    l = alpha * l + jnp.sum(p, axis=0, keepdims=True)
    acc = alpha * acc + _dot(vt, p.astype(BF16))
    return m_new, l, acc


def _online_init(n):
    return jnp.full((1, n), NEG, F32), jnp.zeros((1, n), F32), jnp.zeros((HEAD_DIM, n), F32)


def _rank_select(v, n_candidates, k):
    row = lax.broadcasted_iota(jnp.int32, v.shape, 0)
    cnt = jnp.zeros(v.shape, F32)
    for j in range(n_candidates):
        r = v[j:j + 1, :]
        cnt = cnt + jnp.where(r > v, 1.0, jnp.where(r == v, jnp.where(row > j, 1.0, 0.0), 0.0))
    return cnt < float(k)


def _ffn_kernel(x_ref, g_ref, wg_ref, wu_ref, wd_ref, fin_ref, o_ref, t_ref, *, final_norm):
    x = x_ref[...]
    h = _rms(x, g_ref[...]).astype(BF16)
    for f in range(D_FF // FF_CHUNK):
        c = slice(f * FF_CHUNK, (f + 1) * FF_CHUNK)
        a = _dot(h, wg_ref[:, c])
        b = _dot(h, wu_ref[:, c])
        t_ref[:, c] = (a * jax.nn.sigmoid(a) * b).astype(BF16)
    y = x + 0.5 * _dot(t_ref[...], wd_ref[...])
    if final_norm:
        y = _rms(y, fin_ref[...])
    o_ref[...] = y


def _resident(shape):
    nd = len(shape)
    return pl.BlockSpec(shape, lambda *_: (0,) * nd, pipeline_mode=pl.Buffered(1))


def _layer_resident(stacked, layer):
    return pl.BlockSpec((None,) + stacked.shape[1:], lambda *_: (layer, 0, 0), pipeline_mode=pl.Buffered(1))


def _ffn(x, layer, g, wg, wu, wd, fin, final_norm):
    t = x.shape[0]
    row = pl.BlockSpec((TOKEN_TILE, D_MODEL), lambda i: (i, 0))
    return pl.pallas_call(
        functools.partial(_ffn_kernel, final_norm=final_norm),
        grid=(t // TOKEN_TILE,),
        in_specs=[row, _layer_resident(g, layer), _layer_resident(wg, layer), _layer_resident(wu, layer),
                  _layer_resident(wd, layer), _resident((1, D_MODEL))],
        out_specs=row,
        out_shape=jax.ShapeDtypeStruct(x.shape, F32),
        scratch_shapes=[pltpu.VMEM((TOKEN_TILE, D_FF), BF16)],
        compiler_params=pltpu.CompilerParams(dimension_semantics=("parallel",),
                                             vmem_limit_bytes=VMEM_LIMIT),
        name="ffn",
    )(x, g, wg, wu, wd, fin)


def _proj_kernel(x_ref, g_ref, w_ref, cos_ref, sin_ref,
                 qa_ref, kc_ref, vc_ref, kk_ref, vv_ref, qb_ref, kb_ref, vb_ref, gs_ref, kmean_ref):
    h = _rms(x_ref[...], g_ref[...]).astype(BF16)
    cos = cos_ref[...]
    sin = sin_ref[...]

    def mm(c0, n):
        return _dot(h, w_ref[:, c0:c0 + n])

    qa_ref[...] = (_rope_slab(mm(COL_QA, NSA_WIDTH), cos, sin) * Q_SCALE).astype(BF16)
    kcvc = mm(COL_KC, 2 * KV_WIDTH)
    kc_ref[...] = kcvc[:, :KV_WIDTH].astype(BF16)
    vc_ref[...] = kcvc[:, KV_WIDTH:].astype(BF16)
    kv4 = mm(COL_KV4, 4 * KV_WIDTH)
    kk_ref[...] = jnp.concatenate(
        [_rope_slab(kv4[:, 0:KV_WIDTH], cos, sin), _rope_slab(kv4[:, 2 * KV_WIDTH:3 * KV_WIDTH], cos, sin)],
        axis=1).astype(BF16)
    vv_ref[...] = jnp.concatenate([kv4[:, KV_WIDTH:2 * KV_WIDTH], kv4[:, 3 * KV_WIDTH:]], axis=1).astype(BF16)
    qb_ref[...] = (_rope_slab(mm(COL_QB, MOBA_WIDTH), cos, sin) * Q_SCALE).astype(BF16)
    kb = _rope_slab(mm(COL_KB, MOBA_WIDTH), cos, sin)
    kb_ref[...] = kb.astype(BF16)
    means = [jnp.mean(kb[i * MOBA_BLOCK:(i + 1) * MOBA_BLOCK], axis=0, keepdims=True)
             for i in range(TOKEN_TILE // MOBA_BLOCK)]
    kmean_ref[0] = jnp.concatenate(means, axis=0)
    vb_ref[...] = mm(COL_VB, MOBA_WIDTH).astype(BF16)
    gs_ref[...] = jax.nn.sigmoid(mm(COL_GS, LANES))


def _split_w_in(w_in):
    off_ga = NSA_WIDTH
    off_rest = off_ga + N_GATE_COLS
    off_gates = off_rest + 6 * KV_WIDTH + 3 * MOBA_WIDTH
    w = w_in.astype(BF16)
    pad = jnp.zeros(w.shape[:2] + (LANES - N_GATE_COLS,), BF16)
    attn = jnp.concatenate([w[..., :off_ga], w[..., off_rest:off_gates], w[..., off_ga:off_rest], pad], axis=-1)
    return attn, w[..., off_gates:]


def _proj(x, layer, g, w, cos_t, sin_t, seq):
    t = x.shape[0]
    tiles_per_seq = seq // TOKEN_TILE
    blocks_per_tile = TOKEN_TILE // MOBA_BLOCK

    def row(n):
        return pl.BlockSpec((TOKEN_TILE, n), lambda i: (i, 0))

    def act(n, dtype=BF16):
        return jax.ShapeDtypeStruct((t, n), dtype)

    tab = pl.BlockSpec((TOKEN_TILE, LANES), lambda i: (i % tiles_per_seq, 0))
    widths = (NSA_WIDTH, KV_WIDTH, KV_WIDTH, 2 * KV_WIDTH, 2 * KV_WIDTH, MOBA_WIDTH, MOBA_WIDTH, MOBA_WIDTH)
    out_shape = tuple(act(n) for n in widths) + (
        act(LANES, F32), jax.ShapeDtypeStruct((t // TOKEN_TILE, blocks_per_tile, MOBA_WIDTH), F32))
    out_specs = tuple(row(n) for n in widths) + (
        row(LANES), pl.BlockSpec((1, blocks_per_tile, MOBA_WIDTH), lambda i: (i, 0, 0)))
    return pl.pallas_call(
        _proj_kernel,
        grid=(t // TOKEN_TILE,),
        in_specs=[row(D_MODEL), _layer_resident(g, layer), _layer_resident(w, layer), tab, tab],
        out_specs=out_specs,
        out_shape=out_shape,
        compiler_params=pltpu.CompilerParams(dimension_semantics=("parallel",),
                                             vmem_limit_bytes=VMEM_LIMIT),
        name="proj",
    )(x, g, w, cos_t, sin_t)


def _gelu_tanh(x):
    return 0.5 * x * (1.0 + jnp.tanh(np.sqrt(2.0 / np.pi).astype(np.float32) * (x + 0.044715 * (x * x * x))))


def _compress_kernel(kc_ref, vc_ref, kw1_ref, kw1f_ref, kpos_ref, kw2_ref,
                     vw1_ref, vw1f_ref, vpos_ref, vw2_ref, cos_ref, sin_ref, kcb_ref, vcb_ref):
    n_chunks = kc_ref.shape[1]

    def compress(x_ref, w1_ref, w1f_ref, pos_ref, w2_ref):
        u = _dot(x_ref[0], w1_ref[...])
        first = u[:, :2 * CMP_HIDDEN]
        second = pltpu.roll(u[:, 2 * CMP_HIDDEN:], n_chunks - 1, 0)
        pos = jnp.broadcast_to(pos_ref[...], (8, CMP_LEN * HEAD_DIM)).astype(BF16)
        bias = _dot(pos, w1f_ref[...])[0:1]
        pre = first + second + jnp.concatenate([bias, bias], axis=1)
        return _dot(_gelu_tanh(pre).astype(BF16), w2_ref[...])

    k = compress(kc_ref, kw1_ref, kw1f_ref, kpos_ref, kw2_ref)
    kcb_ref[0] = _rope_slab(k, cos_ref[...], sin_ref[...]).astype(BF16)
    vcb_ref[0] = compress(vc_ref, vw1_ref, vw1f_ref, vpos_ref, vw2_ref).astype(BF16)


def _compress_weights(pos, w1, w2):
    w1r = w1.reshape(2, CMP_STRIDE, HEAD_DIM, CMP_HIDDEN)
    eye = jnp.eye(NSA_GROUPS, dtype=w1.dtype)
    big = jnp.einsum('hpdj,ge->pgdhej', w1r, eye)
    big = big.reshape(CMP_STRIDE * KV_WIDTH, 2 * NSA_GROUPS * CMP_HIDDEN).astype(BF16)
    w2big = jnp.einsum('jd,ge->gjed', w2, eye).reshape(NSA_GROUPS * CMP_HIDDEN, KV_WIDTH).astype(BF16)
    return big, w1.astype(BF16), pos.reshape(1, CMP_LEN * HEAD_DIM), w2big


def _compress(kc, vc, kparams, vparams, cos_c, sin_c):
    b, n_chunks, width = kc.shape
    args = (kc, vc) + _compress_weights(*kparams) + _compress_weights(*vparams) + (cos_c, sin_c)
    x_spec = pl.BlockSpec((1, n_chunks, width), lambda i: (i, 0, 0))
    in_specs = [x_spec, x_spec] + [_resident(a.shape) for a in args[2:]]
    o_spec = pl.BlockSpec((1, n_chunks, KV_WIDTH), lambda i: (i, 0, 0))
    o_shape = jax.ShapeDtypeStruct((b, n_chunks, KV_WIDTH), BF16)
    return pl.pallas_call(
        _compress_kernel,
        grid=(b,),
        in_specs=in_specs,
        out_specs=(o_spec, o_spec),
        out_shape=(o_shape, o_shape),
        compiler_params=pltpu.CompilerParams(dimension_semantics=("parallel",),
                                             vmem_limit_bytes=VMEM_LIMIT),
        name="compress",
    )(*args)


def _nsa_kernel(qa_ref, gst_ref, kcb_ref, vcbt_ref, kk_ref, vst_ref, vwt_ref, ovt_ref, blk_ref, o_ref,
                s_ref, m_ref, l_ref, acc_ref, *, seq):
    qi = pl.program_id(1)
    t0 = qi * NSA_Q
    cols = NSA_REP * NSA_Q
    n_cmp = seq // CMP_STRIDE - 1
    n_sel = seq // SEL_BLOCK
    gst = gst_ref[0]
    ovt = ovt_ref[...]

    lane = lax.broadcasted_iota(jnp.int32, (NSA_Q, LANES), 1)

    def key_and_query_pos(n, k0):
        shape = (n, NSA_Q)
        return k0 + lax.broadcasted_iota(jnp.int32, shape, 0), t0 + lax.broadcasted_iota(jnp.int32, shape, 1)

    groups = range(NSA_GROUPS)
    d_rows = [slice(g * HEAD_DIM, (g + 1) * HEAD_DIM) for g in groups]
    g_cols = [slice(g * cols, (g + 1) * cols) for g in groups]

    heads = []
    for n in range(NSA_HEADS):
        g = n // NSA_REP
        x = qa_ref[0, :, (n // 2) * LANES:(n // 2 + 1) * LANES].astype(F32)
        if (n % 2) != g:
            x = pltpu.roll(x, HEAD_DIM, 1)
        heads.append(jnp.where((lane >= g * HEAD_DIM) & (lane < (g + 1) * HEAD_DIM), x, 0.0).astype(BF16))
    q = jnp.concatenate(heads, axis=0)

    n_win = WINDOW + NSA_Q
    w0 = pl.multiple_of(jnp.maximum(t0 - WINDOW, 0), NSA_Q)
    s_win = _dot_nt(kk_ref[0, pl.ds(w0, n_win), KV_WIDTH:2 * KV_WIDTH], q)
    s = _dot_nt(kcb_ref[0], q)

    n_idx, tq = key_and_query_pos(s.shape[0], 0)
    cmask = (n_idx * CMP_STRIDE + (CMP_LEN - 1) <= tq) & (n_idx < n_cmp)
    p_cmp = _masked_softmax_keys(s, cmask).astype(BF16)
    o_cmp = [_dot(vcbt_ref[0, d_rows[g], :], p_cmp[:, g_cols[g]]) for g in groups]

    pi = _dot(ovt, p_cmp)
    jj, tsel = key_and_query_pos(n_sel, 0)
    tblk = tsel >> 6
    bonus = jnp.where((jj == 0) | (jj == tblk) | (jj == tblk - 1), FORCE_BONUS, 0.0)
    pad = jnp.zeros((LANES - n_sel, NSA_Q), F32)
    bias_t = []
    for g in groups:
        imp = pi[:, g * cols:g * cols + NSA_Q]
        for r in range(1, NSA_REP):
            imp = imp + pi[:, g * cols + r * NSA_Q:g * cols + (r + 1) * NSA_Q]
        imp = jnp.where(jj <= tblk, imp + bonus, NEG)
        bias = jnp.where(_rank_select(imp, n_sel, SEL_TOPK) & (jj <= tblk), 0.0, NEG)
        bias_t.append(jnp.transpose(jnp.concatenate([bias, pad], axis=0)).astype(BF16))
    q_sel = jnp.concatenate(
        [jnp.concatenate([heads[n], bias_t[n // NSA_REP]], axis=1) for n in range(NSA_HEADS)], axis=0)

    def tile_start(c):
        return pl.multiple_of(c * SEL_KEYS, SEL_KEYS)

    def sel_scores(c):
        k0 = tile_start(c)
        keys = jnp.concatenate([kk_ref[0, pl.ds(k0, SEL_KEYS), 0:KV_WIDTH], blk_ref[pl.ds(k0, SEL_KEYS), :]],
                               axis=1)
        return _dot_nt(keys, q_sel)

    def sel_update(c, s):
        for g in groups:
            m, l, acc = _online_update((m_ref[g], l_ref[g], acc_ref[g]), s[:, g_cols[g]],
                                       vst_ref[0, d_rows[g], pl.ds(tile_start(c), SEL_KEYS)])
            m_ref[g], l_ref[g], acc_ref[g] = m, l, acc

    n_past = qi // (SEL_KEYS // NSA_Q)
    m_ref[...] = jnp.full(m_ref.shape, NEG, F32)
    l_ref[...] = jnp.zeros(l_ref.shape, F32)
    acc_ref[...] = jnp.zeros(acc_ref.shape, F32)
    s_ref[0] = sel_scores(0)

    kpos, tq = key_and_query_pos(n_win, w0)
    s_win = _mask_keys(s_win, (kpos <= tq) & (tq - kpos < WINDOW))
    p_win = jnp.exp2(s_win - jnp.max(s_win, axis=0, keepdims=True))
    l_win = jnp.maximum(jnp.sum(p_win, axis=0, keepdims=True), TINY)
    p_win = p_win.astype(BF16)
    o_win = [_dot(vwt_ref[0, d_rows[g], pl.ds(w0, n_win)], p_win[:, g_cols[g]]) / l_win[:, g_cols[g]]
             for g in groups]

    for c in range(seq // SEL_KEYS - 1):
        @pl.when(c < n_past)
        def _():
            s_ref[(c + 1) % 2] = sel_scores(c + 1)
            sel_update(c, s_ref[c % 2])

    kpos, tq = key_and_query_pos(SEL_KEYS, tile_start(n_past))
    sel_update(n_past, _mask_keys(s_ref[n_past % 2], kpos <= tq))
    o_sel = [acc_ref[g] / jnp.maximum(l_ref[g], TINY) for g in groups]

    outs = []
    for n in range(NSA_HEADS):
        g, r = divmod(n, NSA_REP)
        sl = slice(r * NSA_Q, (r + 1) * NSA_Q)
        outs.append(gst[3 * n:3 * n + 1, :] * o_cmp[g][:, sl] + gst[3 * n + 1:3 * n + 2, :] * o_sel[g][:, sl]
                    + gst[3 * n + 2:3 * n + 3, :] * o_win[g][:, sl])
    o_ref[0] = jnp.concatenate(outs, axis=0).astype(BF16)


def _nsa(qa, gst, kcb, vcbt, kk, vvt, ovt, blk):
    b, seq, _ = qa.shape
    n_chunks = kcb.shape[1]
    return pl.pallas_call(
        functools.partial(_nsa_kernel, seq=seq),
        grid=(b, seq // NSA_Q),
        in_specs=[pl.BlockSpec((1, NSA_Q, NSA_WIDTH), lambda i, j: (i, j, 0)),
                  pl.BlockSpec((1, LANES, NSA_Q), lambda i, j: (i, 0, j)),
                  pl.BlockSpec((1, n_chunks, KV_WIDTH), lambda i, j: (i, 0, 0)),
                  pl.BlockSpec((1, KV_WIDTH, n_chunks), lambda i, j: (i, 0, 0)),
                  pl.BlockSpec((1, seq, 2 * KV_WIDTH), lambda i, j: (i, 0, 0)),
                  pl.BlockSpec((1, KV_WIDTH, seq), lambda i, j: (i, 0, 0)),
                  pl.BlockSpec((1, KV_WIDTH, seq), lambda i, j: (i, 1, 0)),
                  pl.BlockSpec(ovt.shape, lambda i, j: (0, 0)),
                  pl.BlockSpec(blk.shape, lambda i, j: (0, 0))],
        out_specs=pl.BlockSpec((1, NSA_WIDTH, NSA_Q), lambda i, j: (i, 0, j)),
        out_shape=jax.ShapeDtypeStruct((b, NSA_WIDTH, seq), BF16),
        scratch_shapes=[pltpu.VMEM((2, SEL_KEYS, NSA_HEADS * NSA_Q), F32),
                        pltpu.VMEM((NSA_GROUPS, 1, NSA_REP * NSA_Q), F32),
                        pltpu.VMEM((NSA_GROUPS, 1, NSA_REP * NSA_Q), F32),
                        pltpu.VMEM((NSA_GROUPS, HEAD_DIM, NSA_REP * NSA_Q), F32)],
        compiler_params=pltpu.CompilerParams(dimension_semantics=("parallel", "parallel"),
                                             vmem_limit_bytes=VMEM_LIMIT),
        name="nsa",
    )(qa, gst, kcb, vcbt, kk, vvt, vvt, ovt, blk)


def _moba_kernel(q_ref, k_ref, vt_ref, kmean_ref, o_ref, acc_ref):
    c = pl.program_id(1)
    nb = kmean_ref.shape[1]
    k_own = pl.multiple_of(c * MOBA_BLOCK, MOBA_BLOCK)
    lane = lax.broadcasted_iota(jnp.int32, (MOBA_BLOCK, LANES), 1)
    jj = lax.broadcasted_iota(jnp.int32, (BF16_ROWS, MOBA_BLOCK), 0)
    past = jj < c
    pad = jnp.zeros((BF16_ROWS - nb, LANES), F32)

    def pair(h):
        return slice((h // 2) * LANES, (h // 2 + 1) * LANES)

    qs, gscs = [], []
    for h in range(MOBA_HEADS):
        mine = (lane >= (h % 2) * HEAD_DIM) & (lane < (h % 2 + 1) * HEAD_DIM)
        q = jnp.where(mine, q_ref[0, :, pair(h)], jnp.zeros((), BF16))
        km = jnp.concatenate([kmean_ref[0, :, pair(h)], pad], axis=0).astype(BF16)
        qs.append(q)
        gscs.append(_dot_nt(km, q))
    chosen = [jnp.where(_rank_select(jnp.where(past, g, NEG), nb, MOBA_TOPK) & past, 1.0, 0.0) for g in gscs]

    acc_ref[...] = jnp.zeros(acc_ref.shape, F32)

    def past_step(j, ml):
        k0 = pl.multiple_of(j * MOBA_BLOCK, MOBA_BLOCK)
        out = []

        ss = [_dot_nt(k_ref[0, pl.ds(k0, MOBA_BLOCK), pair(h)], qs[h]) for h in range(MOBA_HEADS)]
        for h in range(MOBA_HEADS):
            rows = slice(h * HEAD_DIM, (h + 1) * HEAD_DIM)
            s = ss[h]
            pick = jnp.sum(jnp.where(jj == j, chosen[h], 0.0), axis=0, keepdims=True) > 0.5
            m, l = ml[h]
            m_new = jnp.maximum(m, jnp.where(pick, jnp.max(s, axis=0, keepdims=True), NEG))
            alpha = jnp.exp2(m - m_new)
            p = jnp.exp2(s - jnp.where(pick, m_new, jnp.inf))
            out.append((m_new, alpha * l + jnp.sum(p, axis=0, keepdims=True)))
            acc_ref[rows, :] = alpha * acc_ref[rows, :] + _dot(
                vt_ref[0, rows, pl.ds(k0, MOBA_BLOCK)], p.astype(BF16))
        return tuple(out)

    init = tuple((jnp.full((1, MOBA_BLOCK), NEG, F32), jnp.zeros((1, MOBA_BLOCK), F32))
                 for _ in range(MOBA_HEADS))
    ml = lax.fori_loop(0, c, past_step, init)

    kq = (MOBA_BLOCK, MOBA_BLOCK)
    causal = lax.broadcasted_iota(jnp.int32, kq, 0) <= lax.broadcasted_iota(jnp.int32, kq, 1)
    ss = [_dot_nt(k_ref[0, pl.ds(k_own, MOBA_BLOCK), pair(h)], qs[h]) for h in range(MOBA_HEADS)]
    for h in range(MOBA_HEADS):
        rows = slice(h * HEAD_DIM, (h + 1) * HEAD_DIM)
        carry = (ml[h][0], ml[h][1], acc_ref[rows, :])
        _, l, acc = _online_update(carry, jnp.where(causal, ss[h], NEG), vt_ref[0, rows, pl.ds(k_own, MOBA_BLOCK)])
        o_ref[0, rows, :] = (acc / jnp.maximum(l, TINY)).astype(BF16)


def _moba(qb, kb, vbt, kmean):
    b, seq, _ = qb.shape
    nb = seq // MOBA_BLOCK
    return pl.pallas_call(
        _moba_kernel,
        grid=(b, nb),
        in_specs=[pl.BlockSpec((1, MOBA_BLOCK, MOBA_WIDTH), lambda i, j: (i, j, 0)),
                  pl.BlockSpec((1, seq, MOBA_WIDTH), lambda i, j: (i, 0, 0)),
                  pl.BlockSpec((1, MOBA_WIDTH, seq), lambda i, j: (i, 0, 0)),
                  pl.BlockSpec((1, nb, MOBA_WIDTH), lambda i, j: (i, 0, 0))],
        out_specs=pl.BlockSpec((1, MOBA_WIDTH, MOBA_BLOCK), lambda i, j: (i, 0, j)),
        out_shape=jax.ShapeDtypeStruct((b, MOBA_WIDTH, seq), BF16),
        scratch_shapes=[pltpu.VMEM((MOBA_WIDTH, MOBA_BLOCK), F32)],
        compiler_params=pltpu.CompilerParams(dimension_semantics=("parallel", "parallel"),
                                             vmem_limit_bytes=VMEM_LIMIT),
        name="moba",
    )(qb, kb, vbt, kmean)


def _merge_kernel(x_ref, g_ref, a_ref, b_ref, wgate_ref, pa_ref, pb_ref, wo_ref, o_ref):
    x = x_ref[...]
    h = _rms(x, g_ref[...]).astype(BF16)
    ya = _dot(a_ref[...], pa_ref[...])
    yb = _dot(b_ref[...], pb_ref[...])
    merged = (jax.nn.sigmoid(_dot(h, wgate_ref[:, :D_MODEL])) * ya
              + jax.nn.sigmoid(_dot(h, wgate_ref[:, D_MODEL:])) * yb)
    o_ref[...] = x + _dot(merged.astype(BF16), wo_ref[...])


def _merge(x, layer, g, a, bm, wgate, pa, pb, wo):
    t = x.shape[0]

    def row(n):
        return pl.BlockSpec((TOKEN_TILE, n), lambda i: (i, 0))

    return pl.pallas_call(
        _merge_kernel,
        grid=(t // TOKEN_TILE,),
        in_specs=[row(D_MODEL), _layer_resident(g, layer), row(NSA_WIDTH), row(MOBA_WIDTH)]
                 + [_layer_resident(w, layer) for w in (wgate, pa, pb, wo)],
        out_specs=row(D_MODEL),
        out_shape=jax.ShapeDtypeStruct(x.shape, F32),
        compiler_params=pltpu.CompilerParams(dimension_semantics=("parallel",),
                                             vmem_limit_bytes=VMEM_LIMIT),
        name="merge",
    )(x, g, a, bm, wgate, pa, pb, wo)


def _rope_tables(pos):
    inv = ROPE_THETA ** (-jnp.arange(0, HEAD_DIM, 2, dtype=F32) / HEAD_DIM)
    ang = pos.astype(F32)[:, None] * inv[None, :]
    cos, sin = jnp.cos(ang), jnp.sin(ang)
    reps = LANES // HEAD_DIM
    return (jnp.tile(jnp.concatenate([cos, cos], axis=1), (1, reps)),
            jnp.tile(jnp.concatenate([-sin, sin], axis=1), (1, reps)))


def _overlap_matrix_t(seq):
    n_chunks = seq // CMP_STRIDE
    ci = np.arange(n_chunks)[None, :] * CMP_STRIDE
    sj = np.arange(seq // SEL_BLOCK)[:, None] * SEL_BLOCK
    ov = (ci < sj + SEL_BLOCK) & (ci + CMP_LEN > sj) & (np.arange(n_chunks)[None, :] < n_chunks - 1)
    return jnp.asarray(ov, dtype=BF16)


def _block_onehot(seq):
    return jnp.asarray(np.arange(seq)[:, None] // SEL_BLOCK == np.arange(LANES)[None, :], dtype=BF16)


def kernel(x, ffn1_norm, ffn1_wg, ffn1_wu, ffn1_wd, mix_norm, w_in, cmpk_pos, cmpk_w1, cmpk_w2, cmpv_pos, cmpv_w1, cmpv_w2, w_branch_nsa, w_branch_moba, w_out, ffn2_norm, ffn2_wg, ffn2_wu, ffn2_wd, final_norm):
    b, seq, d = x.shape
    depth = w_in.shape[0]
    assert d == D_MODEL and seq % TOKEN_TILE == 0 and seq // CMP_STRIDE == LANES
    t = b * seq
    n_chunks = seq // CMP_STRIDE
    cos_t, sin_t = _rope_tables(jnp.arange(seq))
    cos_c, sin_c = _rope_tables(jnp.arange(n_chunks) * CMP_STRIDE + (CMP_LEN - 1))
    ovt = _overlap_matrix_t(seq)
    blk = _block_onehot(seq)

    def seq_major(a):
        return a.reshape(b, seq, a.shape[-1]).transpose(0, 2, 1)

    def gain(gs):
        return gs.reshape(depth, 1, d)

    ffn1 = (gain(ffn1_norm), ffn1_wg.astype(BF16), ffn1_wu.astype(BF16), ffn1_wd.astype(BF16))
    ffn2 = (gain(ffn2_norm), ffn2_wg.astype(BF16), ffn2_wu.astype(BF16), ffn2_wd.astype(BF16))
    w_attn, w_gate = _split_w_in(w_in)
    w_merge = (w_gate, w_branch_nsa.astype(BF16), w_branch_moba.astype(BF16), w_out.astype(BF16))
    fin = final_norm.reshape(1, d)

    xf = x.reshape(t, d)
    for l in range(depth):
        xf = _ffn(xf, l, *ffn1, fin, False)
        x_mix = xf
        qa, kc, vc, kk, vv, qb, kb, vb, gs, kmean = _proj(xf, l, gain(mix_norm), w_attn, cos_t, sin_t, seq)
        kcb, vcb = _compress(kc.reshape(b, n_chunks, CMP_STRIDE * KV_WIDTH),
                             vc.reshape(b, n_chunks, CMP_STRIDE * KV_WIDTH),
                             (cmpk_pos[l], cmpk_w1[l], cmpk_w2[l]),
                             (cmpv_pos[l], cmpv_w1[l], cmpv_w2[l]), cos_c, sin_c)
        at = _nsa(qa.reshape(b, seq, NSA_WIDTH), seq_major(gs), kcb, vcb.transpose(0, 2, 1),
                  kk.reshape(b, seq, 2 * KV_WIDTH), seq_major(vv), ovt, blk)
        bt = _moba(qb.reshape(b, seq, MOBA_WIDTH), kb.reshape(b, seq, MOBA_WIDTH), seq_major(vb),
                   kmean.reshape(b, seq // MOBA_BLOCK, MOBA_WIDTH))
        xf = _merge(x_mix, l, gain(mix_norm), at.transpose(0, 2, 1).reshape(t, NSA_WIDTH),
                    bt.transpose(0, 2, 1).reshape(t, MOBA_WIDTH), *w_merge)
        xf = _ffn(xf, l, *ffn2, fin, l == depth - 1)
    return xf.reshape(b, seq, d)
```

```python
import functools

import jax
import jax.numpy as jnp
import numpy as np
from jax import lax
from jax.experimental import pallas as pl
from jax.experimental.pallas import tpu as pltpu

F32 = jnp.float32
BF16 = jnp.bfloat16

D_MODEL = 1024
HEAD_DIM = 64
HALF = HEAD_DIM // 2
NSA_HEADS = 8
NSA_GROUPS = 2
NSA_REP = NSA_HEADS // NSA_GROUPS
CMP_LEN = 32
CMP_STRIDE = 16
CMP_HIDDEN = 128
SEL_BLOCK = 64
SEL_TOPK = 16
WINDOW = 512
MOBA_HEADS = 8
MOBA_BLOCK = 256
MOBA_TOPK = 3
D_FF = 2816
ROPE_THETA = 10000.0
EPS = 1e-6
NEG = -1e30
TINY = 1e-30
FORCE_BONUS = 1e4
Q_SCALE = HEAD_DIM ** -0.5 * float(np.log2(np.e))

NSA_WIDTH = NSA_HEADS * HEAD_DIM
KV_WIDTH = NSA_GROUPS * HEAD_DIM
MOBA_WIDTH = MOBA_HEADS * HEAD_DIM
N_GATE_COLS = 3 * NSA_HEADS

LANES = 128
BF16_ROWS = 16
TOKEN_TILE = 512
FF_CHUNK = 256
NSA_Q = 256
SEL_KEYS = 512
VMEM_LIMIT = 56 * 1024 * 1024

COL_QA = 0
COL_KC = COL_QA + NSA_WIDTH
COL_VC = COL_KC + KV_WIDTH
COL_KV4 = COL_VC + KV_WIDTH
COL_QB = COL_KV4 + 4 * KV_WIDTH
COL_KB = COL_QB + MOBA_WIDTH
COL_VB = COL_KB + MOBA_WIDTH
COL_GS = COL_VB + MOBA_WIDTH


def _rms(x, g):
    return x * lax.rsqrt(jnp.mean(x * x, axis=-1, keepdims=True) + EPS) * g


def _dot(a, b):
    return jnp.dot(a, b, preferred_element_type=F32)


def _dot_nt(a, b):
    return lax.dot_general(a, b, (((1,), (1,)), ((), ())), preferred_element_type=F32)


def _swap_halves(z):
    lane = lax.broadcasted_iota(jnp.int32, z.shape, 1)
    lo = (lane & (HEAD_DIM - 1)) < HALF
    return jnp.where(lo, pltpu.roll(z, LANES - HALF, 1), pltpu.roll(z, HALF, 1))


def _rope_slab(z, cos, sin):
    parts = []
    for c in range(z.shape[1] // LANES):
        zc = z[:, c * LANES:(c + 1) * LANES]
        parts.append(zc * cos + _swap_halves(zc) * sin)
    return parts[0] if len(parts) == 1 else jnp.concatenate(parts, axis=1)


def _slabs(s, n):
    return [s[:, i * n:(i + 1) * n] for i in range(s.shape[1] // n)]


def _masked_softmax_keys(s, mask):
    out = []
    for x in _slabs(s, mask.shape[1]):
        x = jnp.where(mask, x, NEG)
        m = jnp.max(x, axis=0, keepdims=True)
        p = jnp.where(mask, jnp.exp2(x - m), 0.0)
        out.append(p / jnp.maximum(jnp.sum(p, axis=0, keepdims=True), TINY))
    return jnp.concatenate(out, axis=1)


def _mask_keys(s, mask):
    return jnp.concatenate([jnp.where(mask, x, NEG) for x in _slabs(s, mask.shape[1])], axis=1)


def _online_update(carry, s, vt):
    m, l, acc = carry
    m_new = jnp.maximum(m, jnp.max(s, axis=0, keepdims=True))
    alpha = jnp.exp2(m - m_new)
    p = jnp.exp2(s - m_new)
    l = alpha * l + jnp.sum(p, axis=0, keepdims=True)
    acc = alpha * acc + _dot(vt, p.astype(BF16))
    return m_new, l, acc


def _online_init(n):
    return jnp.full((1, n), NEG, F32), jnp.zeros((1, n), F32), jnp.zeros((HEAD_DIM, n), F32)


def _rank_select(v, n_candidates, k):
    row = lax.broadcasted_iota(jnp.int32, v.shape, 0)
    cnt = jnp.zeros(v.shape, F32)
    for j in range(n_candidates):
        r = v[j:j + 1, :]
        cnt = cnt + jnp.where(r > v, 1.0, jnp.where(r == v, jnp.where(row > j, 1.0, 0.0), 0.0))
    return cnt < float(k)


def _ffn_kernel(x_ref, g_ref, wg_ref, wu_ref, wd_ref, fin_ref, o_ref, t_ref, *, final_norm):
    x = x_ref[...]
    h = _rms(x, g_ref[...]).astype(BF16)
    for f in range(D_FF // FF_CHUNK):
        c = slice(f * FF_CHUNK, (f + 1) * FF_CHUNK)
        a = _dot(h, wg_ref[:, c])
        b = _dot(h, wu_ref[:, c])
        t_ref[:, c] = (a * jax.nn.sigmoid(a) * b).astype(BF16)
    y = x + 0.5 * _dot(t_ref[...], wd_ref[...])
    if final_norm:
        y = _rms(y, fin_ref[...])
    o_ref[...] = y


def _resident(shape):
    nd = len(shape)
    return pl.BlockSpec(shape, lambda *_: (0,) * nd, pipeline_mode=pl.Buffered(1))


def _layer_resident(stacked, layer):
    return pl.BlockSpec((None,) + stacked.shape[1:], lambda *_: (layer, 0, 0), pipeline_mode=pl.Buffered(1))


def _ffn(x, layer, g, wg, wu, wd, fin, final_norm):
    t = x.shape[0]
    row = pl.BlockSpec((TOKEN_TILE, D_MODEL), lambda i: (i, 0))
    return pl.pallas_call(
        functools.partial(_ffn_kernel, final_norm=final_norm),
        grid=(t // TOKEN_TILE,),
        in_specs=[row, _layer_resident(g, layer), _layer_resident(wg, layer), _layer_resident(wu, layer),
                  _layer_resident(wd, layer), _resident((1, D_MODEL))],
        out_specs=row,
        out_shape=jax.ShapeDtypeStruct(x.shape, F32),
        scratch_shapes=[pltpu.VMEM((TOKEN_TILE, D_FF), BF16)],
        compiler_params=pltpu.CompilerParams(dimension_semantics=("parallel",),
                                             vmem_limit_bytes=VMEM_LIMIT),
        name="ffn",
    )(x, g, wg, wu, wd, fin)


def _proj_kernel(x_ref, g_ref, w_ref, cos_ref, sin_ref,
                 qat_ref, kc_ref, vc_ref, kk_ref, vvt_ref, qbt_ref, kb_ref, vbt_ref, gst_ref, kmean_ref):
    h = _rms(x_ref[...], g_ref[...]).astype(BF16)
    cos = cos_ref[...]
    sin = sin_ref[...]

    def mm(c0, n):
        return _dot(h, w_ref[:, c0:c0 + n])

    def feature_major(z, dtype=BF16):
        return jnp.transpose(z).astype(dtype)

    qat_ref[0] = feature_major(_rope_slab(mm(COL_QA, NSA_WIDTH), cos, sin) * Q_SCALE)
    kcvc = mm(COL_KC, 2 * KV_WIDTH)
    kc_ref[...] = kcvc[:, :KV_WIDTH].astype(BF16)
    vc_ref[...] = kcvc[:, KV_WIDTH:].astype(BF16)
    kv4 = mm(COL_KV4, 4 * KV_WIDTH)
    kk_ref[...] = jnp.concatenate(
        [_rope_slab(kv4[:, 0:KV_WIDTH], cos, sin), _rope_slab(kv4[:, 2 * KV_WIDTH:3 * KV_WIDTH], cos, sin)],
        axis=1).astype(BF16)
    vvt_ref[0] = feature_major(jnp.concatenate([kv4[:, KV_WIDTH:2 * KV_WIDTH], kv4[:, 3 * KV_WIDTH:]], axis=1))
    qbt_ref[0] = feature_major(_rope_slab(mm(COL_QB, MOBA_WIDTH), cos, sin) * Q_SCALE)
    kb = _rope_slab(mm(COL_KB, MOBA_WIDTH), cos, sin)
    kb_ref[...] = kb.astype(BF16)
    means = [jnp.mean(kb[i * MOBA_BLOCK:(i + 1) * MOBA_BLOCK], axis=0, keepdims=True)
             for i in range(TOKEN_TILE // MOBA_BLOCK)]
    kmean_ref[0] = jnp.concatenate(means, axis=0)
    vbt_ref[0] = feature_major(mm(COL_VB, MOBA_WIDTH))
    gst_ref[0] = feature_major(jax.nn.sigmoid(mm(COL_GS, LANES)), F32)


def _split_w_in(w_in):
    off_ga = NSA_WIDTH
    off_rest = off_ga + N_GATE_COLS
    off_gates = off_rest + 6 * KV_WIDTH + 3 * MOBA_WIDTH
    w = w_in.astype(BF16)
    pad = jnp.zeros(w.shape[:2] + (LANES - N_GATE_COLS,), BF16)
    attn = jnp.concatenate([w[..., :off_ga], w[..., off_rest:off_gates], w[..., off_ga:off_rest], pad], axis=-1)
    return attn, w[..., off_gates:]


def _proj(x, layer, g, w, cos_t, sin_t, seq):
    t = x.shape[0]
    tiles_per_seq = seq // TOKEN_TILE
    blocks_per_tile = TOKEN_TILE // MOBA_BLOCK

    def row(n):
        return pl.BlockSpec((TOKEN_TILE, n), lambda i: (i, 0))

    def token_major(n):
        return jax.ShapeDtypeStruct((t, n), BF16), row(n)

    def feature_major(n, dtype=BF16):
        return (jax.ShapeDtypeStruct((t // seq, n, seq), dtype),
                pl.BlockSpec((1, n, TOKEN_TILE), lambda i: (i // tiles_per_seq, 0, i % tiles_per_seq)))

    tab = pl.BlockSpec((TOKEN_TILE, LANES), lambda i: (i % tiles_per_seq, 0))
    outs = (feature_major(NSA_WIDTH), token_major(KV_WIDTH), token_major(KV_WIDTH), token_major(2 * KV_WIDTH),
            feature_major(2 * KV_WIDTH), feature_major(MOBA_WIDTH), token_major(MOBA_WIDTH),
            feature_major(MOBA_WIDTH), feature_major(LANES, F32),
            (jax.ShapeDtypeStruct((t // TOKEN_TILE, blocks_per_tile, MOBA_WIDTH), F32),
             pl.BlockSpec((1, blocks_per_tile, MOBA_WIDTH), lambda i: (i, 0, 0))))
    out_shape, out_specs = zip(*outs)
    return pl.pallas_call(
        _proj_kernel,
        grid=(t // TOKEN_TILE,),
        in_specs=[row(D_MODEL), _layer_resident(g, layer), _layer_resident(w, layer), tab, tab],
        out_specs=out_specs,
        out_shape=out_shape,
        compiler_params=pltpu.CompilerParams(dimension_semantics=("parallel",),
                                             vmem_limit_bytes=VMEM_LIMIT),
        name="proj",
    )(x, g, w, cos_t, sin_t)


def _gelu_tanh(x):
    return 0.5 * x * (1.0 + jnp.tanh(np.sqrt(2.0 / np.pi).astype(np.float32) * (x + 0.044715 * (x * x * x))))


def _compress_kernel(kc_ref, vc_ref, kw1_ref, kw1f_ref, kpos_ref, kw2_ref,
                     vw1_ref, vw1f_ref, vpos_ref, vw2_ref, cos_ref, sin_ref, kcb_ref, vcb_ref):
    n_chunks = kc_ref.shape[1]

    def compress(x_ref, w1_ref, w1f_ref, pos_ref, w2_ref):
        u = _dot(x_ref[0], w1_ref[...])
        first = u[:, :2 * CMP_HIDDEN]
        second = pltpu.roll(u[:, 2 * CMP_HIDDEN:], n_chunks - 1, 0)
        pos = jnp.broadcast_to(pos_ref[...], (8, CMP_LEN * HEAD_DIM)).astype(BF16)
        bias = _dot(pos, w1f_ref[...])[0:1]
        pre = first + second + jnp.concatenate([bias, bias], axis=1)
        return _dot(_gelu_tanh(pre).astype(BF16), w2_ref[...])

    k = compress(kc_ref, kw1_ref, kw1f_ref, kpos_ref, kw2_ref)
    kcb_ref[0] = _rope_slab(k, cos_ref[...], sin_ref[...]).astype(BF16)
    vcb_ref[0] = compress(vc_ref, vw1_ref, vw1f_ref, vpos_ref, vw2_ref).astype(BF16)


def _compress_weights(pos, w1, w2):
    w1r = w1.reshape(2, CMP_STRIDE, HEAD_DIM, CMP_HIDDEN)
    eye = jnp.eye(NSA_GROUPS, dtype=w1.dtype)
    big = jnp.einsum('hpdj,ge->pgdhej', w1r, eye)
    big = big.reshape(CMP_STRIDE * KV_WIDTH, 2 * NSA_GROUPS * CMP_HIDDEN).astype(BF16)
    w2big = jnp.einsum('jd,ge->gjed', w2, eye).reshape(NSA_GROUPS * CMP_HIDDEN, KV_WIDTH).astype(BF16)
    return big, w1.astype(BF16), pos.reshape(1, CMP_LEN * HEAD_DIM), w2big


def _compress(kc, vc, kparams, vparams, cos_c, sin_c):
    b, n_chunks, width = kc.shape
    args = (kc, vc) + _compress_weights(*kparams) + _compress_weights(*vparams) + (cos_c, sin_c)
    x_spec = pl.BlockSpec((1, n_chunks, width), lambda i: (i, 0, 0))
    in_specs = [x_spec, x_spec] + [_resident(a.shape) for a in args[2:]]
    o_spec = pl.BlockSpec((1, n_chunks, KV_WIDTH), lambda i: (i, 0, 0))
    o_shape = jax.ShapeDtypeStruct((b, n_chunks, KV_WIDTH), BF16)
    return pl.pallas_call(
        _compress_kernel,
        grid=(b,),
        in_specs=in_specs,
        out_specs=(o_spec, o_spec),
        out_shape=(o_shape, o_shape),
        compiler_params=pltpu.CompilerParams(dimension_semantics=("parallel",),
                                             vmem_limit_bytes=VMEM_LIMIT),
        name="compress",
    )(*args)


def _nsa_kernel(qat_ref, gst_ref, kcb_ref, vcbt_ref, kk_ref, vst_ref, vwt_ref, ovt_ref, blk_ref, o_ref,
                s_ref, m_ref, l_ref, acc_ref, *, seq):
    qi = pl.program_id(1)
    t0 = qi * NSA_Q
    cols = NSA_REP * NSA_Q
    n_cmp = seq // CMP_STRIDE - 1
    n_sel = seq // SEL_BLOCK
    gst = gst_ref[0]
    ovt = ovt_ref[...]

    def key_and_query_pos(n, k0):
        shape = (n, NSA_Q)
        return k0 + lax.broadcasted_iota(jnp.int32, shape, 0), t0 + lax.broadcasted_iota(jnp.int32, shape, 1)

    groups = range(NSA_GROUPS)
    d_rows = [slice(g * HEAD_DIM, (g + 1) * HEAD_DIM) for g in groups]
    g_cols = [slice(g * cols, (g + 1) * cols) for g in groups]

    zeros_d = jnp.zeros((HEAD_DIM, NSA_Q), BF16)
    heads = []
    for n in range(NSA_HEADS):
        x = qat_ref[0, n * HEAD_DIM:(n + 1) * HEAD_DIM, :]
        heads.append(jnp.concatenate([x, zeros_d] if n < NSA_REP else [zeros_d, x], axis=0))
    q = jnp.concatenate(heads, axis=1)

    n_win = WINDOW + NSA_Q
    w0 = pl.multiple_of(jnp.maximum(t0 - WINDOW, 0), NSA_Q)
    s_win = _dot(kk_ref[0, pl.ds(w0, n_win), KV_WIDTH:2 * KV_WIDTH], q)
    s = _dot(kcb_ref[0], q)

    n_idx, tq = key_and_query_pos(s.shape[0], 0)
    cmask = (n_idx * CMP_STRIDE + (CMP_LEN - 1) <= tq) & (n_idx < n_cmp)
    p_cmp = _masked_softmax_keys(s, cmask).astype(BF16)
    o_cmp = [_dot(vcbt_ref[0, d_rows[g], :], p_cmp[:, g_cols[g]]) for g in groups]

    pi = _dot(ovt, p_cmp)
    jj, tsel = key_and_query_pos(n_sel, 0)
    tblk = tsel >> 6
    bonus = jnp.where((jj == 0) | (jj == tblk) | (jj == tblk - 1), FORCE_BONUS, 0.0)
    pad = jnp.zeros((LANES - n_sel, NSA_Q), BF16)
    biases = []
    for g in groups:
        imp = pi[:, g * cols:g * cols + NSA_Q]
        for r in range(1, NSA_REP):
            imp = imp + pi[:, g * cols + r * NSA_Q:g * cols + (r + 1) * NSA_Q]
        imp = jnp.where(jj <= tblk, imp + bonus, NEG)
        bias = jnp.where(_rank_select(imp, n_sel, SEL_TOPK) & (jj <= tblk), 0.0, NEG)
        biases.append(jnp.concatenate([bias.astype(BF16), pad], axis=0))
    q_sel = jnp.concatenate(
        [jnp.concatenate([heads[n], biases[n // NSA_REP]], axis=0) for n in range(NSA_HEADS)], axis=1)

    def tile_start(c):
        return pl.multiple_of(c * SEL_KEYS, SEL_KEYS)

    def sel_scores(c):
        k0 = tile_start(c)
        keys = jnp.concatenate([kk_ref[0, pl.ds(k0, SEL_KEYS), 0:KV_WIDTH], blk_ref[pl.ds(k0, SEL_KEYS), :]],
                               axis=1)
        return _dot(keys, q_sel)

    def sel_update(c, s):
        for g in groups:
            m, l, acc = _online_update((m_ref[g], l_ref[g], acc_ref[g]), s[:, g_cols[g]],
                                       vst_ref[0, d_rows[g], pl.ds(tile_start(c), SEL_KEYS)])
            m_ref[g], l_ref[g], acc_ref[g] = m, l, acc

    n_past = qi // (SEL_KEYS // NSA_Q)
    m_ref[...] = jnp.full(m_ref.shape, NEG, F32)
    l_ref[...] = jnp.zeros(l_ref.shape, F32)
    acc_ref[...] = jnp.zeros(acc_ref.shape, F32)
    s_ref[0] = sel_scores(0)

    kpos, tq = key_and_query_pos(n_win, w0)
    s_win = _mask_keys(s_win, (kpos <= tq) & (tq - kpos < WINDOW))
    p_win = jnp.exp2(s_win - jnp.max(s_win, axis=0, keepdims=True))
    l_win = jnp.maximum(jnp.sum(p_win, axis=0, keepdims=True), TINY)
    p_win = p_win.astype(BF16)
    o_win = [_dot(vwt_ref[0, d_rows[g], pl.ds(w0, n_win)], p_win[:, g_cols[g]]) / l_win[:, g_cols[g]]
             for g in groups]

    for c in range(seq // SEL_KEYS - 1):
        @pl.when(c < n_past)
        def _():
            s_ref[(c + 1) % 2] = sel_scores(c + 1)
            sel_update(c, s_ref[c % 2])

    kpos, tq = key_and_query_pos(SEL_KEYS, tile_start(n_past))
    sel_update(n_past, _mask_keys(s_ref[n_past % 2], kpos <= tq))
    o_sel = [acc_ref[g] / jnp.maximum(l_ref[g], TINY) for g in groups]

    outs = []
    for n in range(NSA_HEADS):
        g, r = divmod(n, NSA_REP)
        sl = slice(r * NSA_Q, (r + 1) * NSA_Q)
        outs.append(gst[3 * n:3 * n + 1, :] * o_cmp[g][:, sl] + gst[3 * n + 1:3 * n + 2, :] * o_sel[g][:, sl]
                    + gst[3 * n + 2:3 * n + 3, :] * o_win[g][:, sl])
    o_ref[0] = jnp.concatenate(outs, axis=0).astype(BF16)


def _nsa(qat, gst, kcb, vcbt, kk, vvt, ovt, blk):
    b, _, seq = qat.shape
    n_chunks = kcb.shape[1]
    return pl.pallas_call(
        functools.partial(_nsa_kernel, seq=seq),
        grid=(b, seq // NSA_Q),
        in_specs=[pl.BlockSpec((1, NSA_WIDTH, NSA_Q), lambda i, j: (i, 0, j)),
                  pl.BlockSpec((1, LANES, NSA_Q), lambda i, j: (i, 0, j)),
                  pl.BlockSpec((1, n_chunks, KV_WIDTH), lambda i, j: (i, 0, 0)),
                  pl.BlockSpec((1, KV_WIDTH, n_chunks), lambda i, j: (i, 0, 0)),
                  pl.BlockSpec((1, seq, 2 * KV_WIDTH), lambda i, j: (i, 0, 0)),
                  pl.BlockSpec((1, KV_WIDTH, seq), lambda i, j: (i, 0, 0)),
                  pl.BlockSpec((1, KV_WIDTH, seq), lambda i, j: (i, 1, 0)),
                  pl.BlockSpec(ovt.shape, lambda i, j: (0, 0)),
                  pl.BlockSpec(blk.shape, lambda i, j: (0, 0))],
        out_specs=pl.BlockSpec((1, NSA_WIDTH, NSA_Q), lambda i, j: (i, 0, j)),
        out_shape=jax.ShapeDtypeStruct((b, NSA_WIDTH, seq), BF16),
        scratch_shapes=[pltpu.VMEM((2, SEL_KEYS, NSA_HEADS * NSA_Q), F32),
                        pltpu.VMEM((NSA_GROUPS, 1, NSA_REP * NSA_Q), F32),
                        pltpu.VMEM((NSA_GROUPS, 1, NSA_REP * NSA_Q), F32),
                        pltpu.VMEM((NSA_GROUPS, HEAD_DIM, NSA_REP * NSA_Q), F32)],
        compiler_params=pltpu.CompilerParams(dimension_semantics=("parallel", "parallel"),
                                             vmem_limit_bytes=VMEM_LIMIT),
        name="nsa",
    )(qat, gst, kcb, vcbt, kk, vvt, vvt, ovt, blk)


def _moba_kernel(qt_ref, k_ref, vt_ref, kmean_ref, o_ref, acc_ref):
    c = pl.program_id(1)
    nb = kmean_ref.shape[1]
    k_own = pl.multiple_of(c * MOBA_BLOCK, MOBA_BLOCK)
    jj = lax.broadcasted_iota(jnp.int32, (BF16_ROWS, MOBA_BLOCK), 0)
    past = jj < c
    pad = jnp.zeros((BF16_ROWS - nb, LANES), F32)
    zeros_d = jnp.zeros((HEAD_DIM, MOBA_BLOCK), BF16)

    def pair(h):
        return slice((h // 2) * LANES, (h // 2 + 1) * LANES)

    qs, gscs = [], []
    for h in range(MOBA_HEADS):
        x = qt_ref[0, h * HEAD_DIM:(h + 1) * HEAD_DIM, :]
        q = jnp.concatenate([x, zeros_d] if h % 2 == 0 else [zeros_d, x], axis=0)
        km = jnp.concatenate([kmean_ref[0, :, pair(h)], pad], axis=0).astype(BF16)
        qs.append(q)
        gscs.append(_dot(km, q))
    chosen = [jnp.where(_rank_select(jnp.where(past, g, NEG), nb, MOBA_TOPK) & past, 1.0, 0.0) for g in gscs]

    acc_ref[...] = jnp.zeros(acc_ref.shape, F32)

    def past_step(j, ml):
        k0 = pl.multiple_of(j * MOBA_BLOCK, MOBA_BLOCK)
        out = []

        ss = [_dot(k_ref[0, pl.ds(k0, MOBA_BLOCK), pair(h)], qs[h]) for h in range(MOBA_HEADS)]
        for h in range(MOBA_HEADS):
            rows = slice(h * HEAD_DIM, (h + 1) * HEAD_DIM)
            s = ss[h]
            pick = jnp.sum(jnp.where(jj == j, chosen[h], 0.0), axis=0, keepdims=True) > 0.5
            m, l = ml[h]
            m_new = jnp.maximum(m, jnp.where(pick, jnp.max(s, axis=0, keepdims=True), NEG))
            alpha = jnp.exp2(m - m_new)
            p = jnp.exp2(s - jnp.where(pick, m_new, jnp.inf))
            out.append((m_new, alpha * l + jnp.sum(p, axis=0, keepdims=True)))
            acc_ref[rows, :] = alpha * acc_ref[rows, :] + _dot(
                vt_ref[0, rows, pl.ds(k0, MOBA_BLOCK)], p.astype(BF16))
        return tuple(out)

    init = tuple((jnp.full((1, MOBA_BLOCK), NEG, F32), jnp.zeros((1, MOBA_BLOCK), F32))
                 for _ in range(MOBA_HEADS))
    ml = lax.fori_loop(0, c, past_step, init)

    kq = (MOBA_BLOCK, MOBA_BLOCK)
    causal = lax.broadcasted_iota(jnp.int32, kq, 0) <= lax.broadcasted_iota(jnp.int32, kq, 1)
    ss = [_dot(k_ref[0, pl.ds(k_own, MOBA_BLOCK), pair(h)], qs[h]) for h in range(MOBA_HEADS)]
    for h in range(MOBA_HEADS):
        rows = slice(h * HEAD_DIM, (h + 1) * HEAD_DIM)
        carry = (ml[h][0], ml[h][1], acc_ref[rows, :])
        _, l, acc = _online_update(carry, jnp.where(causal, ss[h], NEG), vt_ref[0, rows, pl.ds(k_own, MOBA_BLOCK)])
        o_ref[0, rows, :] = (acc / jnp.maximum(l, TINY)).astype(BF16)


def _moba(qbt, kb, vbt, kmean):
    b, _, seq = qbt.shape
    nb = seq // MOBA_BLOCK
    return pl.pallas_call(
        _moba_kernel,
        grid=(b, nb),
        in_specs=[pl.BlockSpec((1, MOBA_WIDTH, MOBA_BLOCK), lambda i, j: (i, 0, j)),
                  pl.BlockSpec((1, seq, MOBA_WIDTH), lambda i, j: (i, 0, 0)),
                  pl.BlockSpec((1, MOBA_WIDTH, seq), lambda i, j: (i, 0, 0)),
                  pl.BlockSpec((1, nb, MOBA_WIDTH), lambda i, j: (i, 0, 0))],
        out_specs=pl.BlockSpec((1, MOBA_WIDTH, MOBA_BLOCK), lambda i, j: (i, 0, j)),
        out_shape=jax.ShapeDtypeStruct((b, MOBA_WIDTH, seq), BF16),
        scratch_shapes=[pltpu.VMEM((MOBA_WIDTH, MOBA_BLOCK), F32)],
        compiler_params=pltpu.CompilerParams(dimension_semantics=("parallel", "parallel"),
                                             vmem_limit_bytes=VMEM_LIMIT),
        name="moba",
    )(qbt, kb, vbt, kmean)


def _merge_kernel(x_ref, g_ref, a_ref, b_ref, wgate_ref, pa_ref, pb_ref, wo_ref, o_ref):
    x = x_ref[...]
    h = _rms(x, g_ref[...]).astype(BF16)
    ya = _dot(a_ref[...], pa_ref[...])
    yb = _dot(b_ref[...], pb_ref[...])
    merged = (jax.nn.sigmoid(_dot(h, wgate_ref[:, :D_MODEL])) * ya
              + jax.nn.sigmoid(_dot(h, wgate_ref[:, D_MODEL:])) * yb)
    o_ref[...] = x + _dot(merged.astype(BF16), wo_ref[...])


def _merge(x, layer, g, a, bm, wgate, pa, pb, wo):
    t = x.shape[0]

    def row(n):
        return pl.BlockSpec((TOKEN_TILE, n), lambda i: (i, 0))

    return pl.pallas_call(
        _merge_kernel,
        grid=(t // TOKEN_TILE,),
        in_specs=[row(D_MODEL), _layer_resident(g, layer), row(NSA_WIDTH), row(MOBA_WIDTH)]
                 + [_layer_resident(w, layer) for w in (wgate, pa, pb, wo)],
        out_specs=row(D_MODEL),
        out_shape=jax.ShapeDtypeStruct(x.shape, F32),
        compiler_params=pltpu.CompilerParams(dimension_semantics=("parallel",),
                                             vmem_limit_bytes=VMEM_LIMIT),
        name="merge",
    )(x, g, a, bm, wgate, pa, pb, wo)


def _rope_tables(pos):
    inv = ROPE_THETA ** (-jnp.arange(0, HEAD_DIM, 2, dtype=F32) / HEAD_DIM)
    ang = pos.astype(F32)[:, None] * inv[None, :]
    cos, sin = jnp.cos(ang), jnp.sin(ang)
    reps = LANES // HEAD_DIM
    return (jnp.tile(jnp.concatenate([cos, cos], axis=1), (1, reps)),
            jnp.tile(jnp.concatenate([-sin, sin], axis=1), (1, reps)))


def _overlap_matrix_t(seq):
    n_chunks = seq // CMP_STRIDE
    ci = np.arange(n_chunks)[None, :] * CMP_STRIDE
    sj = np.arange(seq // SEL_BLOCK)[:, None] * SEL_BLOCK
    ov = (ci < sj + SEL_BLOCK) & (ci + CMP_LEN > sj) & (np.arange(n_chunks)[None, :] < n_chunks - 1)
    return jnp.asarray(ov, dtype=BF16)


def _block_onehot(seq):
    return jnp.asarray(np.arange(seq)[:, None] // SEL_BLOCK == np.arange(LANES)[None, :], dtype=BF16)


def kernel(x, ffn1_norm, ffn1_wg, ffn1_wu, ffn1_wd, mix_norm, w_in, cmpk_pos, cmpk_w1, cmpk_w2, cmpv_pos, cmpv_w1, cmpv_w2, w_branch_nsa, w_branch_moba, w_out, ffn2_norm, ffn2_wg, ffn2_wu, ffn2_wd, final_norm):
    b, seq, d = x.shape
    depth = w_in.shape[0]
    assert d == D_MODEL and seq % TOKEN_TILE == 0 and seq // CMP_STRIDE == LANES
    t = b * seq
    n_chunks = seq // CMP_STRIDE
    cos_t, sin_t = _rope_tables(jnp.arange(seq))
    cos_c, sin_c = _rope_tables(jnp.arange(n_chunks) * CMP_STRIDE + (CMP_LEN - 1))
    ovt = _overlap_matrix_t(seq)
    blk = _block_onehot(seq)

    def gain(gs):
        return gs.reshape(depth, 1, d)

    ffn1 = (gain(ffn1_norm), ffn1_wg.astype(BF16), ffn1_wu.astype(BF16), ffn1_wd.astype(BF16))
    ffn2 = (gain(ffn2_norm), ffn2_wg.astype(BF16), ffn2_wu.astype(BF16), ffn2_wd.astype(BF16))
    w_attn, w_gate = _split_w_in(w_in)
    w_merge = (w_gate, w_branch_nsa.astype(BF16), w_branch_moba.astype(BF16), w_out.astype(BF16))
    fin = final_norm.reshape(1, d)

    xf = x.reshape(t, d)
    for l in range(depth):
        xf = _ffn(xf, l, *ffn1, fin, False)
        x_mix = xf
        qat, kc, vc, kk, vvt, qbt, kb, vbt, gst, kmean = _proj(xf, l, gain(mix_norm), w_attn, cos_t, sin_t, seq)
        kcb, vcb = _compress(kc.reshape(b, n_chunks, CMP_STRIDE * KV_WIDTH),
                             vc.reshape(b, n_chunks, CMP_STRIDE * KV_WIDTH),
                             (cmpk_pos[l], cmpk_w1[l], cmpk_w2[l]),
                             (cmpv_pos[l], cmpv_w1[l], cmpv_w2[l]), cos_c, sin_c)
        at = _nsa(qat, gst, kcb, vcb.transpose(0, 2, 1), kk.reshape(b, seq, 2 * KV_WIDTH), vvt, ovt, blk)
        bt = _moba(qbt, kb.reshape(b, seq, MOBA_WIDTH), vbt, kmean.reshape(b, seq // MOBA_BLOCK, MOBA_WIDTH))
        xf = _merge(x_mix, l, gain(mix_norm), at.transpose(0, 2, 1).reshape(t, NSA_WIDTH),
                    bt.transpose(0, 2, 1).reshape(t, MOBA_WIDTH), *w_merge)
        xf = _ffn(xf, l, *ffn2, fin, l == depth - 1)
    return xf.reshape(b, seq, d)
```

```python
import functools

import jax
import jax.numpy as jnp
import numpy as np
from jax import lax
from jax.experimental import pallas as pl
from jax.experimental.pallas import tpu as pltpu

F32 = jnp.float32
BF16 = jnp.bfloat16

D_MODEL = 1024
HEAD_DIM = 64
HALF = HEAD_DIM // 2
NSA_HEADS = 8
NSA_GROUPS = 2
NSA_REP = NSA_HEADS // NSA_GROUPS
CMP_LEN = 32
CMP_STRIDE = 16
CMP_HIDDEN = 128
SEL_BLOCK = 64
SEL_TOPK = 16
WINDOW = 512
MOBA_HEADS = 8
MOBA_BLOCK = 256
MOBA_TOPK = 3
D_FF = 2816
ROPE_THETA = 10000.0
EPS = 1e-6
NEG = -1e30
TINY = 1e-30
FORCE_BONUS = 1e4
Q_SCALE = HEAD_DIM ** -0.5 * float(np.log2(np.e))

NSA_WIDTH = NSA_HEADS * HEAD_DIM
KV_WIDTH = NSA_GROUPS * HEAD_DIM
MOBA_WIDTH = MOBA_HEADS * HEAD_DIM
N_GATE_COLS = 3 * NSA_HEADS

LANES = 128
BF16_ROWS = 16
TOKEN_TILE = 512
FF_CHUNK = 256
NSA_Q = 256
SEL_KEYS = 512
VMEM_LIMIT = 56 * 1024 * 1024

COL_QA = 0
COL_KC = COL_QA + NSA_WIDTH
COL_VC = COL_KC + KV_WIDTH
COL_KV4 = COL_VC + KV_WIDTH
COL_QB = COL_KV4 + 4 * KV_WIDTH
COL_KB = COL_QB + MOBA_WIDTH
COL_VB = COL_KB + MOBA_WIDTH
COL_GS = COL_VB + MOBA_WIDTH


def _rms(x, g):
    return x * lax.rsqrt(jnp.mean(x * x, axis=-1, keepdims=True) + EPS) * g


def _dot(a, b):
    return jnp.dot(a, b, preferred_element_type=F32)


def _dot_nt(a, b):
    return lax.dot_general(a, b, (((1,), (1,)), ((), ())), preferred_element_type=F32)


def _swap_halves(z):
    lane = lax.broadcasted_iota(jnp.int32, z.shape, 1)
    lo = (lane & (HEAD_DIM - 1)) < HALF
    return jnp.where(lo, pltpu.roll(z, LANES - HALF, 1), pltpu.roll(z, HALF, 1))


def _rope_slab(z, cos, sin):
    parts = []
    for c in range(z.shape[1] // LANES):
        zc = z[:, c * LANES:(c + 1) * LANES]
        parts.append(zc * cos + _swap_halves(zc) * sin)
    return parts[0] if len(parts) == 1 else jnp.concatenate(parts, axis=1)


def _slabs(s, n):
    return [s[:, i * n:(i + 1) * n] for i in range(s.shape[1] // n)]


def _masked_softmax_keys(s, mask):
    out = []
    for x in _slabs(s, mask.shape[1]):
        x = jnp.where(mask, x, NEG)
        m = jnp.max(x, axis=0, keepdims=True)
        p = jnp.where(mask, jnp.exp2(x - m), 0.0)
        out.append(p / jnp.maximum(jnp.sum(p, axis=0, keepdims=True), TINY))
    return jnp.concatenate(out, axis=1)


def _mask_keys(s, mask):
    return jnp.concatenate([jnp.where(mask, x, NEG) for x in _slabs(s, mask.shape[1])], axis=1)


def _online_update(carry, s, vt):
    m, l, acc = carry
    m_new = jnp.maximum(m, jnp.max(s, axis=0, keepdims=True))
    alpha = jnp.exp2(m - m_new)
    p = jnp.exp2(s - m_new)
    l = alpha * l + jnp.sum(p, axis=0, keepdims=True)
    acc = alpha * acc + _dot(vt, p.astype(BF16))
    return m_new, l, acc


def _online_init(n):
    return jnp.full((1, n), NEG, F32), jnp.zeros((1, n), F32), jnp.zeros((HEAD_DIM, n), F32)


def _rank_select(v, n_candidates, k):
    row = lax.broadcasted_iota(jnp.int32, v.shape, 0)
    cnt = jnp.zeros(v.shape, F32)
    for j in range(n_candidates):
        r = v[j:j + 1, :]
        cnt = cnt + jnp.where(r > v, 1.0, jnp.where(r == v, jnp.where(row > j, 1.0, 0.0), 0.0))
    return cnt < float(k)


def _ffn_kernel(x_ref, g_ref, wg_ref, wu_ref, wd_ref, fin_ref, o_ref, t_ref, *, final_norm):
    x = x_ref[...]
    h = _rms(x, g_ref[...]).astype(BF16)
    for f in range(D_FF // FF_CHUNK):
        c = slice(f * FF_CHUNK, (f + 1) * FF_CHUNK)
        a = _dot(h, wg_ref[:, c])
        b = _dot(h, wu_ref[:, c])
        t_ref[:, c] = (a * jax.nn.sigmoid(a) * b).astype(BF16)
    y = x + 0.5 * _dot(t_ref[...], wd_ref[...])
    if final_norm:
        y = _rms(y, fin_ref[...])
    o_ref[...] = y


def _resident(shape):
    nd = len(shape)
    return pl.BlockSpec(shape, lambda *_: (0,) * nd, pipeline_mode=pl.Buffered(1))


def _layer_resident(stacked, layer):
    return pl.BlockSpec((None,) + stacked.shape[1:], lambda *_: (layer, 0, 0), pipeline_mode=pl.Buffered(1))


def _ffn(x, layer, g, wg, wu, wd, fin, final_norm):
    t = x.shape[0]
    row = pl.BlockSpec((TOKEN_TILE, D_MODEL), lambda i: (i, 0))
    return pl.pallas_call(
        functools.partial(_ffn_kernel, final_norm=final_norm),
        grid=(t // TOKEN_TILE,),
        in_specs=[row, _layer_resident(g, layer), _layer_resident(wg, layer), _layer_resident(wu, layer),
                  _layer_resident(wd, layer), _resident((1, D_MODEL))],
        out_specs=row,
        out_shape=jax.ShapeDtypeStruct(x.shape, F32),
        scratch_shapes=[pltpu.VMEM((TOKEN_TILE, D_FF), BF16)],
        compiler_params=pltpu.CompilerParams(dimension_semantics=("parallel",),
                                             vmem_limit_bytes=VMEM_LIMIT),
        name="ffn",
    )(x, g, wg, wu, wd, fin)


def _proj_kernel(x_ref, g_ref, w_ref, cos_ref, sin_ref,
                 qat_ref, kc_ref, vc_ref, kk_ref, vvt_ref, qbt_ref, kb_ref, vbt_ref, gst_ref, kmean_ref):
    h = _rms(x_ref[...], g_ref[...]).astype(BF16)
    cos = cos_ref[...]
    sin = sin_ref[...]

    def mm(c0, n):
        return _dot(h, w_ref[:, c0:c0 + n])

    def feature_major(z, dtype=BF16):
        return jnp.transpose(z).astype(dtype)

    qat_ref[0] = feature_major(_rope_slab(mm(COL_QA, NSA_WIDTH), cos, sin) * Q_SCALE)
    kcvc = mm(COL_KC, 2 * KV_WIDTH)
    kc_ref[...] = kcvc[:, :KV_WIDTH].astype(BF16)
    vc_ref[...] = kcvc[:, KV_WIDTH:].astype(BF16)
    kv4 = mm(COL_KV4, 4 * KV_WIDTH)
    kk_ref[...] = jnp.concatenate(
        [_rope_slab(kv4[:, 0:KV_WIDTH], cos, sin), _rope_slab(kv4[:, 2 * KV_WIDTH:3 * KV_WIDTH], cos, sin)],
        axis=1).astype(BF16)
    vvt_ref[0] = feature_major(jnp.concatenate([kv4[:, KV_WIDTH:2 * KV_WIDTH], kv4[:, 3 * KV_WIDTH:]], axis=1))
    qbt_ref[0] = feature_major(_rope_slab(mm(COL_QB, MOBA_WIDTH), cos, sin) * Q_SCALE)
    kb = _rope_slab(mm(COL_KB, MOBA_WIDTH), cos, sin)
    kb_ref[...] = kb.astype(BF16)
    means = [jnp.mean(kb[i * MOBA_BLOCK:(i + 1) * MOBA_BLOCK], axis=0, keepdims=True)
             for i in range(TOKEN_TILE // MOBA_BLOCK)]
    kmean_ref[0] = jnp.concatenate(means, axis=0)
    vbt_ref[0] = feature_major(mm(COL_VB, MOBA_WIDTH))
    gst_ref[0] = feature_major(jax.nn.sigmoid(mm(COL_GS, LANES)), F32)


def _split_w_in(w_in):
    off_ga = NSA_WIDTH
    off_rest = off_ga + N_GATE_COLS
    off_gates = off_rest + 6 * KV_WIDTH + 3 * MOBA_WIDTH
    pad = jnp.zeros(w_in.shape[:2] + (LANES - N_GATE_COLS,), w_in.dtype)
    attn = jnp.concatenate([w_in[..., :off_ga], w_in[..., off_rest:off_gates], w_in[..., off_ga:off_rest], pad],
                           axis=-1)
    return attn.astype(BF16), w_in[..., off_gates:].astype(BF16)


def _proj(x, layer, g, w, cos_t, sin_t, seq):
    t = x.shape[0]
    tiles_per_seq = seq // TOKEN_TILE
    blocks_per_tile = TOKEN_TILE // MOBA_BLOCK

    def row(n):
        return pl.BlockSpec((TOKEN_TILE, n), lambda i: (i, 0))

    def token_major(n):
        return jax.ShapeDtypeStruct((t, n), BF16), row(n)

    def feature_major(n, dtype=BF16):
        return (jax.ShapeDtypeStruct((t // seq, n, seq), dtype),
                pl.BlockSpec((1, n, TOKEN_TILE), lambda i: (i // tiles_per_seq, 0, i % tiles_per_seq)))

    tab = pl.BlockSpec((TOKEN_TILE, LANES), lambda i: (i % tiles_per_seq, 0))
    outs = (feature_major(NSA_WIDTH), token_major(KV_WIDTH), token_major(KV_WIDTH), token_major(2 * KV_WIDTH),
            feature_major(2 * KV_WIDTH), feature_major(MOBA_WIDTH), token_major(MOBA_WIDTH),
            feature_major(MOBA_WIDTH), feature_major(LANES, F32),
            (jax.ShapeDtypeStruct((t // TOKEN_TILE, blocks_per_tile, MOBA_WIDTH), F32),
             pl.BlockSpec((1, blocks_per_tile, MOBA_WIDTH), lambda i: (i, 0, 0))))
    out_shape, out_specs = zip(*outs)
    return pl.pallas_call(
        _proj_kernel,
        grid=(t // TOKEN_TILE,),
        in_specs=[row(D_MODEL), _layer_resident(g, layer), _layer_resident(w, layer), tab, tab],
        out_specs=out_specs,
        out_shape=out_shape,
        compiler_params=pltpu.CompilerParams(dimension_semantics=("parallel",),
                                             vmem_limit_bytes=VMEM_LIMIT),
        name="proj",
    )(x, g, w, cos_t, sin_t)


def _gelu_tanh(x):
    return 0.5 * x * (1.0 + jnp.tanh(np.sqrt(2.0 / np.pi).astype(np.float32) * (x + 0.044715 * (x * x * x))))


def _compress_kernel(kc_ref, vc_ref, kw1_ref, kw1f_ref, kpos_ref, kw2_ref,
                     vw1_ref, vw1f_ref, vpos_ref, vw2_ref, cos_ref, sin_ref, kcb_ref, vcb_ref):
    n_chunks = kc_ref.shape[1]

    def compress(x_ref, w1_ref, w1f_ref, pos_ref, w2_ref):
        u = _dot(x_ref[0], w1_ref[...])
        first = u[:, :2 * CMP_HIDDEN]
        second = pltpu.roll(u[:, 2 * CMP_HIDDEN:], n_chunks - 1, 0)
        pos = jnp.broadcast_to(pos_ref[...], (8, CMP_LEN * HEAD_DIM)).astype(BF16)
        bias = _dot(pos, w1f_ref[...])[0:1]
        pre = first + second + jnp.concatenate([bias, bias], axis=1)
        return _dot(_gelu_tanh(pre).astype(BF16), w2_ref[...])

    k = compress(kc_ref, kw1_ref, kw1f_ref, kpos_ref, kw2_ref)
    kcb_ref[0] = _rope_slab(k, cos_ref[...], sin_ref[...]).astype(BF16)
    vcb_ref[0] = compress(vc_ref, vw1_ref, vw1f_ref, vpos_ref, vw2_ref).astype(BF16)


def _compress_weights(pos, w1, w2):
    w1r = w1.reshape(2, CMP_STRIDE, HEAD_DIM, CMP_HIDDEN)
    eye = jnp.eye(NSA_GROUPS, dtype=w1.dtype)
    big = jnp.einsum('hpdj,ge->pgdhej', w1r, eye)
    big = big.reshape(CMP_STRIDE * KV_WIDTH, 2 * NSA_GROUPS * CMP_HIDDEN).astype(BF16)
    w2big = jnp.einsum('jd,ge->gjed', w2, eye).reshape(NSA_GROUPS * CMP_HIDDEN, KV_WIDTH).astype(BF16)
    return big, w1.astype(BF16), pos.reshape(1, CMP_LEN * HEAD_DIM), w2big


def _compress(kc, vc, kparams, vparams, cos_c, sin_c):
    b, n_chunks, width = kc.shape
    args = (kc, vc) + _compress_weights(*kparams) + _compress_weights(*vparams) + (cos_c, sin_c)
    x_spec = pl.BlockSpec((1, n_chunks, width), lambda i: (i, 0, 0))
    in_specs = [x_spec, x_spec] + [_resident(a.shape) for a in args[2:]]
    o_spec = pl.BlockSpec((1, n_chunks, KV_WIDTH), lambda i: (i, 0, 0))
    o_shape = jax.ShapeDtypeStruct((b, n_chunks, KV_WIDTH), BF16)
    return pl.pallas_call(
        _compress_kernel,
        grid=(b,),
        in_specs=in_specs,
        out_specs=(o_spec, o_spec),
        out_shape=(o_shape, o_shape),
        compiler_params=pltpu.CompilerParams(dimension_semantics=("parallel",),
                                             vmem_limit_bytes=VMEM_LIMIT),
        name="compress",
    )(*args)


def _nsa_kernel(qat_ref, gst_ref, kcb_ref, vcbt_ref, kk_ref, vst_ref, vwt_ref, ovt_ref, blk_ref, o_ref,
                s_ref, m_ref, l_ref, acc_ref, *, seq):
    qi = pl.program_id(1)
    t0 = qi * NSA_Q
    cols = NSA_REP * NSA_Q
    n_cmp = seq // CMP_STRIDE - 1
    n_sel = seq // SEL_BLOCK
    gst = gst_ref[0]
    ovt = ovt_ref[...]

    def key_and_query_pos(n, k0):
        shape = (n, NSA_Q)
        return k0 + lax.broadcasted_iota(jnp.int32, shape, 0), t0 + lax.broadcasted_iota(jnp.int32, shape, 1)

    groups = range(NSA_GROUPS)
    d_rows = [slice(g * HEAD_DIM, (g + 1) * HEAD_DIM) for g in groups]
    g_cols = [slice(g * cols, (g + 1) * cols) for g in groups]

    zeros_d = jnp.zeros((HEAD_DIM, NSA_Q), BF16)
    heads = []
    for n in range(NSA_HEADS):
        x = qat_ref[0, n * HEAD_DIM:(n + 1) * HEAD_DIM, :]
        heads.append(jnp.concatenate([x, zeros_d] if n < NSA_REP else [zeros_d, x], axis=0))
    q = jnp.concatenate(heads, axis=1)

    n_win = WINDOW + NSA_Q
    w0 = pl.multiple_of(jnp.maximum(t0 - WINDOW, 0), NSA_Q)
    s_win = _dot(kk_ref[0, pl.ds(w0, n_win), KV_WIDTH:2 * KV_WIDTH], q)
    s = _dot(kcb_ref[0], q)

    n_idx, tq = key_and_query_pos(s.shape[0], 0)
    cmask = (n_idx * CMP_STRIDE + (CMP_LEN - 1) <= tq) & (n_idx < n_cmp)
    p_cmp = _masked_softmax_keys(s, cmask).astype(BF16)
    o_cmp = [_dot(vcbt_ref[0, d_rows[g], :], p_cmp[:, g_cols[g]]) for g in groups]

    pi = _dot(ovt, p_cmp)
    jj, tsel = key_and_query_pos(n_sel, 0)
    tblk = tsel >> 6
    bonus = jnp.where((jj == 0) | (jj == tblk) | (jj == tblk - 1), FORCE_BONUS, 0.0)
    pad = jnp.zeros((LANES - n_sel, NSA_Q), BF16)
    biases = []
    for g in groups:
        imp = pi[:, g * cols:g * cols + NSA_Q]
        for r in range(1, NSA_REP):
            imp = imp + pi[:, g * cols + r * NSA_Q:g * cols + (r + 1) * NSA_Q]
        imp = jnp.where(jj <= tblk, imp + bonus, NEG)
        bias = jnp.where(_rank_select(imp, n_sel, SEL_TOPK) & (jj <= tblk), 0.0, NEG)
        biases.append(jnp.concatenate([bias.astype(BF16), pad], axis=0))
    q_sel = jnp.concatenate(
        [jnp.concatenate([heads[n], biases[n // NSA_REP]], axis=0) for n in range(NSA_HEADS)], axis=1)

    def tile_start(c):
        return pl.multiple_of(c * SEL_KEYS, SEL_KEYS)

    def sel_scores(c):
        k0 = tile_start(c)
        keys = jnp.concatenate([kk_ref[0, pl.ds(k0, SEL_KEYS), 0:KV_WIDTH], blk_ref[pl.ds(k0, SEL_KEYS), :]],
                               axis=1)
        return _dot(keys, q_sel)

    def sel_update(c, s):
        for g in groups:
            m, l, acc = _online_update((m_ref[g], l_ref[g], acc_ref[g]), s[:, g_cols[g]],
                                       vst_ref[0, d_rows[g], pl.ds(tile_start(c), SEL_KEYS)])
            m_ref[g], l_ref[g], acc_ref[g] = m, l, acc

    n_past = qi // (SEL_KEYS // NSA_Q)
    m_ref[...] = jnp.full(m_ref.shape, NEG, F32)
    l_ref[...] = jnp.zeros(l_ref.shape, F32)
    acc_ref[...] = jnp.zeros(acc_ref.shape, F32)
    s_ref[0] = sel_scores(0)

    kpos, tq = key_and_query_pos(n_win, w0)
    s_win = _mask_keys(s_win, (kpos <= tq) & (tq - kpos < WINDOW))
    p_win = jnp.exp2(s_win - jnp.max(s_win, axis=0, keepdims=True))
    l_win = jnp.maximum(jnp.sum(p_win, axis=0, keepdims=True), TINY)
    p_win = p_win.astype(BF16)
    o_win = [_dot(vwt_ref[0, d_rows[g], pl.ds(w0, n_win)], p_win[:, g_cols[g]]) / l_win[:, g_cols[g]]
             for g in groups]

    for c in range(seq // SEL_KEYS - 1):
        @pl.when(c < n_past)
        def _():
            s_ref[(c + 1) % 2] = sel_scores(c + 1)
            sel_update(c, s_ref[c % 2])

    kpos, tq = key_and_query_pos(SEL_KEYS, tile_start(n_past))
    sel_update(n_past, _mask_keys(s_ref[n_past % 2], kpos <= tq))
    o_sel = [acc_ref[g] / jnp.maximum(l_ref[g], TINY) for g in groups]

    outs = []
    for n in range(NSA_HEADS):
        g, r = divmod(n, NSA_REP)
        sl = slice(r * NSA_Q, (r + 1) * NSA_Q)
        outs.append(gst[3 * n:3 * n + 1, :] * o_cmp[g][:, sl] + gst[3 * n + 1:3 * n + 2, :] * o_sel[g][:, sl]
                    + gst[3 * n + 2:3 * n + 3, :] * o_win[g][:, sl])
    o_ref[0] = jnp.transpose(jnp.concatenate(outs, axis=0)).astype(BF16)


def _nsa(qat, gst, kcb, vcbt, kk, vvt, ovt, blk):
    b, _, seq = qat.shape
    n_chunks = kcb.shape[1]
    return pl.pallas_call(
        functools.partial(_nsa_kernel, seq=seq),
        grid=(b, seq // NSA_Q),
        in_specs=[pl.BlockSpec((1, NSA_WIDTH, NSA_Q), lambda i, j: (i, 0, j)),
                  pl.BlockSpec((1, LANES, NSA_Q), lambda i, j: (i, 0, j)),
                  pl.BlockSpec((1, n_chunks, KV_WIDTH), lambda i, j: (i, 0, 0)),
                  pl.BlockSpec((1, KV_WIDTH, n_chunks), lambda i, j: (i, 0, 0)),
                  pl.BlockSpec((1, seq, 2 * KV_WIDTH), lambda i, j: (i, 0, 0)),
                  pl.BlockSpec((1, KV_WIDTH, seq), lambda i, j: (i, 0, 0)),
                  pl.BlockSpec((1, KV_WIDTH, seq), lambda i, j: (i, 1, 0)),
                  pl.BlockSpec(ovt.shape, lambda i, j: (0, 0)),
                  pl.BlockSpec(blk.shape, lambda i, j: (0, 0))],
        out_specs=pl.BlockSpec((1, NSA_Q, NSA_WIDTH), lambda i, j: (i, j, 0)),
        out_shape=jax.ShapeDtypeStruct((b, seq, NSA_WIDTH), BF16),
        scratch_shapes=[pltpu.VMEM((2, SEL_KEYS, NSA_HEADS * NSA_Q), F32),
                        pltpu.VMEM((NSA_GROUPS, 1, NSA_REP * NSA_Q), F32),
                        pltpu.VMEM((NSA_GROUPS, 1, NSA_REP * NSA_Q), F32),
                        pltpu.VMEM((NSA_GROUPS, HEAD_DIM, NSA_REP * NSA_Q), F32)],
        compiler_params=pltpu.CompilerParams(dimension_semantics=("parallel", "parallel"),
                                             vmem_limit_bytes=VMEM_LIMIT),
        name="nsa",
    )(qat, gst, kcb, vcbt, kk, vvt, vvt, ovt, blk)


def _moba_kernel(qt_ref, k_ref, vt_ref, kmean_ref, o_ref, s_ref, m_ref, l_ref, acc_ref):
    c = pl.program_id(1)
    nb = kmean_ref.shape[1]
    k_own = pl.multiple_of(c * MOBA_BLOCK, MOBA_BLOCK)
    jj = lax.broadcasted_iota(jnp.int32, (BF16_ROWS, MOBA_BLOCK), 0)
    past = jj < c
    pad = jnp.zeros((BF16_ROWS - nb, LANES), F32)
    zeros_d = jnp.zeros((HEAD_DIM, MOBA_BLOCK), BF16)

    def pair(h):
        return slice((h // 2) * LANES, (h // 2 + 1) * LANES)

    qs, gscs = [], []
    for h in range(MOBA_HEADS):
        x = qt_ref[0, h * HEAD_DIM:(h + 1) * HEAD_DIM, :]
        q = jnp.concatenate([x, zeros_d] if h % 2 == 0 else [zeros_d, x], axis=0)
        km = jnp.concatenate([kmean_ref[0, :, pair(h)], pad], axis=0).astype(BF16)
        qs.append(q)
        gscs.append(_dot(km, q))
    chosen = [jnp.where(_rank_select(jnp.where(past, g, NEG), nb, MOBA_TOPK) & past, 1.0, 0.0) for g in gscs]

    m_ref[...] = jnp.full(m_ref.shape, NEG, F32)
    l_ref[...] = jnp.zeros(l_ref.shape, F32)
    acc_ref[...] = jnp.zeros(acc_ref.shape, F32)

    def block_start(j):
        return pl.multiple_of(j * MOBA_BLOCK, MOBA_BLOCK)

    def scores(j, slot):
        for h in range(MOBA_HEADS):
            s_ref[slot, h] = _dot(k_ref[0, pl.ds(block_start(j), MOBA_BLOCK), pair(h)], qs[h])

    def past_update(j, slot):
        for h in range(MOBA_HEADS):
            rows = slice(h * HEAD_DIM, (h + 1) * HEAD_DIM)
            s = s_ref[slot, h]
            pick = jnp.sum(jnp.where(jj == j, chosen[h], 0.0), axis=0, keepdims=True) > 0.5
            m = m_ref[h]
            m_new = jnp.maximum(m, jnp.where(pick, jnp.max(s, axis=0, keepdims=True), NEG))
            alpha = jnp.exp2(m - m_new)
            p = jnp.exp2(s - jnp.where(pick, m_new, jnp.inf))
            m_ref[h] = m_new
            l_ref[h] = alpha * l_ref[h] + jnp.sum(p, axis=0, keepdims=True)
            acc_ref[rows, :] = alpha * acc_ref[rows, :] + _dot(
                vt_ref[0, rows, pl.ds(block_start(j), MOBA_BLOCK)], p.astype(BF16))

    scores(0, 0)

    def two_blocks(i, _):
        j = 2 * i
        scores(j + 1, 1)
        past_update(j, 0)
        scores(j + 2, 0)
        past_update(j + 1, 1)
        return 0

    lax.fori_loop(0, c // 2, two_blocks, 0)

    @pl.when(c % 2 == 1)
    def _():
        scores(c, 1)
        past_update(c - 1, 0)

    kq = (MOBA_BLOCK, MOBA_BLOCK)
    causal = lax.broadcasted_iota(jnp.int32, kq, 0) <= lax.broadcasted_iota(jnp.int32, kq, 1)
    for h in range(MOBA_HEADS):
        rows = slice(h * HEAD_DIM, (h + 1) * HEAD_DIM)
        carry = (m_ref[h], l_ref[h], acc_ref[rows, :])
        _, l, acc = _online_update(carry, jnp.where(causal, s_ref[c % 2, h], NEG),
                                   vt_ref[0, rows, pl.ds(k_own, MOBA_BLOCK)])
        acc_ref[rows, :] = acc / jnp.maximum(l, TINY)
    o_ref[0] = jnp.transpose(acc_ref[...]).astype(BF16)


def _moba(qbt, kb, vbt, kmean):
    b, _, seq = qbt.shape
    nb = seq // MOBA_BLOCK
    return pl.pallas_call(
        _moba_kernel,
        grid=(b, nb),
        in_specs=[pl.BlockSpec((1, MOBA_WIDTH, MOBA_BLOCK), lambda i, j: (i, 0, j)),
                  pl.BlockSpec((1, seq, MOBA_WIDTH), lambda i, j: (i, 0, 0)),
                  pl.BlockSpec((1, MOBA_WIDTH, seq), lambda i, j: (i, 0, 0)),
                  pl.BlockSpec((1, nb, MOBA_WIDTH), lambda i, j: (i, 0, 0))],
        out_specs=pl.BlockSpec((1, MOBA_BLOCK, MOBA_WIDTH), lambda i, j: (i, j, 0)),
        out_shape=jax.ShapeDtypeStruct((b, seq, MOBA_WIDTH), BF16),
        scratch_shapes=[pltpu.VMEM((2, MOBA_HEADS, MOBA_BLOCK, MOBA_BLOCK), F32),
                        pltpu.VMEM((MOBA_HEADS, 1, MOBA_BLOCK), F32), pltpu.VMEM((MOBA_HEADS, 1, MOBA_BLOCK), F32),
                        pltpu.VMEM((MOBA_WIDTH, MOBA_BLOCK), F32)],
        compiler_params=pltpu.CompilerParams(dimension_semantics=("parallel", "parallel"),
                                             vmem_limit_bytes=VMEM_LIMIT),
        name="moba",
    )(qbt, kb, vbt, kmean)


def _merge_kernel(x_ref, g_ref, a_ref, b_ref, wgate_ref, pa_ref, pb_ref, wo_ref, o_ref):
    x = x_ref[...]
    h = _rms(x, g_ref[...]).astype(BF16)
    ya = _dot(a_ref[...], pa_ref[...])
    yb = _dot(b_ref[...], pb_ref[...])
    merged = (jax.nn.sigmoid(_dot(h, wgate_ref[:, :D_MODEL])) * ya
              + jax.nn.sigmoid(_dot(h, wgate_ref[:, D_MODEL:])) * yb)
    o_ref[...] = x + _dot(merged.astype(BF16), wo_ref[...])


def _merge(x, layer, g, a, bm, wgate, pa, pb, wo):
    t = x.shape[0]

    def row(n):
        return pl.BlockSpec((TOKEN_TILE, n), lambda i: (i, 0))

    return pl.pallas_call(
        _merge_kernel,
        grid=(t // TOKEN_TILE,),
        in_specs=[row(D_MODEL), _layer_resident(g, layer), row(NSA_WIDTH), row(MOBA_WIDTH)]
                 + [_layer_resident(w, layer) for w in (wgate, pa, pb, wo)],
        out_specs=row(D_MODEL),
        out_shape=jax.ShapeDtypeStruct(x.shape, F32),
        compiler_params=pltpu.CompilerParams(dimension_semantics=("parallel",),
                                             vmem_limit_bytes=VMEM_LIMIT),
        name="merge",
    )(x, g, a, bm, wgate, pa, pb, wo)


def _rope_tables(pos):
    inv = ROPE_THETA ** (-jnp.arange(0, HEAD_DIM, 2, dtype=F32) / HEAD_DIM)
    ang = pos.astype(F32)[:, None] * inv[None, :]
    cos, sin = jnp.cos(ang), jnp.sin(ang)
    reps = LANES // HEAD_DIM
    return (jnp.tile(jnp.concatenate([cos, cos], axis=1), (1, reps)),
            jnp.tile(jnp.concatenate([-sin, sin], axis=1), (1, reps)))


def _overlap_matrix_t(seq):
    n_chunks = seq // CMP_STRIDE
    ci = np.arange(n_chunks)[None, :] * CMP_STRIDE
    sj = np.arange(seq // SEL_BLOCK)[:, None] * SEL_BLOCK
    ov = (ci < sj + SEL_BLOCK) & (ci + CMP_LEN > sj) & (np.arange(n_chunks)[None, :] < n_chunks - 1)
    return jnp.asarray(ov, dtype=BF16)


def _block_onehot(seq):
    return jnp.asarray(np.arange(seq)[:, None] // SEL_BLOCK == np.arange(LANES)[None, :], dtype=BF16)


def kernel(x, ffn1_norm, ffn1_wg, ffn1_wu, ffn1_wd, mix_norm, w_in, cmpk_pos, cmpk_w1, cmpk_w2, cmpv_pos, cmpv_w1, cmpv_w2, w_branch_nsa, w_branch_moba, w_out, ffn2_norm, ffn2_wg, ffn2_wu, ffn2_wd, final_norm):
    b, seq, d = x.shape
    depth = w_in.shape[0]
    assert d == D_MODEL and seq % TOKEN_TILE == 0 and seq // CMP_STRIDE == LANES
    t = b * seq
    n_chunks = seq // CMP_STRIDE
    cos_t, sin_t = _rope_tables(jnp.arange(seq))
    cos_c, sin_c = _rope_tables(jnp.arange(n_chunks) * CMP_STRIDE + (CMP_LEN - 1))
    ovt = _overlap_matrix_t(seq)
    blk = _block_onehot(seq)

    def gain(gs):
        return gs.reshape(depth, 1, d)

    ffn1 = (gain(ffn1_norm), ffn1_wg.astype(BF16), ffn1_wu.astype(BF16), ffn1_wd.astype(BF16))
    ffn2 = (gain(ffn2_norm), ffn2_wg.astype(BF16), ffn2_wu.astype(BF16), ffn2_wd.astype(BF16))
    w_attn, w_gate = _split_w_in(w_in)
    w_merge = (w_gate, w_branch_nsa.astype(BF16), w_branch_moba.astype(BF16), w_out.astype(BF16))
    fin = final_norm.reshape(1, d)

    xf = x.reshape(t, d)
    for l in range(depth):
        xf = _ffn(xf, l, *ffn1, fin, False)
        x_mix = xf
        qat, kc, vc, kk, vvt, qbt, kb, vbt, gst, kmean = _proj(xf, l, gain(mix_norm), w_attn, cos_t, sin_t, seq)
        kcb, vcb = _compress(kc.reshape(b, n_chunks, CMP_STRIDE * KV_WIDTH),
                             vc.reshape(b, n_chunks, CMP_STRIDE * KV_WIDTH),
                             (cmpk_pos[l], cmpk_w1[l], cmpk_w2[l]),
                             (cmpv_pos[l], cmpv_w1[l], cmpv_w2[l]), cos_c, sin_c)
        a = _nsa(qat, gst, kcb, vcb.transpose(0, 2, 1), kk.reshape(b, seq, 2 * KV_WIDTH), vvt, ovt, blk)
        bm = _moba(qbt, kb.reshape(b, seq, MOBA_WIDTH), vbt, kmean.reshape(b, seq // MOBA_BLOCK, MOBA_WIDTH))
        xf = _merge(x_mix, l, gain(mix_norm), a.reshape(t, NSA_WIDTH), bm.reshape(t, MOBA_WIDTH), *w_merge)
        xf = _ffn(xf, l, *ffn2, fin, l == depth - 1)
    return xf.reshape(b, seq, d)
```

```python
import functools

import jax
import jax.numpy as jnp
import numpy as np
from jax import lax
from jax.experimental import pallas as pl
from jax.experimental.pallas import tpu as pltpu

F32 = jnp.float32
BF16 = jnp.bfloat16

D_MODEL = 1024
HEAD_DIM = 64
HALF = HEAD_DIM // 2
NSA_HEADS = 8
NSA_GROUPS = 2
NSA_REP = NSA_HEADS // NSA_GROUPS
CMP_LEN = 32
CMP_STRIDE = 16
CMP_HIDDEN = 128
SEL_BLOCK = 64
SEL_TOPK = 16
WINDOW = 512
MOBA_HEADS = 8
MOBA_BLOCK = 256
MOBA_TOPK = 3
D_FF = 2816
ROPE_THETA = 10000.0
EPS = 1e-6
NEG = -1e30
TINY = 1e-30
FORCE_BONUS = 1e4
Q_SCALE = HEAD_DIM ** -0.5 * float(np.log2(np.e))

NSA_WIDTH = NSA_HEADS * HEAD_DIM
KV_WIDTH = NSA_GROUPS * HEAD_DIM
MOBA_WIDTH = MOBA_HEADS * HEAD_DIM
N_GATE_COLS = 3 * NSA_HEADS

LANES = 128
BF16_ROWS = 16
TOKEN_TILE = 512
FF_CHUNK = 256
NSA_Q = 256
SEL_KEYS = 512
VMEM_LIMIT = 56 * 1024 * 1024

COL_QA = 0
COL_KC = COL_QA + NSA_WIDTH
COL_VC = COL_KC + KV_WIDTH
COL_KV4 = COL_VC + KV_WIDTH
COL_QB = COL_KV4 + 4 * KV_WIDTH
COL_KB = COL_QB + MOBA_WIDTH
COL_VB = COL_KB + MOBA_WIDTH
COL_GS = COL_VB + MOBA_WIDTH


def _rms(x, g):
    return x * lax.rsqrt(jnp.mean(x * x, axis=-1, keepdims=True) + EPS) * g


def _dot(a, b):
    return jnp.dot(a, b, preferred_element_type=F32)


def _dot_nt(a, b):
    return lax.dot_general(a, b, (((1,), (1,)), ((), ())), preferred_element_type=F32)


def _swap_halves(z):
    lane = lax.broadcasted_iota(jnp.int32, z.shape, 1)
    lo = (lane & (HEAD_DIM - 1)) < HALF
    return jnp.where(lo, pltpu.roll(z, LANES - HALF, 1), pltpu.roll(z, HALF, 1))


def _rope_slab(z, cos, sin):
    parts = []
    for c in range(z.shape[1] // LANES):
        zc = z[:, c * LANES:(c + 1) * LANES]
        parts.append(zc * cos + _swap_halves(zc) * sin)
    return parts[0] if len(parts) == 1 else jnp.concatenate(parts, axis=1)


def _slabs(s, n):
    return [s[:, i * n:(i + 1) * n] for i in range(s.shape[1] // n)]


def _masked_softmax_keys(s, mask):
    out = []
    for x in _slabs(s, mask.shape[1]):
        x = jnp.where(mask, x, NEG)
        m = jnp.max(x, axis=0, keepdims=True)
        p = jnp.where(mask, jnp.exp2(x - m), 0.0)
        out.append(p / jnp.maximum(jnp.sum(p, axis=0, keepdims=True), TINY))
    return jnp.concatenate(out, axis=1)


def _mask_keys(s, mask):
    return jnp.concatenate([jnp.where(mask, x, NEG) for x in _slabs(s, mask.shape[1])], axis=1)


def _online_update(carry, s, vt):
    m, l, acc = carry
    m_new = jnp.maximum(m, jnp.max(s, axis=0, keepdims=True))
    alpha = jnp.exp2(m - m_new)
    p = jnp.exp2(s - m_new)
    l = alpha * l + jnp.sum(p, axis=0, keepdims=True)
    acc = alpha * acc + _dot(vt, p.astype(BF16))
    return m_new, l, acc


def _online_init(n):
    return jnp.full((1, n), NEG, F32), jnp.zeros((1, n), F32), jnp.zeros((HEAD_DIM, n), F32)


def _rank_count_step(v, row, cnt, j):
    r = v[j:j + 1, :]
    return cnt + jnp.where(r > v, 1.0, jnp.where(r == v, jnp.where(row > j, 1.0, 0.0), 0.0))


def _rank_select(v, n_candidates, k):
    row = lax.broadcasted_iota(jnp.int32, v.shape, 0)
    cnt = jnp.zeros(v.shape, F32)
    for j in range(n_candidates):
        cnt = _rank_count_step(v, row, cnt, j)
    return cnt < float(k)


def _ffn_kernel(x_ref, g_ref, wg_ref, wu_ref, wd_ref, fin_ref, o_ref, t_ref, *, final_norm):
    x = x_ref[...]
    h = (x * g_ref[...]).astype(BF16)
    inv = lax.rsqrt(jnp.mean(x * x, axis=-1, keepdims=True) + EPS)
    for f in range(D_FF // FF_CHUNK):
        c = slice(f * FF_CHUNK, (f + 1) * FF_CHUNK)
        a = _dot(h, wg_ref[:, c]) * inv
        b = _dot(h, wu_ref[:, c]) * inv
        t_ref[:, c] = (a * jax.nn.sigmoid(a) * b).astype(BF16)
    y = x + 0.5 * _dot(t_ref[...], wd_ref[...])
    if final_norm:
        y = _rms(y, fin_ref[...])
    o_ref[...] = y


def _resident(shape):
    nd = len(shape)
    return pl.BlockSpec(shape, lambda *_: (0,) * nd, pipeline_mode=pl.Buffered(1))


def _layer_resident(stacked, layer):
    return pl.BlockSpec((None,) + stacked.shape[1:], lambda *_: (layer, 0, 0), pipeline_mode=pl.Buffered(1))


def _ffn(x, layer, g, wg, wu, wd, fin, final_norm):
    t = x.shape[0]
    row = pl.BlockSpec((TOKEN_TILE, D_MODEL), lambda i: (i, 0))
    return pl.pallas_call(
        functools.partial(_ffn_kernel, final_norm=final_norm),
        grid=(t // TOKEN_TILE,),
        in_specs=[row, _layer_resident(g, layer), _layer_resident(wg, layer), _layer_resident(wu, layer),
                  _layer_resident(wd, layer), _resident((1, D_MODEL))],
        out_specs=row,
        out_shape=jax.ShapeDtypeStruct(x.shape, F32),
        scratch_shapes=[pltpu.VMEM((TOKEN_TILE, D_FF), BF16)],
        compiler_params=pltpu.CompilerParams(dimension_semantics=("parallel",),
                                             vmem_limit_bytes=VMEM_LIMIT),
        name="ffn",
    )(x, g, wg, wu, wd, fin)


def _proj_kernel(x_ref, g_ref, w_ref, cos_ref, sin_ref,
                 qat_ref, kc_ref, vc_ref, kk_ref, vvt_ref, qbt_ref, kb_ref, vbt_ref, gst_ref, kmean_ref):
    h = _rms(x_ref[...], g_ref[...]).astype(BF16)
    cos = cos_ref[...]
    sin = sin_ref[...]

    def mm(c0, n):
        return _dot(h, w_ref[:, c0:c0 + n])

    def feature_major(z, dtype=BF16):
        return jnp.transpose(z).astype(dtype)

    qat_ref[0] = feature_major(_rope_slab(mm(COL_QA, NSA_WIDTH), cos, sin) * Q_SCALE)
    kcvc = mm(COL_KC, 2 * KV_WIDTH)
    kc_ref[...] = kcvc[:, :KV_WIDTH]
    vc_ref[...] = kcvc[:, KV_WIDTH:]
    kv4 = mm(COL_KV4, 4 * KV_WIDTH)
    kk_ref[...] = jnp.concatenate(
        [_rope_slab(kv4[:, 0:KV_WIDTH], cos, sin), _rope_slab(kv4[:, 2 * KV_WIDTH:3 * KV_WIDTH], cos, sin)],
        axis=1).astype(BF16)
    vvt_ref[0] = feature_major(jnp.concatenate([kv4[:, KV_WIDTH:2 * KV_WIDTH], kv4[:, 3 * KV_WIDTH:]], axis=1))
    qbt_ref[0] = feature_major(_rope_slab(mm(COL_QB, MOBA_WIDTH), cos, sin) * Q_SCALE)
    kb = _rope_slab(mm(COL_KB, MOBA_WIDTH), cos, sin)
    kb_ref[...] = kb.astype(BF16)
    means = [jnp.mean(kb[i * MOBA_BLOCK:(i + 1) * MOBA_BLOCK], axis=0, keepdims=True)
             for i in range(TOKEN_TILE // MOBA_BLOCK)]
    kmean_ref[0] = jnp.concatenate(means, axis=0)
    vbt_ref[0] = feature_major(mm(COL_VB, MOBA_WIDTH))
    gst_ref[0] = feature_major(jax.nn.sigmoid(mm(COL_GS, LANES)), F32)


def _split_w_in(w_in):
    off_ga = NSA_WIDTH
    off_rest = off_ga + N_GATE_COLS
    off_gates = off_rest + 6 * KV_WIDTH + 3 * MOBA_WIDTH
    pad = jnp.zeros(w_in.shape[:2] + (LANES - N_GATE_COLS,), w_in.dtype)
    attn = jnp.concatenate([w_in[..., :off_ga], w_in[..., off_rest:off_gates], w_in[..., off_ga:off_rest], pad],
                           axis=-1)
    return attn.astype(BF16), w_in[..., off_gates:].astype(BF16)


def _proj(x, layer, g, w, cos_t, sin_t, seq):
    t = x.shape[0]
    tiles_per_seq = seq // TOKEN_TILE
    blocks_per_tile = TOKEN_TILE // MOBA_BLOCK

    def row(n):
        return pl.BlockSpec((TOKEN_TILE, n), lambda i: (i, 0))

    def token_major(n, dtype=BF16):
        return jax.ShapeDtypeStruct((t, n), dtype), row(n)

    def feature_major(n, dtype=BF16):
        return (jax.ShapeDtypeStruct((t // seq, n, seq), dtype),
                pl.BlockSpec((1, n, TOKEN_TILE), lambda i: (i // tiles_per_seq, 0, i % tiles_per_seq)))

    tab = pl.BlockSpec((TOKEN_TILE, LANES), lambda i: (i % tiles_per_seq, 0))
    outs = (feature_major(NSA_WIDTH), token_major(KV_WIDTH, F32), token_major(KV_WIDTH, F32),
            token_major(2 * KV_WIDTH),
            feature_major(2 * KV_WIDTH), feature_major(MOBA_WIDTH), token_major(MOBA_WIDTH),
            feature_major(MOBA_WIDTH), feature_major(LANES, F32),
            (jax.ShapeDtypeStruct((t // TOKEN_TILE, blocks_per_tile, MOBA_WIDTH), F32),
             pl.BlockSpec((1, blocks_per_tile, MOBA_WIDTH), lambda i: (i, 0, 0))))
    out_shape, out_specs = zip(*outs)
    return pl.pallas_call(
        _proj_kernel,
        grid=(t // TOKEN_TILE,),
        in_specs=[row(D_MODEL), _layer_resident(g, layer), _layer_resident(w, layer), tab, tab],
        out_specs=out_specs,
        out_shape=out_shape,
        compiler_params=pltpu.CompilerParams(dimension_semantics=("parallel",),
                                             vmem_limit_bytes=VMEM_LIMIT),
        name="proj",
    )(x, g, w, cos_t, sin_t)


def _gelu_tanh(x):
    return 0.5 * x * (1.0 + jnp.tanh(np.sqrt(2.0 / np.pi).astype(np.float32) * (x + 0.044715 * (x * x * x))))


def _compress_kernel(kc_ref, vc_ref, kw1_ref, kw1f_ref, kpos_ref, kw2_ref,
                     vw1_ref, vw1f_ref, vpos_ref, vw2_ref, cos_ref, sin_ref, kcb_ref, vcb_ref):
    n_chunks = kc_ref.shape[1] // CMP_STRIDE

    def compress(x_ref, w1_ref, w1f_ref, pos_ref, w2_ref):
        u = jnp.zeros((n_chunks, 4 * CMP_HIDDEN), F32)
        for p in range(CMP_STRIDE):
            rows = x_ref[0, pl.ds(p, n_chunks, stride=CMP_STRIDE), :].astype(BF16)
            u = u + _dot(rows, w1_ref[p * KV_WIDTH:(p + 1) * KV_WIDTH, :])
        first = u[:, :2 * CMP_HIDDEN]
        second = pltpu.roll(u[:, 2 * CMP_HIDDEN:], n_chunks - 1, 0)
        pos = jnp.broadcast_to(pos_ref[...], (8, CMP_LEN * HEAD_DIM)).astype(BF16)
        bias = _dot(pos, w1f_ref[...])[0:1]
        pre = first + second + jnp.concatenate([bias, bias], axis=1)
        return _dot(_gelu_tanh(pre).astype(BF16), w2_ref[...])

    k = compress(kc_ref, kw1_ref, kw1f_ref, kpos_ref, kw2_ref)
    kcb_ref[0] = _rope_slab(k, cos_ref[...], sin_ref[...]).astype(BF16)
    vcb_ref[0] = compress(vc_ref, vw1_ref, vw1f_ref, vpos_ref, vw2_ref).astype(BF16)


def _compress_weights(pos, w1, w2):
    w1r = w1.reshape(2, CMP_STRIDE, HEAD_DIM, CMP_HIDDEN)
    eye = jnp.eye(NSA_GROUPS, dtype=w1.dtype)
    big = jnp.einsum('hpdj,ge->pgdhej', w1r, eye)
    big = big.reshape(CMP_STRIDE * KV_WIDTH, 2 * NSA_GROUPS * CMP_HIDDEN).astype(BF16)
    w2big = jnp.einsum('jd,ge->gjed', w2, eye).reshape(NSA_GROUPS * CMP_HIDDEN, KV_WIDTH).astype(BF16)
    return big, w1.astype(BF16), pos.reshape(1, CMP_LEN * HEAD_DIM), w2big


def _compress(kc, vc, kparams, vparams, cos_c, sin_c):
    b, seq, width = kc.shape
    n_chunks = seq // CMP_STRIDE
    args = (kc, vc) + _compress_weights(*kparams) + _compress_weights(*vparams) + (cos_c, sin_c)
    x_spec = pl.BlockSpec((1, seq, width), lambda i: (i, 0, 0))
    in_specs = [x_spec, x_spec] + [_resident(a.shape) for a in args[2:]]
    o_spec = pl.BlockSpec((1, n_chunks, KV_WIDTH), lambda i: (i, 0, 0))
    o_shape = jax.ShapeDtypeStruct((b, n_chunks, KV_WIDTH), BF16)
    return pl.pallas_call(
        _compress_kernel,
        grid=(b,),
        in_specs=in_specs,
        out_specs=(o_spec, o_spec),
        out_shape=(o_shape, o_shape),
        compiler_params=pltpu.CompilerParams(dimension_semantics=("parallel",),
                                             vmem_limit_bytes=VMEM_LIMIT),
        name="compress",
    )(*args)


def _nsa_kernel(qat_ref, gst_ref, kcb_ref, vcbt_ref, kk_ref, vst_ref, vwt_ref, ovt_ref, blk_ref, o_ref,
                s_ref, m_ref, l_ref, acc_ref, *, seq):
    qi = pl.program_id(1)
    t0 = qi * NSA_Q
    cols = NSA_REP * NSA_Q
    n_cmp = seq // CMP_STRIDE - 1
    n_sel = seq // SEL_BLOCK
    gst = gst_ref[0]
    ovt = ovt_ref[...]

    def key_and_query_pos(n, k0):
        shape = (n, NSA_Q)
        return k0 + lax.broadcasted_iota(jnp.int32, shape, 0), t0 + lax.broadcasted_iota(jnp.int32, shape, 1)

    groups = range(NSA_GROUPS)
    d_rows = [slice(g * HEAD_DIM, (g + 1) * HEAD_DIM) for g in groups]
    g_cols = [slice(g * cols, (g + 1) * cols) for g in groups]

    zeros_d = jnp.zeros((HEAD_DIM, NSA_Q), BF16)
    heads = []
    for n in range(NSA_HEADS):
        x = qat_ref[0, n * HEAD_DIM:(n + 1) * HEAD_DIM, :]
        heads.append(jnp.concatenate([x, zeros_d] if n < NSA_REP else [zeros_d, x], axis=0))
    q = jnp.concatenate(heads, axis=1)

    n_win = WINDOW + NSA_Q
    w0 = pl.multiple_of(jnp.maximum(t0 - WINDOW, 0), NSA_Q)
    s = _dot(kcb_ref[0], q)
    k_win = kk_ref[0, pl.ds(w0, n_win), KV_WIDTH:2 * KV_WIDTH]
    s_win_0 = _dot(k_win, q[:, g_cols[0]])

    n_idx, tq = key_and_query_pos(s.shape[0], 0)
    cmask = (n_idx * CMP_STRIDE + (CMP_LEN - 1) <= tq) & (n_idx < n_cmp)
    p_cmp = _masked_softmax_keys(s, cmask).astype(BF16)
    o_cmp = [_dot(vcbt_ref[0, d_rows[g], :], p_cmp[:, g_cols[g]]) for g in groups]

    pi = _dot(ovt, p_cmp)
    s_win = jnp.concatenate([s_win_0, _dot(k_win, q[:, g_cols[1]])], axis=1)
    jj, tsel = key_and_query_pos(n_sel, 0)
    tblk = tsel >> 6
    bonus = jnp.where((jj == 0) | (jj == tblk) | (jj == tblk - 1), FORCE_BONUS, 0.0)
    pad = jnp.zeros((LANES - n_sel, NSA_Q), BF16)
    biases = []
    for g in groups:
        imp = pi[:, g * cols:g * cols + NSA_Q]
        for r in range(1, NSA_REP):
            imp = imp + pi[:, g * cols + r * NSA_Q:g * cols + (r + 1) * NSA_Q]
        imp = jnp.where(jj <= tblk, imp + bonus, NEG)
        bias = jnp.where(_rank_select(imp, n_sel, SEL_TOPK) & (jj <= tblk), 0.0, NEG)
        biases.append(jnp.concatenate([bias.astype(BF16), pad], axis=0))
    q_sel = jnp.concatenate(
        [jnp.concatenate([heads[n], biases[n // NSA_REP]], axis=0) for n in range(NSA_HEADS)], axis=1)

    def tile_start(c):
        return pl.multiple_of(c * SEL_KEYS, SEL_KEYS)

    def sel_scores(c):
        k0 = tile_start(c)
        keys = jnp.concatenate([kk_ref[0, pl.ds(k0, SEL_KEYS), 0:KV_WIDTH], blk_ref[pl.ds(k0, SEL_KEYS), :]],
                               axis=1)
        return _dot(keys, q_sel)

    def sel_update(c, s):
        for g in groups:
            m, l, acc = _online_update((m_ref[g], l_ref[g], acc_ref[g]), s[:, g_cols[g]],
                                       vst_ref[0, d_rows[g], pl.ds(tile_start(c), SEL_KEYS)])
            m_ref[g], l_ref[g], acc_ref[g] = m, l, acc

    n_past = qi // (SEL_KEYS // NSA_Q)
    m_ref[...] = jnp.full(m_ref.shape, NEG, F32)
    l_ref[...] = jnp.zeros(l_ref.shape, F32)
    acc_ref[...] = jnp.zeros(acc_ref.shape, F32)
    s_ref[0] = sel_scores(0)

    kpos, tq = key_and_query_pos(n_win, w0)
    s_win = _mask_keys(s_win, (kpos <= tq) & (tq - kpos < WINDOW))
    p_win = jnp.exp2(s_win - jnp.max(s_win, axis=0, keepdims=True))
    l_win = jnp.maximum(jnp.sum(p_win, axis=0, keepdims=True), TINY)
    p_win = p_win.astype(BF16)
    o_win = [_dot(vwt_ref[0, d_rows[g], pl.ds(w0, n_win)], p_win[:, g_cols[g]]) / l_win[:, g_cols[g]]
             for g in groups]

    for c in range(seq // SEL_KEYS - 1):
        @pl.when(c < n_past)
        def _():
            s_ref[(c + 1) % 2] = sel_scores(c + 1)
            sel_update(c, s_ref[c % 2])

    kpos, tq = key_and_query_pos(SEL_KEYS, tile_start(n_past))
    sel_update(n_past, _mask_keys(s_ref[n_past % 2], kpos <= tq))
    o_sel = [acc_ref[g] / jnp.maximum(l_ref[g], TINY) for g in groups]

    outs = []
    for n in range(NSA_HEADS):
        g, r = divmod(n, NSA_REP)
        sl = slice(r * NSA_Q, (r + 1) * NSA_Q)
        outs.append(gst[3 * n:3 * n + 1, :] * o_cmp[g][:, sl] + gst[3 * n + 1:3 * n + 2, :] * o_sel[g][:, sl]
                    + gst[3 * n + 2:3 * n + 3, :] * o_win[g][:, sl])
    o_ref[0] = jnp.transpose(jnp.concatenate(outs, axis=0)).astype(BF16)


def _nsa(qat, gst, kcb, vcbt, kk, vvt, ovt, blk):
    b, _, seq = qat.shape
    n_chunks = kcb.shape[1]
    return pl.pallas_call(
        functools.partial(_nsa_kernel, seq=seq),
        grid=(b, seq // NSA_Q),
        in_specs=[pl.BlockSpec((1, NSA_WIDTH, NSA_Q), lambda i, j: (i, 0, j)),
                  pl.BlockSpec((1, LANES, NSA_Q), lambda i, j: (i, 0, j)),
                  pl.BlockSpec((1, n_chunks, KV_WIDTH), lambda i, j: (i, 0, 0)),
                  pl.BlockSpec((1, KV_WIDTH, n_chunks), lambda i, j: (i, 0, 0)),
                  pl.BlockSpec((1, seq, 2 * KV_WIDTH), lambda i, j: (i, 0, 0)),
                  pl.BlockSpec((1, KV_WIDTH, seq), lambda i, j: (i, 0, 0)),
                  pl.BlockSpec((1, KV_WIDTH, seq), lambda i, j: (i, 1, 0)),
                  pl.BlockSpec(ovt.shape, lambda i, j: (0, 0)),
                  pl.BlockSpec(blk.shape, lambda i, j: (0, 0))],
        out_specs=pl.BlockSpec((1, NSA_Q, NSA_WIDTH), lambda i, j: (i, j, 0)),
        out_shape=jax.ShapeDtypeStruct((b, seq, NSA_WIDTH), BF16),
        scratch_shapes=[pltpu.VMEM((2, SEL_KEYS, NSA_HEADS * NSA_Q), F32),
                        pltpu.VMEM((NSA_GROUPS, 1, NSA_REP * NSA_Q), F32),
                        pltpu.VMEM((NSA_GROUPS, 1, NSA_REP * NSA_Q), F32),
                        pltpu.VMEM((NSA_GROUPS, HEAD_DIM, NSA_REP * NSA_Q), F32)],
        compiler_params=pltpu.CompilerParams(dimension_semantics=("parallel", "parallel"),
                                             vmem_limit_bytes=VMEM_LIMIT),
        name="nsa",
    )(qat, gst, kcb, vcbt, kk, vvt, vvt, ovt, blk)


def _moba_kernel(qt_ref, k_ref, vt_ref, kmean_ref, o_ref, s_ref, m_ref, l_ref, acc_ref):
    c = pl.program_id(1)
    nb = kmean_ref.shape[1]
    k_own = pl.multiple_of(c * MOBA_BLOCK, MOBA_BLOCK)
    jj = lax.broadcasted_iota(jnp.int32, (BF16_ROWS, MOBA_BLOCK), 0)
    past = jj < c
    pad = jnp.zeros((BF16_ROWS - nb, LANES), F32)
    zeros_d = jnp.zeros((HEAD_DIM, MOBA_BLOCK), BF16)

    def pair(h):
        return slice((h // 2) * LANES, (h // 2 + 1) * LANES)

    qs, gscs = [], []
    for h in range(MOBA_HEADS):
        x = qt_ref[0, h * HEAD_DIM:(h + 1) * HEAD_DIM, :]
        q = jnp.concatenate([x, zeros_d] if h % 2 == 0 else [zeros_d, x], axis=0)
        km = jnp.concatenate([kmean_ref[0, :, pair(h)], pad], axis=0).astype(BF16)
        qs.append(q)
        gscs.append(_dot(km, q))
    chosen = [jnp.where(_rank_select(jnp.where(past, g, NEG), nb, MOBA_TOPK) & past, 1.0, 0.0) for g in gscs]

    m_ref[...] = jnp.full(m_ref.shape, NEG, F32)
    l_ref[...] = jnp.zeros(l_ref.shape, F32)
    acc_ref[...] = jnp.zeros(acc_ref.shape, F32)

    def block_start(j):
        return pl.multiple_of(j * MOBA_BLOCK, MOBA_BLOCK)

    def scores(j, slot):
        for h in range(MOBA_HEADS):
            s_ref[slot, h] = _dot(k_ref[0, pl.ds(block_start(j), MOBA_BLOCK), pair(h)], qs[h])

    def past_update(j, slot):
        for h in range(MOBA_HEADS):
            rows = slice(h * HEAD_DIM, (h + 1) * HEAD_DIM)
            s = s_ref[slot, h]
            pick = jnp.sum(jnp.where(jj == j, chosen[h], 0.0), axis=0, keepdims=True) > 0.5
            m = m_ref[h]
            m_new = jnp.maximum(m, jnp.where(pick, jnp.max(s, axis=0, keepdims=True), NEG))
            alpha = jnp.exp2(m - m_new)
            p = jnp.exp2(s - jnp.where(pick, m_new, jnp.inf))
            m_ref[h] = m_new
            l_ref[h] = alpha * l_ref[h] + jnp.sum(p, axis=0, keepdims=True)
            acc_ref[rows, :] = alpha * acc_ref[rows, :] + _dot(
                vt_ref[0, rows, pl.ds(block_start(j), MOBA_BLOCK)], p.astype(BF16))

    scores(0, 0)

    def two_blocks(i, _):
        j = 2 * i
        scores(j + 1, 1)
        past_update(j, 0)
        scores(j + 2, 0)
        past_update(j + 1, 1)
        return 0

    lax.fori_loop(0, c // 2, two_blocks, 0)

    @pl.when(c % 2 == 1)
    def _():
        scores(c, 1)
        past_update(c - 1, 0)

    kq = (MOBA_BLOCK, MOBA_BLOCK)
    causal = lax.broadcasted_iota(jnp.int32, kq, 0) <= lax.broadcasted_iota(jnp.int32, kq, 1)
    for h in range(MOBA_HEADS):
        rows = slice(h * HEAD_DIM, (h + 1) * HEAD_DIM)
        carry = (m_ref[h], l_ref[h], acc_ref[rows, :])
        _, l, acc = _online_update(carry, jnp.where(causal, s_ref[c % 2, h], NEG),
                                   vt_ref[0, rows, pl.ds(k_own, MOBA_BLOCK)])
        acc_ref[rows, :] = acc / jnp.maximum(l, TINY)
    o_ref[0] = jnp.transpose(acc_ref[...]).astype(BF16)


def _moba(qbt, kb, vbt, kmean):
    b, _, seq = qbt.shape
    nb = seq // MOBA_BLOCK
    return pl.pallas_call(
        _moba_kernel,
        grid=(b, nb),
        in_specs=[pl.BlockSpec((1, MOBA_WIDTH, MOBA_BLOCK), lambda i, j: (i, 0, j)),
                  pl.BlockSpec((1, seq, MOBA_WIDTH), lambda i, j: (i, 0, 0)),
                  pl.BlockSpec((1, MOBA_WIDTH, seq), lambda i, j: (i, 0, 0)),
                  pl.BlockSpec((1, nb, MOBA_WIDTH), lambda i, j: (i, 0, 0))],
        out_specs=pl.BlockSpec((1, MOBA_BLOCK, MOBA_WIDTH), lambda i, j: (i, j, 0)),
        out_shape=jax.ShapeDtypeStruct((b, seq, MOBA_WIDTH), BF16),
        scratch_shapes=[pltpu.VMEM((2, MOBA_HEADS, MOBA_BLOCK, MOBA_BLOCK), F32),
                        pltpu.VMEM((MOBA_HEADS, 1, MOBA_BLOCK), F32), pltpu.VMEM((MOBA_HEADS, 1, MOBA_BLOCK), F32),
                        pltpu.VMEM((MOBA_WIDTH, MOBA_BLOCK), F32)],
        compiler_params=pltpu.CompilerParams(dimension_semantics=("parallel", "parallel"),
                                             vmem_limit_bytes=VMEM_LIMIT),
        name="moba",
    )(qbt, kb, vbt, kmean)


def _merge_kernel(x_ref, g_ref, a_ref, b_ref, wgate_ref, pa_ref, pb_ref, wo_ref, o_ref):
    x = x_ref[...]
    h = _rms(x, g_ref[...]).astype(BF16)
    ya = _dot(a_ref[...], pa_ref[...])
    yb = _dot(b_ref[...], pb_ref[...])
    merged = (jax.nn.sigmoid(_dot(h, wgate_ref[:, :D_MODEL])) * ya
              + jax.nn.sigmoid(_dot(h, wgate_ref[:, D_MODEL:])) * yb)
    o_ref[...] = x + _dot(merged.astype(BF16), wo_ref[...])


def _merge(x, layer, g, a, bm, wgate, pa, pb, wo):
    t = x.shape[0]

    def row(n):
        return pl.BlockSpec((TOKEN_TILE, n), lambda i: (i, 0))

    return pl.pallas_call(
        _merge_kernel,
        grid=(t // TOKEN_TILE,),
        in_specs=[row(D_MODEL), _layer_resident(g, layer), row(NSA_WIDTH), row(MOBA_WIDTH)]
                 + [_layer_resident(w, layer) for w in (wgate, pa, pb, wo)],
        out_specs=row(D_MODEL),
        out_shape=jax.ShapeDtypeStruct(x.shape, F32),
        compiler_params=pltpu.CompilerParams(dimension_semantics=("parallel",),
                                             vmem_limit_bytes=VMEM_LIMIT),
        name="merge",
    )(x, g, a, bm, wgate, pa, pb, wo)


def _rope_tables(pos):
    inv = ROPE_THETA ** (-jnp.arange(0, HEAD_DIM, 2, dtype=F32) / HEAD_DIM)
    ang = pos.astype(F32)[:, None] * inv[None, :]
    cos, sin = jnp.cos(ang), jnp.sin(ang)
    reps = LANES // HEAD_DIM
    return (jnp.tile(jnp.concatenate([cos, cos], axis=1), (1, reps)),
            jnp.tile(jnp.concatenate([-sin, sin], axis=1), (1, reps)))


def _overlap_matrix_t(seq):
    n_chunks = seq // CMP_STRIDE
    ci = np.arange(n_chunks)[None, :] * CMP_STRIDE
    sj = np.arange(seq // SEL_BLOCK)[:, None] * SEL_BLOCK
    ov = (ci < sj + SEL_BLOCK) & (ci + CMP_LEN > sj) & (np.arange(n_chunks)[None, :] < n_chunks - 1)
    return jnp.asarray(ov, dtype=BF16)


def _block_onehot(seq):
    return jnp.asarray(np.arange(seq)[:, None] // SEL_BLOCK == np.arange(LANES)[None, :], dtype=BF16)


def kernel(x, ffn1_norm, ffn1_wg, ffn1_wu, ffn1_wd, mix_norm, w_in, cmpk_pos, cmpk_w1, cmpk_w2, cmpv_pos, cmpv_w1, cmpv_w2, w_branch_nsa, w_branch_moba, w_out, ffn2_norm, ffn2_wg, ffn2_wu, ffn2_wd, final_norm):
    b, seq, d = x.shape
    depth = w_in.shape[0]
    assert d == D_MODEL and seq % TOKEN_TILE == 0 and seq // CMP_STRIDE == LANES
    t = b * seq
    n_chunks = seq // CMP_STRIDE
    cos_t, sin_t = _rope_tables(jnp.arange(seq))
    cos_c, sin_c = _rope_tables(jnp.arange(n_chunks) * CMP_STRIDE + (CMP_LEN - 1))
    ovt = _overlap_matrix_t(seq)
    blk = _block_onehot(seq)

    def gain(gs):
        return gs.reshape(depth, 1, d)

    ffn1 = (gain(ffn1_norm), ffn1_wg.astype(BF16), ffn1_wu.astype(BF16), ffn1_wd.astype(BF16))
    ffn2 = (gain(ffn2_norm), ffn2_wg.astype(BF16), ffn2_wu.astype(BF16), ffn2_wd.astype(BF16))
    w_attn, w_gate = _split_w_in(w_in)
    w_merge = (w_gate, w_branch_nsa.astype(BF16), w_branch_moba.astype(BF16), w_out.astype(BF16))
    fin = final_norm.reshape(1, d)

    xf = x.reshape(t, d)
    for l in range(depth):
        xf = _ffn(xf, l, *ffn1, fin, False)
        x_mix = xf
        qat, kc, vc, kk, vvt, qbt, kb, vbt, gst, kmean = _proj(xf, l, gain(mix_norm), w_attn, cos_t, sin_t, seq)
        kcb, vcb = _compress(kc.reshape(b, seq, KV_WIDTH), vc.reshape(b, seq, KV_WIDTH),
                             (cmpk_pos[l], cmpk_w1[l], cmpk_w2[l]),
                             (cmpv_pos[l], cmpv_w1[l], cmpv_w2[l]), cos_c, sin_c)
        a = _nsa(qat, gst, kcb, vcb.transpose(0, 2, 1), kk.reshape(b, seq, 2 * KV_WIDTH), vvt, ovt, blk)
        bm = _moba(qbt, kb.reshape(b, seq, MOBA_WIDTH), vbt, kmean.reshape(b, seq // MOBA_BLOCK, MOBA_WIDTH))
        xf = _merge(x_mix, l, gain(mix_norm), a.reshape(t, NSA_WIDTH), bm.reshape(t, MOBA_WIDTH), *w_merge)
        xf = _ffn(xf, l, *ffn2, fin, l == depth - 1)
    return xf.reshape(b, seq, d)
```

```python
import functools

import jax
import jax.numpy as jnp
import numpy as np
from jax import lax
from jax.experimental import pallas as pl
from jax.experimental.pallas import tpu as pltpu

F32 = jnp.float32
BF16 = jnp.bfloat16

D_MODEL = 1024
HEAD_DIM = 64
HALF = HEAD_DIM // 2
NSA_HEADS = 8
NSA_GROUPS = 2
NSA_REP = NSA_HEADS // NSA_GROUPS
CMP_LEN = 32
CMP_STRIDE = 16
CMP_HIDDEN = 128
SEL_BLOCK = 64
SEL_TOPK = 16
WINDOW = 512
MOBA_HEADS = 8
MOBA_BLOCK = 256
MOBA_TOPK = 3
D_FF = 2816
ROPE_THETA = 10000.0
EPS = 1e-6
NEG = -1e30
TINY = 1e-30
FORCE_BONUS = 1e4
Q_SCALE = HEAD_DIM ** -0.5 * float(np.log2(np.e))

NSA_WIDTH = NSA_HEADS * HEAD_DIM
KV_WIDTH = NSA_GROUPS * HEAD_DIM
MOBA_WIDTH = MOBA_HEADS * HEAD_DIM
N_GATE_COLS = 3 * NSA_HEADS

LANES = 128
BF16_ROWS = 16
TOKEN_TILE = 512
FF_CHUNK = 256
NSA_Q = 256
SEL_KEYS = 512
VMEM_LIMIT = 56 * 1024 * 1024

COL_QA = 0
COL_KC = COL_QA + NSA_WIDTH
COL_VC = COL_KC + KV_WIDTH
COL_KV4 = COL_VC + KV_WIDTH
COL_QB = COL_KV4 + 4 * KV_WIDTH
COL_KB = COL_QB + MOBA_WIDTH
COL_VB = COL_KB + MOBA_WIDTH
COL_GS = COL_VB + MOBA_WIDTH


def _rms(x, g):
    return x * lax.rsqrt(jnp.mean(x * x, axis=-1, keepdims=True) + EPS) * g


def _dot(a, b):
    return jnp.dot(a, b, preferred_element_type=F32)


def _dot_nt(a, b):
    return lax.dot_general(a, b, (((1,), (1,)), ((), ())), preferred_element_type=F32)


def _swap_halves(z):
    lane = lax.broadcasted_iota(jnp.int32, z.shape, 1)
    lo = (lane & (HEAD_DIM - 1)) < HALF
    return jnp.where(lo, pltpu.roll(z, LANES - HALF, 1), pltpu.roll(z, HALF, 1))


def _rope_slab(z, cos, sin):
    parts = []
    for c in range(z.shape[1] // LANES):
        zc = z[:, c * LANES:(c + 1) * LANES]
        parts.append(zc * cos + _swap_halves(zc) * sin)
    return parts[0] if len(parts) == 1 else jnp.concatenate(parts, axis=1)


def _slabs(s, n):
    return [s[:, i * n:(i + 1) * n] for i in range(s.shape[1] // n)]


def _masked_softmax_keys(s, mask):
    out = []
    for x in _slabs(s, mask.shape[1]):
        x = jnp.where(mask, x, NEG)
        m = jnp.max(x, axis=0, keepdims=True)
        p = jnp.where(mask, jnp.exp2(x - m), 0.0)
        out.append(p / jnp.maximum(jnp.sum(p, axis=0, keepdims=True), TINY))
    return jnp.concatenate(out, axis=1)


def _mask_keys(s, mask):
    return jnp.concatenate([jnp.where(mask, x, NEG) for x in _slabs(s, mask.shape[1])], axis=1)


def _online_update(carry, s, vt):
    m, l, acc = carry
    m_new = jnp.maximum(m, jnp.max(s, axis=0, keepdims=True))
    alpha = jnp.exp2(m - m_new)
    p = jnp.exp2(s - m_new).astype(BF16)
    ones_rows = jnp.where(lax.broadcasted_iota(jnp.int32, (BF16_ROWS, vt.shape[1]), 0) == 0, 1.0, 0.0).astype(BF16)
    pv = _dot(jnp.concatenate([vt, ones_rows], axis=0), p)
    return m_new, alpha * l + pv[HEAD_DIM:HEAD_DIM + 1], alpha * acc + pv[:HEAD_DIM]


def _online_init(n):
    return jnp.full((1, n), NEG, F32), jnp.zeros((1, n), F32), jnp.zeros((HEAD_DIM, n), F32)


def _rank_count_step(v, row, cnt, j):
    r = v[j:j + 1, :]
    return cnt + jnp.where(r > v, 1.0, jnp.where(r == v, jnp.where(row > j, 1.0, 0.0), 0.0))


def _rank_select(v, n_candidates, k):
    row = lax.broadcasted_iota(jnp.int32, v.shape, 0)
    cnt = jnp.zeros(v.shape, F32)
    for j in range(n_candidates):
        cnt = _rank_count_step(v, row, cnt, j)
    return cnt < float(k)


def _ffn_kernel(x_ref, g_ref, wg_ref, wu_ref, wd_ref, fin_ref, o_ref, t_ref, *, final_norm):
    x = x_ref[...]
    h = (x * g_ref[...]).astype(BF16)
    inv = lax.rsqrt(jnp.mean(x * x, axis=-1, keepdims=True) + EPS)
    for f in range(D_FF // FF_CHUNK):
        c = slice(f * FF_CHUNK, (f + 1) * FF_CHUNK)
        a = _dot(h, wg_ref[:, c]) * inv
        b = _dot(h, wu_ref[:, c]) * inv
        t_ref[:, c] = (a * jax.nn.sigmoid(a) * b).astype(BF16)
    y = x + 0.5 * _dot(t_ref[...], wd_ref[...])
    if final_norm:
        y = _rms(y, fin_ref[...])
    o_ref[...] = y


def _resident(shape):
    nd = len(shape)
    return pl.BlockSpec(shape, lambda *_: (0,) * nd, pipeline_mode=pl.Buffered(1))


def _layer_resident(stacked, layer):
    return pl.BlockSpec((None,) + stacked.shape[1:], lambda *_: (layer, 0, 0), pipeline_mode=pl.Buffered(1))


def _ffn(x, layer, g, wg, wu, wd, fin, final_norm):
    t = x.shape[0]
    row = pl.BlockSpec((TOKEN_TILE, D_MODEL), lambda i: (i, 0))
    return pl.pallas_call(
        functools.partial(_ffn_kernel, final_norm=final_norm),
        grid=(t // TOKEN_TILE,),
        in_specs=[row, _layer_resident(g, layer), _layer_resident(wg, layer), _layer_resident(wu, layer),
                  _layer_resident(wd, layer), _resident((1, D_MODEL))],
        out_specs=row,
        out_shape=jax.ShapeDtypeStruct(x.shape, F32),
        scratch_shapes=[pltpu.VMEM((TOKEN_TILE, D_FF), BF16)],
        compiler_params=pltpu.CompilerParams(dimension_semantics=("parallel",),
                                             vmem_limit_bytes=VMEM_LIMIT),
        name="ffn",
    )(x, g, wg, wu, wd, fin)


def _proj_kernel(x_ref, g_ref, w_ref, cos_ref, sin_ref,
                 qat_ref, kc_ref, vc_ref, kk_ref, vvt_ref, qbt_ref, kb_ref, vbt_ref, gst_ref, kmean_ref):
    h = _rms(x_ref[...], g_ref[...]).astype(BF16)
    cos = cos_ref[...]
    sin = sin_ref[...]

    def mm(c0, n):
        return _dot(h, w_ref[:, c0:c0 + n])

    def feature_major(z, dtype=BF16):
        return jnp.transpose(z).astype(dtype)

    qat_ref[0] = feature_major(_rope_slab(mm(COL_QA, NSA_WIDTH), cos, sin) * Q_SCALE)
    kcvc = mm(COL_KC, 2 * KV_WIDTH)
    kc_ref[...] = kcvc[:, :KV_WIDTH]
    vc_ref[...] = kcvc[:, KV_WIDTH:]
    kv4 = mm(COL_KV4, 4 * KV_WIDTH)
    kk_ref[...] = jnp.concatenate(
        [_rope_slab(kv4[:, 0:KV_WIDTH], cos, sin), _rope_slab(kv4[:, 2 * KV_WIDTH:3 * KV_WIDTH], cos, sin)],
        axis=1).astype(BF16)
    vvt_ref[0] = feature_major(jnp.concatenate([kv4[:, KV_WIDTH:2 * KV_WIDTH], kv4[:, 3 * KV_WIDTH:]], axis=1))
    qbt_ref[0] = feature_major(_rope_slab(mm(COL_QB, MOBA_WIDTH), cos, sin) * Q_SCALE)
    kb = _rope_slab(mm(COL_KB, MOBA_WIDTH), cos, sin)
    kb_ref[...] = kb.astype(BF16)
    means = [jnp.mean(kb[i * MOBA_BLOCK:(i + 1) * MOBA_BLOCK], axis=0, keepdims=True)
             for i in range(TOKEN_TILE // MOBA_BLOCK)]
    kmean_ref[0] = jnp.concatenate(means, axis=0)
    vbt_ref[0] = feature_major(mm(COL_VB, MOBA_WIDTH))
    gst_ref[0] = feature_major(jax.nn.sigmoid(mm(COL_GS, LANES)), F32)


def _split_w_in(w_in):
    off_ga = NSA_WIDTH
    off_rest = off_ga + N_GATE_COLS
    off_gates = off_rest + 6 * KV_WIDTH + 3 * MOBA_WIDTH
    pad = jnp.zeros(w_in.shape[:2] + (LANES - N_GATE_COLS,), w_in.dtype)
    attn = jnp.concatenate([w_in[..., :off_ga], w_in[..., off_rest:off_gates], w_in[..., off_ga:off_rest], pad],
                           axis=-1)
    return attn.astype(BF16), w_in[..., off_gates:].astype(BF16)


def _proj(x, layer, g, w, cos_t, sin_t, seq):
    t = x.shape[0]
    tiles_per_seq = seq // TOKEN_TILE
    blocks_per_tile = TOKEN_TILE // MOBA_BLOCK

    def row(n):
        return pl.BlockSpec((TOKEN_TILE, n), lambda i: (i, 0))

    def token_major(n, dtype=BF16):
        return jax.ShapeDtypeStruct((t, n), dtype), row(n)

    def feature_major(n, dtype=BF16):
        return (jax.ShapeDtypeStruct((t // seq, n, seq), dtype),
                pl.BlockSpec((1, n, TOKEN_TILE), lambda i: (i // tiles_per_seq, 0, i % tiles_per_seq)))

    tab = pl.BlockSpec((TOKEN_TILE, LANES), lambda i: (i % tiles_per_seq, 0))
    outs = (feature_major(NSA_WIDTH), token_major(KV_WIDTH, F32), token_major(KV_WIDTH, F32),
            token_major(2 * KV_WIDTH),
            feature_major(2 * KV_WIDTH), feature_major(MOBA_WIDTH), token_major(MOBA_WIDTH),
            feature_major(MOBA_WIDTH), feature_major(LANES, F32),
            (jax.ShapeDtypeStruct((t // TOKEN_TILE, blocks_per_tile, MOBA_WIDTH), F32),
             pl.BlockSpec((1, blocks_per_tile, MOBA_WIDTH), lambda i: (i, 0, 0))))
    out_shape, out_specs = zip(*outs)
    return pl.pallas_call(
        _proj_kernel,
        grid=(t // TOKEN_TILE,),
        in_specs=[row(D_MODEL), _layer_resident(g, layer), _layer_resident(w, layer), tab, tab],
        out_specs=out_specs,
        out_shape=out_shape,
        compiler_params=pltpu.CompilerParams(dimension_semantics=("parallel",),
                                             vmem_limit_bytes=VMEM_LIMIT),
        name="proj",
    )(x, g, w, cos_t, sin_t)


def _gelu_tanh(x):
    return 0.5 * x * (1.0 + jnp.tanh(np.sqrt(2.0 / np.pi).astype(np.float32) * (x + 0.044715 * (x * x * x))))


def _compress_kernel(kc_ref, vc_ref, kw1_ref, kw1f_ref, kpos_ref, kw2_ref,
                     vw1_ref, vw1f_ref, vpos_ref, vw2_ref, cos_ref, sin_ref, kcb_ref, vcb_ref):
    n_chunks = kc_ref.shape[1] // CMP_STRIDE

    def compress(x_ref, w1_ref, w1f_ref, pos_ref, w2_ref):
        u = jnp.zeros((n_chunks, 4 * CMP_HIDDEN), F32)
        for p in range(CMP_STRIDE):
            rows = x_ref[0, pl.ds(p, n_chunks, stride=CMP_STRIDE), :].astype(BF16)
            u = u + _dot(rows, w1_ref[p * KV_WIDTH:(p + 1) * KV_WIDTH, :])
        first = u[:, :2 * CMP_HIDDEN]
        second = pltpu.roll(u[:, 2 * CMP_HIDDEN:], n_chunks - 1, 0)
        pos = jnp.broadcast_to(pos_ref[...], (8, CMP_LEN * HEAD_DIM)).astype(BF16)
        bias = _dot(pos, w1f_ref[...])[0:1]
        pre = first + second + jnp.concatenate([bias, bias], axis=1)
        return _dot(_gelu_tanh(pre).astype(BF16), w2_ref[...])

    k = compress(kc_ref, kw1_ref, kw1f_ref, kpos_ref, kw2_ref)
    kcb_ref[0] = _rope_slab(k, cos_ref[...], sin_ref[...]).astype(BF16)
    vcb_ref[0] = compress(vc_ref, vw1_ref, vw1f_ref, vpos_ref, vw2_ref).astype(BF16)


def _compress_weights(pos, w1, w2):
    w1r = w1.reshape(2, CMP_STRIDE, HEAD_DIM, CMP_HIDDEN)
    eye = jnp.eye(NSA_GROUPS, dtype=w1.dtype)
    big = jnp.einsum('hpdj,ge->pgdhej', w1r, eye)
    big = big.reshape(CMP_STRIDE * KV_WIDTH, 2 * NSA_GROUPS * CMP_HIDDEN).astype(BF16)
    w2big = jnp.einsum('jd,ge->gjed', w2, eye).reshape(NSA_GROUPS * CMP_HIDDEN, KV_WIDTH).astype(BF16)
    return big, w1.astype(BF16), pos.reshape(1, CMP_LEN * HEAD_DIM), w2big


def _compress(kc, vc, kparams, vparams, cos_c, sin_c):
    b, seq, width = kc.shape
    n_chunks = seq // CMP_STRIDE
    args = (kc, vc) + _compress_weights(*kparams) + _compress_weights(*vparams) + (cos_c, sin_c)
    x_spec = pl.BlockSpec((1, seq, width), lambda i: (i, 0, 0))
    in_specs = [x_spec, x_spec] + [_resident(a.shape) for a in args[2:]]
    o_spec = pl.BlockSpec((1, n_chunks, KV_WIDTH), lambda i: (i, 0, 0))
    o_shape = jax.ShapeDtypeStruct((b, n_chunks, KV_WIDTH), BF16)
    return pl.pallas_call(
        _compress_kernel,
        grid=(b,),
        in_specs=in_specs,
        out_specs=(o_spec, o_spec),
        out_shape=(o_shape, o_shape),
        compiler_params=pltpu.CompilerParams(dimension_semantics=("parallel",),
                                             vmem_limit_bytes=VMEM_LIMIT),
        name="compress",
    )(*args)


def _nsa_kernel(qat_ref, gst_ref, kcb_ref, vcbt_ref, kk_ref, vst_ref, vwt_ref, ovt_ref, blk_ref, o_ref,
                s_ref, m_ref, l_ref, acc_ref, *, seq):
    qi = pl.program_id(1)
    t0 = qi * NSA_Q
    cols = NSA_REP * NSA_Q
    n_cmp = seq // CMP_STRIDE - 1
    n_sel = seq // SEL_BLOCK
    gst = gst_ref[0]
    ovt = ovt_ref[...]

    def key_and_query_pos(n, k0):
        shape = (n, NSA_Q)
        return k0 + lax.broadcasted_iota(jnp.int32, shape, 0), t0 + lax.broadcasted_iota(jnp.int32, shape, 1)

    groups = range(NSA_GROUPS)
    d_rows = [slice(g * HEAD_DIM, (g + 1) * HEAD_DIM) for g in groups]
    g_cols = [slice(g * cols, (g + 1) * cols) for g in groups]

    zeros_d = jnp.zeros((HEAD_DIM, NSA_Q), BF16)
    heads = []
    for n in range(NSA_HEADS):
        x = qat_ref[0, n * HEAD_DIM:(n + 1) * HEAD_DIM, :]
        heads.append(jnp.concatenate([x, zeros_d] if n < NSA_REP else [zeros_d, x], axis=0))
    q = jnp.concatenate(heads, axis=1)

    n_win = WINDOW + NSA_Q
    w0 = pl.multiple_of(jnp.maximum(t0 - WINDOW, 0), NSA_Q)
    s = _dot(kcb_ref[0], q)
    k_win = kk_ref[0, pl.ds(w0, n_win), KV_WIDTH:2 * KV_WIDTH]
    s_win_0 = _dot(k_win, q[:, g_cols[0]])

    n_idx, tq = key_and_query_pos(s.shape[0], 0)
    cmask = (n_idx * CMP_STRIDE + (CMP_LEN - 1) <= tq) & (n_idx < n_cmp)
    p_cmp = _masked_softmax_keys(s, cmask).astype(BF16)
    o_cmp = [_dot(vcbt_ref[0, d_rows[g], :], p_cmp[:, g_cols[g]]) for g in groups]

    pi = _dot(ovt, p_cmp)
    s_win = jnp.concatenate([s_win_0, _dot(k_win, q[:, g_cols[1]])], axis=1)
    jj, tsel = key_and_query_pos(n_sel, 0)
    tblk = tsel >> 6
    bonus = jnp.where((jj == 0) | (jj == tblk) | (jj == tblk - 1), FORCE_BONUS, 0.0)
    pad = jnp.zeros((LANES - n_sel, NSA_Q), BF16)
    biases = []
    for g in groups:
        imp = pi[:, g * cols:g * cols + NSA_Q]
        for r in range(1, NSA_REP):
            imp = imp + pi[:, g * cols + r * NSA_Q:g * cols + (r + 1) * NSA_Q]
        imp = jnp.where(jj <= tblk, imp + bonus, NEG)
        bias = jnp.where(_rank_select(imp, n_sel, SEL_TOPK) & (jj <= tblk), 0.0, NEG)
        biases.append(jnp.concatenate([bias.astype(BF16), pad], axis=0))
    q_sel = jnp.concatenate(
        [jnp.concatenate([heads[n], biases[n // NSA_REP]], axis=0) for n in range(NSA_HEADS)], axis=1)

    def tile_start(c):
        return pl.multiple_of(c * SEL_KEYS, SEL_KEYS)

    def sel_scores(c):
        k0 = tile_start(c)
        keys = jnp.concatenate([kk_ref[0, pl.ds(k0, SEL_KEYS), 0:KV_WIDTH], blk_ref[pl.ds(k0, SEL_KEYS), :]],
                               axis=1)
        return _dot(keys, q_sel)

    def sel_update(c, s):
        for g in groups:
            m, l, acc = _online_update((m_ref[g], l_ref[g], acc_ref[g]), s[:, g_cols[g]],
                                       vst_ref[0, d_rows[g], pl.ds(tile_start(c), SEL_KEYS)])
            m_ref[g], l_ref[g], acc_ref[g] = m, l, acc

    n_past = qi // (SEL_KEYS // NSA_Q)
    m_ref[...] = jnp.full(m_ref.shape, NEG, F32)
    l_ref[...] = jnp.zeros(l_ref.shape, F32)
    acc_ref[...] = jnp.zeros(acc_ref.shape, F32)
    s_ref[0] = sel_scores(0)

    kpos, tq = key_and_query_pos(n_win, w0)
    s_win = _mask_keys(s_win, (kpos <= tq) & (tq - kpos < WINDOW))
    p_win = jnp.exp2(s_win - jnp.max(s_win, axis=0, keepdims=True)).astype(BF16)
    ones_rows = jnp.where(lax.broadcasted_iota(jnp.int32, (BF16_ROWS, n_win), 0) == 0, 1.0, 0.0).astype(BF16)
    o_win = []
    for g in groups:
        pv = _dot(jnp.concatenate([vwt_ref[0, d_rows[g], pl.ds(w0, n_win)], ones_rows], axis=0), p_win[:, g_cols[g]])
        o_win.append(pv[:HEAD_DIM] / jnp.maximum(pv[HEAD_DIM:HEAD_DIM + 1], TINY))

    for c in range(seq // SEL_KEYS - 1):
        @pl.when(c < n_past)
        def _():
            s_ref[(c + 1) % 2] = sel_scores(c + 1)
            sel_update(c, s_ref[c % 2])

    kpos, tq = key_and_query_pos(SEL_KEYS, tile_start(n_past))
    sel_update(n_past, _mask_keys(s_ref[n_past % 2], kpos <= tq))
    o_sel = [acc_ref[g] / jnp.maximum(l_ref[g], TINY) for g in groups]

    outs = []
    for n in range(NSA_HEADS):
        g, r = divmod(n, NSA_REP)
        sl = slice(r * NSA_Q, (r + 1) * NSA_Q)
        outs.append(gst[3 * n:3 * n + 1, :] * o_cmp[g][:, sl] + gst[3 * n + 1:3 * n + 2, :] * o_sel[g][:, sl]
                    + gst[3 * n + 2:3 * n + 3, :] * o_win[g][:, sl])
    o_ref[0] = jnp.transpose(jnp.concatenate(outs, axis=0)).astype(BF16)


def _nsa(qat, gst, kcb, vcbt, kk, vvt, ovt, blk):
    b, _, seq = qat.shape
    n_chunks = kcb.shape[1]
    return pl.pallas_call(
        functools.partial(_nsa_kernel, seq=seq),
        grid=(b, seq // NSA_Q),
        in_specs=[pl.BlockSpec((1, NSA_WIDTH, NSA_Q), lambda i, j: (i, 0, j)),
                  pl.BlockSpec((1, LANES, NSA_Q), lambda i, j: (i, 0, j)),
                  pl.BlockSpec((1, n_chunks, KV_WIDTH), lambda i, j: (i, 0, 0)),
                  pl.BlockSpec((1, KV_WIDTH, n_chunks), lambda i, j: (i, 0, 0)),
                  pl.BlockSpec((1, seq, 2 * KV_WIDTH), lambda i, j: (i, 0, 0)),
                  pl.BlockSpec((1, KV_WIDTH, seq), lambda i, j: (i, 0, 0)),
                  pl.BlockSpec((1, KV_WIDTH, seq), lambda i, j: (i, 1, 0)),
                  pl.BlockSpec(ovt.shape, lambda i, j: (0, 0)),
                  pl.BlockSpec(blk.shape, lambda i, j: (0, 0))],
        out_specs=pl.BlockSpec((1, NSA_Q, NSA_WIDTH), lambda i, j: (i, j, 0)),
        out_shape=jax.ShapeDtypeStruct((b, seq, NSA_WIDTH), BF16),
        scratch_shapes=[pltpu.VMEM((2, SEL_KEYS, NSA_HEADS * NSA_Q), F32),
                        pltpu.VMEM((NSA_GROUPS, 1, NSA_REP * NSA_Q), F32),
                        pltpu.VMEM((NSA_GROUPS, 1, NSA_REP * NSA_Q), F32),
                        pltpu.VMEM((NSA_GROUPS, HEAD_DIM, NSA_REP * NSA_Q), F32)],
        compiler_params=pltpu.CompilerParams(dimension_semantics=("parallel", "parallel"),
                                             vmem_limit_bytes=VMEM_LIMIT),
        name="nsa",
    )(qat, gst, kcb, vcbt, kk, vvt, vvt, ovt, blk)


def _moba_kernel(qt_ref, k_ref, vt_ref, kmean_ref, o_ref, s_ref, m_ref, l_ref, acc_ref):
    c = pl.program_id(1)
    nb = kmean_ref.shape[1]
    k_own = pl.multiple_of(c * MOBA_BLOCK, MOBA_BLOCK)
    jj = lax.broadcasted_iota(jnp.int32, (BF16_ROWS, MOBA_BLOCK), 0)
    past = jj < c
    pad = jnp.zeros((BF16_ROWS - nb, LANES), F32)
    zeros_d = jnp.zeros((HEAD_DIM, MOBA_BLOCK), BF16)
    ones_rows = jnp.where(lax.broadcasted_iota(jnp.int32, (BF16_ROWS, MOBA_BLOCK), 0) == 0, 1.0, 0.0).astype(BF16)

    def pair(h):
        return slice((h // 2) * LANES, (h // 2 + 1) * LANES)

    qs, gscs = [], []
    for h in range(MOBA_HEADS):
        x = qt_ref[0, h * HEAD_DIM:(h + 1) * HEAD_DIM, :]
        q = jnp.concatenate([x, zeros_d] if h % 2 == 0 else [zeros_d, x], axis=0)
        km = jnp.concatenate([kmean_ref[0, :, pair(h)], pad], axis=0).astype(BF16)
        qs.append(q)
        gscs.append(_dot(km, q))
    chosen = [jnp.where(_rank_select(jnp.where(past, g, NEG), nb, MOBA_TOPK) & past, 1.0, 0.0) for g in gscs]

    m_ref[...] = jnp.full(m_ref.shape, NEG, F32)
    l_ref[...] = jnp.zeros(l_ref.shape, F32)
    acc_ref[...] = jnp.zeros(acc_ref.shape, F32)

    def block_start(j):
        return pl.multiple_of(j * MOBA_BLOCK, MOBA_BLOCK)

    def scores(j, slot):
        for h in range(MOBA_HEADS):
            s_ref[slot, h] = _dot(k_ref[0, pl.ds(block_start(j), MOBA_BLOCK), pair(h)], qs[h])

    def past_update(j, slot):
        for h in range(MOBA_HEADS):
            rows = slice(h * HEAD_DIM, (h + 1) * HEAD_DIM)
            s = s_ref[slot, h]
            pick = jnp.sum(jnp.where(jj == j, chosen[h], 0.0), axis=0, keepdims=True) > 0.5
            m = m_ref[h]
            m_new = jnp.maximum(m, jnp.where(pick, jnp.max(s, axis=0, keepdims=True), NEG))
            alpha = jnp.exp2(m - m_new)
            p = jnp.exp2(s - jnp.where(pick, m_new, jnp.inf))
            m_ref[h] = m_new
            pv = _dot(jnp.concatenate([vt_ref[0, rows, pl.ds(block_start(j), MOBA_BLOCK)], ones_rows], axis=0),
                      p.astype(BF16))
            l_ref[h] = alpha * l_ref[h] + pv[HEAD_DIM:HEAD_DIM + 1]
            acc_ref[rows, :] = alpha * acc_ref[rows, :] + pv[:HEAD_DIM]

    scores(0, 0)

    def two_blocks(i, _):
        j = 2 * i
        scores(j + 1, 1)
        past_update(j, 0)
        scores(j + 2, 0)
        past_update(j + 1, 1)
        return 0

    lax.fori_loop(0, c // 2, two_blocks, 0)

    @pl.when(c % 2 == 1)
    def _():
        scores(c, 1)
        past_update(c - 1, 0)

    kq = (MOBA_BLOCK, MOBA_BLOCK)
    causal = lax.broadcasted_iota(jnp.int32, kq, 0) <= lax.broadcasted_iota(jnp.int32, kq, 1)
    for h in range(MOBA_HEADS):
        rows = slice(h * HEAD_DIM, (h + 1) * HEAD_DIM)
        carry = (m_ref[h], l_ref[h], acc_ref[rows, :])
        _, l, acc = _online_update(carry, jnp.where(causal, s_ref[c % 2, h], NEG),
                                   vt_ref[0, rows, pl.ds(k_own, MOBA_BLOCK)])
        acc_ref[rows, :] = acc / jnp.maximum(l, TINY)
    o_ref[0] = jnp.transpose(acc_ref[...]).astype(BF16)


def _moba(qbt, kb, vbt, kmean):
    b, _, seq = qbt.shape
    nb = seq // MOBA_BLOCK
    return pl.pallas_call(
        _moba_kernel,
        grid=(b, nb),
        in_specs=[pl.BlockSpec((1, MOBA_WIDTH, MOBA_BLOCK), lambda i, j: (i, 0, j)),
                  pl.BlockSpec((1, seq, MOBA_WIDTH), lambda i, j: (i, 0, 0)),
                  pl.BlockSpec((1, MOBA_WIDTH, seq), lambda i, j: (i, 0, 0)),
                  pl.BlockSpec((1, nb, MOBA_WIDTH), lambda i, j: (i, 0, 0))],
        out_specs=pl.BlockSpec((1, MOBA_BLOCK, MOBA_WIDTH), lambda i, j: (i, j, 0)),
        out_shape=jax.ShapeDtypeStruct((b, seq, MOBA_WIDTH), BF16),
        scratch_shapes=[pltpu.VMEM((2, MOBA_HEADS, MOBA_BLOCK, MOBA_BLOCK), F32),
                        pltpu.VMEM((MOBA_HEADS, 1, MOBA_BLOCK), F32), pltpu.VMEM((MOBA_HEADS, 1, MOBA_BLOCK), F32),
                        pltpu.VMEM((MOBA_WIDTH, MOBA_BLOCK), F32)],
        compiler_params=pltpu.CompilerParams(dimension_semantics=("parallel", "parallel"),
                                             vmem_limit_bytes=VMEM_LIMIT),
        name="moba",
    )(qbt, kb, vbt, kmean)


def _merge_kernel(x_ref, g_ref, a_ref, b_ref, wgate_ref, pa_ref, pb_ref, wo_ref, o_ref):
    x = x_ref[...]
    h = _rms(x, g_ref[...]).astype(BF16)
    ya = _dot(a_ref[...], pa_ref[...])
    yb = _dot(b_ref[...], pb_ref[...])
    merged = (jax.nn.sigmoid(_dot(h, wgate_ref[:, :D_MODEL])) * ya
              + jax.nn.sigmoid(_dot(h, wgate_ref[:, D_MODEL:])) * yb)
    o_ref[...] = x + _dot(merged.astype(BF16), wo_ref[...])


def _merge(x, layer, g, a, bm, wgate, pa, pb, wo):
    t = x.shape[0]

    def row(n):
        return pl.BlockSpec((TOKEN_TILE, n), lambda i: (i, 0))

    return pl.pallas_call(
        _merge_kernel,
        grid=(t // TOKEN_TILE,),
        in_specs=[row(D_MODEL), _layer_resident(g, layer), row(NSA_WIDTH), row(MOBA_WIDTH)]
                 + [_layer_resident(w, layer) for w in (wgate, pa, pb, wo)],
        out_specs=row(D_MODEL),
        out_shape=jax.ShapeDtypeStruct(x.shape, F32),
        compiler_params=pltpu.CompilerParams(dimension_semantics=("parallel",),
                                             vmem_limit_bytes=VMEM_LIMIT),
        name="merge",
    )(x, g, a, bm, wgate, pa, pb, wo)


def _rope_tables(pos):
    inv = ROPE_THETA ** (-jnp.arange(0, HEAD_DIM, 2, dtype=F32) / HEAD_DIM)
    ang = pos.astype(F32)[:, None] * inv[None, :]
    cos, sin = jnp.cos(ang), jnp.sin(ang)
    reps = LANES // HEAD_DIM
    return (jnp.tile(jnp.concatenate([cos, cos], axis=1), (1, reps)),
            jnp.tile(jnp.concatenate([-sin, sin], axis=1), (1, reps)))


def _overlap_matrix_t(seq):
    n_chunks = seq // CMP_STRIDE
    ci = np.arange(n_chunks)[None, :] * CMP_STRIDE
    sj = np.arange(seq // SEL_BLOCK)[:, None] * SEL_BLOCK
    ov = (ci < sj + SEL_BLOCK) & (ci + CMP_LEN > sj) & (np.arange(n_chunks)[None, :] < n_chunks - 1)
    return jnp.asarray(ov, dtype=BF16)


def _block_onehot(seq):
    return jnp.asarray(np.arange(seq)[:, None] // SEL_BLOCK == np.arange(LANES)[None, :], dtype=BF16)


def kernel(x, ffn1_norm, ffn1_wg, ffn1_wu, ffn1_wd, mix_norm, w_in, cmpk_pos, cmpk_w1, cmpk_w2, cmpv_pos, cmpv_w1, cmpv_w2, w_branch_nsa, w_branch_moba, w_out, ffn2_norm, ffn2_wg, ffn2_wu, ffn2_wd, final_norm):
    b, seq, d = x.shape
    depth = w_in.shape[0]
    assert d == D_MODEL and seq % TOKEN_TILE == 0 and seq // CMP_STRIDE == LANES
    t = b * seq
    n_chunks = seq // CMP_STRIDE
    cos_t, sin_t = _rope_tables(jnp.arange(seq))
    cos_c, sin_c = _rope_tables(jnp.arange(n_chunks) * CMP_STRIDE + (CMP_LEN - 1))
    ovt = _overlap_matrix_t(seq)
    blk = _block_onehot(seq)

    def gain(gs):
        return gs.reshape(depth, 1, d)

    ffn1 = (gain(ffn1_norm), ffn1_wg.astype(BF16), ffn1_wu.astype(BF16), ffn1_wd.astype(BF16))
    ffn2 = (gain(ffn2_norm), ffn2_wg.astype(BF16), ffn2_wu.astype(BF16), ffn2_wd.astype(BF16))
    w_attn, w_gate = _split_w_in(w_in)
    w_merge = (w_gate, w_branch_nsa.astype(BF16), w_branch_moba.astype(BF16), w_out.astype(BF16))
    fin = final_norm.reshape(1, d)

    xf = x.reshape(t, d)
    for l in range(depth):
        xf = _ffn(xf, l, *ffn1, fin, False)
        x_mix = xf
        qat, kc, vc, kk, vvt, qbt, kb, vbt, gst, kmean = _proj(xf, l, gain(mix_norm), w_attn, cos_t, sin_t, seq)
        kcb, vcb = _compress(kc.reshape(b, seq, KV_WIDTH), vc.reshape(b, seq, KV_WIDTH),
                             (cmpk_pos[l], cmpk_w1[l], cmpk_w2[l]),
                             (cmpv_pos[l], cmpv_w1[l], cmpv_w2[l]), cos_c, sin_c)
        a = _nsa(qat, gst, kcb, vcb.transpose(0, 2, 1), kk.reshape(b, seq, 2 * KV_WIDTH), vvt, ovt, blk)
        bm = _moba(qbt, kb.reshape(b, seq, MOBA_WIDTH), vbt, kmean.reshape(b, seq // MOBA_BLOCK, MOBA_WIDTH))
        xf = _merge(x_mix, l, gain(mix_norm), a.reshape(t, NSA_WIDTH), bm.reshape(t, MOBA_WIDTH), *w_merge)
        xf = _ffn(xf, l, *ffn2, fin, l == depth - 1)
    return xf.reshape(b, seq, d)
```

```python
import functools

import jax
import jax.numpy as jnp
import numpy as np
from jax import lax
from jax.experimental import pallas as pl
from jax.experimental.pallas import tpu as pltpu

F32 = jnp.float32
BF16 = jnp.bfloat16

D_MODEL = 1024
HEAD_DIM = 64
HALF = HEAD_DIM // 2
NSA_HEADS = 8
NSA_GROUPS = 2
NSA_REP = NSA_HEADS // NSA_GROUPS
CMP_LEN = 32
CMP_STRIDE = 16
CMP_HIDDEN = 128
SEL_BLOCK = 64
SEL_TOPK = 16
WINDOW = 512
MOBA_HEADS = 8
MOBA_BLOCK = 256
MOBA_TOPK = 3
D_FF = 2816
ROPE_THETA = 10000.0
EPS = 1e-6
NEG = -1e30
TINY = 1e-30
FORCE_BONUS = 1e4
Q_SCALE = HEAD_DIM ** -0.5 * float(np.log2(np.e))

NSA_WIDTH = NSA_HEADS * HEAD_DIM
KV_WIDTH = NSA_GROUPS * HEAD_DIM
MOBA_WIDTH = MOBA_HEADS * HEAD_DIM
N_GATE_COLS = 3 * NSA_HEADS

LANES = 128
BF16_ROWS = 16
TOKEN_TILE = 1024
FFN_TILE = 1024
FF_CHUNK = 256
NSA_Q = 256
SEL_KEYS = 512
VMEM_LIMIT = 56 * 1024 * 1024

COL_QA = 0
COL_KC = COL_QA + NSA_WIDTH
COL_VC = COL_KC + KV_WIDTH
COL_KV4 = COL_VC + KV_WIDTH
COL_QB = COL_KV4 + 4 * KV_WIDTH
COL_KB = COL_QB + MOBA_WIDTH
COL_VB = COL_KB + MOBA_WIDTH
COL_GS = COL_VB + MOBA_WIDTH


def _rms(x, g):
    return x * lax.rsqrt(jnp.mean(x * x, axis=-1, keepdims=True) + EPS) * g


def _dot(a, b):
    return jnp.dot(a, b, preferred_element_type=F32)


def _dot_nt(a, b):
    return lax.dot_general(a, b, (((1,), (1,)), ((), ())), preferred_element_type=F32)


def _swap_halves(z):
    lane = lax.broadcasted_iota(jnp.int32, z.shape, 1)
    lo = (lane & (HEAD_DIM - 1)) < HALF
    return jnp.where(lo, pltpu.roll(z, LANES - HALF, 1), pltpu.roll(z, HALF, 1))


def _rope_slab(z, cos, sin):
    parts = []
    for c in range(z.shape[1] // LANES):
        zc = z[:, c * LANES:(c + 1) * LANES]
        parts.append(zc * cos + _swap_halves(zc) * sin)
    return parts[0] if len(parts) == 1 else jnp.concatenate(parts, axis=1)


def _slabs(s, n):
    return [s[:, i * n:(i + 1) * n] for i in range(s.shape[1] // n)]


def _masked_softmax_keys(s, mask):
    out = []
    for x in _slabs(s, mask.shape[1]):
        x = jnp.where(mask, x, NEG)
        m = jnp.max(x, axis=0, keepdims=True)
        p = jnp.where(mask, jnp.exp2(x - m), 0.0)
        out.append(p / jnp.maximum(jnp.sum(p, axis=0, keepdims=True), TINY))
    return jnp.concatenate(out, axis=1)


def _mask_keys(s, mask):
    return jnp.concatenate([jnp.where(mask, x, NEG) for x in _slabs(s, mask.shape[1])], axis=1)


def _online_update(carry, s, vt):
    m, l, acc = carry
    m_new = jnp.maximum(m, jnp.max(s, axis=0, keepdims=True))
    alpha = jnp.exp2(m - m_new)
    p = jnp.exp2(s - m_new).astype(BF16)
    ones_rows = jnp.where(lax.broadcasted_iota(jnp.int32, (BF16_ROWS, vt.shape[1]), 0) == 0, 1.0, 0.0).astype(BF16)
    pv = _dot(jnp.concatenate([vt, ones_rows], axis=0), p)
    return m_new, alpha * l + pv[HEAD_DIM:HEAD_DIM + 1], alpha * acc + pv[:HEAD_DIM]


def _online_init(n):
    return jnp.full((1, n), NEG, F32), jnp.zeros((1, n), F32), jnp.zeros((HEAD_DIM, n), F32)


def _rank_count_step(v, row, cnt, j):
    r = v[j:j + 1, :]
    return cnt + jnp.where(r > v, 1.0, jnp.where(r == v, jnp.where(row > j, 1.0, 0.0), 0.0))


def _rank_select(v, n_candidates, k):
    row = lax.broadcasted_iota(jnp.int32, v.shape, 0)
    cnt = jnp.zeros(v.shape, F32)
    for j in range(n_candidates):
        cnt = _rank_count_step(v, row, cnt, j)
    return cnt < float(k)


def _ffn_kernel(x_ref, g_ref, wg_ref, wu_ref, wd_ref, fin_ref, o_ref, t_ref, *, final_norm):
    x = x_ref[...]
    h = (x * g_ref[...]).astype(BF16)
    inv = lax.rsqrt(jnp.mean(x * x, axis=-1, keepdims=True) + EPS)
    for f in range(D_FF // FF_CHUNK):
        c = slice(f * FF_CHUNK, (f + 1) * FF_CHUNK)
        a = _dot(h, wg_ref[:, c]) * inv
        b = _dot(h, wu_ref[:, c]) * inv
        t_ref[:, c] = (a * jax.nn.sigmoid(a) * b).astype(BF16)
    y = x + 0.5 * _dot(t_ref[...], wd_ref[...])
    if final_norm:
        y = _rms(y, fin_ref[...])
    o_ref[...] = y


def _resident(shape):
    nd = len(shape)
    return pl.BlockSpec(shape, lambda *_: (0,) * nd, pipeline_mode=pl.Buffered(1))


def _layer_resident(stacked, layer):
    return pl.BlockSpec((None,) + stacked.shape[1:], lambda *_: (layer, 0, 0), pipeline_mode=pl.Buffered(1))


def _ffn(x, layer, g, wg, wu, wd, fin, final_norm):
    t = x.shape[0]
    row = pl.BlockSpec((FFN_TILE, D_MODEL), lambda i: (i, 0))
    return pl.pallas_call(
        functools.partial(_ffn_kernel, final_norm=final_norm),
        grid=(t // FFN_TILE,),
        in_specs=[row, _layer_resident(g, layer), _layer_resident(wg, layer), _layer_resident(wu, layer),
                  _layer_resident(wd, layer), _resident((1, D_MODEL))],
        out_specs=row,
        out_shape=jax.ShapeDtypeStruct(x.shape, F32),
        scratch_shapes=[pltpu.VMEM((FFN_TILE, D_FF), BF16)],
        compiler_params=pltpu.CompilerParams(dimension_semantics=("parallel",),
                                             vmem_limit_bytes=VMEM_LIMIT),
        name="ffn",
    )(x, g, wg, wu, wd, fin)


def _proj_kernel(x_ref, g_ref, w_ref, cos_ref, sin_ref,
                 qat_ref, kc_ref, vc_ref, kk_ref, vvt_ref, qbt_ref, kb_ref, vbt_ref, gst_ref, kmean_ref):
    h = _rms(x_ref[...], g_ref[...]).astype(BF16)
    cos = cos_ref[...]
    sin = sin_ref[...]

    def mm(c0, n):
        return _dot(h, w_ref[:, c0:c0 + n])

    def feature_major(z, dtype=BF16):
        return jnp.transpose(z).astype(dtype)

    qat_ref[0] = feature_major(_rope_slab(mm(COL_QA, NSA_WIDTH), cos, sin) * Q_SCALE)
    kcvc = mm(COL_KC, 2 * KV_WIDTH)
    kc_ref[0] = kcvc[:, :KV_WIDTH]
    vc_ref[0] = kcvc[:, KV_WIDTH:]
    kv4 = mm(COL_KV4, 4 * KV_WIDTH)
    kk_ref[0] = jnp.concatenate(
        [_rope_slab(kv4[:, 0:KV_WIDTH], cos, sin), _rope_slab(kv4[:, 2 * KV_WIDTH:3 * KV_WIDTH], cos, sin)],
        axis=1).astype(BF16)
    vvt_ref[0] = feature_major(jnp.concatenate([kv4[:, KV_WIDTH:2 * KV_WIDTH], kv4[:, 3 * KV_WIDTH:]], axis=1))
    qbt_ref[0] = feature_major(_rope_slab(mm(COL_QB, MOBA_WIDTH), cos, sin) * Q_SCALE)
    kb = _rope_slab(mm(COL_KB, MOBA_WIDTH), cos, sin)
    kb_ref[0] = kb.astype(BF16)
    means = [jnp.mean(kb[i * MOBA_BLOCK:(i + 1) * MOBA_BLOCK], axis=0, keepdims=True)
             for i in range(TOKEN_TILE // MOBA_BLOCK)]
    kmean_ref[0] = jnp.concatenate(means, axis=0)
    vbt_ref[0] = feature_major(mm(COL_VB, MOBA_WIDTH))
    gst_ref[0] = feature_major(jax.nn.sigmoid(mm(COL_GS, LANES)), F32)


def _split_w_in(w_in):
    off_ga = NSA_WIDTH
    off_rest = off_ga + N_GATE_COLS
    off_gates = off_rest + 6 * KV_WIDTH + 3 * MOBA_WIDTH
    pad = jnp.zeros(w_in.shape[:2] + (LANES - N_GATE_COLS,), w_in.dtype)
    attn = jnp.concatenate([w_in[..., :off_ga], w_in[..., off_rest:off_gates], w_in[..., off_ga:off_rest], pad],
                           axis=-1)
    return attn.astype(BF16), w_in[..., off_gates:].astype(BF16)


def _proj(x, layer, g, w, cos_t, sin_t, seq):
    t = x.shape[0]
    tiles_per_seq = seq // TOKEN_TILE
    blocks_per_tile = TOKEN_TILE // MOBA_BLOCK

    def row(n):
        return pl.BlockSpec((TOKEN_TILE, n), lambda i: (i, 0))

    def token_major(n, dtype=BF16):
        return (jax.ShapeDtypeStruct((t // seq, seq, n), dtype),
                pl.BlockSpec((1, TOKEN_TILE, n), lambda i: (i // tiles_per_seq, i % tiles_per_seq, 0)))

    def feature_major(n, dtype=BF16):
        return (jax.ShapeDtypeStruct((t // seq, n, seq), dtype),
                pl.BlockSpec((1, n, TOKEN_TILE), lambda i: (i // tiles_per_seq, 0, i % tiles_per_seq)))

    tab = pl.BlockSpec((TOKEN_TILE, LANES), lambda i: (i % tiles_per_seq, 0))
    outs = (feature_major(NSA_WIDTH), token_major(KV_WIDTH, F32), token_major(KV_WIDTH, F32),
            token_major(2 * KV_WIDTH),
            feature_major(2 * KV_WIDTH), feature_major(MOBA_WIDTH), token_major(MOBA_WIDTH),
            feature_major(MOBA_WIDTH), feature_major(LANES, F32),
            (jax.ShapeDtypeStruct((t // TOKEN_TILE, blocks_per_tile, MOBA_WIDTH), F32),
             pl.BlockSpec((1, blocks_per_tile, MOBA_WIDTH), lambda i: (i, 0, 0))))
    out_shape, out_specs = zip(*outs)
    return pl.pallas_call(
        _proj_kernel,
        grid=(t // TOKEN_TILE,),
        in_specs=[row(D_MODEL), _layer_resident(g, layer), _layer_resident(w, layer), tab, tab],
        out_specs=out_specs,
        out_shape=out_shape,
        compiler_params=pltpu.CompilerParams(dimension_semantics=("parallel",),
                                             vmem_limit_bytes=VMEM_LIMIT),
        name="proj",
    )(x, g, w, cos_t, sin_t)


def _gelu_tanh(x):
    return 0.5 * x * (1.0 + jnp.tanh(np.sqrt(2.0 / np.pi).astype(np.float32) * (x + 0.044715 * (x * x * x))))


def _compress_kernel(kc_ref, vc_ref, kw1_ref, kw1f_ref, kpos_ref, kw2_ref,
                     vw1_ref, vw1f_ref, vpos_ref, vw2_ref, cos_ref, sin_ref, kcb_ref, vcb_ref):
    n_chunks = kc_ref.shape[1] // CMP_STRIDE

    def compress(x_ref, w1_ref, w1f_ref, pos_ref, w2_ref):
        u = jnp.zeros((n_chunks, 4 * CMP_HIDDEN), F32)
        for p in range(CMP_STRIDE):
            rows = x_ref[0, pl.ds(p, n_chunks, stride=CMP_STRIDE), :].astype(BF16)
            u = u + _dot(rows, w1_ref[p * KV_WIDTH:(p + 1) * KV_WIDTH, :])
        first = u[:, :2 * CMP_HIDDEN]
        second = pltpu.roll(u[:, 2 * CMP_HIDDEN:], n_chunks - 1, 0)
        pos = jnp.broadcast_to(pos_ref[...], (8, CMP_LEN * HEAD_DIM)).astype(BF16)
        bias = _dot(pos, w1f_ref[...])[0:1]
        pre = first + second + jnp.concatenate([bias, bias], axis=1)
        return _dot(_gelu_tanh(pre).astype(BF16), w2_ref[...])

    k = compress(kc_ref, kw1_ref, kw1f_ref, kpos_ref, kw2_ref)
    kcb_ref[0] = _rope_slab(k, cos_ref[...], sin_ref[...]).astype(BF16)
    vcb_ref[0] = compress(vc_ref, vw1_ref, vw1f_ref, vpos_ref, vw2_ref).astype(BF16)


def _compress_weights(pos, w1, w2):
    w1r = w1.reshape(2, CMP_STRIDE, HEAD_DIM, CMP_HIDDEN)
    eye = jnp.eye(NSA_GROUPS, dtype=w1.dtype)
    big = jnp.einsum('hpdj,ge->pgdhej', w1r, eye)
    big = big.reshape(CMP_STRIDE * KV_WIDTH, 2 * NSA_GROUPS * CMP_HIDDEN).astype(BF16)
    w2big = jnp.einsum('jd,ge->gjed', w2, eye).reshape(NSA_GROUPS * CMP_HIDDEN, KV_WIDTH).astype(BF16)
    return big, w1.astype(BF16), pos.reshape(1, CMP_LEN * HEAD_DIM), w2big


def _compress(kc, vc, kparams, vparams, cos_c, sin_c):
    b, seq, width = kc.shape
    n_chunks = seq // CMP_STRIDE
    args = (kc, vc) + _compress_weights(*kparams) + _compress_weights(*vparams) + (cos_c, sin_c)
    x_spec = pl.BlockSpec((1, seq, width), lambda i: (i, 0, 0))
    in_specs = [x_spec, x_spec] + [_resident(a.shape) for a in args[2:]]
    o_spec = pl.BlockSpec((1, n_chunks, KV_WIDTH), lambda i: (i, 0, 0))
    o_shape = jax.ShapeDtypeStruct((b, n_chunks, KV_WIDTH), BF16)
    return pl.pallas_call(
        _compress_kernel,
        grid=(b,),
        in_specs=in_specs,
        out_specs=(o_spec, o_spec),
        out_shape=(o_shape, o_shape),
        compiler_params=pltpu.CompilerParams(dimension_semantics=("parallel",),
                                             vmem_limit_bytes=VMEM_LIMIT),
        name="compress",
    )(*args)


def _nsa_kernel(qat_ref, gst_ref, kcb_ref, vcbt_ref, kk_ref, vst_ref, vwt_ref, ovt_ref, blk_ref, o_ref,
                s_ref, m_ref, l_ref, acc_ref, *, seq):
    qi = pl.program_id(1)
    t0 = qi * NSA_Q
    cols = NSA_REP * NSA_Q
    n_cmp = seq // CMP_STRIDE - 1
    n_sel = seq // SEL_BLOCK
    gst = gst_ref[0]
    ovt = ovt_ref[...]

    def key_and_query_pos(n, k0):
        shape = (n, NSA_Q)
        return k0 + lax.broadcasted_iota(jnp.int32, shape, 0), t0 + lax.broadcasted_iota(jnp.int32, shape, 1)

    groups = range(NSA_GROUPS)
    d_rows = [slice(g * HEAD_DIM, (g + 1) * HEAD_DIM) for g in groups]
    g_cols = [slice(g * cols, (g + 1) * cols) for g in groups]

    zeros_d = jnp.zeros((HEAD_DIM, NSA_Q), BF16)
    heads = []
    for n in range(NSA_HEADS):
        x = qat_ref[0, n * HEAD_DIM:(n + 1) * HEAD_DIM, :]
        heads.append(jnp.concatenate([x, zeros_d] if n < NSA_REP else [zeros_d, x], axis=0))
    q = jnp.concatenate(heads, axis=1)

    n_win = WINDOW + NSA_Q
    w0 = pl.multiple_of(jnp.maximum(t0 - WINDOW, 0), NSA_Q)
    s = _dot(kcb_ref[0], q)
    k_win = kk_ref[0, pl.ds(w0, n_win), KV_WIDTH:2 * KV_WIDTH]
    s_win_0 = _dot(k_win, q[:, g_cols[0]])

    n_idx, tq = key_and_query_pos(s.shape[0], 0)
    cmask = (n_idx * CMP_STRIDE + (CMP_LEN - 1) <= tq) & (n_idx < n_cmp)
    p_cmp = _masked_softmax_keys(s, cmask).astype(BF16)
    o_cmp = [_dot(vcbt_ref[0, d_rows[g], :], p_cmp[:, g_cols[g]]) for g in groups]

    pi = _dot(ovt, p_cmp)
    s_win = jnp.concatenate([s_win_0, _dot(k_win, q[:, g_cols[1]])], axis=1)
    jj, tsel = key_and_query_pos(n_sel, 0)
    tblk = tsel >> 6
    bonus = jnp.where((jj == 0) | (jj == tblk) | (jj == tblk - 1), FORCE_BONUS, 0.0)
    pad = jnp.zeros((LANES - n_sel, NSA_Q), BF16)
    biases = []
    for g in groups:
        imp = pi[:, g * cols:g * cols + NSA_Q]
        for r in range(1, NSA_REP):
            imp = imp + pi[:, g * cols + r * NSA_Q:g * cols + (r + 1) * NSA_Q]
        imp = jnp.where(jj <= tblk, imp + bonus, NEG)
        bias = jnp.where(_rank_select(imp, n_sel, SEL_TOPK) & (jj <= tblk), 0.0, NEG)
        biases.append(jnp.concatenate([bias.astype(BF16), pad], axis=0))
    q_sel = jnp.concatenate(
        [jnp.concatenate([heads[n], biases[n // NSA_REP]], axis=0) for n in range(NSA_HEADS)], axis=1)

    def tile_start(c):
        return pl.multiple_of(c * SEL_KEYS, SEL_KEYS)

    def sel_scores(c):
        k0 = tile_start(c)
        keys = jnp.concatenate([kk_ref[0, pl.ds(k0, SEL_KEYS), 0:KV_WIDTH], blk_ref[pl.ds(k0, SEL_KEYS), :]],
                               axis=1)
        return _dot(keys, q_sel)

    def sel_update(c, s):
        for g in groups:
            m, l, acc = _online_update((m_ref[g], l_ref[g], acc_ref[g]), s[:, g_cols[g]],
                                       vst_ref[0, d_rows[g], pl.ds(tile_start(c), SEL_KEYS)])
            m_ref[g], l_ref[g], acc_ref[g] = m, l, acc

    n_past = qi // (SEL_KEYS // NSA_Q)
    m_ref[...] = jnp.full(m_ref.shape, NEG, F32)
    l_ref[...] = jnp.zeros(l_ref.shape, F32)
    acc_ref[...] = jnp.zeros(acc_ref.shape, F32)
    s_ref[0] = sel_scores(0)

    kpos, tq = key_and_query_pos(n_win, w0)
    s_win = _mask_keys(s_win, (kpos <= tq) & (tq - kpos < WINDOW))
    p_win = jnp.exp2(s_win - jnp.max(s_win, axis=0, keepdims=True)).astype(BF16)
    ones_rows = jnp.where(lax.broadcasted_iota(jnp.int32, (BF16_ROWS, n_win), 0) == 0, 1.0, 0.0).astype(BF16)
    o_win = []
    for g in groups:
        pv = _dot(jnp.concatenate([vwt_ref[0, d_rows[g], pl.ds(w0, n_win)], ones_rows], axis=0), p_win[:, g_cols[g]])
        o_win.append(pv[:HEAD_DIM] / jnp.maximum(pv[HEAD_DIM:HEAD_DIM + 1], TINY))

    for c in range(seq // SEL_KEYS - 1):
        @pl.when(c < n_past)
        def _():
            s_ref[(c + 1) % 2] = sel_scores(c + 1)
            sel_update(c, s_ref[c % 2])

    kpos, tq = key_and_query_pos(SEL_KEYS, tile_start(n_past))
    sel_update(n_past, _mask_keys(s_ref[n_past % 2], kpos <= tq))
    o_sel = [acc_ref[g] / jnp.maximum(l_ref[g], TINY) for g in groups]

    outs = []
    for n in range(NSA_HEADS):
        g, r = divmod(n, NSA_REP)
        sl = slice(r * NSA_Q, (r + 1) * NSA_Q)
        outs.append(gst[3 * n:3 * n + 1, :] * o_cmp[g][:, sl] + gst[3 * n + 1:3 * n + 2, :] * o_sel[g][:, sl]
                    + gst[3 * n + 2:3 * n + 3, :] * o_win[g][:, sl])
    o_ref[0] = jnp.transpose(jnp.concatenate(outs, axis=0)).astype(BF16)


def _nsa(qat, gst, kcb, vcbt, kk, vvt, ovt, blk):
    b, _, seq = qat.shape
    n_chunks = kcb.shape[1]
    return pl.pallas_call(
        functools.partial(_nsa_kernel, seq=seq),
        grid=(b, seq // NSA_Q),
        in_specs=[pl.BlockSpec((1, NSA_WIDTH, NSA_Q), lambda i, j: (i, 0, j)),
                  pl.BlockSpec((1, LANES, NSA_Q), lambda i, j: (i, 0, j)),
                  pl.BlockSpec((1, n_chunks, KV_WIDTH), lambda i, j: (i, 0, 0)),
                  pl.BlockSpec((1, KV_WIDTH, n_chunks), lambda i, j: (i, 0, 0)),
                  pl.BlockSpec((1, seq, 2 * KV_WIDTH), lambda i, j: (i, 0, 0)),
                  pl.BlockSpec((1, KV_WIDTH, seq), lambda i, j: (i, 0, 0)),
                  pl.BlockSpec((1, KV_WIDTH, seq), lambda i, j: (i, 1, 0)),
                  pl.BlockSpec(ovt.shape, lambda i, j: (0, 0)),
                  pl.BlockSpec(blk.shape, lambda i, j: (0, 0))],
        out_specs=pl.BlockSpec((1, NSA_Q, NSA_WIDTH), lambda i, j: (i, j, 0)),
        out_shape=jax.ShapeDtypeStruct((b, seq, NSA_WIDTH), BF16),
        scratch_shapes=[pltpu.VMEM((2, SEL_KEYS, NSA_HEADS * NSA_Q), F32),
                        pltpu.VMEM((NSA_GROUPS, 1, NSA_REP * NSA_Q), F32),
                        pltpu.VMEM((NSA_GROUPS, 1, NSA_REP * NSA_Q), F32),
                        pltpu.VMEM((NSA_GROUPS, HEAD_DIM, NSA_REP * NSA_Q), F32)],
        compiler_params=pltpu.CompilerParams(dimension_semantics=("parallel", "parallel"),
                                             vmem_limit_bytes=VMEM_LIMIT),
        name="nsa",
    )(qat, gst, kcb, vcbt, kk, vvt, vvt, ovt, blk)


def _moba_kernel(qt_ref, k_ref, vt_ref, kmean_ref, o_ref, s_ref, m_ref, l_ref, acc_ref):
    c = pl.program_id(1)
    nb = kmean_ref.shape[1]
    k_own = pl.multiple_of(c * MOBA_BLOCK, MOBA_BLOCK)
    jj = lax.broadcasted_iota(jnp.int32, (BF16_ROWS, MOBA_BLOCK), 0)
    past = jj < c
    pad = jnp.zeros((BF16_ROWS - nb, LANES), F32)
    zeros_d = jnp.zeros((HEAD_DIM, MOBA_BLOCK), BF16)
    ones_rows = jnp.where(lax.broadcasted_iota(jnp.int32, (BF16_ROWS, MOBA_BLOCK), 0) == 0, 1.0, 0.0).astype(BF16)

    def pair(h):
        return slice((h // 2) * LANES, (h // 2 + 1) * LANES)

    qs, gscs = [], []
    for h in range(MOBA_HEADS):
        x = qt_ref[0, h * HEAD_DIM:(h + 1) * HEAD_DIM, :]
        q = jnp.concatenate([x, zeros_d] if h % 2 == 0 else [zeros_d, x], axis=0)
        km = jnp.concatenate([kmean_ref[0, :, pair(h)], pad], axis=0).astype(BF16)
        qs.append(q)
        gscs.append(_dot(km, q))
    chosen = [jnp.where(_rank_select(jnp.where(past, g, NEG), nb, MOBA_TOPK) & past, 1.0, 0.0) for g in gscs]

    m_ref[...] = jnp.full(m_ref.shape, NEG, F32)
    l_ref[...] = jnp.zeros(l_ref.shape, F32)
    acc_ref[...] = jnp.zeros(acc_ref.shape, F32)

    def block_start(j):
        return pl.multiple_of(j * MOBA_BLOCK, MOBA_BLOCK)

    def scores(j, slot):
        for h in range(MOBA_HEADS):
            s_ref[slot, h] = _dot(k_ref[0, pl.ds(block_start(j), MOBA_BLOCK), pair(h)], qs[h])

    def past_update(j, slot):
        for h in range(MOBA_HEADS):
            rows = slice(h * HEAD_DIM, (h + 1) * HEAD_DIM)
            s = s_ref[slot, h]
            pick = jnp.sum(jnp.where(jj == j, chosen[h], 0.0), axis=0, keepdims=True) > 0.5
            m = m_ref[h]
            m_new = jnp.maximum(m, jnp.where(pick, jnp.max(s, axis=0, keepdims=True), NEG))
            alpha = jnp.exp2(m - m_new)
            p = jnp.exp2(s - jnp.where(pick, m_new, jnp.inf))
            m_ref[h] = m_new
            pv = _dot(jnp.concatenate([vt_ref[0, rows, pl.ds(block_start(j), MOBA_BLOCK)], ones_rows], axis=0),
                      p.astype(BF16))
            l_ref[h] = alpha * l_ref[h] + pv[HEAD_DIM:HEAD_DIM + 1]
            acc_ref[rows, :] = alpha * acc_ref[rows, :] + pv[:HEAD_DIM]

    scores(0, 0)

    def two_blocks(i, _):
        j = 2 * i
        scores(j + 1, 1)
        past_update(j, 0)
        scores(j + 2, 0)
        past_update(j + 1, 1)
        return 0

    lax.fori_loop(0, c // 2, two_blocks, 0)

    @pl.when(c % 2 == 1)
    def _():
        scores(c, 1)
        past_update(c - 1, 0)

    kq = (MOBA_BLOCK, MOBA_BLOCK)
    causal = lax.broadcasted_iota(jnp.int32, kq, 0) <= lax.broadcasted_iota(jnp.int32, kq, 1)
    for h in range(MOBA_HEADS):
        rows = slice(h * HEAD_DIM, (h + 1) * HEAD_DIM)
        carry = (m_ref[h], l_ref[h], acc_ref[rows, :])
        _, l, acc = _online_update(carry, jnp.where(causal, s_ref[c % 2, h], NEG),
                                   vt_ref[0, rows, pl.ds(k_own, MOBA_BLOCK)])
        acc_ref[rows, :] = acc / jnp.maximum(l, TINY)
    o_ref[0] = jnp.transpose(acc_ref[...]).astype(BF16)


def _moba(qbt, kb, vbt, kmean):
    b, _, seq = qbt.shape
    nb = seq // MOBA_BLOCK
    return pl.pallas_call(
        _moba_kernel,
        grid=(b, nb),
        in_specs=[pl.BlockSpec((1, MOBA_WIDTH, MOBA_BLOCK), lambda i, j: (i, 0, j)),
                  pl.BlockSpec((1, seq, MOBA_WIDTH), lambda i, j: (i, 0, 0)),
                  pl.BlockSpec((1, MOBA_WIDTH, seq), lambda i, j: (i, 0, 0)),
                  pl.BlockSpec((1, nb, MOBA_WIDTH), lambda i, j: (i, 0, 0))],
        out_specs=pl.BlockSpec((1, MOBA_BLOCK, MOBA_WIDTH), lambda i, j: (i, j, 0)),
        out_shape=jax.ShapeDtypeStruct((b, seq, MOBA_WIDTH), BF16),
        scratch_shapes=[pltpu.VMEM((2, MOBA_HEADS, MOBA_BLOCK, MOBA_BLOCK), F32),
                        pltpu.VMEM((MOBA_HEADS, 1, MOBA_BLOCK), F32), pltpu.VMEM((MOBA_HEADS, 1, MOBA_BLOCK), F32),
                        pltpu.VMEM((MOBA_WIDTH, MOBA_BLOCK), F32)],
        compiler_params=pltpu.CompilerParams(dimension_semantics=("parallel", "parallel"),
                                             vmem_limit_bytes=VMEM_LIMIT),
        name="moba",
    )(qbt, kb, vbt, kmean)


def _merge_kernel(x_ref, g_ref, a_ref, b_ref, wgate_ref, pa_ref, pb_ref, wo_ref, o_ref):
    x = x_ref[...]
    h = _rms(x, g_ref[...]).astype(BF16)
    ya = _dot(a_ref[0], pa_ref[...])
    yb = _dot(b_ref[0], pb_ref[...])
    merged = (jax.nn.sigmoid(_dot(h, wgate_ref[:, :D_MODEL])) * ya
              + jax.nn.sigmoid(_dot(h, wgate_ref[:, D_MODEL:])) * yb)
    o_ref[...] = x + _dot(merged.astype(BF16), wo_ref[...])


def _merge(x, layer, g, a, bm, wgate, pa, pb, wo):
    t = x.shape[0]
    tiles_per_seq = a.shape[1] // TOKEN_TILE

    def row(n):
        return pl.BlockSpec((TOKEN_TILE, n), lambda i: (i, 0))

    def seq_row(n):
        return pl.BlockSpec((1, TOKEN_TILE, n), lambda i: (i // tiles_per_seq, i % tiles_per_seq, 0))

    return pl.pallas_call(
        _merge_kernel,
        grid=(t // TOKEN_TILE,),
        in_specs=[row(D_MODEL), _layer_resident(g, layer), seq_row(NSA_WIDTH), seq_row(MOBA_WIDTH)]
                 + [_layer_resident(w, layer) for w in (wgate, pa, pb, wo)],
        out_specs=row(D_MODEL),
        out_shape=jax.ShapeDtypeStruct(x.shape, F32),
        compiler_params=pltpu.CompilerParams(dimension_semantics=("parallel",),
                                             vmem_limit_bytes=VMEM_LIMIT),
        name="merge",
    )(x, g, a, bm, wgate, pa, pb, wo)


def _rope_tables(pos):
    inv = ROPE_THETA ** (-jnp.arange(0, HEAD_DIM, 2, dtype=F32) / HEAD_DIM)
    ang = pos.astype(F32)[:, None] * inv[None, :]
    cos, sin = jnp.cos(ang), jnp.sin(ang)
    reps = LANES // HEAD_DIM
    return (jnp.tile(jnp.concatenate([cos, cos], axis=1), (1, reps)),
            jnp.tile(jnp.concatenate([-sin, sin], axis=1), (1, reps)))


def _overlap_matrix_t(seq):
    n_chunks = seq // CMP_STRIDE
    ci = np.arange(n_chunks)[None, :] * CMP_STRIDE
    sj = np.arange(seq // SEL_BLOCK)[:, None] * SEL_BLOCK
    ov = (ci < sj + SEL_BLOCK) & (ci + CMP_LEN > sj) & (np.arange(n_chunks)[None, :] < n_chunks - 1)
    return jnp.asarray(ov, dtype=BF16)


def _block_onehot(seq):
    return jnp.asarray(np.arange(seq)[:, None] // SEL_BLOCK == np.arange(LANES)[None, :], dtype=BF16)


def kernel(x, ffn1_norm, ffn1_wg, ffn1_wu, ffn1_wd, mix_norm, w_in, cmpk_pos, cmpk_w1, cmpk_w2, cmpv_pos, cmpv_w1, cmpv_w2, w_branch_nsa, w_branch_moba, w_out, ffn2_norm, ffn2_wg, ffn2_wu, ffn2_wd, final_norm):
    b, seq, d = x.shape
    depth = w_in.shape[0]
    assert d == D_MODEL and seq % TOKEN_TILE == 0 and seq // CMP_STRIDE == LANES
    t = b * seq
    n_chunks = seq // CMP_STRIDE
    cos_t, sin_t = _rope_tables(jnp.arange(seq))
    cos_c, sin_c = _rope_tables(jnp.arange(n_chunks) * CMP_STRIDE + (CMP_LEN - 1))
    ovt = _overlap_matrix_t(seq)
    blk = _block_onehot(seq)

    def gain(gs):
        return gs.reshape(depth, 1, d)

    ffn1 = (gain(ffn1_norm), ffn1_wg.astype(BF16), ffn1_wu.astype(BF16), ffn1_wd.astype(BF16))
    ffn2 = (gain(ffn2_norm), ffn2_wg.astype(BF16), ffn2_wu.astype(BF16), ffn2_wd.astype(BF16))
    w_attn, w_gate = _split_w_in(w_in)
    w_merge = (w_gate, w_branch_nsa.astype(BF16), w_branch_moba.astype(BF16), w_out.astype(BF16))
    fin = final_norm.reshape(1, d)

    xf = x.reshape(t, d)
    for l in range(depth):
        xf = _ffn(xf, l, *ffn1, fin, False)
        x_mix = xf
        qat, kc, vc, kk, vvt, qbt, kb, vbt, gst, kmean = _proj(xf, l, gain(mix_norm), w_attn, cos_t, sin_t, seq)
        kcb, vcb = _compress(kc, vc,
                             (cmpk_pos[l], cmpk_w1[l], cmpk_w2[l]),
                             (cmpv_pos[l], cmpv_w1[l], cmpv_w2[l]), cos_c, sin_c)
        a = _nsa(qat, gst, kcb, vcb.transpose(0, 2, 1), kk, vvt, ovt, blk)
        bm = _moba(qbt, kb, vbt, kmean.reshape(b, seq // MOBA_BLOCK, MOBA_WIDTH))
        xf = _merge(x_mix, l, gain(mix_norm), a, bm, *w_merge)
        xf = _ffn(xf, l, *ffn2, fin, l == depth - 1)
    return xf.reshape(b, seq, d)
```

```python
import functools

import jax
import jax.numpy as jnp
import numpy as np
from jax import lax
from jax.experimental import pallas as pl
from jax.experimental.pallas import tpu as pltpu

F32 = jnp.float32
BF16 = jnp.bfloat16

D_MODEL = 1024
HEAD_DIM = 64
HALF = HEAD_DIM // 2
NSA_HEADS = 8
NSA_GROUPS = 2
NSA_REP = NSA_HEADS // NSA_GROUPS
CMP_LEN = 32
CMP_STRIDE = 16
CMP_HIDDEN = 128
SEL_BLOCK = 64
SEL_TOPK = 16
WINDOW = 512
MOBA_HEADS = 8
MOBA_BLOCK = 256
MOBA_TOPK = 3
D_FF = 2816
ROPE_THETA = 10000.0
EPS = 1e-6
NEG = -1e30
TINY = 1e-30
FORCE_BONUS = 1e4
Q_SCALE = HEAD_DIM ** -0.5 * float(np.log2(np.e))

NSA_WIDTH = NSA_HEADS * HEAD_DIM
KV_WIDTH = NSA_GROUPS * HEAD_DIM
MOBA_WIDTH = MOBA_HEADS * HEAD_DIM
N_GATE_COLS = 3 * NSA_HEADS

LANES = 128
BF16_ROWS = 16
TOKEN_TILE = 1024
FFN_TILE = 1024
FF_CHUNK = 256
NSA_Q = 256
SEL_KEYS = 512
VMEM_LIMIT = 56 * 1024 * 1024

COL_QA = 0
COL_KC = COL_QA + NSA_WIDTH
COL_VC = COL_KC + KV_WIDTH
COL_KV4 = COL_VC + KV_WIDTH
COL_QB = COL_KV4 + 4 * KV_WIDTH
COL_KB = COL_QB + MOBA_WIDTH
COL_VB = COL_KB + MOBA_WIDTH
COL_GS = COL_VB + MOBA_WIDTH


def _rms(x, g):
    return x * lax.rsqrt(jnp.mean(x * x, axis=-1, keepdims=True) + EPS) * g


def _dot(a, b):
    return jnp.dot(a, b, preferred_element_type=F32)


def _dot_nt(a, b):
    return lax.dot_general(a, b, (((1,), (1,)), ((), ())), preferred_element_type=F32)


def _swap_halves(z):
    lane = lax.broadcasted_iota(jnp.int32, z.shape, 1)
    lo = (lane & (HEAD_DIM - 1)) < HALF
    return jnp.where(lo, pltpu.roll(z, LANES - HALF, 1), pltpu.roll(z, HALF, 1))


def _rope_slab(z, cos, sin):
    parts = []
    for c in range(z.shape[1] // LANES):
        zc = z[:, c * LANES:(c + 1) * LANES]
        parts.append(zc * cos + _swap_halves(zc) * sin)
    return parts[0] if len(parts) == 1 else jnp.concatenate(parts, axis=1)


def _slabs(s, n):
    return [s[:, i * n:(i + 1) * n] for i in range(s.shape[1] // n)]


def _masked_softmax_keys(s, mask):
    out = []
    for x in _slabs(s, mask.shape[1]):
        x = jnp.where(mask, x, NEG)
        m = jnp.max(x, axis=0, keepdims=True)
        p = jnp.where(mask, jnp.exp2(x - m), 0.0)
        out.append(p / jnp.maximum(jnp.sum(p, axis=0, keepdims=True), TINY))
    return jnp.concatenate(out, axis=1)


def _mask_keys(s, mask):
    return jnp.concatenate([jnp.where(mask, x, NEG) for x in _slabs(s, mask.shape[1])], axis=1)


def _online_update(carry, s, vt):
    m, l, acc = carry
    m_new = jnp.maximum(m, jnp.max(s, axis=0, keepdims=True))
    alpha = jnp.exp2(m - m_new)
    p = jnp.exp2(s - m_new).astype(BF16)
    ones_rows = jnp.where(lax.broadcasted_iota(jnp.int32, (BF16_ROWS, vt.shape[1]), 0) == 0, 1.0, 0.0).astype(BF16)
    pv = _dot(jnp.concatenate([vt, ones_rows], axis=0), p)
    return m_new, alpha * l + pv[HEAD_DIM:HEAD_DIM + 1], alpha * acc + pv[:HEAD_DIM]


def _online_init(n):
    return jnp.full((1, n), NEG, F32), jnp.zeros((1, n), F32), jnp.zeros((HEAD_DIM, n), F32)


def _rank_count_step(v, row, cnt, j):
    r = v[j:j + 1, :]
    return cnt + jnp.where(r > v, 1.0, jnp.where(r == v, jnp.where(row > j, 1.0, 0.0), 0.0))


def _rank_select(v, n_candidates, k):
    row = lax.broadcasted_iota(jnp.int32, v.shape, 0)
    cnt = jnp.zeros(v.shape, F32)
    for j in range(n_candidates):
        cnt = _rank_count_step(v, row, cnt, j)
    return cnt < float(k)


def _ffn_kernel(x_ref, g_ref, wg_ref, wu_ref, wd_ref, fin_ref, o_ref, t_ref, *, final_norm):
    x = x_ref[...]
    h = (x * g_ref[...]).astype(BF16)
    inv = lax.rsqrt(jnp.mean(x * x, axis=-1, keepdims=True) + EPS)
    for f in range(D_FF // FF_CHUNK):
        c = slice(f * FF_CHUNK, (f + 1) * FF_CHUNK)
        a = _dot(h, wg_ref[:, c]) * inv
        b = _dot(h, wu_ref[:, c]) * inv
        t_ref[:, c] = (a * jax.nn.sigmoid(a) * b).astype(BF16)
    y = x + 0.5 * _dot(t_ref[...], wd_ref[...])
    if final_norm:
        y = _rms(y, fin_ref[...])
    o_ref[...] = y


def _resident(shape):
    nd = len(shape)
    return pl.BlockSpec(shape, lambda *_: (0,) * nd, pipeline_mode=pl.Buffered(1))


def _layer_resident(stacked, layer):
    return pl.BlockSpec((None,) + stacked.shape[1:], lambda *_: (layer, 0, 0), pipeline_mode=pl.Buffered(1))


def _ffn(x, layer, g, wg, wu, wd, fin, final_norm):
    t = x.shape[0]
    row = pl.BlockSpec((FFN_TILE, D_MODEL), lambda i: (i, 0))
    return pl.pallas_call(
        functools.partial(_ffn_kernel, final_norm=final_norm),
        grid=(t // FFN_TILE,),
        in_specs=[row, _layer_resident(g, layer), _layer_resident(wg, layer), _layer_resident(wu, layer),
                  _layer_resident(wd, layer), _resident((1, D_MODEL))],
        out_specs=row,
        out_shape=jax.ShapeDtypeStruct(x.shape, F32),
        scratch_shapes=[pltpu.VMEM((FFN_TILE, D_FF), BF16)],
        compiler_params=pltpu.CompilerParams(dimension_semantics=("parallel",),
                                             vmem_limit_bytes=VMEM_LIMIT),
        name="ffn",
    )(x, g, wg, wu, wd, fin)


def _proj_kernel(x_ref, g_ref, w_ref, cos_ref, sin_ref,
                 qat_ref, kc_ref, vc_ref, kk_ref, vvt_ref, qbt_ref, kb_ref, vbt_ref, gst_ref, kmean_ref):
    h = _rms(x_ref[...], g_ref[...]).astype(BF16)
    cos = cos_ref[...]
    sin = sin_ref[...]

    def mm(c0, n):
        return _dot(h, w_ref[:, c0:c0 + n])

    def feature_major(z, dtype=BF16):
        return jnp.transpose(z).astype(dtype)

    qat_ref[0] = feature_major(_rope_slab(mm(COL_QA, NSA_WIDTH), cos, sin) * Q_SCALE)
    kcvc = mm(COL_KC, 2 * KV_WIDTH)
    kc_ref[0] = kcvc[:, :KV_WIDTH]
    vc_ref[0] = kcvc[:, KV_WIDTH:]
    kv4 = mm(COL_KV4, 4 * KV_WIDTH)
    kk_ref[0] = jnp.concatenate(
        [_rope_slab(kv4[:, 0:KV_WIDTH], cos, sin), _rope_slab(kv4[:, 2 * KV_WIDTH:3 * KV_WIDTH], cos, sin)],
        axis=1).astype(BF16)
    vvt_ref[0] = feature_major(jnp.concatenate([kv4[:, KV_WIDTH:2 * KV_WIDTH], kv4[:, 3 * KV_WIDTH:]], axis=1))
    qbt_ref[0] = feature_major(_rope_slab(mm(COL_QB, MOBA_WIDTH), cos, sin) * Q_SCALE)
    kb = _rope_slab(mm(COL_KB, MOBA_WIDTH), cos, sin)
    kb_ref[0] = kb.astype(BF16)
    means = [jnp.mean(kb[i * MOBA_BLOCK:(i + 1) * MOBA_BLOCK], axis=0, keepdims=True)
             for i in range(TOKEN_TILE // MOBA_BLOCK)]
    kmean_ref[0] = jnp.concatenate(means, axis=0)
    vbt_ref[0] = feature_major(mm(COL_VB, MOBA_WIDTH))
    gst_ref[0] = feature_major(jax.nn.sigmoid(mm(COL_GS, LANES)), F32)


W_IN_ROWS = 256


def _split_w_in_kernel(w_ref, attn_ref, gate_ref):
    off_ga = NSA_WIDTH
    off_rest = off_ga + N_GATE_COLS
    off_gates = off_rest + 6 * KV_WIDTH + 3 * MOBA_WIDTH
    w = w_ref[...]
    attn_ref[:, :off_ga] = w[:, :off_ga].astype(BF16)
    attn_ref[:, off_ga:COL_GS] = w[:, off_rest:off_gates].astype(BF16)
    pad = jnp.zeros((w.shape[0], LANES - N_GATE_COLS), F32)
    attn_ref[:, COL_GS:] = jnp.concatenate([w[:, off_ga:off_rest], pad], axis=1).astype(BF16)
    gate_ref[...] = w[:, off_gates:].astype(BF16)


def _split_w_in(w_in):
    depth, d, n = w_in.shape
    steps = d // W_IN_ROWS
    return pl.pallas_call(
        _split_w_in_kernel,
        grid=(depth, steps),
        in_specs=[pl.BlockSpec((None, W_IN_ROWS, n), lambda l, i: (l, i, 0))],
        out_specs=(pl.BlockSpec((None, W_IN_ROWS, COL_GS + LANES), lambda l, i: (l, i, 0)),
                   pl.BlockSpec((None, W_IN_ROWS, 2 * D_MODEL), lambda l, i: (l, i, 0))),
        out_shape=(jax.ShapeDtypeStruct((depth, d, COL_GS + LANES), BF16),
                   jax.ShapeDtypeStruct((depth, d, 2 * D_MODEL), BF16)),
        compiler_params=pltpu.CompilerParams(dimension_semantics=("parallel", "parallel"),
                                             vmem_limit_bytes=VMEM_LIMIT),
        name="split_w_in",
    )(w_in)


def _proj(x, layer, g, w, cos_t, sin_t, seq):
    t = x.shape[0]
    tiles_per_seq = seq // TOKEN_TILE
    blocks_per_tile = TOKEN_TILE // MOBA_BLOCK

    def row(n):
        return pl.BlockSpec((TOKEN_TILE, n), lambda i: (i, 0))

    def token_major(n, dtype=BF16):
        return (jax.ShapeDtypeStruct((t // seq, seq, n), dtype),
                pl.BlockSpec((1, TOKEN_TILE, n), lambda i: (i // tiles_per_seq, i % tiles_per_seq, 0)))

    def feature_major(n, dtype=BF16):
        return (jax.ShapeDtypeStruct((t // seq, n, seq), dtype),
                pl.BlockSpec((1, n, TOKEN_TILE), lambda i: (i // tiles_per_seq, 0, i % tiles_per_seq)))

    tab = pl.BlockSpec((TOKEN_TILE, LANES), lambda i: (i % tiles_per_seq, 0))
    outs = (feature_major(NSA_WIDTH), token_major(KV_WIDTH, F32), token_major(KV_WIDTH, F32),
            token_major(2 * KV_WIDTH),
            feature_major(2 * KV_WIDTH), feature_major(MOBA_WIDTH), token_major(MOBA_WIDTH),
            feature_major(MOBA_WIDTH), feature_major(LANES, F32),
            (jax.ShapeDtypeStruct((t // TOKEN_TILE, blocks_per_tile, MOBA_WIDTH), F32),
             pl.BlockSpec((1, blocks_per_tile, MOBA_WIDTH), lambda i: (i, 0, 0))))
    out_shape, out_specs = zip(*outs)
    return pl.pallas_call(
        _proj_kernel,
        grid=(t // TOKEN_TILE,),
        in_specs=[row(D_MODEL), _layer_resident(g, layer), _layer_resident(w, layer), tab, tab],
        out_specs=out_specs,
        out_shape=out_shape,
        compiler_params=pltpu.CompilerParams(dimension_semantics=("parallel",),
                                             vmem_limit_bytes=VMEM_LIMIT),
        name="proj",
    )(x, g, w, cos_t, sin_t)


def _gelu_tanh(x):
    return 0.5 * x * (1.0 + jnp.tanh(np.sqrt(2.0 / np.pi).astype(np.float32) * (x + 0.044715 * (x * x * x))))


def _compress_kernel(kc_ref, vc_ref, kw1_ref, kw1f_ref, kpos_ref, kw2_ref,
                     vw1_ref, vw1f_ref, vpos_ref, vw2_ref, cos_ref, sin_ref, kcb_ref, vcb_ref):
    n_chunks = kc_ref.shape[1] // CMP_STRIDE

    def compress(x_ref, w1_ref, w1f_ref, pos_ref, w2_ref):
        u = jnp.zeros((n_chunks, 4 * CMP_HIDDEN), F32)
        for p in range(CMP_STRIDE):
            rows = x_ref[0, pl.ds(p, n_chunks, stride=CMP_STRIDE), :].astype(BF16)
            u = u + _dot(rows, w1_ref[p * KV_WIDTH:(p + 1) * KV_WIDTH, :])
        first = u[:, :2 * CMP_HIDDEN]
        second = pltpu.roll(u[:, 2 * CMP_HIDDEN:], n_chunks - 1, 0)
        pos = jnp.broadcast_to(pos_ref[...], (8, CMP_LEN * HEAD_DIM)).astype(BF16)
        bias = _dot(pos, w1f_ref[...])[0:1]
        pre = first + second + jnp.concatenate([bias, bias], axis=1)
        return _dot(_gelu_tanh(pre).astype(BF16), w2_ref[...])

    k = compress(kc_ref, kw1_ref, kw1f_ref, kpos_ref, kw2_ref)
    kcb_ref[0] = _rope_slab(k, cos_ref[...], sin_ref[...]).astype(BF16)
    vcb_ref[0] = compress(vc_ref, vw1_ref, vw1f_ref, vpos_ref, vw2_ref).astype(BF16)


def _compress_weights(pos, w1, w2):
    w1r = w1.reshape(2, CMP_STRIDE, HEAD_DIM, CMP_HIDDEN)
    eye = jnp.eye(NSA_GROUPS, dtype=w1.dtype)
    big = jnp.einsum('hpdj,ge->pgdhej', w1r, eye)
    big = big.reshape(CMP_STRIDE * KV_WIDTH, 2 * NSA_GROUPS * CMP_HIDDEN).astype(BF16)
    w2big = jnp.einsum('jd,ge->gjed', w2, eye).reshape(NSA_GROUPS * CMP_HIDDEN, KV_WIDTH).astype(BF16)
    return big, w1.astype(BF16), pos.reshape(1, CMP_LEN * HEAD_DIM), w2big


def _compress(kc, vc, kparams, vparams, cos_c, sin_c):
    b, seq, width = kc.shape
    n_chunks = seq // CMP_STRIDE
    args = (kc, vc) + _compress_weights(*kparams) + _compress_weights(*vparams) + (cos_c, sin_c)
    x_spec = pl.BlockSpec((1, seq, width), lambda i: (i, 0, 0))
    in_specs = [x_spec, x_spec] + [_resident(a.shape) for a in args[2:]]
    o_spec = pl.BlockSpec((1, n_chunks, KV_WIDTH), lambda i: (i, 0, 0))
    o_shape = jax.ShapeDtypeStruct((b, n_chunks, KV_WIDTH), BF16)
    return pl.pallas_call(
        _compress_kernel,
        grid=(b,),
        in_specs=in_specs,
        out_specs=(o_spec, o_spec),
        out_shape=(o_shape, o_shape),
        compiler_params=pltpu.CompilerParams(dimension_semantics=("parallel",),
                                             vmem_limit_bytes=VMEM_LIMIT),
        name="compress",
    )(*args)


def _nsa_kernel(qat_ref, gst_ref, kcb_ref, vcbt_ref, kk_ref, vst_ref, vwt_ref, ovt_ref, blk_ref, o_ref,
                s_ref, m_ref, l_ref, acc_ref, *, seq):
    qi = pl.program_id(1)
    t0 = qi * NSA_Q
    cols = NSA_REP * NSA_Q
    n_cmp = seq // CMP_STRIDE - 1
    n_sel = seq // SEL_BLOCK
    gst = gst_ref[0]
    ovt = ovt_ref[...]

    def key_and_query_pos(n, k0):
        shape = (n, NSA_Q)
        return k0 + lax.broadcasted_iota(jnp.int32, shape, 0), t0 + lax.broadcasted_iota(jnp.int32, shape, 1)

    groups = range(NSA_GROUPS)
    d_rows = [slice(g * HEAD_DIM, (g + 1) * HEAD_DIM) for g in groups]
    g_cols = [slice(g * cols, (g + 1) * cols) for g in groups]

    zeros_d = jnp.zeros((HEAD_DIM, NSA_Q), BF16)
    heads = []
    for n in range(NSA_HEADS):
        x = qat_ref[0, n * HEAD_DIM:(n + 1) * HEAD_DIM, :]
        heads.append(jnp.concatenate([x, zeros_d] if n < NSA_REP else [zeros_d, x], axis=0))
    q = jnp.concatenate(heads, axis=1)

    n_win = WINDOW + NSA_Q
    w0 = pl.multiple_of(jnp.maximum(t0 - WINDOW, 0), NSA_Q)
    s = _dot(kcb_ref[0], q)
    k_win = kk_ref[0, pl.ds(w0, n_win), KV_WIDTH:2 * KV_WIDTH]
    s_win_0 = _dot(k_win, q[:, g_cols[0]])

    n_idx, tq = key_and_query_pos(s.shape[0], 0)
    cmask = (n_idx * CMP_STRIDE + (CMP_LEN - 1) <= tq) & (n_idx < n_cmp)
    p_cmp = _masked_softmax_keys(s, cmask).astype(BF16)
    o_cmp = [_dot(vcbt_ref[0, d_rows[g], :], p_cmp[:, g_cols[g]]) for g in groups]

    pi = _dot(ovt, p_cmp)
    s_win = jnp.concatenate([s_win_0, _dot(k_win, q[:, g_cols[1]])], axis=1)
    jj, tsel = key_and_query_pos(n_sel, 0)
    tblk = tsel >> 6
    bonus = jnp.where((jj == 0) | (jj == tblk) | (jj == tblk - 1), FORCE_BONUS, 0.0)
    pad = jnp.zeros((LANES - n_sel, NSA_Q), BF16)
    biases = []
    for g in groups:
        imp = pi[:, g * cols:g * cols + NSA_Q]
        for r in range(1, NSA_REP):
            imp = imp + pi[:, g * cols + r * NSA_Q:g * cols + (r + 1) * NSA_Q]
        imp = jnp.where(jj <= tblk, imp + bonus, NEG)
        bias = jnp.where(_rank_select(imp, n_sel, SEL_TOPK) & (jj <= tblk), 0.0, NEG)
        biases.append(jnp.concatenate([bias.astype(BF16), pad], axis=0))
    q_sel = jnp.concatenate(
        [jnp.concatenate([heads[n], biases[n // NSA_REP]], axis=0) for n in range(NSA_HEADS)], axis=1)

    def tile_start(c):
        return pl.multiple_of(c * SEL_KEYS, SEL_KEYS)

    def sel_scores(c):
        k0 = tile_start(c)
        keys = jnp.concatenate([kk_ref[0, pl.ds(k0, SEL_KEYS), 0:KV_WIDTH], blk_ref[pl.ds(k0, SEL_KEYS), :]],
                               axis=1)
        return _dot(keys, q_sel)

    def sel_update(c, s):
        for g in groups:
            m, l, acc = _online_update((m_ref[g], l_ref[g], acc_ref[g]), s[:, g_cols[g]],
                                       vst_ref[0, d_rows[g], pl.ds(tile_start(c), SEL_KEYS)])
            m_ref[g], l_ref[g], acc_ref[g] = m, l, acc

    n_past = qi // (SEL_KEYS // NSA_Q)
    m_ref[...] = jnp.full(m_ref.shape, NEG, F32)
    l_ref[...] = jnp.zeros(l_ref.shape, F32)
    acc_ref[...] = jnp.zeros(acc_ref.shape, F32)
    s_ref[0] = sel_scores(0)

    kpos, tq = key_and_query_pos(n_win, w0)
    s_win = _mask_keys(s_win, (kpos <= tq) & (tq - kpos < WINDOW))
    p_win = jnp.exp2(s_win - jnp.max(s_win, axis=0, keepdims=True)).astype(BF16)
    ones_rows = jnp.where(lax.broadcasted_iota(jnp.int32, (BF16_ROWS, n_win), 0) == 0, 1.0, 0.0).astype(BF16)
    o_win = []
    for g in groups:
        pv = _dot(jnp.concatenate([vwt_ref[0, d_rows[g], pl.ds(w0, n_win)], ones_rows], axis=0), p_win[:, g_cols[g]])
        o_win.append(pv[:HEAD_DIM] / jnp.maximum(pv[HEAD_DIM:HEAD_DIM + 1], TINY))

    for c in range(seq // SEL_KEYS - 1):
        @pl.when(c < n_past)
        def _():
            s_ref[(c + 1) % 2] = sel_scores(c + 1)
            sel_update(c, s_ref[c % 2])

    kpos, tq = key_and_query_pos(SEL_KEYS, tile_start(n_past))
    sel_update(n_past, _mask_keys(s_ref[n_past % 2], kpos <= tq))
    o_sel = [acc_ref[g] / jnp.maximum(l_ref[g], TINY) for g in groups]

    outs = []
    for n in range(NSA_HEADS):
        g, r = divmod(n, NSA_REP)
        sl = slice(r * NSA_Q, (r + 1) * NSA_Q)
        outs.append(gst[3 * n:3 * n + 1, :] * o_cmp[g][:, sl] + gst[3 * n + 1:3 * n + 2, :] * o_sel[g][:, sl]
                    + gst[3 * n + 2:3 * n + 3, :] * o_win[g][:, sl])
    o_ref[0] = jnp.transpose(jnp.concatenate(outs, axis=0)).astype(BF16)


def _nsa(qat, gst, kcb, vcbt, kk, vvt, ovt, blk):
    b, _, seq = qat.shape
    n_chunks = kcb.shape[1]
    return pl.pallas_call(
        functools.partial(_nsa_kernel, seq=seq),
        grid=(b, seq // NSA_Q),
        in_specs=[pl.BlockSpec((1, NSA_WIDTH, NSA_Q), lambda i, j: (i, 0, j)),
                  pl.BlockSpec((1, LANES, NSA_Q), lambda i, j: (i, 0, j)),
                  pl.BlockSpec((1, n_chunks, KV_WIDTH), lambda i, j: (i, 0, 0)),
                  pl.BlockSpec((1, KV_WIDTH, n_chunks), lambda i, j: (i, 0, 0)),
                  pl.BlockSpec((1, seq, 2 * KV_WIDTH), lambda i, j: (i, 0, 0)),
                  pl.BlockSpec((1, KV_WIDTH, seq), lambda i, j: (i, 0, 0)),
                  pl.BlockSpec((1, KV_WIDTH, seq), lambda i, j: (i, 1, 0)),
                  pl.BlockSpec(ovt.shape, lambda i, j: (0, 0)),
                  pl.BlockSpec(blk.shape, lambda i, j: (0, 0))],
        out_specs=pl.BlockSpec((1, NSA_Q, NSA_WIDTH), lambda i, j: (i, j, 0)),
        out_shape=jax.ShapeDtypeStruct((b, seq, NSA_WIDTH), BF16),
        scratch_shapes=[pltpu.VMEM((2, SEL_KEYS, NSA_HEADS * NSA_Q), F32),
                        pltpu.VMEM((NSA_GROUPS, 1, NSA_REP * NSA_Q), F32),
                        pltpu.VMEM((NSA_GROUPS, 1, NSA_REP * NSA_Q), F32),
                        pltpu.VMEM((NSA_GROUPS, HEAD_DIM, NSA_REP * NSA_Q), F32)],
        compiler_params=pltpu.CompilerParams(dimension_semantics=("parallel", "parallel"),
                                             vmem_limit_bytes=VMEM_LIMIT),
        name="nsa",
    )(qat, gst, kcb, vcbt, kk, vvt, vvt, ovt, blk)


def _moba_kernel(qt_ref, k_ref, vt_ref, kmean_ref, o_ref, s_ref, m_ref, l_ref, acc_ref):
    c = pl.program_id(1)
    nb = kmean_ref.shape[1]
    k_own = pl.multiple_of(c * MOBA_BLOCK, MOBA_BLOCK)
    jj = lax.broadcasted_iota(jnp.int32, (BF16_ROWS, MOBA_BLOCK), 0)
    past = jj < c
    pad = jnp.zeros((BF16_ROWS - nb, LANES), F32)
    zeros_d = jnp.zeros((HEAD_DIM, MOBA_BLOCK), BF16)
    ones_rows = jnp.where(lax.broadcasted_iota(jnp.int32, (BF16_ROWS, MOBA_BLOCK), 0) == 0, 1.0, 0.0).astype(BF16)

    def pair(h):
        return slice((h // 2) * LANES, (h // 2 + 1) * LANES)

    qs, gscs = [], []
    for h in range(MOBA_HEADS):
        x = qt_ref[0, h * HEAD_DIM:(h + 1) * HEAD_DIM, :]
        q = jnp.concatenate([x, zeros_d] if h % 2 == 0 else [zeros_d, x], axis=0)
        km = jnp.concatenate([kmean_ref[0, :, pair(h)], pad], axis=0).astype(BF16)
        qs.append(q)
        gscs.append(_dot(km, q))
    chosen = [jnp.where(_rank_select(jnp.where(past, g, NEG), nb, MOBA_TOPK) & past, 1.0, 0.0) for g in gscs]

    m_ref[...] = jnp.full(m_ref.shape, NEG, F32)
    l_ref[...] = jnp.zeros(l_ref.shape, F32)
    acc_ref[...] = jnp.zeros(acc_ref.shape, F32)

    def block_start(j):
        return pl.multiple_of(j * MOBA_BLOCK, MOBA_BLOCK)

    def scores(j, slot):
        for h in range(MOBA_HEADS):
            s_ref[slot, h] = _dot(k_ref[0, pl.ds(block_start(j), MOBA_BLOCK), pair(h)], qs[h])

    def past_update(j, slot):
        for h in range(MOBA_HEADS):
            rows = slice(h * HEAD_DIM, (h + 1) * HEAD_DIM)
            s = s_ref[slot, h]
            pick = jnp.sum(jnp.where(jj == j, chosen[h], 0.0), axis=0, keepdims=True) > 0.5
            m = m_ref[h]
            m_new = jnp.maximum(m, jnp.where(pick, jnp.max(s, axis=0, keepdims=True), NEG))
            alpha = jnp.exp2(m - m_new)
            p = jnp.exp2(s - jnp.where(pick, m_new, jnp.inf))
            m_ref[h] = m_new
            pv = _dot(jnp.concatenate([vt_ref[0, rows, pl.ds(block_start(j), MOBA_BLOCK)], ones_rows], axis=0),
                      p.astype(BF16))
            l_ref[h] = alpha * l_ref[h] + pv[HEAD_DIM:HEAD_DIM + 1]
            acc_ref[rows, :] = alpha * acc_ref[rows, :] + pv[:HEAD_DIM]

    scores(0, 0)

    def two_blocks(i, _):
        j = 2 * i
        scores(j + 1, 1)
        past_update(j, 0)
        scores(j + 2, 0)
        past_update(j + 1, 1)
        return 0

    lax.fori_loop(0, c // 2, two_blocks, 0)

    @pl.when(c % 2 == 1)
    def _():
        scores(c, 1)
        past_update(c - 1, 0)

    kq = (MOBA_BLOCK, MOBA_BLOCK)
    causal = lax.broadcasted_iota(jnp.int32, kq, 0) <= lax.broadcasted_iota(jnp.int32, kq, 1)
    for h in range(MOBA_HEADS):
        rows = slice(h * HEAD_DIM, (h + 1) * HEAD_DIM)
        carry = (m_ref[h], l_ref[h], acc_ref[rows, :])
        _, l, acc = _online_update(carry, jnp.where(causal, s_ref[c % 2, h], NEG),
                                   vt_ref[0, rows, pl.ds(k_own, MOBA_BLOCK)])
        acc_ref[rows, :] = acc / jnp.maximum(l, TINY)
    o_ref[0] = jnp.transpose(acc_ref[...]).astype(BF16)


def _moba(qbt, kb, vbt, kmean):
    b, _, seq = qbt.shape
    nb = seq // MOBA_BLOCK
    return pl.pallas_call(
        _moba_kernel,
        grid=(b, nb),
        in_specs=[pl.BlockSpec((1, MOBA_WIDTH, MOBA_BLOCK), lambda i, j: (i, 0, j)),
                  pl.BlockSpec((1, seq, MOBA_WIDTH), lambda i, j: (i, 0, 0)),
                  pl.BlockSpec((1, MOBA_WIDTH, seq), lambda i, j: (i, 0, 0)),
                  pl.BlockSpec((1, nb, MOBA_WIDTH), lambda i, j: (i, 0, 0))],
        out_specs=pl.BlockSpec((1, MOBA_BLOCK, MOBA_WIDTH), lambda i, j: (i, j, 0)),
        out_shape=jax.ShapeDtypeStruct((b, seq, MOBA_WIDTH), BF16),
        scratch_shapes=[pltpu.VMEM((2, MOBA_HEADS, MOBA_BLOCK, MOBA_BLOCK), F32),
                        pltpu.VMEM((MOBA_HEADS, 1, MOBA_BLOCK), F32), pltpu.VMEM((MOBA_HEADS, 1, MOBA_BLOCK), F32),
                        pltpu.VMEM((MOBA_WIDTH, MOBA_BLOCK), F32)],
        compiler_params=pltpu.CompilerParams(dimension_semantics=("parallel", "parallel"),
                                             vmem_limit_bytes=VMEM_LIMIT),
        name="moba",
    )(qbt, kb, vbt, kmean)


def _merge_kernel(x_ref, g_ref, a_ref, b_ref, wgate_ref, pa_ref, pb_ref, wo_ref, o_ref):
    x = x_ref[...]
    h = _rms(x, g_ref[...]).astype(BF16)
    ya = _dot(a_ref[0], pa_ref[...])
    yb = _dot(b_ref[0], pb_ref[...])
    merged = (jax.nn.sigmoid(_dot(h, wgate_ref[:, :D_MODEL])) * ya
              + jax.nn.sigmoid(_dot(h, wgate_ref[:, D_MODEL:])) * yb)
    o_ref[...] = x + _dot(merged.astype(BF16), wo_ref[...])


def _merge(x, layer, g, a, bm, wgate, pa, pb, wo):
    t = x.shape[0]
    tiles_per_seq = a.shape[1] // TOKEN_TILE

    def row(n):
        return pl.BlockSpec((TOKEN_TILE, n), lambda i: (i, 0))

    def seq_row(n):
        return pl.BlockSpec((1, TOKEN_TILE, n), lambda i: (i // tiles_per_seq, i % tiles_per_seq, 0))

    return pl.pallas_call(
        _merge_kernel,
        grid=(t // TOKEN_TILE,),
        in_specs=[row(D_MODEL), _layer_resident(g, layer), seq_row(NSA_WIDTH), seq_row(MOBA_WIDTH)]
                 + [_layer_resident(w, layer) for w in (wgate, pa, pb, wo)],
        out_specs=row(D_MODEL),
        out_shape=jax.ShapeDtypeStruct(x.shape, F32),
        compiler_params=pltpu.CompilerParams(dimension_semantics=("parallel",),
                                             vmem_limit_bytes=VMEM_LIMIT),
        name="merge",
    )(x, g, a, bm, wgate, pa, pb, wo)


def _rope_tables(pos):
    inv = ROPE_THETA ** (-jnp.arange(0, HEAD_DIM, 2, dtype=F32) / HEAD_DIM)
    ang = pos.astype(F32)[:, None] * inv[None, :]
    cos, sin = jnp.cos(ang), jnp.sin(ang)
    reps = LANES // HEAD_DIM
    return (jnp.tile(jnp.concatenate([cos, cos], axis=1), (1, reps)),
            jnp.tile(jnp.concatenate([-sin, sin], axis=1), (1, reps)))


def _overlap_matrix_t(seq):
    n_chunks = seq // CMP_STRIDE
    ci = np.arange(n_chunks)[None, :] * CMP_STRIDE
    sj = np.arange(seq // SEL_BLOCK)[:, None] * SEL_BLOCK
    ov = (ci < sj + SEL_BLOCK) & (ci + CMP_LEN > sj) & (np.arange(n_chunks)[None, :] < n_chunks - 1)
    return jnp.asarray(ov, dtype=BF16)


def _block_onehot(seq):
    return jnp.asarray(np.arange(seq)[:, None] // SEL_BLOCK == np.arange(LANES)[None, :], dtype=BF16)


def kernel(x, ffn1_norm, ffn1_wg, ffn1_wu, ffn1_wd, mix_norm, w_in, cmpk_pos, cmpk_w1, cmpk_w2, cmpv_pos, cmpv_w1, cmpv_w2, w_branch_nsa, w_branch_moba, w_out, ffn2_norm, ffn2_wg, ffn2_wu, ffn2_wd, final_norm):
    b, seq, d = x.shape
    depth = w_in.shape[0]
    assert d == D_MODEL and seq % TOKEN_TILE == 0 and seq // CMP_STRIDE == LANES
    t = b * seq
    n_chunks = seq // CMP_STRIDE
    cos_t, sin_t = _rope_tables(jnp.arange(seq))
    cos_c, sin_c = _rope_tables(jnp.arange(n_chunks) * CMP_STRIDE + (CMP_LEN - 1))
    ovt = _overlap_matrix_t(seq)
    blk = _block_onehot(seq)

    def gain(gs):
        return gs.reshape(depth, 1, d)

    ffn1 = (gain(ffn1_norm), ffn1_wg.astype(BF16), ffn1_wu.astype(BF16), ffn1_wd.astype(BF16))
    ffn2 = (gain(ffn2_norm), ffn2_wg.astype(BF16), ffn2_wu.astype(BF16), ffn2_wd.astype(BF16))
    w_attn, w_gate = _split_w_in(w_in)
    w_merge = (w_gate, w_branch_nsa.astype(BF16), w_branch_moba.astype(BF16), w_out.astype(BF16))
    fin = final_norm.reshape(1, d)

    xf = x.reshape(t, d)
    for l in range(depth):
        xf = _ffn(xf, l, *ffn1, fin, False)
        x_mix = xf
        qat, kc, vc, kk, vvt, qbt, kb, vbt, gst, kmean = _proj(xf, l, gain(mix_norm), w_attn, cos_t, sin_t, seq)
        kcb, vcb = _compress(kc, vc,
                             (cmpk_pos[l], cmpk_w1[l], cmpk_w2[l]),
                             (cmpv_pos[l], cmpv_w1[l], cmpv_w2[l]), cos_c, sin_c)
        a = _nsa(qat, gst, kcb, vcb.transpose(0, 2, 1), kk, vvt, ovt, blk)
        bm = _moba(qbt, kb, vbt, kmean.reshape(b, seq // MOBA_BLOCK, MOBA_WIDTH))
        xf = _merge(x_mix, l, gain(mix_norm), a, bm, *w_merge)
        xf = _ffn(xf, l, *ffn2, fin, l == depth - 1)
    return xf.reshape(b, seq, d)
```

```python
import functools

import jax
import jax.numpy as jnp
import numpy as np
from jax import lax
from jax.experimental import pallas as pl
from jax.experimental.pallas import tpu as pltpu

F32 = jnp.float32
BF16 = jnp.bfloat16

D_MODEL = 1024
HEAD_DIM = 64
HALF = HEAD_DIM // 2
NSA_HEADS = 8
NSA_GROUPS = 2
NSA_REP = NSA_HEADS // NSA_GROUPS
CMP_LEN = 32
CMP_STRIDE = 16
CMP_HIDDEN = 128
SEL_BLOCK = 64
SEL_TOPK = 16
WINDOW = 512
MOBA_HEADS = 8
MOBA_BLOCK = 256
MOBA_TOPK = 3
D_FF = 2816
ROPE_THETA = 10000.0
EPS = 1e-6
NEG = -1e30
TINY = 1e-30
FORCE_BONUS = 1e4
Q_SCALE = HEAD_DIM ** -0.5 * float(np.log2(np.e))

NSA_WIDTH = NSA_HEADS * HEAD_DIM
KV_WIDTH = NSA_GROUPS * HEAD_DIM
MOBA_WIDTH = MOBA_HEADS * HEAD_DIM
N_GATE_COLS = 3 * NSA_HEADS

LANES = 128
BF16_ROWS = 16
TOKEN_TILE = 1024
FFN_TILE = 1024
FF_CHUNK = 256
NSA_Q = 256
SEL_KEYS = 512
VMEM_LIMIT = 56 * 1024 * 1024

COL_QA = 0
COL_KC = COL_QA + NSA_WIDTH
COL_VC = COL_KC + KV_WIDTH
COL_KV4 = COL_VC + KV_WIDTH
COL_QB = COL_KV4 + 4 * KV_WIDTH
COL_KB = COL_QB + MOBA_WIDTH
COL_VB = COL_KB + MOBA_WIDTH
COL_GS = COL_VB + MOBA_WIDTH


def _rms(x, g):
    return x * lax.rsqrt(jnp.mean(x * x, axis=-1, keepdims=True) + EPS) * g


def _dot(a, b):
    return jnp.dot(a, b, preferred_element_type=F32)


def _dot_nt(a, b):
    return lax.dot_general(a, b, (((1,), (1,)), ((), ())), preferred_element_type=F32)


def _swap_halves(z):
    lane = lax.broadcasted_iota(jnp.int32, z.shape, 1)
    lo = (lane & (HEAD_DIM - 1)) < HALF
    return jnp.where(lo, pltpu.roll(z, LANES - HALF, 1), pltpu.roll(z, HALF, 1))


def _rope_slab(z, cos, sin):
    parts = []
    for c in range(z.shape[1] // LANES):
        zc = z[:, c * LANES:(c + 1) * LANES]
        parts.append(zc * cos + _swap_halves(zc) * sin)
    return parts[0] if len(parts) == 1 else jnp.concatenate(parts, axis=1)


def _slabs(s, n):
    return [s[:, i * n:(i + 1) * n] for i in range(s.shape[1] // n)]


def _masked_softmax_keys(s, mask):
    out = []
    for x in _slabs(s, mask.shape[1]):
        x = jnp.where(mask, x, NEG)
        m = jnp.max(x, axis=0, keepdims=True)
        p = jnp.where(mask, jnp.exp2(x - m), 0.0)
        out.append(p / jnp.maximum(jnp.sum(p, axis=0, keepdims=True), TINY))
    return jnp.concatenate(out, axis=1)


def _mask_keys(s, mask):
    return jnp.concatenate([jnp.where(mask, x, NEG) for x in _slabs(s, mask.shape[1])], axis=1)


def _online_update(carry, s, vt):
    m, l, acc = carry
    m_new = jnp.maximum(m, jnp.max(s, axis=0, keepdims=True))
    alpha = jnp.exp2(m - m_new)
    p = jnp.exp2(s - m_new).astype(BF16)
    ones_rows = jnp.where(lax.broadcasted_iota(jnp.int32, (BF16_ROWS, vt.shape[1]), 0) == 0, 1.0, 0.0).astype(BF16)
    pv = _dot(jnp.concatenate([vt, ones_rows], axis=0), p)
    return m_new, alpha * l + pv[HEAD_DIM:HEAD_DIM + 1], alpha * acc + pv[:HEAD_DIM]


def _online_init(n):
    return jnp.full((1, n), NEG, F32), jnp.zeros((1, n), F32), jnp.zeros((HEAD_DIM, n), F32)


def _rank_count_step(v, row, cnt, j):
    r = v[j:j + 1, :]
    return cnt + jnp.where(r > v, 1.0, jnp.where(r == v, jnp.where(row > j, 1.0, 0.0), 0.0))


def _rank_select(v, n_candidates, k):
    row = lax.broadcasted_iota(jnp.int32, v.shape, 0)
    cnt = jnp.zeros(v.shape, F32)
    for j in range(n_candidates):
        cnt = _rank_count_step(v, row, cnt, j)
    return cnt < float(k)


def _ffn_kernel(x_ref, g_ref, wg_ref, wu_ref, wd_ref, fin_ref, o_ref, t_ref, *, final_norm):
    x = x_ref[...]
    h = (x * g_ref[...]).astype(BF16)
    inv = lax.rsqrt(jnp.mean(x * x, axis=-1, keepdims=True) + EPS)
    for f in range(D_FF // FF_CHUNK):
        c = slice(f * FF_CHUNK, (f + 1) * FF_CHUNK)
        a = _dot(h, wg_ref[:, c]) * inv
        b = _dot(h, wu_ref[:, c]) * inv
        t_ref[:, c] = (a * jax.nn.sigmoid(a) * b).astype(BF16)
    y = x + 0.5 * _dot(t_ref[...], wd_ref[...])
    if final_norm:
        y = _rms(y, fin_ref[...])
    o_ref[...] = y


def _resident(shape):
    nd = len(shape)
    return pl.BlockSpec(shape, lambda *_: (0,) * nd, pipeline_mode=pl.Buffered(1))


def _layer_resident(stacked, layer):
    return pl.BlockSpec((None,) + stacked.shape[1:], lambda *_: (layer, 0, 0), pipeline_mode=pl.Buffered(1))


def _ffn(x, layer, g, wg, wu, wd, fin, final_norm):
    t = x.shape[0]
    row = pl.BlockSpec((FFN_TILE, D_MODEL), lambda i: (i, 0))
    return pl.pallas_call(
        functools.partial(_ffn_kernel, final_norm=final_norm),
        grid=(t // FFN_TILE,),
        in_specs=[row, _layer_resident(g, layer), _layer_resident(wg, layer), _layer_resident(wu, layer),
                  _layer_resident(wd, layer), _resident((1, D_MODEL))],
        out_specs=row,
        out_shape=jax.ShapeDtypeStruct(x.shape, F32),
        scratch_shapes=[pltpu.VMEM((FFN_TILE, D_FF), BF16)],
        compiler_params=pltpu.CompilerParams(dimension_semantics=("parallel",),
                                             vmem_limit_bytes=VMEM_LIMIT),
        name="ffn",
    )(x, g, wg, wu, wd, fin)


def _proj_kernel(x_ref, g_ref, w_ref, cos_ref, sin_ref,
                 qat_ref, kc_ref, vc_ref, kk_ref, vvt_ref, qbt_ref, kb_ref, vbt_ref, gst_ref, kmean_ref):
    h = _rms(x_ref[...], g_ref[...]).astype(BF16)
    cos = cos_ref[...]
    sin = sin_ref[...]

    def mm(c0, n):
        return _dot(h, w_ref[:, c0:c0 + n])

    def feature_major(z, dtype=BF16):
        return jnp.transpose(z).astype(dtype)

    qat_ref[0] = feature_major(_rope_slab(mm(COL_QA, NSA_WIDTH), cos, sin) * Q_SCALE)
    kcvc = mm(COL_KC, 2 * KV_WIDTH)
    kc_ref[0] = kcvc[:, :KV_WIDTH]
    vc_ref[0] = kcvc[:, KV_WIDTH:]
    kv4 = mm(COL_KV4, 4 * KV_WIDTH)
    kk_ref[0] = jnp.concatenate(
        [_rope_slab(kv4[:, 0:KV_WIDTH], cos, sin), _rope_slab(kv4[:, 2 * KV_WIDTH:3 * KV_WIDTH], cos, sin)],
        axis=1).astype(BF16)
    vvt_ref[0] = feature_major(jnp.concatenate([kv4[:, KV_WIDTH:2 * KV_WIDTH], kv4[:, 3 * KV_WIDTH:]], axis=1))
    qbt_ref[0] = feature_major(_rope_slab(mm(COL_QB, MOBA_WIDTH), cos, sin) * Q_SCALE)
    kb = _rope_slab(mm(COL_KB, MOBA_WIDTH), cos, sin)
    kb_ref[0] = kb.astype(BF16)
    means = [jnp.mean(kb[i * MOBA_BLOCK:(i + 1) * MOBA_BLOCK], axis=0, keepdims=True)
             for i in range(TOKEN_TILE // MOBA_BLOCK)]
    kmean_ref[0] = jnp.concatenate(means, axis=0)
    vbt_ref[0] = feature_major(mm(COL_VB, MOBA_WIDTH))
    gst_ref[0] = feature_major(jax.nn.sigmoid(mm(COL_GS, LANES)), F32)


W_IN_ROWS = 256


def _split_w_in_kernel(wt_ref, attn_ref, gate_ref):
    off_ga = NSA_WIDTH
    off_rest = off_ga + N_GATE_COLS
    off_gates = off_rest + 6 * KV_WIDTH + 3 * MOBA_WIDTH

    def rows_major(a, b):
        return jnp.transpose(wt_ref[a:b, :]).astype(BF16)

    attn_ref[:, :off_ga] = rows_major(0, off_ga)
    attn_ref[:, off_ga:COL_GS] = rows_major(off_rest, off_gates)
    pad = jnp.zeros((LANES - N_GATE_COLS, wt_ref.shape[1]), F32)
    attn_ref[:, COL_GS:] = jnp.transpose(jnp.concatenate([wt_ref[off_ga:off_rest, :], pad], axis=0)).astype(BF16)
    gate_ref[...] = rows_major(off_gates, wt_ref.shape[0])


def _split_w_in(w_in):
    depth, d, n = w_in.shape
    steps = d // W_IN_ROWS
    return pl.pallas_call(
        _split_w_in_kernel,
        grid=(depth, steps),
        in_specs=[pl.BlockSpec((None, n, W_IN_ROWS), lambda l, i: (l, 0, i))],
        out_specs=(pl.BlockSpec((None, W_IN_ROWS, COL_GS + LANES), lambda l, i: (l, i, 0)),
                   pl.BlockSpec((None, W_IN_ROWS, 2 * D_MODEL), lambda l, i: (l, i, 0))),
        out_shape=(jax.ShapeDtypeStruct((depth, d, COL_GS + LANES), BF16),
                   jax.ShapeDtypeStruct((depth, d, 2 * D_MODEL), BF16)),
        compiler_params=pltpu.CompilerParams(dimension_semantics=("parallel", "parallel"),
                                             vmem_limit_bytes=VMEM_LIMIT),
        name="split_w_in",
    )(jnp.swapaxes(w_in, 1, 2))


def _proj(x, layer, g, w, cos_t, sin_t, seq):
    t = x.shape[0]
    tiles_per_seq = seq // TOKEN_TILE
    blocks_per_tile = TOKEN_TILE // MOBA_BLOCK

    def row(n):
        return pl.BlockSpec((TOKEN_TILE, n), lambda i: (i, 0))

    def token_major(n, dtype=BF16):
        return (jax.ShapeDtypeStruct((t // seq, seq, n), dtype),
                pl.BlockSpec((1, TOKEN_TILE, n), lambda i: (i // tiles_per_seq, i % tiles_per_seq, 0)))

    def feature_major(n, dtype=BF16):
        return (jax.ShapeDtypeStruct((t // seq, n, seq), dtype),
                pl.BlockSpec((1, n, TOKEN_TILE), lambda i: (i // tiles_per_seq, 0, i % tiles_per_seq)))

    tab = pl.BlockSpec((TOKEN_TILE, LANES), lambda i: (i % tiles_per_seq, 0))
    outs = (feature_major(NSA_WIDTH), token_major(KV_WIDTH, F32), token_major(KV_WIDTH, F32),
            token_major(2 * KV_WIDTH),
            feature_major(2 * KV_WIDTH), feature_major(MOBA_WIDTH), token_major(MOBA_WIDTH),
            feature_major(MOBA_WIDTH), feature_major(LANES, F32),
            (jax.ShapeDtypeStruct((t // TOKEN_TILE, blocks_per_tile, MOBA_WIDTH), F32),
             pl.BlockSpec((1, blocks_per_tile, MOBA_WIDTH), lambda i: (i, 0, 0))))
    out_shape, out_specs = zip(*outs)
    return pl.pallas_call(
        _proj_kernel,
        grid=(t // TOKEN_TILE,),
        in_specs=[row(D_MODEL), _layer_resident(g, layer), _layer_resident(w, layer), tab, tab],
        out_specs=out_specs,
        out_shape=out_shape,
        compiler_params=pltpu.CompilerParams(dimension_semantics=("parallel",),
                                             vmem_limit_bytes=VMEM_LIMIT),
        name="proj",
    )(x, g, w, cos_t, sin_t)


def _gelu_tanh(x):
    return 0.5 * x * (1.0 + jnp.tanh(np.sqrt(2.0 / np.pi).astype(np.float32) * (x + 0.044715 * (x * x * x))))


def _compress_kernel(kc_ref, vc_ref, kw1_ref, kw1f_ref, kpos_ref, kw2_ref,
                     vw1_ref, vw1f_ref, vpos_ref, vw2_ref, cos_ref, sin_ref, kcb_ref, vcb_ref):
    n_chunks = kc_ref.shape[1] // CMP_STRIDE

    def compress(x_ref, w1_ref, w1f_ref, pos_ref, w2_ref):
        u = jnp.zeros((n_chunks, 4 * CMP_HIDDEN), F32)
        for p in range(CMP_STRIDE):
            rows = x_ref[0, pl.ds(p, n_chunks, stride=CMP_STRIDE), :].astype(BF16)
            u = u + _dot(rows, w1_ref[p * KV_WIDTH:(p + 1) * KV_WIDTH, :])
        first = u[:, :2 * CMP_HIDDEN]
        second = pltpu.roll(u[:, 2 * CMP_HIDDEN:], n_chunks - 1, 0)
        pos = jnp.broadcast_to(pos_ref[...], (8, CMP_LEN * HEAD_DIM)).astype(BF16)
        bias = _dot(pos, w1f_ref[...])[0:1]
        pre = first + second + jnp.concatenate([bias, bias], axis=1)
        return _dot(_gelu_tanh(pre).astype(BF16), w2_ref[...])

    k = compress(kc_ref, kw1_ref, kw1f_ref, kpos_ref, kw2_ref)
    kcb_ref[0] = _rope_slab(k, cos_ref[...], sin_ref[...]).astype(BF16)
    vcb_ref[0] = compress(vc_ref, vw1_ref, vw1f_ref, vpos_ref, vw2_ref).astype(BF16)


def _compress_weights(pos, w1, w2):
    assert NSA_GROUPS == 2
    half = CMP_STRIDE * HEAD_DIM
    a0 = w1[:half].reshape(CMP_STRIDE, HEAD_DIM, CMP_HIDDEN)
    a1 = w1[half:].reshape(CMP_STRIDE, HEAD_DIM, CMP_HIDDEN)
    z = jnp.zeros_like(a0)
    big = jnp.concatenate([jnp.concatenate([a0, z, a1, z], axis=2), jnp.concatenate([z, a0, z, a1], axis=2)], axis=1)
    big = big.reshape(CMP_STRIDE * KV_WIDTH, 2 * NSA_GROUPS * CMP_HIDDEN).astype(BF16)
    z2 = jnp.zeros_like(w2)
    w2big = jnp.concatenate([jnp.concatenate([w2, z2], axis=1), jnp.concatenate([z2, w2], axis=1)], axis=0)
    return big, w1.astype(BF16), pos.reshape(1, CMP_LEN * HEAD_DIM), w2big.astype(BF16)


def _compress(kc, vc, kparams, vparams, cos_c, sin_c):
    b, seq, width = kc.shape
    n_chunks = seq // CMP_STRIDE
    args = (kc, vc) + _compress_weights(*kparams) + _compress_weights(*vparams) + (cos_c, sin_c)
    x_spec = pl.BlockSpec((1, seq, width), lambda i: (i, 0, 0))
    in_specs = [x_spec, x_spec] + [_resident(a.shape) for a in args[2:]]
    o_spec = pl.BlockSpec((1, n_chunks, KV_WIDTH), lambda i: (i, 0, 0))
    o_shape = jax.ShapeDtypeStruct((b, n_chunks, KV_WIDTH), BF16)
    return pl.pallas_call(
        _compress_kernel,
        grid=(b,),
        in_specs=in_specs,
        out_specs=(o_spec, o_spec),
        out_shape=(o_shape, o_shape),
        compiler_params=pltpu.CompilerParams(dimension_semantics=("parallel",),
                                             vmem_limit_bytes=VMEM_LIMIT),
        name="compress",
    )(*args)


def _nsa_kernel(qat_ref, gst_ref, kcb_ref, vcbt_ref, kk_ref, vst_ref, vwt_ref, ovt_ref, blk_ref, o_ref,
                s_ref, m_ref, l_ref, acc_ref, *, seq):
    qi = pl.program_id(1)
    t0 = qi * NSA_Q
    cols = NSA_REP * NSA_Q
    n_cmp = seq // CMP_STRIDE - 1
    n_sel = seq // SEL_BLOCK
    gst = gst_ref[0]
    ovt = ovt_ref[...]

    def key_and_query_pos(n, k0):
        shape = (n, NSA_Q)
        return k0 + lax.broadcasted_iota(jnp.int32, shape, 0), t0 + lax.broadcasted_iota(jnp.int32, shape, 1)

    groups = range(NSA_GROUPS)
    d_rows = [slice(g * HEAD_DIM, (g + 1) * HEAD_DIM) for g in groups]
    g_cols = [slice(g * cols, (g + 1) * cols) for g in groups]

    zeros_d = jnp.zeros((HEAD_DIM, NSA_Q), BF16)
    heads = []
    for n in range(NSA_HEADS):
        x = qat_ref[0, n * HEAD_DIM:(n + 1) * HEAD_DIM, :]
        heads.append(jnp.concatenate([x, zeros_d] if n < NSA_REP else [zeros_d, x], axis=0))
    q = jnp.concatenate(heads, axis=1)

    n_win = WINDOW + NSA_Q
    w0 = pl.multiple_of(jnp.maximum(t0 - WINDOW, 0), NSA_Q)
    s = _dot(kcb_ref[0], q)
    k_win = kk_ref[0, pl.ds(w0, n_win), KV_WIDTH:2 * KV_WIDTH]
    s_win_0 = _dot(k_win, q[:, g_cols[0]])

    n_idx, tq = key_and_query_pos(s.shape[0], 0)
    cmask = (n_idx * CMP_STRIDE + (CMP_LEN - 1) <= tq) & (n_idx < n_cmp)
    p_cmp = _masked_softmax_keys(s, cmask).astype(BF16)
    o_cmp = [_dot(vcbt_ref[0, d_rows[g], :], p_cmp[:, g_cols[g]]) for g in groups]

    pi = _dot(ovt, p_cmp)
    s_win = jnp.concatenate([s_win_0, _dot(k_win, q[:, g_cols[1]])], axis=1)
    jj, tsel = key_and_query_pos(n_sel, 0)
    tblk = tsel >> 6
    bonus = jnp.where((jj == 0) | (jj == tblk) | (jj == tblk - 1), FORCE_BONUS, 0.0)
    pad = jnp.zeros((LANES - n_sel, NSA_Q), BF16)
    biases = []
    for g in groups:
        imp = pi[:, g * cols:g * cols + NSA_Q]
        for r in range(1, NSA_REP):
            imp = imp + pi[:, g * cols + r * NSA_Q:g * cols + (r + 1) * NSA_Q]
        imp = jnp.where(jj <= tblk, imp + bonus, NEG)
        bias = jnp.where(_rank_select(imp, n_sel, SEL_TOPK) & (jj <= tblk), 0.0, NEG)
        biases.append(jnp.concatenate([bias.astype(BF16), pad], axis=0))
    q_sel = jnp.concatenate(
        [jnp.concatenate([heads[n], biases[n // NSA_REP]], axis=0) for n in range(NSA_HEADS)], axis=1)

    def tile_start(c):
        return pl.multiple_of(c * SEL_KEYS, SEL_KEYS)

    def sel_scores(c):
        k0 = tile_start(c)
        keys = jnp.concatenate([kk_ref[0, pl.ds(k0, SEL_KEYS), 0:KV_WIDTH], blk_ref[pl.ds(k0, SEL_KEYS), :]],
                               axis=1)
        return _dot(keys, q_sel)

    def sel_update(c, s):
        for g in groups:
            m, l, acc = _online_update((m_ref[g], l_ref[g], acc_ref[g]), s[:, g_cols[g]],
                                       vst_ref[0, d_rows[g], pl.ds(tile_start(c), SEL_KEYS)])
            m_ref[g], l_ref[g], acc_ref[g] = m, l, acc

    n_past = qi // (SEL_KEYS // NSA_Q)
    m_ref[...] = jnp.full(m_ref.shape, NEG, F32)
    l_ref[...] = jnp.zeros(l_ref.shape, F32)
    acc_ref[...] = jnp.zeros(acc_ref.shape, F32)
    s_ref[0] = sel_scores(0)

    kpos, tq = key_and_query_pos(n_win, w0)
    s_win = _mask_keys(s_win, (kpos <= tq) & (tq - kpos < WINDOW))
    p_win = jnp.exp2(s_win - jnp.max(s_win, axis=0, keepdims=True)).astype(BF16)
    ones_rows = jnp.where(lax.broadcasted_iota(jnp.int32, (BF16_ROWS, n_win), 0) == 0, 1.0, 0.0).astype(BF16)
    o_win = []
    for g in groups:
        pv = _dot(jnp.concatenate([vwt_ref[0, d_rows[g], pl.ds(w0, n_win)], ones_rows], axis=0), p_win[:, g_cols[g]])
        o_win.append(pv[:HEAD_DIM] / jnp.maximum(pv[HEAD_DIM:HEAD_DIM + 1], TINY))

    for c in range(seq // SEL_KEYS - 1):
        @pl.when(c < n_past)
        def _():
            s_ref[(c + 1) % 2] = sel_scores(c + 1)
            sel_update(c, s_ref[c % 2])

    kpos, tq = key_and_query_pos(SEL_KEYS, tile_start(n_past))
    sel_update(n_past, _mask_keys(s_ref[n_past % 2], kpos <= tq))
    o_sel = [acc_ref[g] / jnp.maximum(l_ref[g], TINY) for g in groups]

    outs = []
    for n in range(NSA_HEADS):
        g, r = divmod(n, NSA_REP)
        sl = slice(r * NSA_Q, (r + 1) * NSA_Q)
        outs.append(gst[3 * n:3 * n + 1, :] * o_cmp[g][:, sl] + gst[3 * n + 1:3 * n + 2, :] * o_sel[g][:, sl]
                    + gst[3 * n + 2:3 * n + 3, :] * o_win[g][:, sl])
    o_ref[0] = jnp.transpose(jnp.concatenate(outs, axis=0)).astype(BF16)


def _nsa(qat, gst, kcb, vcbt, kk, vvt, ovt, blk):
    b, _, seq = qat.shape
    n_chunks = kcb.shape[1]
    return pl.pallas_call(
        functools.partial(_nsa_kernel, seq=seq),
        grid=(b, seq // NSA_Q),
        in_specs=[pl.BlockSpec((1, NSA_WIDTH, NSA_Q), lambda i, j: (i, 0, j)),
                  pl.BlockSpec((1, LANES, NSA_Q), lambda i, j: (i, 0, j)),
                  pl.BlockSpec((1, n_chunks, KV_WIDTH), lambda i, j: (i, 0, 0)),
                  pl.BlockSpec((1, KV_WIDTH, n_chunks), lambda i, j: (i, 0, 0)),
                  pl.BlockSpec((1, seq, 2 * KV_WIDTH), lambda i, j: (i, 0, 0)),
                  pl.BlockSpec((1, KV_WIDTH, seq), lambda i, j: (i, 0, 0)),
                  pl.BlockSpec((1, KV_WIDTH, seq), lambda i, j: (i, 1, 0)),
                  pl.BlockSpec(ovt.shape, lambda i, j: (0, 0)),
                  pl.BlockSpec(blk.shape, lambda i, j: (0, 0))],
        out_specs=pl.BlockSpec((1, NSA_Q, NSA_WIDTH), lambda i, j: (i, j, 0)),
        out_shape=jax.ShapeDtypeStruct((b, seq, NSA_WIDTH), BF16),
        scratch_shapes=[pltpu.VMEM((2, SEL_KEYS, NSA_HEADS * NSA_Q), F32),
                        pltpu.VMEM((NSA_GROUPS, 1, NSA_REP * NSA_Q), F32),
                        pltpu.VMEM((NSA_GROUPS, 1, NSA_REP * NSA_Q), F32),
                        pltpu.VMEM((NSA_GROUPS, HEAD_DIM, NSA_REP * NSA_Q), F32)],
        compiler_params=pltpu.CompilerParams(dimension_semantics=("parallel", "parallel"),
                                             vmem_limit_bytes=VMEM_LIMIT),
        name="nsa",
    )(qat, gst, kcb, vcbt, kk, vvt, vvt, ovt, blk)


def _moba_kernel(qt_ref, k_ref, vt_ref, kmean_ref, o_ref, s_ref, m_ref, l_ref, acc_ref):
    c = pl.program_id(1)
    nb = kmean_ref.shape[1]
    k_own = pl.multiple_of(c * MOBA_BLOCK, MOBA_BLOCK)
    jj = lax.broadcasted_iota(jnp.int32, (BF16_ROWS, MOBA_BLOCK), 0)
    past = jj < c
    pad = jnp.zeros((BF16_ROWS - nb, LANES), F32)
    zeros_d = jnp.zeros((HEAD_DIM, MOBA_BLOCK), BF16)
    ones_rows = jnp.where(lax.broadcasted_iota(jnp.int32, (BF16_ROWS, MOBA_BLOCK), 0) == 0, 1.0, 0.0).astype(BF16)

    def pair(h):
        return slice((h // 2) * LANES, (h // 2 + 1) * LANES)

    qs, gscs = [], []
    for h in range(MOBA_HEADS):
        x = qt_ref[0, h * HEAD_DIM:(h + 1) * HEAD_DIM, :]
        q = jnp.concatenate([x, zeros_d] if h % 2 == 0 else [zeros_d, x], axis=0)
        km = jnp.concatenate([kmean_ref[0, :, pair(h)], pad], axis=0).astype(BF16)
        qs.append(q)
        gscs.append(_dot(km, q))
    chosen = [jnp.where(_rank_select(jnp.where(past, g, NEG), nb, MOBA_TOPK) & past, 1.0, 0.0) for g in gscs]

    m_ref[...] = jnp.full(m_ref.shape, NEG, F32)
    l_ref[...] = jnp.zeros(l_ref.shape, F32)
    acc_ref[...] = jnp.zeros(acc_ref.shape, F32)

    def block_start(j):
        return pl.multiple_of(j * MOBA_BLOCK, MOBA_BLOCK)

    def scores(j, slot):
        for h in range(MOBA_HEADS):
            s_ref[slot, h] = _dot(k_ref[0, pl.ds(block_start(j), MOBA_BLOCK), pair(h)], qs[h])

    def past_update(j, slot):
        for h in range(MOBA_HEADS):
            rows = slice(h * HEAD_DIM, (h + 1) * HEAD_DIM)
            s = s_ref[slot, h]
            pick = jnp.sum(jnp.where(jj == j, chosen[h], 0.0), axis=0, keepdims=True) > 0.5
            m = m_ref[h]
            m_new = jnp.maximum(m, jnp.where(pick, jnp.max(s, axis=0, keepdims=True), NEG))
            alpha = jnp.exp2(m - m_new)
            p = jnp.exp2(s - jnp.where(pick, m_new, jnp.inf))
            m_ref[h] = m_new
            pv = _dot(jnp.concatenate([vt_ref[0, rows, pl.ds(block_start(j), MOBA_BLOCK)], ones_rows], axis=0),
                      p.astype(BF16))
            l_ref[h] = alpha * l_ref[h] + pv[HEAD_DIM:HEAD_DIM + 1]
            acc_ref[rows, :] = alpha * acc_ref[rows, :] + pv[:HEAD_DIM]

    scores(0, 0)

    def two_blocks(i, _):
        j = 2 * i
        scores(j + 1, 1)
        past_update(j, 0)
        scores(j + 2, 0)
        past_update(j + 1, 1)
        return 0

    lax.fori_loop(0, c // 2, two_blocks, 0)

    @pl.when(c % 2 == 1)
    def _():
        scores(c, 1)
        past_update(c - 1, 0)

    kq = (MOBA_BLOCK, MOBA_BLOCK)
    causal = lax.broadcasted_iota(jnp.int32, kq, 0) <= lax.broadcasted_iota(jnp.int32, kq, 1)
    for h in range(MOBA_HEADS):
        rows = slice(h * HEAD_DIM, (h + 1) * HEAD_DIM)
        carry = (m_ref[h], l_ref[h], acc_ref[rows, :])
        _, l, acc = _online_update(carry, jnp.where(causal, s_ref[c % 2, h], NEG),
                                   vt_ref[0, rows, pl.ds(k_own, MOBA_BLOCK)])
        acc_ref[rows, :] = acc / jnp.maximum(l, TINY)
    o_ref[0] = jnp.transpose(acc_ref[...]).astype(BF16)


def _moba(qbt, kb, vbt, kmean):
    b, _, seq = qbt.shape
    nb = seq // MOBA_BLOCK
    return pl.pallas_call(
        _moba_kernel,
        grid=(b, nb),
        in_specs=[pl.BlockSpec((1, MOBA_WIDTH, MOBA_BLOCK), lambda i, j: (i, 0, j)),
                  pl.BlockSpec((1, seq, MOBA_WIDTH), lambda i, j: (i, 0, 0)),
                  pl.BlockSpec((1, MOBA_WIDTH, seq), lambda i, j: (i, 0, 0)),
                  pl.BlockSpec((1, nb, MOBA_WIDTH), lambda i, j: (i, 0, 0))],
        out_specs=pl.BlockSpec((1, MOBA_BLOCK, MOBA_WIDTH), lambda i, j: (i, j, 0)),
        out_shape=jax.ShapeDtypeStruct((b, seq, MOBA_WIDTH), BF16),
        scratch_shapes=[pltpu.VMEM((2, MOBA_HEADS, MOBA_BLOCK, MOBA_BLOCK), F32),
                        pltpu.VMEM((MOBA_HEADS, 1, MOBA_BLOCK), F32), pltpu.VMEM((MOBA_HEADS, 1, MOBA_BLOCK), F32),
                        pltpu.VMEM((MOBA_WIDTH, MOBA_BLOCK), F32)],
        compiler_params=pltpu.CompilerParams(dimension_semantics=("parallel", "parallel"),
                                             vmem_limit_bytes=VMEM_LIMIT),
        name="moba",
    )(qbt, kb, vbt, kmean)


def _merge_kernel(x_ref, g_ref, a_ref, b_ref, wgate_ref, pa_ref, pb_ref, wo_ref, o_ref):
    x = x_ref[...]
    h = _rms(x, g_ref[...]).astype(BF16)
    ya = _dot(a_ref[0], pa_ref[...])
    yb = _dot(b_ref[0], pb_ref[...])
    merged = (jax.nn.sigmoid(_dot(h, wgate_ref[:, :D_MODEL])) * ya
              + jax.nn.sigmoid(_dot(h, wgate_ref[:, D_MODEL:])) * yb)
    o_ref[...] = x + _dot(merged.astype(BF16), wo_ref[...])


def _merge(x, layer, g, a, bm, wgate, pa, pb, wo):
    t = x.shape[0]
    tiles_per_seq = a.shape[1] // TOKEN_TILE

    def row(n):
        return pl.BlockSpec((TOKEN_TILE, n), lambda i: (i, 0))

    def seq_row(n):
        return pl.BlockSpec((1, TOKEN_TILE, n), lambda i: (i // tiles_per_seq, i % tiles_per_seq, 0))

    return pl.pallas_call(
        _merge_kernel,
        grid=(t // TOKEN_TILE,),
        in_specs=[row(D_MODEL), _layer_resident(g, layer), seq_row(NSA_WIDTH), seq_row(MOBA_WIDTH)]
                 + [_layer_resident(w, layer) for w in (wgate, pa, pb, wo)],
        out_specs=row(D_MODEL),
        out_shape=jax.ShapeDtypeStruct(x.shape, F32),
        compiler_params=pltpu.CompilerParams(dimension_semantics=("parallel",),
                                             vmem_limit_bytes=VMEM_LIMIT),
        name="merge",
    )(x, g, a, bm, wgate, pa, pb, wo)


def _rope_tables(pos):
    inv = ROPE_THETA ** (-jnp.arange(0, HEAD_DIM, 2, dtype=F32) / HEAD_DIM)
    ang = pos.astype(F32)[:, None] * inv[None, :]
    cos, sin = jnp.cos(ang), jnp.sin(ang)
    reps = LANES // HEAD_DIM
    return (jnp.tile(jnp.concatenate([cos, cos], axis=1), (1, reps)),
            jnp.tile(jnp.concatenate([-sin, sin], axis=1), (1, reps)))


def _overlap_matrix_t(seq):
    n_chunks = seq // CMP_STRIDE
    ci = np.arange(n_chunks)[None, :] * CMP_STRIDE
    sj = np.arange(seq // SEL_BLOCK)[:, None] * SEL_BLOCK
    ov = (ci < sj + SEL_BLOCK) & (ci + CMP_LEN > sj) & (np.arange(n_chunks)[None, :] < n_chunks - 1)
    return jnp.asarray(ov, dtype=BF16)


def _block_onehot(seq):
    return jnp.asarray(np.arange(seq)[:, None] // SEL_BLOCK == np.arange(LANES)[None, :], dtype=BF16)


def kernel(x, ffn1_norm, ffn1_wg, ffn1_wu, ffn1_wd, mix_norm, w_in, cmpk_pos, cmpk_w1, cmpk_w2, cmpv_pos, cmpv_w1, cmpv_w2, w_branch_nsa, w_branch_moba, w_out, ffn2_norm, ffn2_wg, ffn2_wu, ffn2_wd, final_norm):
    b, seq, d = x.shape
    depth = w_in.shape[0]
    assert d == D_MODEL and seq % TOKEN_TILE == 0 and seq // CMP_STRIDE == LANES
    t = b * seq
    n_chunks = seq // CMP_STRIDE
    cos_t, sin_t = _rope_tables(jnp.arange(seq))
    cos_c, sin_c = _rope_tables(jnp.arange(n_chunks) * CMP_STRIDE + (CMP_LEN - 1))
    ovt = _overlap_matrix_t(seq)
    blk = _block_onehot(seq)

    def gain(gs):
        return gs.reshape(depth, 1, d)

    ffn1 = (gain(ffn1_norm), ffn1_wg.astype(BF16), ffn1_wu.astype(BF16), ffn1_wd.astype(BF16))
    ffn2 = (gain(ffn2_norm), ffn2_wg.astype(BF16), ffn2_wu.astype(BF16), ffn2_wd.astype(BF16))
    w_attn, w_gate = _split_w_in(w_in)
    w_merge = (w_gate, w_branch_nsa.astype(BF16), w_branch_moba.astype(BF16), w_out.astype(BF16))
    fin = final_norm.reshape(1, d)

    xf = x.reshape(t, d)
    for l in range(depth):
        xf = _ffn(xf, l, *ffn1, fin, False)
        x_mix = xf
        qat, kc, vc, kk, vvt, qbt, kb, vbt, gst, kmean = _proj(xf, l, gain(mix_norm), w_attn, cos_t, sin_t, seq)
        kcb, vcb = _compress(kc, vc,
                             (cmpk_pos[l], cmpk_w1[l], cmpk_w2[l]),
                             (cmpv_pos[l], cmpv_w1[l], cmpv_w2[l]), cos_c, sin_c)
        a = _nsa(qat, gst, kcb, vcb.transpose(0, 2, 1), kk, vvt, ovt, blk)
        bm = _moba(qbt, kb, vbt, kmean.reshape(b, seq // MOBA_BLOCK, MOBA_WIDTH))
        xf = _merge(x_mix, l, gain(mix_norm), a, bm, *w_merge)
        xf = _ffn(xf, l, *ffn2, fin, l == depth - 1)
    return xf.reshape(b, seq, d)
```

```python
import functools

import jax
import jax.numpy as jnp
import numpy as np
from jax import lax
from jax.experimental import pallas as pl
from jax.experimental.pallas import tpu as pltpu

F32 = jnp.float32
BF16 = jnp.bfloat16

D_MODEL = 1024
HEAD_DIM = 64
HALF = HEAD_DIM // 2
NSA_HEADS = 8
NSA_GROUPS = 2
NSA_REP = NSA_HEADS // NSA_GROUPS
CMP_LEN = 32
CMP_STRIDE = 16
CMP_HIDDEN = 128
SEL_BLOCK = 64
SEL_TOPK = 16
WINDOW = 512
MOBA_HEADS = 8
MOBA_BLOCK = 256
MOBA_TOPK = 3
D_FF = 2816
ROPE_THETA = 10000.0
EPS = 1e-6
NEG = -1e30
TINY = 1e-30
FORCE_BONUS = 1e4
Q_SCALE = HEAD_DIM ** -0.5 * float(np.log2(np.e))

NSA_WIDTH = NSA_HEADS * HEAD_DIM
KV_WIDTH = NSA_GROUPS * HEAD_DIM
MOBA_WIDTH = MOBA_HEADS * HEAD_DIM
N_GATE_COLS = 3 * NSA_HEADS

LANES = 128
BF16_ROWS = 16
TOKEN_TILE = 1024
FFN_TILE = 1024
FF_CHUNK = 256
NSA_Q = 256
SEL_KEYS = 512
VMEM_LIMIT = 56 * 1024 * 1024

COL_QA = 0
COL_KC = COL_QA + NSA_WIDTH
COL_VC = COL_KC + KV_WIDTH
COL_KV4 = COL_VC + KV_WIDTH
COL_QB = COL_KV4 + 4 * KV_WIDTH
COL_KB = COL_QB + MOBA_WIDTH
COL_VB = COL_KB + MOBA_WIDTH
COL_GS = COL_VB + MOBA_WIDTH


def _rms(x, g):
    return x * lax.rsqrt(jnp.mean(x * x, axis=-1, keepdims=True) + EPS) * g


def _dot(a, b):
    return jnp.dot(a, b, preferred_element_type=F32)


def _swap_halves(z):
    lane = lax.broadcasted_iota(jnp.int32, z.shape, 1)
    lo = (lane & (HEAD_DIM - 1)) < HALF
    return jnp.where(lo, pltpu.roll(z, LANES - HALF, 1), pltpu.roll(z, HALF, 1))


def _rope_slab(z, cos, sin):
    parts = []
    for c in range(z.shape[1] // LANES):
        zc = z[:, c * LANES:(c + 1) * LANES]
        parts.append(zc * cos + _swap_halves(zc) * sin)
    return parts[0] if len(parts) == 1 else jnp.concatenate(parts, axis=1)


def _slabs(s, n):
    return [s[:, i * n:(i + 1) * n] for i in range(s.shape[1] // n)]


def _masked_softmax_keys(s, mask):
    out = []
    for x in _slabs(s, mask.shape[1]):
        x = jnp.where(mask, x, NEG)
        m = jnp.max(x, axis=0, keepdims=True)
        p = jnp.where(mask, jnp.exp2(x - m), 0.0)
        out.append(p / jnp.maximum(jnp.sum(p, axis=0, keepdims=True), TINY))
    return jnp.concatenate(out, axis=1)


def _mask_keys(s, mask):
    return jnp.concatenate([jnp.where(mask, x, NEG) for x in _slabs(s, mask.shape[1])], axis=1)


def _online_update(carry, s, vt):
    m, l, acc = carry
    m_new = jnp.maximum(m, jnp.max(s, axis=0, keepdims=True))
    alpha = jnp.exp2(m - m_new)
    p = jnp.exp2(s - m_new).astype(BF16)
    ones_rows = jnp.where(lax.broadcasted_iota(jnp.int32, (BF16_ROWS, vt.shape[1]), 0) == 0, 1.0, 0.0).astype(BF16)
    pv = _dot(jnp.concatenate([vt, ones_rows], axis=0), p)
    return m_new, alpha * l + pv[HEAD_DIM:HEAD_DIM + 1], alpha * acc + pv[:HEAD_DIM]


def _rank_select(v, n_candidates, k):
    row = lax.broadcasted_iota(jnp.int32, v.shape, 0)
    cnt = jnp.zeros(v.shape, F32)
    for j in range(n_candidates):
        r = v[j:j + 1, :]
        cnt = cnt + jnp.where(r > v, 1.0, jnp.where(r == v, jnp.where(row > j, 1.0, 0.0), 0.0))
    return cnt < float(k)


def _ffn_kernel(x_ref, g_ref, wg_ref, wu_ref, wd_ref, fin_ref, o_ref, t_ref, *, final_norm):
    x = x_ref[...]
    h = (x * g_ref[...]).astype(BF16)
    inv = lax.rsqrt(jnp.mean(x * x, axis=-1, keepdims=True) + EPS)
    for f in range(D_FF // FF_CHUNK):
        c = slice(f * FF_CHUNK, (f + 1) * FF_CHUNK)
        a = _dot(h, wg_ref[:, c]) * inv
        b = _dot(h, wu_ref[:, c]) * inv
        t_ref[:, c] = (a * jax.nn.sigmoid(a) * b).astype(BF16)
    y = x + 0.5 * _dot(t_ref[...], wd_ref[...])
    if final_norm:
        y = _rms(y, fin_ref[...])
    o_ref[...] = y


def _resident(shape):
    nd = len(shape)
    return pl.BlockSpec(shape, lambda *_: (0,) * nd, pipeline_mode=pl.Buffered(1))


def _layer_resident(stacked, layer):
    return pl.BlockSpec((None,) + stacked.shape[1:], lambda *_: (layer, 0, 0), pipeline_mode=pl.Buffered(1))


def _ffn(x, layer, g, wg, wu, wd, fin, final_norm):
    t = x.shape[0]
    row = pl.BlockSpec((FFN_TILE, D_MODEL), lambda i: (i, 0))
    return pl.pallas_call(
        functools.partial(_ffn_kernel, final_norm=final_norm),
        grid=(t // FFN_TILE,),
        in_specs=[row, _layer_resident(g, layer), _layer_resident(wg, layer), _layer_resident(wu, layer),
                  _layer_resident(wd, layer), _resident((1, D_MODEL))],
        out_specs=row,
        out_shape=jax.ShapeDtypeStruct(x.shape, F32),
        scratch_shapes=[pltpu.VMEM((FFN_TILE, D_FF), BF16)],
        compiler_params=pltpu.CompilerParams(dimension_semantics=("parallel",),
                                             vmem_limit_bytes=VMEM_LIMIT),
        name="ffn",
    )(x, g, wg, wu, wd, fin)


def _proj_kernel(x_ref, g_ref, w_ref, cos_ref, sin_ref,
                 qat_ref, kc_ref, vc_ref, kk_ref, vvt_ref, qbt_ref, kb_ref, vbt_ref, gst_ref, kmean_ref):
    h = _rms(x_ref[...], g_ref[...]).astype(BF16)
    cos = cos_ref[...]
    sin = sin_ref[...]

    def mm(c0, n):
        return _dot(h, w_ref[:, c0:c0 + n])

    def feature_major(z, dtype=BF16):
        return jnp.transpose(z).astype(dtype)

    qat_ref[0] = feature_major(_rope_slab(mm(COL_QA, NSA_WIDTH), cos, sin) * Q_SCALE)
    kcvc = mm(COL_KC, 2 * KV_WIDTH)
    kc_ref[0] = kcvc[:, :KV_WIDTH]
    vc_ref[0] = kcvc[:, KV_WIDTH:]
    kv4 = mm(COL_KV4, 4 * KV_WIDTH)
    kk_ref[0] = jnp.concatenate(
        [_rope_slab(kv4[:, 0:KV_WIDTH], cos, sin), _rope_slab(kv4[:, 2 * KV_WIDTH:3 * KV_WIDTH], cos, sin)],
        axis=1).astype(BF16)
    vvt_ref[0] = feature_major(jnp.concatenate([kv4[:, KV_WIDTH:2 * KV_WIDTH], kv4[:, 3 * KV_WIDTH:]], axis=1))
    qbt_ref[0] = feature_major(_rope_slab(mm(COL_QB, MOBA_WIDTH), cos, sin) * Q_SCALE)
    kb = _rope_slab(mm(COL_KB, MOBA_WIDTH), cos, sin)
    kb_ref[0] = kb.astype(BF16)
    means = [jnp.mean(kb[i * MOBA_BLOCK:(i + 1) * MOBA_BLOCK], axis=0, keepdims=True)
             for i in range(TOKEN_TILE // MOBA_BLOCK)]
    kmean_ref[0] = jnp.concatenate(means, axis=0)
    vbt_ref[0] = feature_major(mm(COL_VB, MOBA_WIDTH))
    gst_ref[0] = feature_major(jax.nn.sigmoid(mm(COL_GS, LANES)), F32)


W_IN_ROWS = 256


def _split_w_in_kernel(wt_ref, attn_ref, gate_ref):
    off_ga = NSA_WIDTH
    off_rest = off_ga + N_GATE_COLS
    off_gates = off_rest + 6 * KV_WIDTH + 3 * MOBA_WIDTH

    def rows_major(a, b):
        return jnp.transpose(wt_ref[a:b, :]).astype(BF16)

    attn_ref[:, :off_ga] = rows_major(0, off_ga)
    attn_ref[:, off_ga:COL_GS] = rows_major(off_rest, off_gates)
    pad = jnp.zeros((LANES - N_GATE_COLS, wt_ref.shape[1]), F32)
    attn_ref[:, COL_GS:] = jnp.transpose(jnp.concatenate([wt_ref[off_ga:off_rest, :], pad], axis=0)).astype(BF16)
    gate_ref[...] = rows_major(off_gates, wt_ref.shape[0])


def _split_w_in(w_in):
    depth, d, n = w_in.shape
    steps = d // W_IN_ROWS
    return pl.pallas_call(
        _split_w_in_kernel,
        grid=(depth, steps),
        in_specs=[pl.BlockSpec((None, n, W_IN_ROWS), lambda l, i: (l, 0, i))],
        out_specs=(pl.BlockSpec((None, W_IN_ROWS, COL_GS + LANES), lambda l, i: (l, i, 0)),
                   pl.BlockSpec((None, W_IN_ROWS, 2 * D_MODEL), lambda l, i: (l, i, 0))),
        out_shape=(jax.ShapeDtypeStruct((depth, d, COL_GS + LANES), BF16),
                   jax.ShapeDtypeStruct((depth, d, 2 * D_MODEL), BF16)),
        compiler_params=pltpu.CompilerParams(dimension_semantics=("parallel", "parallel"),
                                             vmem_limit_bytes=VMEM_LIMIT),
        name="split_w_in",
    )(jnp.swapaxes(w_in, 1, 2))


def _proj(x, layer, g, w, cos_t, sin_t, seq):
    t = x.shape[0]
    tiles_per_seq = seq // TOKEN_TILE
    blocks_per_tile = TOKEN_TILE // MOBA_BLOCK

    def row(n):
        return pl.BlockSpec((TOKEN_TILE, n), lambda i: (i, 0))

    def token_major(n, dtype=BF16):
        return (jax.ShapeDtypeStruct((t // seq, seq, n), dtype),
                pl.BlockSpec((1, TOKEN_TILE, n), lambda i: (i // tiles_per_seq, i % tiles_per_seq, 0)))

    def feature_major(n, dtype=BF16):
        return (jax.ShapeDtypeStruct((t // seq, n, seq), dtype),
                pl.BlockSpec((1, n, TOKEN_TILE), lambda i: (i // tiles_per_seq, 0, i % tiles_per_seq)))

    tab = pl.BlockSpec((TOKEN_TILE, LANES), lambda i: (i % tiles_per_seq, 0))
    outs = (feature_major(NSA_WIDTH), token_major(KV_WIDTH, F32), token_major(KV_WIDTH, F32),
            token_major(2 * KV_WIDTH),
            feature_major(2 * KV_WIDTH), feature_major(MOBA_WIDTH), token_major(MOBA_WIDTH),
            feature_major(MOBA_WIDTH), feature_major(LANES, F32),
            (jax.ShapeDtypeStruct((t // TOKEN_TILE, blocks_per_tile, MOBA_WIDTH), F32),
             pl.BlockSpec((1, blocks_per_tile, MOBA_WIDTH), lambda i: (i, 0, 0))))
    out_shape, out_specs = zip(*outs)
    return pl.pallas_call(
        _proj_kernel,
        grid=(t // TOKEN_TILE,),
        in_specs=[row(D_MODEL), _layer_resident(g, layer), _layer_resident(w, layer), tab, tab],
        out_specs=out_specs,
        out_shape=out_shape,
        compiler_params=pltpu.CompilerParams(dimension_semantics=("parallel",),
                                             vmem_limit_bytes=VMEM_LIMIT),
        name="proj",
    )(x, g, w, cos_t, sin_t)


def _gelu_tanh(x):
    return 0.5 * x * (1.0 + jnp.tanh(np.sqrt(2.0 / np.pi).astype(np.float32) * (x + 0.044715 * (x * x * x))))


def _compress_kernel(kc_ref, vc_ref, kw1_ref, kw1f_ref, kpos_ref, kw2_ref,
                     vw1_ref, vw1f_ref, vpos_ref, vw2_ref, cos_ref, sin_ref, kcb_ref, vcb_ref):
    n_chunks = kc_ref.shape[1] // CMP_STRIDE

    def compress(x_ref, w1_ref, w1f_ref, pos_ref, w2_ref):
        u = jnp.zeros((n_chunks, 4 * CMP_HIDDEN), F32)
        for p in range(CMP_STRIDE):
            rows = x_ref[0, pl.ds(p, n_chunks, stride=CMP_STRIDE), :].astype(BF16)
            u = u + _dot(rows, w1_ref[p * KV_WIDTH:(p + 1) * KV_WIDTH, :])
        first = u[:, :2 * CMP_HIDDEN]
        second = pltpu.roll(u[:, 2 * CMP_HIDDEN:], n_chunks - 1, 0)
        pos = jnp.broadcast_to(pos_ref[...], (8, CMP_LEN * HEAD_DIM)).astype(BF16)
        bias = _dot(pos, w1f_ref[...])[0:1]
        pre = first + second + jnp.concatenate([bias, bias], axis=1)
        return _dot(_gelu_tanh(pre).astype(BF16), w2_ref[...])

    k = compress(kc_ref, kw1_ref, kw1f_ref, kpos_ref, kw2_ref)
    kcb_ref[0] = _rope_slab(k, cos_ref[...], sin_ref[...]).astype(BF16)
    vcb_ref[0] = compress(vc_ref, vw1_ref, vw1f_ref, vpos_ref, vw2_ref).astype(BF16)


def _compress_weights(pos, w1, w2):
    assert NSA_GROUPS == 2
    half = CMP_STRIDE * HEAD_DIM
    a0 = w1[:half].reshape(CMP_STRIDE, HEAD_DIM, CMP_HIDDEN)
    a1 = w1[half:].reshape(CMP_STRIDE, HEAD_DIM, CMP_HIDDEN)
    z = jnp.zeros_like(a0)
    big = jnp.concatenate([jnp.concatenate([a0, z, a1, z], axis=2), jnp.concatenate([z, a0, z, a1], axis=2)], axis=1)
    big = big.reshape(CMP_STRIDE * KV_WIDTH, 2 * NSA_GROUPS * CMP_HIDDEN).astype(BF16)
    z2 = jnp.zeros_like(w2)
    w2big = jnp.concatenate([jnp.concatenate([w2, z2], axis=1), jnp.concatenate([z2, w2], axis=1)], axis=0)
    return big, w1.astype(BF16), pos.reshape(1, CMP_LEN * HEAD_DIM), w2big.astype(BF16)


def _compress(kc, vc, kparams, vparams, cos_c, sin_c):
    b, seq, width = kc.shape
    n_chunks = seq // CMP_STRIDE
    args = (kc, vc) + _compress_weights(*kparams) + _compress_weights(*vparams) + (cos_c, sin_c)
    x_spec = pl.BlockSpec((1, seq, width), lambda i: (i, 0, 0))
    in_specs = [x_spec, x_spec] + [_resident(a.shape) for a in args[2:]]
    o_spec = pl.BlockSpec((1, n_chunks, KV_WIDTH), lambda i: (i, 0, 0))
    o_shape = jax.ShapeDtypeStruct((b, n_chunks, KV_WIDTH), BF16)
    return pl.pallas_call(
        _compress_kernel,
        grid=(b,),
        in_specs=in_specs,
        out_specs=(o_spec, o_spec),
        out_shape=(o_shape, o_shape),
        compiler_params=pltpu.CompilerParams(dimension_semantics=("parallel",),
                                             vmem_limit_bytes=VMEM_LIMIT),
        name="compress",
    )(*args)


def _nsa_kernel(qat_ref, gst_ref, kcb_ref, vcbt_ref, kk_ref, vst_ref, vwt_ref, ovt_ref, blk_ref, o_ref,
                s_ref, win_ref, m_ref, l_ref, acc_ref, *, seq):
    qi = pl.program_id(1)
    t0 = qi * NSA_Q
    cols = NSA_REP * NSA_Q
    n_cmp = seq // CMP_STRIDE - 1
    n_sel = seq // SEL_BLOCK
    gst = gst_ref[0]
    ovt = ovt_ref[...]

    def key_and_query_pos(n, k0):
        shape = (n, NSA_Q)
        return k0 + lax.broadcasted_iota(jnp.int32, shape, 0), t0 + lax.broadcasted_iota(jnp.int32, shape, 1)

    groups = range(NSA_GROUPS)
    d_rows = [slice(g * HEAD_DIM, (g + 1) * HEAD_DIM) for g in groups]
    g_cols = [slice(g * cols, (g + 1) * cols) for g in groups]

    zeros_d = jnp.zeros((HEAD_DIM, NSA_Q), BF16)
    heads = []
    for n in range(NSA_HEADS):
        x = qat_ref[0, n * HEAD_DIM:(n + 1) * HEAD_DIM, :]
        heads.append(jnp.concatenate([x, zeros_d] if n < NSA_REP else [zeros_d, x], axis=0))
    q = jnp.concatenate(heads, axis=1)

    n_win = WINDOW + NSA_Q
    w0 = pl.multiple_of(jnp.maximum(t0 - WINDOW, 0), NSA_Q)
    s = _dot(kcb_ref[0], q)
    k_win = kk_ref[0, pl.ds(w0, n_win), KV_WIDTH:2 * KV_WIDTH]
    s_win_0 = _dot(k_win, q[:, g_cols[0]])

    n_idx, tq = key_and_query_pos(s.shape[0], 0)
    cmask = (n_idx * CMP_STRIDE + (CMP_LEN - 1) <= tq) & (n_idx < n_cmp)
    p_cmp = _masked_softmax_keys(s, cmask).astype(BF16)
    o_cmp = [_dot(vcbt_ref[0, d_rows[g], :], p_cmp[:, g_cols[g]]) for g in groups]

    pi = _dot(ovt, p_cmp)
    s_win = jnp.concatenate([s_win_0, _dot(k_win, q[:, g_cols[1]])], axis=1)
    jj, tsel = key_and_query_pos(n_sel, 0)
    tblk = tsel >> 6
    bonus = jnp.where((jj == 0) | (jj == tblk) | (jj == tblk - 1), FORCE_BONUS, 0.0)
    pad = jnp.zeros((LANES - n_sel, NSA_Q), BF16)
    biases = []
    for g in groups:
        imp = pi[:, g * cols:g * cols + NSA_Q]
        for r in range(1, NSA_REP):
            imp = imp + pi[:, g * cols + r * NSA_Q:g * cols + (r + 1) * NSA_Q]
        imp = jnp.where(jj <= tblk, imp + bonus, NEG)
        bias = jnp.where(_rank_select(imp, n_sel, SEL_TOPK) & (jj <= tblk), 0.0, NEG)
        biases.append(jnp.concatenate([bias.astype(BF16), pad], axis=0))
    q_sel = jnp.concatenate(
        [jnp.concatenate([heads[n], biases[n // NSA_REP]], axis=0) for n in range(NSA_HEADS)], axis=1)

    def tile_start(c):
        return pl.multiple_of(c * SEL_KEYS, SEL_KEYS)

    def sel_scores(c):
        k0 = tile_start(c)
        keys = jnp.concatenate([kk_ref[0, pl.ds(k0, SEL_KEYS), 0:KV_WIDTH], blk_ref[pl.ds(k0, SEL_KEYS), :]],
                               axis=1)
        return _dot(keys, q_sel)

    def sel_update(c, s):
        for g in groups:
            m, l, acc = _online_update((m_ref[g], l_ref[g], acc_ref[g]), s[:, g_cols[g]],
                                       vst_ref[0, d_rows[g], pl.ds(tile_start(c), SEL_KEYS)])
            m_ref[g], l_ref[g], acc_ref[g] = m, l, acc

    n_past = qi // (SEL_KEYS // NSA_Q)
    m_ref[...] = jnp.full(m_ref.shape, NEG, F32)
    l_ref[...] = jnp.zeros(l_ref.shape, F32)
    acc_ref[...] = jnp.zeros(acc_ref.shape, F32)
    s_ref[0] = sel_scores(0)

    kpos, tq = key_and_query_pos(n_win, w0)
    s_win = _mask_keys(s_win, (kpos <= tq) & (tq - kpos < WINDOW))
    win_ref[...] = s_win
    m_win = jnp.max(s_win, axis=0, keepdims=True)

    for c in range(seq // SEL_KEYS - 1):
        @pl.when(c < n_past)
        def _():
            s_ref[(c + 1) % 2] = sel_scores(c + 1)
            sel_update(c, s_ref[c % 2])

    kpos, tq = key_and_query_pos(SEL_KEYS, tile_start(n_past))
    sel_update(n_past, _mask_keys(s_ref[n_past % 2], kpos <= tq))
    o_sel = [acc_ref[g] / jnp.maximum(l_ref[g], TINY) for g in groups]

    p_win = jnp.exp2(win_ref[...] - m_win).astype(BF16)
    ones_rows = jnp.where(lax.broadcasted_iota(jnp.int32, (BF16_ROWS, n_win), 0) == 0, 1.0, 0.0).astype(BF16)
    o_win = []
    for g in groups:
        pv = _dot(jnp.concatenate([vwt_ref[0, d_rows[g], pl.ds(w0, n_win)], ones_rows], axis=0), p_win[:, g_cols[g]])
        o_win.append(pv[:HEAD_DIM] / jnp.maximum(pv[HEAD_DIM:HEAD_DIM + 1], TINY))

    outs = []
    for n in range(NSA_HEADS):
        g, r = divmod(n, NSA_REP)
        sl = slice(r * NSA_Q, (r + 1) * NSA_Q)
        outs.append(gst[3 * n:3 * n + 1, :] * o_cmp[g][:, sl] + gst[3 * n + 1:3 * n + 2, :] * o_sel[g][:, sl]
                    + gst[3 * n + 2:3 * n + 3, :] * o_win[g][:, sl])
    o_ref[0] = jnp.transpose(jnp.concatenate(outs, axis=0)).astype(BF16)


def _nsa(qat, gst, kcb, vcbt, kk, vvt, ovt, blk):
    b, _, seq = qat.shape
    n_chunks = kcb.shape[1]
    return pl.pallas_call(
        functools.partial(_nsa_kernel, seq=seq),
        grid=(b, seq // NSA_Q),
        in_specs=[pl.BlockSpec((1, NSA_WIDTH, NSA_Q), lambda i, j: (i, 0, j)),
                  pl.BlockSpec((1, LANES, NSA_Q), lambda i, j: (i, 0, j)),
                  pl.BlockSpec((1, n_chunks, KV_WIDTH), lambda i, j: (i, 0, 0)),
                  pl.BlockSpec((1, KV_WIDTH, n_chunks), lambda i, j: (i, 0, 0)),
                  pl.BlockSpec((1, seq, 2 * KV_WIDTH), lambda i, j: (i, 0, 0)),
                  pl.BlockSpec((1, KV_WIDTH, seq), lambda i, j: (i, 0, 0)),
                  pl.BlockSpec((1, KV_WIDTH, seq), lambda i, j: (i, 1, 0)),
                  pl.BlockSpec(ovt.shape, lambda i, j: (0, 0)),
                  pl.BlockSpec(blk.shape, lambda i, j: (0, 0))],
        out_specs=pl.BlockSpec((1, NSA_Q, NSA_WIDTH), lambda i, j: (i, j, 0)),
        out_shape=jax.ShapeDtypeStruct((b, seq, NSA_WIDTH), BF16),
        scratch_shapes=[pltpu.VMEM((2, SEL_KEYS, NSA_HEADS * NSA_Q), F32),
                        pltpu.VMEM((WINDOW + NSA_Q, NSA_HEADS * NSA_Q), F32),
                        pltpu.VMEM((NSA_GROUPS, 1, NSA_REP * NSA_Q), F32),
                        pltpu.VMEM((NSA_GROUPS, 1, NSA_REP * NSA_Q), F32),
                        pltpu.VMEM((NSA_GROUPS, HEAD_DIM, NSA_REP * NSA_Q), F32)],
        compiler_params=pltpu.CompilerParams(dimension_semantics=("parallel", "parallel"),
                                             vmem_limit_bytes=VMEM_LIMIT),
        name="nsa",
    )(qat, gst, kcb, vcbt, kk, vvt, vvt, ovt, blk)


def _moba_kernel(qt_ref, k_ref, vt_ref, kmean_ref, o_ref, s_ref, m_ref, l_ref, acc_ref):
    c = pl.program_id(1)
    nb = kmean_ref.shape[1]
    k_own = pl.multiple_of(c * MOBA_BLOCK, MOBA_BLOCK)
    jj = lax.broadcasted_iota(jnp.int32, (BF16_ROWS, MOBA_BLOCK), 0)
    past = jj < c
    pad = jnp.zeros((BF16_ROWS - nb, LANES), F32)
    zeros_d = jnp.zeros((HEAD_DIM, MOBA_BLOCK), BF16)
    ones_rows = jnp.where(lax.broadcasted_iota(jnp.int32, (BF16_ROWS, MOBA_BLOCK), 0) == 0, 1.0, 0.0).astype(BF16)

    def pair(h):
        return slice((h // 2) * LANES, (h // 2 + 1) * LANES)

    qs, gscs = [], []
    for h in range(MOBA_HEADS):
        x = qt_ref[0, h * HEAD_DIM:(h + 1) * HEAD_DIM, :]
        q = jnp.concatenate([x, zeros_d] if h % 2 == 0 else [zeros_d, x], axis=0)
        km = jnp.concatenate([kmean_ref[0, :, pair(h)], pad], axis=0).astype(BF16)
        qs.append(q)
        gscs.append(_dot(km, q))
    chosen = [jnp.where(_rank_select(jnp.where(past, g, NEG), nb, MOBA_TOPK) & past, 1.0, 0.0) for g in gscs]

    m_ref[...] = jnp.full(m_ref.shape, NEG, F32)
    l_ref[...] = jnp.zeros(l_ref.shape, F32)
    acc_ref[...] = jnp.zeros(acc_ref.shape, F32)

    def block_start(j):
        return pl.multiple_of(j * MOBA_BLOCK, MOBA_BLOCK)

    def scores(j, slot):
        for h in range(MOBA_HEADS):
            s_ref[slot, h] = _dot(k_ref[0, pl.ds(block_start(j), MOBA_BLOCK), pair(h)], qs[h])

    def past_update(j, slot):
        for h in range(MOBA_HEADS):
            rows = slice(h * HEAD_DIM, (h + 1) * HEAD_DIM)
            s = s_ref[slot, h]
            pick = jnp.sum(jnp.where(jj == j, chosen[h], 0.0), axis=0, keepdims=True) > 0.5
            m = m_ref[h]
            m_new = jnp.maximum(m, jnp.where(pick, jnp.max(s, axis=0, keepdims=True), NEG))
            alpha = jnp.exp2(m - m_new)
            p = jnp.exp2(s - jnp.where(pick, m_new, jnp.inf))
            m_ref[h] = m_new
            pv = _dot(jnp.concatenate([vt_ref[0, rows, pl.ds(block_start(j), MOBA_BLOCK)], ones_rows], axis=0),
                      p.astype(BF16))
            l_ref[h] = alpha * l_ref[h] + pv[HEAD_DIM:HEAD_DIM + 1]
            acc_ref[rows, :] = alpha * acc_ref[rows, :] + pv[:HEAD_DIM]

    scores(0, 0)

    def two_blocks(i, _):
        j = 2 * i
        scores(j + 1, 1)
        past_update(j, 0)
        scores(j + 2, 0)
        past_update(j + 1, 1)
        return 0

    lax.fori_loop(0, c // 2, two_blocks, 0)

    @pl.when(c % 2 == 1)
    def _():
        scores(c, 1)
        past_update(c - 1, 0)

    kq = (MOBA_BLOCK, MOBA_BLOCK)
    causal = lax.broadcasted_iota(jnp.int32, kq, 0) <= lax.broadcasted_iota(jnp.int32, kq, 1)
    for h in range(MOBA_HEADS):
        rows = slice(h * HEAD_DIM, (h + 1) * HEAD_DIM)
        carry = (m_ref[h], l_ref[h], acc_ref[rows, :])
        _, l, acc = _online_update(carry, jnp.where(causal, s_ref[c % 2, h], NEG),
                                   vt_ref[0, rows, pl.ds(k_own, MOBA_BLOCK)])
        acc_ref[rows, :] = acc / jnp.maximum(l, TINY)
    o_ref[0] = jnp.transpose(acc_ref[...]).astype(BF16)


def _moba(qbt, kb, vbt, kmean):
    b, _, seq = qbt.shape
    nb = seq // MOBA_BLOCK
    return pl.pallas_call(
        _moba_kernel,
        grid=(b, nb),
        in_specs=[pl.BlockSpec((1, MOBA_WIDTH, MOBA_BLOCK), lambda i, j: (i, 0, j)),
                  pl.BlockSpec((1, seq, MOBA_WIDTH), lambda i, j: (i, 0, 0)),
                  pl.BlockSpec((1, MOBA_WIDTH, seq), lambda i, j: (i, 0, 0)),
                  pl.BlockSpec((1, nb, MOBA_WIDTH), lambda i, j: (i, 0, 0))],
        out_specs=pl.BlockSpec((1, MOBA_BLOCK, MOBA_WIDTH), lambda i, j: (i, j, 0)),
        out_shape=jax.ShapeDtypeStruct((b, seq, MOBA_WIDTH), BF16),
        scratch_shapes=[pltpu.VMEM((2, MOBA_HEADS, MOBA_BLOCK, MOBA_BLOCK), F32),
                        pltpu.VMEM((MOBA_HEADS, 1, MOBA_BLOCK), F32), pltpu.VMEM((MOBA_HEADS, 1, MOBA_BLOCK), F32),
                        pltpu.VMEM((MOBA_WIDTH, MOBA_BLOCK), F32)],
        compiler_params=pltpu.CompilerParams(dimension_semantics=("parallel", "parallel"),
                                             vmem_limit_bytes=VMEM_LIMIT),
        name="moba",
    )(qbt, kb, vbt, kmean)


def _merge_kernel(x_ref, g_ref, a_ref, b_ref, wgate_ref, pa_ref, pb_ref, wo_ref, o_ref):
    x = x_ref[...]
    h = _rms(x, g_ref[...]).astype(BF16)
    ya = _dot(a_ref[0], pa_ref[...])
    yb = _dot(b_ref[0], pb_ref[...])
    merged = (jax.nn.sigmoid(_dot(h, wgate_ref[:, :D_MODEL])) * ya
              + jax.nn.sigmoid(_dot(h, wgate_ref[:, D_MODEL:])) * yb)
    o_ref[...] = x + _dot(merged.astype(BF16), wo_ref[...])


def _merge(x, layer, g, a, bm, wgate, pa, pb, wo):
    t = x.shape[0]
    tiles_per_seq = a.shape[1] // TOKEN_TILE

    def row(n):
        return pl.BlockSpec((TOKEN_TILE, n), lambda i: (i, 0))

    def seq_row(n):
        return pl.BlockSpec((1, TOKEN_TILE, n), lambda i: (i // tiles_per_seq, i % tiles_per_seq, 0))

    return pl.pallas_call(
        _merge_kernel,
        grid=(t // TOKEN_TILE,),
        in_specs=[row(D_MODEL), _layer_resident(g, layer), seq_row(NSA_WIDTH), seq_row(MOBA_WIDTH)]
                 + [_layer_resident(w, layer) for w in (wgate, pa, pb, wo)],
        out_specs=row(D_MODEL),
        out_shape=jax.ShapeDtypeStruct(x.shape, F32),
        compiler_params=pltpu.CompilerParams(dimension_semantics=("parallel",),
                                             vmem_limit_bytes=VMEM_LIMIT),
        name="merge",
    )(x, g, a, bm, wgate, pa, pb, wo)


def _rope_tables(pos):
    inv = ROPE_THETA ** (-jnp.arange(0, HEAD_DIM, 2, dtype=F32) / HEAD_DIM)
    ang = pos.astype(F32)[:, None] * inv[None, :]
    cos, sin = jnp.cos(ang), jnp.sin(ang)
    reps = LANES // HEAD_DIM
    return (jnp.tile(jnp.concatenate([cos, cos], axis=1), (1, reps)),
            jnp.tile(jnp.concatenate([-sin, sin], axis=1), (1, reps)))


def _overlap_matrix_t(seq):
    n_chunks = seq // CMP_STRIDE
    ci = np.arange(n_chunks)[None, :] * CMP_STRIDE
    sj = np.arange(seq // SEL_BLOCK)[:, None] * SEL_BLOCK
    ov = (ci < sj + SEL_BLOCK) & (ci + CMP_LEN > sj) & (np.arange(n_chunks)[None, :] < n_chunks - 1)
    return jnp.asarray(ov, dtype=BF16)


def _block_onehot(seq):
    return jnp.asarray(np.arange(seq)[:, None] // SEL_BLOCK == np.arange(LANES)[None, :], dtype=BF16)


def kernel(x, ffn1_norm, ffn1_wg, ffn1_wu, ffn1_wd, mix_norm, w_in, cmpk_pos, cmpk_w1, cmpk_w2, cmpv_pos, cmpv_w1, cmpv_w2, w_branch_nsa, w_branch_moba, w_out, ffn2_norm, ffn2_wg, ffn2_wu, ffn2_wd, final_norm):
    b, seq, d = x.shape
    depth = w_in.shape[0]
    assert d == D_MODEL and seq % TOKEN_TILE == 0 and seq // CMP_STRIDE == LANES
    t = b * seq
    n_chunks = seq // CMP_STRIDE
    cos_t, sin_t = _rope_tables(jnp.arange(seq))
    cos_c, sin_c = _rope_tables(jnp.arange(n_chunks) * CMP_STRIDE + (CMP_LEN - 1))
    ovt = _overlap_matrix_t(seq)
    blk = _block_onehot(seq)

    def gain(gs):
        return gs.reshape(depth, 1, d)

    ffn1 = (gain(ffn1_norm), ffn1_wg.astype(BF16), ffn1_wu.astype(BF16), ffn1_wd.astype(BF16))
    ffn2 = (gain(ffn2_norm), ffn2_wg.astype(BF16), ffn2_wu.astype(BF16), ffn2_wd.astype(BF16))
    w_attn, w_gate = _split_w_in(w_in)
    w_merge = (w_gate, w_branch_nsa.astype(BF16), w_branch_moba.astype(BF16), w_out.astype(BF16))
    fin = final_norm.reshape(1, d)

    xf = x.reshape(t, d)
    for l in range(depth):
        xf = _ffn(xf, l, *ffn1, fin, False)
        x_mix = xf
        qat, kc, vc, kk, vvt, qbt, kb, vbt, gst, kmean = _proj(xf, l, gain(mix_norm), w_attn, cos_t, sin_t, seq)
        kcb, vcb = _compress(kc, vc,
                             (cmpk_pos[l], cmpk_w1[l], cmpk_w2[l]),
                             (cmpv_pos[l], cmpv_w1[l], cmpv_w2[l]), cos_c, sin_c)
        a = _nsa(qat, gst, kcb, vcb.transpose(0, 2, 1), kk, vvt, ovt, blk)
        bm = _moba(qbt, kb, vbt, kmean.reshape(b, seq // MOBA_BLOCK, MOBA_WIDTH))
        xf = _merge(x_mix, l, gain(mix_norm), a, bm, *w_merge)
        xf = _ffn(xf, l, *ffn2, fin, l == depth - 1)
    return xf.reshape(b, seq, d)
```

```python
import functools

import jax
import jax.numpy as jnp
import numpy as np
from jax import lax
from jax.experimental import pallas as pl
from jax.experimental.pallas import tpu as pltpu

F32 = jnp.float32
BF16 = jnp.bfloat16

D_MODEL = 1024
HEAD_DIM = 64
HALF = HEAD_DIM // 2
NSA_HEADS = 8
NSA_GROUPS = 2
NSA_REP = NSA_HEADS // NSA_GROUPS
CMP_LEN = 32
CMP_STRIDE = 16
CMP_HIDDEN = 128
SEL_BLOCK = 64
SEL_TOPK = 16
WINDOW = 512
MOBA_HEADS = 8
MOBA_BLOCK = 256
MOBA_TOPK = 3
D_FF = 2816
ROPE_THETA = 10000.0
EPS = 1e-6
NEG = -1e30
TINY = 1e-30
FORCE_BONUS = 1e4
Q_SCALE = HEAD_DIM ** -0.5 * float(np.log2(np.e))

NSA_WIDTH = NSA_HEADS * HEAD_DIM
KV_WIDTH = NSA_GROUPS * HEAD_DIM
MOBA_WIDTH = MOBA_HEADS * HEAD_DIM
N_GATE_COLS = 3 * NSA_HEADS

LANES = 128
BF16_ROWS = 16
TOKEN_TILE = 1024
FFN_TILE = 1024
FF_CHUNK = 256
NSA_Q = 256
SEL_KEYS = 512
SEL_UNIT = 2
VMEM_LIMIT = 56 * 1024 * 1024

COL_QA = 0
COL_KC = COL_QA + NSA_WIDTH
COL_VC = COL_KC + KV_WIDTH
COL_KV4 = COL_VC + KV_WIDTH
COL_QB = COL_KV4 + 4 * KV_WIDTH
COL_KB = COL_QB + MOBA_WIDTH
COL_VB = COL_KB + MOBA_WIDTH
COL_GS = COL_VB + MOBA_WIDTH


def _rms(x, g):
    return x * lax.rsqrt(jnp.mean(x * x, axis=-1, keepdims=True) + EPS) * g


def _dot(a, b):
    return jnp.dot(a, b, preferred_element_type=F32)


def _swap_halves(z):
    lane = lax.broadcasted_iota(jnp.int32, z.shape, 1)
    lo = (lane & (HEAD_DIM - 1)) < HALF
    return jnp.where(lo, pltpu.roll(z, LANES - HALF, 1), pltpu.roll(z, HALF, 1))


def _rope_slab(z, cos, sin):
    parts = []
    for c in range(z.shape[1] // LANES):
        zc = z[:, c * LANES:(c + 1) * LANES]
        parts.append(zc * cos + _swap_halves(zc) * sin)
    return parts[0] if len(parts) == 1 else jnp.concatenate(parts, axis=1)


def _slabs(s, n):
    return [s[:, i * n:(i + 1) * n] for i in range(s.shape[1] // n)]


def _masked_softmax_keys(s, mask):
    out = []
    for x in _slabs(s, mask.shape[1]):
        x = jnp.where(mask, x, NEG)
        m = jnp.max(x, axis=0, keepdims=True)
        p = jnp.where(mask, jnp.exp2(x - m), 0.0)
        out.append(p / jnp.maximum(jnp.sum(p, axis=0, keepdims=True), TINY))
    return jnp.concatenate(out, axis=1)


def _mask_keys(s, mask):
    return jnp.concatenate([jnp.where(mask, x, NEG) for x in _slabs(s, mask.shape[1])], axis=1)


def _online_update(carry, s, vt):
    m, l, acc = carry
    m_new = jnp.maximum(m, jnp.max(s, axis=0, keepdims=True))
    alpha = jnp.exp2(m - m_new)
    p = jnp.exp2(s - m_new).astype(BF16)
    ones_rows = jnp.where(lax.broadcasted_iota(jnp.int32, (BF16_ROWS, vt.shape[1]), 0) == 0, 1.0, 0.0).astype(BF16)
    pv = _dot(jnp.concatenate([vt, ones_rows], axis=0), p)
    return m_new, alpha * l + pv[HEAD_DIM:HEAD_DIM + 1], alpha * acc + pv[:HEAD_DIM]


def _rank_select(v, n_candidates, k):
    row = lax.broadcasted_iota(jnp.int32, v.shape, 0)
    cnt = jnp.zeros(v.shape, F32)
    for j in range(n_candidates):
        r = v[j:j + 1, :]
        cnt = cnt + jnp.where(r > v, 1.0, jnp.where(r == v, jnp.where(row > j, 1.0, 0.0), 0.0))
    return cnt < float(k)


def _ffn_kernel(x_ref, g_ref, wg_ref, wu_ref, wd_ref, fin_ref, o_ref, t_ref, *, final_norm):
    x = x_ref[...]
    h = (x * g_ref[...]).astype(BF16)
    inv = lax.rsqrt(jnp.mean(x * x, axis=-1, keepdims=True) + EPS)
    for f in range(D_FF // FF_CHUNK):
        c = slice(f * FF_CHUNK, (f + 1) * FF_CHUNK)
        a = _dot(h, wg_ref[:, c]) * inv
        b = _dot(h, wu_ref[:, c]) * inv
        t_ref[:, c] = (a * jax.nn.sigmoid(a) * b).astype(BF16)
    y = x + 0.5 * _dot(t_ref[...], wd_ref[...])
    if final_norm:
        y = _rms(y, fin_ref[...])
    o_ref[...] = y


def _resident(shape):
    nd = len(shape)
    return pl.BlockSpec(shape, lambda *_: (0,) * nd, pipeline_mode=pl.Buffered(1))


def _layer_resident(stacked, layer):
    return pl.BlockSpec((None,) + stacked.shape[1:], lambda *_: (layer, 0, 0), pipeline_mode=pl.Buffered(1))


def _ffn(x, layer, g, wg, wu, wd, fin, final_norm):
    t = x.shape[0]
    row = pl.BlockSpec((FFN_TILE, D_MODEL), lambda i: (i, 0))
    return pl.pallas_call(
        functools.partial(_ffn_kernel, final_norm=final_norm),
        grid=(t // FFN_TILE,),
        in_specs=[row, _layer_resident(g, layer), _layer_resident(wg, layer), _layer_resident(wu, layer),
                  _layer_resident(wd, layer), _resident((1, D_MODEL))],
        out_specs=row,
        out_shape=jax.ShapeDtypeStruct(x.shape, F32),
        scratch_shapes=[pltpu.VMEM((FFN_TILE, D_FF), BF16)],
        compiler_params=pltpu.CompilerParams(dimension_semantics=("parallel",),
                                             vmem_limit_bytes=VMEM_LIMIT),
        name="ffn",
    )(x, g, wg, wu, wd, fin)


def _proj_kernel(x_ref, g_ref, w_ref, cos_ref, sin_ref,
                 qat_ref, kc_ref, vc_ref, kk_ref, vvt_ref, qbt_ref, kb_ref, vbt_ref, gst_ref, kmean_ref):
    h = _rms(x_ref[...], g_ref[...]).astype(BF16)
    cos = cos_ref[...]
    sin = sin_ref[...]

    def mm(c0, n):
        return _dot(h, w_ref[:, c0:c0 + n])

    def feature_major(z, dtype=BF16):
        return jnp.transpose(z).astype(dtype)

    qat_ref[0] = feature_major(_rope_slab(mm(COL_QA, NSA_WIDTH), cos, sin) * Q_SCALE)
    kcvc = mm(COL_KC, 2 * KV_WIDTH)
    kc_ref[0] = kcvc[:, :KV_WIDTH]
    vc_ref[0] = kcvc[:, KV_WIDTH:]
    kv4 = mm(COL_KV4, 4 * KV_WIDTH)
    kk_ref[0] = jnp.concatenate(
        [_rope_slab(kv4[:, 0:KV_WIDTH], cos, sin), _rope_slab(kv4[:, 2 * KV_WIDTH:3 * KV_WIDTH], cos, sin)],
        axis=1).astype(BF16)
    vvt_ref[0] = feature_major(jnp.concatenate([kv4[:, KV_WIDTH:2 * KV_WIDTH], kv4[:, 3 * KV_WIDTH:]], axis=1))
    qbt_ref[0] = feature_major(_rope_slab(mm(COL_QB, MOBA_WIDTH), cos, sin) * Q_SCALE)
    kb = _rope_slab(mm(COL_KB, MOBA_WIDTH), cos, sin)
    kb_ref[0] = kb.astype(BF16)
    means = [jnp.mean(kb[i * MOBA_BLOCK:(i + 1) * MOBA_BLOCK], axis=0, keepdims=True)
             for i in range(TOKEN_TILE // MOBA_BLOCK)]
    kmean_ref[0] = jnp.concatenate(means, axis=0)
    vbt_ref[0] = feature_major(mm(COL_VB, MOBA_WIDTH))
    gst_ref[0] = feature_major(jax.nn.sigmoid(mm(COL_GS, LANES)), F32)


W_IN_ROWS = 256


def _split_w_in_kernel(wt_ref, attn_ref, gate_ref):
    off_ga = NSA_WIDTH
    off_rest = off_ga + N_GATE_COLS
    off_gates = off_rest + 6 * KV_WIDTH + 3 * MOBA_WIDTH

    def rows_major(a, b):
        return jnp.transpose(wt_ref[a:b, :]).astype(BF16)

    attn_ref[:, :off_ga] = rows_major(0, off_ga)
    attn_ref[:, off_ga:COL_GS] = rows_major(off_rest, off_gates)
    pad = jnp.zeros((LANES - N_GATE_COLS, wt_ref.shape[1]), F32)
    attn_ref[:, COL_GS:] = jnp.transpose(jnp.concatenate([wt_ref[off_ga:off_rest, :], pad], axis=0)).astype(BF16)
    gate_ref[...] = rows_major(off_gates, wt_ref.shape[0])


def _split_w_in(w_in):
    depth, d, n = w_in.shape
    steps = d // W_IN_ROWS
    return pl.pallas_call(
        _split_w_in_kernel,
        grid=(depth, steps),
        in_specs=[pl.BlockSpec((None, n, W_IN_ROWS), lambda l, i: (l, 0, i))],
        out_specs=(pl.BlockSpec((None, W_IN_ROWS, COL_GS + LANES), lambda l, i: (l, i, 0)),
                   pl.BlockSpec((None, W_IN_ROWS, 2 * D_MODEL), lambda l, i: (l, i, 0))),
        out_shape=(jax.ShapeDtypeStruct((depth, d, COL_GS + LANES), BF16),
                   jax.ShapeDtypeStruct((depth, d, 2 * D_MODEL), BF16)),
        compiler_params=pltpu.CompilerParams(dimension_semantics=("parallel", "parallel"),
                                             vmem_limit_bytes=VMEM_LIMIT),
        name="split_w_in",
    )(jnp.swapaxes(w_in, 1, 2))


def _proj(x, layer, g, w, cos_t, sin_t, seq):
    t = x.shape[0]
    tiles_per_seq = seq // TOKEN_TILE
    blocks_per_tile = TOKEN_TILE // MOBA_BLOCK

    def row(n):
        return pl.BlockSpec((TOKEN_TILE, n), lambda i: (i, 0))

    def token_major(n, dtype=BF16):
        return (jax.ShapeDtypeStruct((t // seq, seq, n), dtype),
                pl.BlockSpec((1, TOKEN_TILE, n), lambda i: (i // tiles_per_seq, i % tiles_per_seq, 0)))

    def feature_major(n, dtype=BF16):
        return (jax.ShapeDtypeStruct((t // seq, n, seq), dtype),
                pl.BlockSpec((1, n, TOKEN_TILE), lambda i: (i // tiles_per_seq, 0, i % tiles_per_seq)))

    tab = pl.BlockSpec((TOKEN_TILE, LANES), lambda i: (i % tiles_per_seq, 0))
    outs = (feature_major(NSA_WIDTH), token_major(KV_WIDTH, F32), token_major(KV_WIDTH, F32),
            token_major(2 * KV_WIDTH),
            feature_major(2 * KV_WIDTH), feature_major(MOBA_WIDTH), token_major(MOBA_WIDTH),
            feature_major(MOBA_WIDTH), feature_major(LANES, F32),
            (jax.ShapeDtypeStruct((t // TOKEN_TILE, blocks_per_tile, MOBA_WIDTH), F32),
             pl.BlockSpec((1, blocks_per_tile, MOBA_WIDTH), lambda i: (i, 0, 0))))
    out_shape, out_specs = zip(*outs)
    return pl.pallas_call(
        _proj_kernel,
        grid=(t // TOKEN_TILE,),
        in_specs=[row(D_MODEL), _layer_resident(g, layer), _layer_resident(w, layer), tab, tab],
        out_specs=out_specs,
        out_shape=out_shape,
        compiler_params=pltpu.CompilerParams(dimension_semantics=("parallel",),
                                             vmem_limit_bytes=VMEM_LIMIT),
        name="proj",
    )(x, g, w, cos_t, sin_t)


def _gelu_tanh(x):
    return 0.5 * x * (1.0 + jnp.tanh(np.sqrt(2.0 / np.pi).astype(np.float32) * (x + 0.044715 * (x * x * x))))


def _compress_kernel(kc_ref, vc_ref, kw1_ref, kw1f_ref, kpos_ref, kw2_ref,
                     vw1_ref, vw1f_ref, vpos_ref, vw2_ref, cos_ref, sin_ref, kcb_ref, vcb_ref):
    n_chunks = kc_ref.shape[1] // CMP_STRIDE

    def compress(x_ref, w1_ref, w1f_ref, pos_ref, w2_ref):
        u = jnp.zeros((n_chunks, 4 * CMP_HIDDEN), F32)
        for p in range(CMP_STRIDE):
            rows = x_ref[0, pl.ds(p, n_chunks, stride=CMP_STRIDE), :].astype(BF16)
            u = u + _dot(rows, w1_ref[p * KV_WIDTH:(p + 1) * KV_WIDTH, :])
        first = u[:, :2 * CMP_HIDDEN]
        second = pltpu.roll(u[:, 2 * CMP_HIDDEN:], n_chunks - 1, 0)
        pos = jnp.broadcast_to(pos_ref[...], (8, CMP_LEN * HEAD_DIM)).astype(BF16)
        bias = _dot(pos, w1f_ref[...])[0:1]
        pre = first + second + jnp.concatenate([bias, bias], axis=1)
        return _dot(_gelu_tanh(pre).astype(BF16), w2_ref[...])

    k = compress(kc_ref, kw1_ref, kw1f_ref, kpos_ref, kw2_ref)
    kcb_ref[0] = _rope_slab(k, cos_ref[...], sin_ref[...]).astype(BF16)
    vcb_ref[0] = compress(vc_ref, vw1_ref, vw1f_ref, vpos_ref, vw2_ref).astype(BF16)


def _compress_weights(pos, w1, w2):
    assert NSA_GROUPS == 2
    half = CMP_STRIDE * HEAD_DIM
    a0 = w1[:half].reshape(CMP_STRIDE, HEAD_DIM, CMP_HIDDEN)
    a1 = w1[half:].reshape(CMP_STRIDE, HEAD_DIM, CMP_HIDDEN)
    z = jnp.zeros_like(a0)
    big = jnp.concatenate([jnp.concatenate([a0, z, a1, z], axis=2), jnp.concatenate([z, a0, z, a1], axis=2)], axis=1)
    big = big.reshape(CMP_STRIDE * KV_WIDTH, 2 * NSA_GROUPS * CMP_HIDDEN).astype(BF16)
    z2 = jnp.zeros_like(w2)
    w2big = jnp.concatenate([jnp.concatenate([w2, z2], axis=1), jnp.concatenate([z2, w2], axis=1)], axis=0)
    return big, w1.astype(BF16), pos.reshape(1, CMP_LEN * HEAD_DIM), w2big.astype(BF16)


def _compress(kc, vc, kparams, vparams, cos_c, sin_c):
    b, seq, width = kc.shape
    n_chunks = seq // CMP_STRIDE
    args = (kc, vc) + _compress_weights(*kparams) + _compress_weights(*vparams) + (cos_c, sin_c)
    x_spec = pl.BlockSpec((1, seq, width), lambda i: (i, 0, 0))
    in_specs = [x_spec, x_spec] + [_resident(a.shape) for a in args[2:]]
    o_spec = pl.BlockSpec((1, n_chunks, KV_WIDTH), lambda i: (i, 0, 0))
    o_shape = jax.ShapeDtypeStruct((b, n_chunks, KV_WIDTH), BF16)
    return pl.pallas_call(
        _compress_kernel,
        grid=(b,),
        in_specs=in_specs,
        out_specs=(o_spec, o_spec),
        out_shape=(o_shape, o_shape),
        compiler_params=pltpu.CompilerParams(dimension_semantics=("parallel",),
                                             vmem_limit_bytes=VMEM_LIMIT),
        name="compress",
    )(*args)


def _nsa_kernel(qat_ref, gst_ref, kcb_ref, vcbt_ref, kk_ref, vst_ref, vwt_ref, ovt_ref, blk_ref, o_ref,
                s_ref, win_ref, m_ref, l_ref, acc_ref, *, seq):
    qi = pl.program_id(1)
    t0 = qi * NSA_Q
    cols = NSA_REP * NSA_Q
    n_cmp = seq // CMP_STRIDE - 1
    n_sel = seq // SEL_BLOCK
    gst = gst_ref[0]
    ovt = ovt_ref[...]

    def key_and_query_pos(n, k0):
        shape = (n, NSA_Q)
        return k0 + lax.broadcasted_iota(jnp.int32, shape, 0), t0 + lax.broadcasted_iota(jnp.int32, shape, 1)

    groups = range(NSA_GROUPS)
    d_rows = [slice(g * HEAD_DIM, (g + 1) * HEAD_DIM) for g in groups]
    g_cols = [slice(g * cols, (g + 1) * cols) for g in groups]

    zeros_d = jnp.zeros((HEAD_DIM, NSA_Q), BF16)
    heads = []
    for n in range(NSA_HEADS):
        x = qat_ref[0, n * HEAD_DIM:(n + 1) * HEAD_DIM, :]
        heads.append(jnp.concatenate([x, zeros_d] if n < NSA_REP else [zeros_d, x], axis=0))
    q = jnp.concatenate(heads, axis=1)

    n_win = WINDOW + NSA_Q
    w0 = pl.multiple_of(jnp.maximum(t0 - WINDOW, 0), NSA_Q)
    s = _dot(kcb_ref[0], q)
    k_win = kk_ref[0, pl.ds(w0, n_win), KV_WIDTH:2 * KV_WIDTH]
    s_win_0 = _dot(k_win, q[:, g_cols[0]])

    n_idx, tq = key_and_query_pos(s.shape[0], 0)
    cmask = (n_idx * CMP_STRIDE + (CMP_LEN - 1) <= tq) & (n_idx < n_cmp)
    p_cmp = _masked_softmax_keys(s, cmask).astype(BF16)
    o_cmp = [_dot(vcbt_ref[0, d_rows[g], :], p_cmp[:, g_cols[g]]) for g in groups]

    pi = _dot(ovt, p_cmp)
    s_win = jnp.concatenate([s_win_0, _dot(k_win, q[:, g_cols[1]])], axis=1)
    jj, tsel = key_and_query_pos(n_sel, 0)
    tblk = tsel >> 6
    bonus = jnp.where((jj == 0) | (jj == tblk) | (jj == tblk - 1), FORCE_BONUS, 0.0)
    pad = jnp.zeros((LANES - n_sel, NSA_Q), BF16)
    biases = []
    for g in groups:
        imp = pi[:, g * cols:g * cols + NSA_Q]
        for r in range(1, NSA_REP):
            imp = imp + pi[:, g * cols + r * NSA_Q:g * cols + (r + 1) * NSA_Q]
        imp = jnp.where(jj <= tblk, imp + bonus, NEG)
        bias = jnp.where(_rank_select(imp, n_sel, SEL_TOPK) & (jj <= tblk), 0.0, NEG)
        biases.append(jnp.concatenate([bias.astype(BF16), pad], axis=0))
    q_sel = jnp.concatenate(
        [jnp.concatenate([heads[n], biases[n // NSA_REP]], axis=0) for n in range(NSA_HEADS)], axis=1)

    def tile_start(c):
        return pl.multiple_of(c * SEL_KEYS, SEL_KEYS)

    units = tuple(range(n, n + SEL_UNIT) for n in range(0, NSA_HEADS, SEL_UNIT))

    def unit_cols(unit):
        return slice(unit[0] * NSA_Q, (unit[-1] + 1) * NSA_Q)

    def sel_scores(c, slot, unit=range(NSA_HEADS)):
        k0 = tile_start(c)
        keys = jnp.concatenate([kk_ref[0, pl.ds(k0, SEL_KEYS), 0:KV_WIDTH], blk_ref[pl.ds(k0, SEL_KEYS), :]],
                               axis=1)
        s_ref[slot, :, unit_cols(unit)] = _dot(keys, q_sel[:, unit_cols(unit)])

    def sel_update(c, s, unit=range(NSA_HEADS)):
        for lo_head in range(unit[0], unit[-1] + 1, min(len(unit), NSA_REP)):
            g = lo_head // NSA_REP
            n_heads = min(len(unit), NSA_REP)
            sl = slice((lo_head % NSA_REP) * NSA_Q, (lo_head % NSA_REP + n_heads) * NSA_Q)
            gl = slice(lo_head * NSA_Q, (lo_head + n_heads) * NSA_Q)
            m, l, acc = _online_update((m_ref[g, :, sl], l_ref[g, :, sl], acc_ref[g, :, sl]), s[:, gl],
                                       vst_ref[0, d_rows[g], pl.ds(tile_start(c), SEL_KEYS)])
            m_ref[g, :, sl], l_ref[g, :, sl], acc_ref[g, :, sl] = m, l, acc

    n_past = qi // (SEL_KEYS // NSA_Q)
    m_ref[...] = jnp.full(m_ref.shape, NEG, F32)
    l_ref[...] = jnp.zeros(l_ref.shape, F32)
    acc_ref[...] = jnp.zeros(acc_ref.shape, F32)
    sel_scores(0, 0)

    kpos, tq = key_and_query_pos(n_win, w0)
    s_win = _mask_keys(s_win, (kpos <= tq) & (tq - kpos < WINDOW))
    win_ref[...] = s_win
    m_win = jnp.max(s_win, axis=0, keepdims=True)

    for c in range(seq // SEL_KEYS - 1):
        @pl.when(c < n_past)
        def _():
            for unit in units:
                sel_scores(c + 1, (c + 1) % 2, unit)
                sel_update(c, s_ref[c % 2], unit)

    kpos, tq = key_and_query_pos(SEL_KEYS, tile_start(n_past))
    sel_update(n_past, _mask_keys(s_ref[n_past % 2], kpos <= tq))
    o_sel = [acc_ref[g] / jnp.maximum(l_ref[g], TINY) for g in groups]

    p_win = jnp.exp2(win_ref[...] - m_win).astype(BF16)
    ones_rows = jnp.where(lax.broadcasted_iota(jnp.int32, (BF16_ROWS, n_win), 0) == 0, 1.0, 0.0).astype(BF16)
    o_win = []
    for g in groups:
        pv = _dot(jnp.concatenate([vwt_ref[0, d_rows[g], pl.ds(w0, n_win)], ones_rows], axis=0), p_win[:, g_cols[g]])
        o_win.append(pv[:HEAD_DIM] / jnp.maximum(pv[HEAD_DIM:HEAD_DIM + 1], TINY))

    outs = []
    for n in range(NSA_HEADS):
        g, r = divmod(n, NSA_REP)
        sl = slice(r * NSA_Q, (r + 1) * NSA_Q)
        outs.append(gst[3 * n:3 * n + 1, :] * o_cmp[g][:, sl] + gst[3 * n + 1:3 * n + 2, :] * o_sel[g][:, sl]
                    + gst[3 * n + 2:3 * n + 3, :] * o_win[g][:, sl])
    o_ref[0] = jnp.transpose(jnp.concatenate(outs, axis=0)).astype(BF16)


def _nsa(qat, gst, kcb, vcbt, kk, vvt, ovt, blk):
    b, _, seq = qat.shape
    n_chunks = kcb.shape[1]
    return pl.pallas_call(
        functools.partial(_nsa_kernel, seq=seq),
        grid=(b, seq // NSA_Q),
        in_specs=[pl.BlockSpec((1, NSA_WIDTH, NSA_Q), lambda i, j: (i, 0, j)),
                  pl.BlockSpec((1, LANES, NSA_Q), lambda i, j: (i, 0, j)),
                  pl.BlockSpec((1, n_chunks, KV_WIDTH), lambda i, j: (i, 0, 0)),
                  pl.BlockSpec((1, KV_WIDTH, n_chunks), lambda i, j: (i, 0, 0)),
                  pl.BlockSpec((1, seq, 2 * KV_WIDTH), lambda i, j: (i, 0, 0)),
                  pl.BlockSpec((1, KV_WIDTH, seq), lambda i, j: (i, 0, 0)),
                  pl.BlockSpec((1, KV_WIDTH, seq), lambda i, j: (i, 1, 0)),
                  pl.BlockSpec(ovt.shape, lambda i, j: (0, 0)),
                  pl.BlockSpec(blk.shape, lambda i, j: (0, 0))],
        out_specs=pl.BlockSpec((1, NSA_Q, NSA_WIDTH), lambda i, j: (i, j, 0)),
        out_shape=jax.ShapeDtypeStruct((b, seq, NSA_WIDTH), BF16),
        scratch_shapes=[pltpu.VMEM((2, SEL_KEYS, NSA_HEADS * NSA_Q), F32),
                        pltpu.VMEM((WINDOW + NSA_Q, NSA_HEADS * NSA_Q), F32),
                        pltpu.VMEM((NSA_GROUPS, 1, NSA_REP * NSA_Q), F32),
                        pltpu.VMEM((NSA_GROUPS, 1, NSA_REP * NSA_Q), F32),
                        pltpu.VMEM((NSA_GROUPS, HEAD_DIM, NSA_REP * NSA_Q), F32)],
        compiler_params=pltpu.CompilerParams(dimension_semantics=("parallel", "parallel"),
                                             vmem_limit_bytes=VMEM_LIMIT),
        name="nsa",
    )(qat, gst, kcb, vcbt, kk, vvt, vvt, ovt, blk)


def _moba_kernel(qt_ref, k_ref, vt_ref, kmean_ref, o_ref, s_ref, m_ref, l_ref, acc_ref):
    c = pl.program_id(1)
    nb = kmean_ref.shape[1]
    k_own = pl.multiple_of(c * MOBA_BLOCK, MOBA_BLOCK)
    jj = lax.broadcasted_iota(jnp.int32, (BF16_ROWS, MOBA_BLOCK), 0)
    past = jj < c
    pad = jnp.zeros((BF16_ROWS - nb, LANES), F32)
    zeros_d = jnp.zeros((HEAD_DIM, MOBA_BLOCK), BF16)
    ones_rows = jnp.where(lax.broadcasted_iota(jnp.int32, (BF16_ROWS, MOBA_BLOCK), 0) == 0, 1.0, 0.0).astype(BF16)

    def pair(h):
        return slice((h // 2) * LANES, (h // 2 + 1) * LANES)

    qs, gscs = [], []
    for h in range(MOBA_HEADS):
        x = qt_ref[0, h * HEAD_DIM:(h + 1) * HEAD_DIM, :]
        q = jnp.concatenate([x, zeros_d] if h % 2 == 0 else [zeros_d, x], axis=0)
        km = jnp.concatenate([kmean_ref[0, :, pair(h)], pad], axis=0).astype(BF16)
        qs.append(q)
        gscs.append(_dot(km, q))
    chosen = [jnp.where(_rank_select(jnp.where(past, g, NEG), nb, MOBA_TOPK) & past, 1.0, 0.0) for g in gscs]

    m_ref[...] = jnp.full(m_ref.shape, NEG, F32)
    l_ref[...] = jnp.zeros(l_ref.shape, F32)
    acc_ref[...] = jnp.zeros(acc_ref.shape, F32)

    def block_start(j):
        return pl.multiple_of(j * MOBA_BLOCK, MOBA_BLOCK)

    head_pairs = tuple(range(h, h + 2) for h in range(0, MOBA_HEADS, 2))

    def scores(j, slot, heads=range(MOBA_HEADS)):
        for h in heads:
            s_ref[slot, h] = _dot(k_ref[0, pl.ds(block_start(j), MOBA_BLOCK), pair(h)], qs[h])

    def past_update(j, slot, heads=range(MOBA_HEADS)):
        for h in heads:
            rows = slice(h * HEAD_DIM, (h + 1) * HEAD_DIM)
            s = s_ref[slot, h]
            pick = jnp.sum(jnp.where(jj == j, chosen[h], 0.0), axis=0, keepdims=True) > 0.5
            m = m_ref[h]
            m_new = jnp.maximum(m, jnp.where(pick, jnp.max(s, axis=0, keepdims=True), NEG))
            alpha = jnp.exp2(m - m_new)
            p = jnp.exp2(s - jnp.where(pick, m_new, jnp.inf))
            m_ref[h] = m_new
            pv = _dot(jnp.concatenate([vt_ref[0, rows, pl.ds(block_start(j), MOBA_BLOCK)], ones_rows], axis=0),
                      p.astype(BF16))
            l_ref[h] = alpha * l_ref[h] + pv[HEAD_DIM:HEAD_DIM + 1]
            acc_ref[rows, :] = alpha * acc_ref[rows, :] + pv[:HEAD_DIM]

    scores(0, 0)

    def two_blocks(i, _):
        j = 2 * i
        for heads in head_pairs:
            scores(j + 1, 1, heads)
            past_update(j, 0, heads)
        for heads in head_pairs:
            scores(j + 2, 0, heads)
            past_update(j + 1, 1, heads)
        return 0

    lax.fori_loop(0, c // 2, two_blocks, 0)

    @pl.when(c % 2 == 1)
    def _():
        for heads in head_pairs:
            scores(c, 1, heads)
            past_update(c - 1, 0, heads)

    kq = (MOBA_BLOCK, MOBA_BLOCK)
    causal = lax.broadcasted_iota(jnp.int32, kq, 0) <= lax.broadcasted_iota(jnp.int32, kq, 1)
    for h in range(MOBA_HEADS):
        rows = slice(h * HEAD_DIM, (h + 1) * HEAD_DIM)
        carry = (m_ref[h], l_ref[h], acc_ref[rows, :])
        _, l, acc = _online_update(carry, jnp.where(causal, s_ref[c % 2, h], NEG),
                                   vt_ref[0, rows, pl.ds(k_own, MOBA_BLOCK)])
        acc_ref[rows, :] = acc / jnp.maximum(l, TINY)
    o_ref[0] = jnp.transpose(acc_ref[...]).astype(BF16)


def _moba(qbt, kb, vbt, kmean):
    b, _, seq = qbt.shape
    nb = seq // MOBA_BLOCK
    return pl.pallas_call(
        _moba_kernel,
        grid=(b, nb),
        in_specs=[pl.BlockSpec((1, MOBA_WIDTH, MOBA_BLOCK), lambda i, j: (i, 0, j)),
                  pl.BlockSpec((1, seq, MOBA_WIDTH), lambda i, j: (i, 0, 0)),
                  pl.BlockSpec((1, MOBA_WIDTH, seq), lambda i, j: (i, 0, 0)),
                  pl.BlockSpec((1, nb, MOBA_WIDTH), lambda i, j: (i, 0, 0))],
        out_specs=pl.BlockSpec((1, MOBA_BLOCK, MOBA_WIDTH), lambda i, j: (i, j, 0)),
        out_shape=jax.ShapeDtypeStruct((b, seq, MOBA_WIDTH), BF16),
        scratch_shapes=[pltpu.VMEM((2, MOBA_HEADS, MOBA_BLOCK, MOBA_BLOCK), F32),
                        pltpu.VMEM((MOBA_HEADS, 1, MOBA_BLOCK), F32), pltpu.VMEM((MOBA_HEADS, 1, MOBA_BLOCK), F32),
                        pltpu.VMEM((MOBA_WIDTH, MOBA_BLOCK), F32)],
        compiler_params=pltpu.CompilerParams(dimension_semantics=("parallel", "parallel"),
                                             vmem_limit_bytes=VMEM_LIMIT),
        name="moba",
    )(qbt, kb, vbt, kmean)


def _merge_kernel(x_ref, g_ref, a_ref, b_ref, wgate_ref, pa_ref, pb_ref, wo_ref, o_ref):
    x = x_ref[...]
    h = _rms(x, g_ref[...]).astype(BF16)
    ya = _dot(a_ref[0], pa_ref[...])
    yb = _dot(b_ref[0], pb_ref[...])
    merged = (jax.nn.sigmoid(_dot(h, wgate_ref[:, :D_MODEL])) * ya
              + jax.nn.sigmoid(_dot(h, wgate_ref[:, D_MODEL:])) * yb)
    o_ref[...] = x + _dot(merged.astype(BF16), wo_ref[...])


def _merge(x, layer, g, a, bm, wgate, pa, pb, wo):
    t = x.shape[0]
    tiles_per_seq = a.shape[1] // TOKEN_TILE

    def row(n):
        return pl.BlockSpec((TOKEN_TILE, n), lambda i: (i, 0))

    def seq_row(n):
        return pl.BlockSpec((1, TOKEN_TILE, n), lambda i: (i // tiles_per_seq, i % tiles_per_seq, 0))

    return pl.pallas_call(
        _merge_kernel,
        grid=(t // TOKEN_TILE,),
        in_specs=[row(D_MODEL), _layer_resident(g, layer), seq_row(NSA_WIDTH), seq_row(MOBA_WIDTH)]
                 + [_layer_resident(w, layer) for w in (wgate, pa, pb, wo)],
        out_specs=row(D_MODEL),
        out_shape=jax.ShapeDtypeStruct(x.shape, F32),
        compiler_params=pltpu.CompilerParams(dimension_semantics=("parallel",),
                                             vmem_limit_bytes=VMEM_LIMIT),
        name="merge",
    )(x, g, a, bm, wgate, pa, pb, wo)


def _rope_tables(pos):
    inv = ROPE_THETA ** (-jnp.arange(0, HEAD_DIM, 2, dtype=F32) / HEAD_DIM)
    ang = pos.astype(F32)[:, None] * inv[None, :]
    cos, sin = jnp.cos(ang), jnp.sin(ang)
    reps = LANES // HEAD_DIM
    return (jnp.tile(jnp.concatenate([cos, cos], axis=1), (1, reps)),
            jnp.tile(jnp.concatenate([-sin, sin], axis=1), (1, reps)))


def _overlap_matrix_t(seq):
    n_chunks = seq // CMP_STRIDE
    ci = np.arange(n_chunks)[None, :] * CMP_STRIDE
    sj = np.arange(seq // SEL_BLOCK)[:, None] * SEL_BLOCK
    ov = (ci < sj + SEL_BLOCK) & (ci + CMP_LEN > sj) & (np.arange(n_chunks)[None, :] < n_chunks - 1)
    return jnp.asarray(ov, dtype=BF16)


def _block_onehot(seq):
    return jnp.asarray(np.arange(seq)[:, None] // SEL_BLOCK == np.arange(LANES)[None, :], dtype=BF16)


def kernel(x, ffn1_norm, ffn1_wg, ffn1_wu, ffn1_wd, mix_norm, w_in, cmpk_pos, cmpk_w1, cmpk_w2, cmpv_pos, cmpv_w1, cmpv_w2, w_branch_nsa, w_branch_moba, w_out, ffn2_norm, ffn2_wg, ffn2_wu, ffn2_wd, final_norm):
    b, seq, d = x.shape
    depth = w_in.shape[0]
    assert d == D_MODEL and seq % TOKEN_TILE == 0 and seq // CMP_STRIDE == LANES
    t = b * seq
    n_chunks = seq // CMP_STRIDE
    cos_t, sin_t = _rope_tables(jnp.arange(seq))
    cos_c, sin_c = _rope_tables(jnp.arange(n_chunks) * CMP_STRIDE + (CMP_LEN - 1))
    ovt = _overlap_matrix_t(seq)
    blk = _block_onehot(seq)

    def gain(gs):
        return gs.reshape(depth, 1, d)

    ffn1 = (gain(ffn1_norm), ffn1_wg.astype(BF16), ffn1_wu.astype(BF16), ffn1_wd.astype(BF16))
    ffn2 = (gain(ffn2_norm), ffn2_wg.astype(BF16), ffn2_wu.astype(BF16), ffn2_wd.astype(BF16))
    w_attn, w_gate = _split_w_in(w_in)
    w_merge = (w_gate, w_branch_nsa.astype(BF16), w_branch_moba.astype(BF16), w_out.astype(BF16))
    fin = final_norm.reshape(1, d)

    xf = x.reshape(t, d)
    for l in range(depth):
        xf = _ffn(xf, l, *ffn1, fin, False)
        x_mix = xf
        qat, kc, vc, kk, vvt, qbt, kb, vbt, gst, kmean = _proj(xf, l, gain(mix_norm), w_attn, cos_t, sin_t, seq)
        kcb, vcb = _compress(kc, vc,
                             (cmpk_pos[l], cmpk_w1[l], cmpk_w2[l]),
                             (cmpv_pos[l], cmpv_w1[l], cmpv_w2[l]), cos_c, sin_c)
        a = _nsa(qat, gst, kcb, vcb.transpose(0, 2, 1), kk, vvt, ovt, blk)
        bm = _moba(qbt, kb, vbt, kmean.reshape(b, seq // MOBA_BLOCK, MOBA_WIDTH))
        xf = _merge(x_mix, l, gain(mix_norm), a, bm, *w_merge)
        xf = _ffn(xf, l, *ffn2, fin, l == depth - 1)
    return xf.reshape(b, seq, d)
```

```python
import functools

import jax
import jax.numpy as jnp
import numpy as np
from jax import lax
from jax.experimental import pallas as pl
from jax.experimental.pallas import tpu as pltpu

F32 = jnp.float32
BF16 = jnp.bfloat16

D_MODEL = 1024
HEAD_DIM = 64
HALF = HEAD_DIM // 2
NSA_HEADS = 8
NSA_GROUPS = 2
NSA_REP = NSA_HEADS // NSA_GROUPS
CMP_LEN = 32
CMP_STRIDE = 16
CMP_HIDDEN = 128
SEL_BLOCK = 64
SEL_TOPK = 16
WINDOW = 512
MOBA_HEADS = 8
MOBA_BLOCK = 256
MOBA_TOPK = 3
D_FF = 2816
ROPE_THETA = 10000.0
EPS = 1e-6
NEG = -1e30
TINY = 1e-30
FORCE_BONUS = 1e4
Q_SCALE = HEAD_DIM ** -0.5 * float(np.log2(np.e))

NSA_WIDTH = NSA_HEADS * HEAD_DIM
KV_WIDTH = NSA_GROUPS * HEAD_DIM
MOBA_WIDTH = MOBA_HEADS * HEAD_DIM
N_GATE_COLS = 3 * NSA_HEADS

LANES = 128
BF16_ROWS = 16
TOKEN_TILE = 1024
FFN_TILE = 1024
FF_CHUNK = 256
NSA_Q = 256
SEL_KEYS = 512
SEL_UNIT = 2
VMEM_LIMIT = 56 * 1024 * 1024

COL_QA = 0
COL_KC = COL_QA + NSA_WIDTH
COL_VC = COL_KC + KV_WIDTH
COL_KV4 = COL_VC + KV_WIDTH
COL_QB = COL_KV4 + 4 * KV_WIDTH
COL_KB = COL_QB + MOBA_WIDTH
COL_VB = COL_KB + MOBA_WIDTH
COL_GS = COL_VB + MOBA_WIDTH


def _rms(x, g):
    return x * lax.rsqrt(jnp.mean(x * x, axis=-1, keepdims=True) + EPS) * g


def _dot(a, b):
    return jnp.dot(a, b, preferred_element_type=F32)


def _swap_halves(z):
    lane = lax.broadcasted_iota(jnp.int32, z.shape, 1)
    lo = (lane & (HEAD_DIM - 1)) < HALF
    return jnp.where(lo, pltpu.roll(z, LANES - HALF, 1), pltpu.roll(z, HALF, 1))


def _rope_slab(z, cos, sin):
    parts = []
    for c in range(z.shape[1] // LANES):
        zc = z[:, c * LANES:(c + 1) * LANES]
        parts.append(zc * cos + _swap_halves(zc) * sin)
    return parts[0] if len(parts) == 1 else jnp.concatenate(parts, axis=1)


def _slabs(s, n):
    return [s[:, i * n:(i + 1) * n] for i in range(s.shape[1] // n)]


def _masked_softmax_keys(s, mask):
    out = []
    for x in _slabs(s, mask.shape[1]):
        x = jnp.where(mask, x, NEG)
        m = jnp.max(x, axis=0, keepdims=True)
        p = jnp.where(mask, jnp.exp2(x - m), 0.0)
        out.append(p / jnp.maximum(jnp.sum(p, axis=0, keepdims=True), TINY))
    return jnp.concatenate(out, axis=1)


def _mask_keys(s, mask):
    return jnp.concatenate([jnp.where(mask, x, NEG) for x in _slabs(s, mask.shape[1])], axis=1)


def _online_update(carry, s, vt):
    m, l, acc = carry
    m_new = jnp.maximum(m, jnp.max(s, axis=0, keepdims=True))
    alpha = jnp.exp2(m - m_new)
    p = jnp.exp2(s - m_new).astype(BF16)
    ones_rows = jnp.where(lax.broadcasted_iota(jnp.int32, (BF16_ROWS, vt.shape[1]), 0) == 0, 1.0, 0.0).astype(BF16)
    pv = _dot(jnp.concatenate([vt, ones_rows], axis=0), p)
    return m_new, alpha * l + pv[HEAD_DIM:HEAD_DIM + 1], alpha * acc + pv[:HEAD_DIM]


def _rank_select(v, n_candidates, k):
    row = lax.broadcasted_iota(jnp.int32, v.shape, 0)
    cnt = jnp.zeros(v.shape, F32)
    for j in range(n_candidates):
        r = v[j:j + 1, :]
        cnt = cnt + jnp.where(r > v, 1.0, jnp.where(r == v, jnp.where(row > j, 1.0, 0.0), 0.0))
    return cnt < float(k)


def _ffn_kernel(x_ref, g_ref, wg_ref, wu_ref, wd_ref, fin_ref, o_ref, t_ref, *, final_norm):
    x = x_ref[...]
    h = (x * g_ref[...]).astype(BF16)
    inv = lax.rsqrt(jnp.mean(x * x, axis=-1, keepdims=True) + EPS)
    for f in range(D_FF // FF_CHUNK):
        c = slice(f * FF_CHUNK, (f + 1) * FF_CHUNK)
        a = _dot(h, wg_ref[:, c]) * inv
        b = _dot(h, wu_ref[:, c]) * inv
        t_ref[:, c] = (a * jax.nn.sigmoid(a) * b).astype(BF16)
    y = x + 0.5 * _dot(t_ref[...], wd_ref[...])
    if final_norm:
        y = _rms(y, fin_ref[...])
    o_ref[...] = y


def _resident(shape):
    nd = len(shape)
    return pl.BlockSpec(shape, lambda *_: (0,) * nd, pipeline_mode=pl.Buffered(1))


def _layer_resident(stacked, layer):
    return pl.BlockSpec((None,) + stacked.shape[1:], lambda *_: (layer, 0, 0), pipeline_mode=pl.Buffered(1))


def _ffn(x, layer, g, wg, wu, wd, fin, final_norm):
    t = x.shape[0]
    row = pl.BlockSpec((FFN_TILE, D_MODEL), lambda i: (i, 0))
    return pl.pallas_call(
        functools.partial(_ffn_kernel, final_norm=final_norm),
        grid=(t // FFN_TILE,),
        in_specs=[row, _layer_resident(g, layer), _layer_resident(wg, layer), _layer_resident(wu, layer),
                  _layer_resident(wd, layer), _resident((1, D_MODEL))],
        out_specs=row,
        out_shape=jax.ShapeDtypeStruct(x.shape, F32),
        scratch_shapes=[pltpu.VMEM((FFN_TILE, D_FF), BF16)],
        compiler_params=pltpu.CompilerParams(dimension_semantics=("parallel",),
                                             vmem_limit_bytes=VMEM_LIMIT),
        name="ffn",
    )(x, g, wg, wu, wd, fin)


def _proj_kernel(x_ref, g_ref, w_ref, cos_ref, sin_ref,
                 qat_ref, kc_ref, vc_ref, kk_ref, vvt_ref, qbt_ref, kb_ref, vbt_ref, gst_ref, kmean_ref):
    h = _rms(x_ref[...], g_ref[...]).astype(BF16)
    cos = cos_ref[...]
    sin = sin_ref[...]

    def mm(c0, n):
        return _dot(h, w_ref[:, c0:c0 + n])

    def feature_major(z, dtype=BF16):
        return jnp.transpose(z).astype(dtype)

    qat_ref[0] = feature_major(_rope_slab(mm(COL_QA, NSA_WIDTH), cos, sin) * Q_SCALE)
    kcvc = mm(COL_KC, 2 * KV_WIDTH)
    kc_ref[0] = kcvc[:, :KV_WIDTH]
    vc_ref[0] = kcvc[:, KV_WIDTH:]
    kv4 = mm(COL_KV4, 4 * KV_WIDTH)
    kk_ref[0] = jnp.concatenate(
        [_rope_slab(kv4[:, 0:KV_WIDTH], cos, sin), _rope_slab(kv4[:, 2 * KV_WIDTH:3 * KV_WIDTH], cos, sin)],
        axis=1).astype(BF16)
    vvt_ref[0] = feature_major(jnp.concatenate([kv4[:, KV_WIDTH:2 * KV_WIDTH], kv4[:, 3 * KV_WIDTH:]], axis=1))
    qbt_ref[0] = feature_major(_rope_slab(mm(COL_QB, MOBA_WIDTH), cos, sin) * Q_SCALE)
    kb = _rope_slab(mm(COL_KB, MOBA_WIDTH), cos, sin)
    kb_ref[0] = kb.astype(BF16)
    means = [jnp.mean(kb[i * MOBA_BLOCK:(i + 1) * MOBA_BLOCK], axis=0, keepdims=True)
             for i in range(TOKEN_TILE // MOBA_BLOCK)]
    kmean_ref[0] = jnp.concatenate(means, axis=0)
    vbt_ref[0] = feature_major(mm(COL_VB, MOBA_WIDTH))
    gst_ref[0] = feature_major(jax.nn.sigmoid(mm(COL_GS, LANES)), F32)


W_IN_ROWS = 256


def _split_w_in_kernel(wt_ref, attn_ref, gate_ref):
    off_ga = NSA_WIDTH
    off_rest = off_ga + N_GATE_COLS
    off_gates = off_rest + 6 * KV_WIDTH + 3 * MOBA_WIDTH

    def rows_major(a, b):
        return jnp.transpose(wt_ref[a:b, :]).astype(BF16)

    attn_ref[:, :off_ga] = rows_major(0, off_ga)
    attn_ref[:, off_ga:COL_GS] = rows_major(off_rest, off_gates)
    pad = jnp.zeros((LANES - N_GATE_COLS, wt_ref.shape[1]), F32)
    attn_ref[:, COL_GS:] = jnp.transpose(jnp.concatenate([wt_ref[off_ga:off_rest, :], pad], axis=0)).astype(BF16)
    gate_ref[...] = rows_major(off_gates, wt_ref.shape[0])


def _split_w_in(w_in):
    depth, d, n = w_in.shape
    steps = d // W_IN_ROWS
    return pl.pallas_call(
        _split_w_in_kernel,
        grid=(depth, steps),
        in_specs=[pl.BlockSpec((None, n, W_IN_ROWS), lambda l, i: (l, 0, i))],
        out_specs=(pl.BlockSpec((None, W_IN_ROWS, COL_GS + LANES), lambda l, i: (l, i, 0)),
                   pl.BlockSpec((None, W_IN_ROWS, 2 * D_MODEL), lambda l, i: (l, i, 0))),
        out_shape=(jax.ShapeDtypeStruct((depth, d, COL_GS + LANES), BF16),
                   jax.ShapeDtypeStruct((depth, d, 2 * D_MODEL), BF16)),
        compiler_params=pltpu.CompilerParams(dimension_semantics=("parallel", "parallel"),
                                             vmem_limit_bytes=VMEM_LIMIT),
        name="split_w_in",
    )(jnp.swapaxes(w_in, 1, 2))


def _proj(x, layer, g, w, cos_t, sin_t, seq):
    t = x.shape[0]
    tiles_per_seq = seq // TOKEN_TILE
    blocks_per_tile = TOKEN_TILE // MOBA_BLOCK

    def row(n):
        return pl.BlockSpec((TOKEN_TILE, n), lambda i: (i, 0))

    def token_major(n, dtype=BF16):
        return (jax.ShapeDtypeStruct((t // seq, seq, n), dtype),
                pl.BlockSpec((1, TOKEN_TILE, n), lambda i: (i // tiles_per_seq, i % tiles_per_seq, 0)))

    def feature_major(n, dtype=BF16):
        return (jax.ShapeDtypeStruct((t // seq, n, seq), dtype),
                pl.BlockSpec((1, n, TOKEN_TILE), lambda i: (i // tiles_per_seq, 0, i % tiles_per_seq)))

    tab = pl.BlockSpec((TOKEN_TILE, LANES), lambda i: (i % tiles_per_seq, 0))
    outs = (feature_major(NSA_WIDTH), token_major(KV_WIDTH, F32), token_major(KV_WIDTH, F32),
            token_major(2 * KV_WIDTH),
            feature_major(2 * KV_WIDTH), feature_major(MOBA_WIDTH), token_major(MOBA_WIDTH),
            feature_major(MOBA_WIDTH), feature_major(LANES, F32),
            (jax.ShapeDtypeStruct((t // TOKEN_TILE, blocks_per_tile, MOBA_WIDTH), F32),
             pl.BlockSpec((1, blocks_per_tile, MOBA_WIDTH), lambda i: (i, 0, 0))))
    out_shape, out_specs = zip(*outs)
    return pl.pallas_call(
        _proj_kernel,
        grid=(t // TOKEN_TILE,),
        in_specs=[row(D_MODEL), _layer_resident(g, layer), _layer_resident(w, layer), tab, tab],
        out_specs=out_specs,
        out_shape=out_shape,
        compiler_params=pltpu.CompilerParams(dimension_semantics=("parallel",),
                                             vmem_limit_bytes=VMEM_LIMIT),
        name="proj",
    )(x, g, w, cos_t, sin_t)


def _gelu_tanh(x):
    return 0.5 * x * (1.0 + jnp.tanh(np.sqrt(2.0 / np.pi).astype(np.float32) * (x + 0.044715 * (x * x * x))))


def _compress_kernel(kc_ref, vc_ref, kw1_ref, kw1f_ref, kpos_ref, kw2_ref,
                     vw1_ref, vw1f_ref, vpos_ref, vw2_ref, cos_ref, sin_ref, kcb_ref, vcb_ref):
    n_chunks = kc_ref.shape[1] // CMP_STRIDE

    def compress(x_ref, w1_ref, w1f_ref, pos_ref, w2_ref):
        u = jnp.zeros((n_chunks, 4 * CMP_HIDDEN), F32)
        for p in range(CMP_STRIDE):
            rows = x_ref[0, pl.ds(p, n_chunks, stride=CMP_STRIDE), :].astype(BF16)
            u = u + _dot(rows, w1_ref[p * KV_WIDTH:(p + 1) * KV_WIDTH, :])
        first = u[:, :2 * CMP_HIDDEN]
        second = pltpu.roll(u[:, 2 * CMP_HIDDEN:], n_chunks - 1, 0)
        pos = jnp.broadcast_to(pos_ref[...], (8, CMP_LEN * HEAD_DIM)).astype(BF16)
        bias = _dot(pos, w1f_ref[...])[0:1]
        pre = first + second + jnp.concatenate([bias, bias], axis=1)
        return _dot(_gelu_tanh(pre).astype(BF16), w2_ref[...])

    k = compress(kc_ref, kw1_ref, kw1f_ref, kpos_ref, kw2_ref)
    kcb_ref[0] = _rope_slab(k, cos_ref[...], sin_ref[...]).astype(BF16)
    vcb_ref[0] = compress(vc_ref, vw1_ref, vw1f_ref, vpos_ref, vw2_ref).astype(BF16)


def _compress_weights(pos, w1, w2):
    assert NSA_GROUPS == 2
    half = CMP_STRIDE * HEAD_DIM
    a0 = w1[:half].reshape(CMP_STRIDE, HEAD_DIM, CMP_HIDDEN)
    a1 = w1[half:].reshape(CMP_STRIDE, HEAD_DIM, CMP_HIDDEN)
    z = jnp.zeros_like(a0)
    big = jnp.concatenate([jnp.concatenate([a0, z, a1, z], axis=2), jnp.concatenate([z, a0, z, a1], axis=2)], axis=1)
    big = big.reshape(CMP_STRIDE * KV_WIDTH, 2 * NSA_GROUPS * CMP_HIDDEN).astype(BF16)
    z2 = jnp.zeros_like(w2)
    w2big = jnp.concatenate([jnp.concatenate([w2, z2], axis=1), jnp.concatenate([z2, w2], axis=1)], axis=0)
    return big, w1.astype(BF16), pos.reshape(1, CMP_LEN * HEAD_DIM), w2big.astype(BF16)


def _compress(kc, vc, kparams, vparams, cos_c, sin_c):
    b, seq, width = kc.shape
    n_chunks = seq // CMP_STRIDE
    args = (kc, vc) + _compress_weights(*kparams) + _compress_weights(*vparams) + (cos_c, sin_c)
    x_spec = pl.BlockSpec((1, seq, width), lambda i: (i, 0, 0))
    in_specs = [x_spec, x_spec] + [_resident(a.shape) for a in args[2:]]
    o_spec = pl.BlockSpec((1, n_chunks, KV_WIDTH), lambda i: (i, 0, 0))
    o_shape = jax.ShapeDtypeStruct((b, n_chunks, KV_WIDTH), BF16)
    return pl.pallas_call(
        _compress_kernel,
        grid=(b,),
        in_specs=in_specs,
        out_specs=(o_spec, o_spec),
        out_shape=(o_shape, o_shape),
        compiler_params=pltpu.CompilerParams(dimension_semantics=("parallel",),
                                             vmem_limit_bytes=VMEM_LIMIT),
        name="compress",
    )(*args)


def _nsa_kernel(qat_ref, gst_ref, kcb_ref, vcbt_ref, kk_ref, vst_ref, vwt_ref, ovt_ref, blk_ref, o_ref,
                s_ref, win_ref, m_ref, l_ref, acc_ref, *, seq):
    qi = pl.program_id(1)
    t0 = qi * NSA_Q
    cols = NSA_REP * NSA_Q
    n_cmp = seq // CMP_STRIDE - 1
    n_sel = seq // SEL_BLOCK
    gst = gst_ref[0]
    ovt = ovt_ref[...]

    def key_and_query_pos(n, k0):
        shape = (n, NSA_Q)
        return k0 + lax.broadcasted_iota(jnp.int32, shape, 0), t0 + lax.broadcasted_iota(jnp.int32, shape, 1)

    groups = range(NSA_GROUPS)
    d_rows = [slice(g * HEAD_DIM, (g + 1) * HEAD_DIM) for g in groups]
    g_cols = [slice(g * cols, (g + 1) * cols) for g in groups]

    zeros_d = jnp.zeros((HEAD_DIM, NSA_Q), BF16)
    heads = []
    for n in range(NSA_HEADS):
        x = qat_ref[0, n * HEAD_DIM:(n + 1) * HEAD_DIM, :]
        heads.append(jnp.concatenate([x, zeros_d] if n < NSA_REP else [zeros_d, x], axis=0))
    q = jnp.concatenate(heads, axis=1)

    n_win = WINDOW + NSA_Q
    w0 = pl.multiple_of(jnp.maximum(t0 - WINDOW, 0), NSA_Q)
    s = _dot(kcb_ref[0], q)
    k_win = kk_ref[0, pl.ds(w0, n_win), KV_WIDTH:2 * KV_WIDTH]
    s_win_0 = _dot(k_win, q[:, g_cols[0]])

    n_idx, tq = key_and_query_pos(s.shape[0], 0)
    cmask = (n_idx * CMP_STRIDE + (CMP_LEN - 1) <= tq) & (n_idx < n_cmp)
    p_cmp = _masked_softmax_keys(s, cmask).astype(BF16)
    o_cmp = [_dot(vcbt_ref[0, d_rows[g], :], p_cmp[:, g_cols[g]]) for g in groups]

    pi = _dot(ovt, p_cmp)
    s_win = jnp.concatenate([s_win_0, _dot(k_win, q[:, g_cols[1]])], axis=1)
    jj, tsel = key_and_query_pos(n_sel, 0)
    tblk = tsel >> 6
    bonus = jnp.where((jj == 0) | (jj == tblk) | (jj == tblk - 1), FORCE_BONUS, 0.0)
    pad = jnp.zeros((LANES - n_sel, NSA_Q), BF16)
    biases = []
    for g in groups:
        imp = pi[:, g * cols:g * cols + NSA_Q]
        for r in range(1, NSA_REP):
            imp = imp + pi[:, g * cols + r * NSA_Q:g * cols + (r + 1) * NSA_Q]
        imp = jnp.where(jj <= tblk, imp + bonus, NEG)
        bias = jnp.where(_rank_select(imp, n_sel, SEL_TOPK) & (jj <= tblk), 0.0, NEG)
        biases.append(jnp.concatenate([bias.astype(BF16), pad], axis=0))
    q_sel = jnp.concatenate(
        [jnp.concatenate([heads[n], biases[n // NSA_REP]], axis=0) for n in range(NSA_HEADS)], axis=1)

    def tile_start(c):
        return pl.multiple_of(c * SEL_KEYS, SEL_KEYS)

    units = tuple(range(n, n + SEL_UNIT) for n in range(0, NSA_HEADS, SEL_UNIT))

    def unit_cols(unit):
        return slice(unit[0] * NSA_Q, (unit[-1] + 1) * NSA_Q)

    def sel_scores(c, slot, unit=range(NSA_HEADS)):
        k0 = tile_start(c)
        keys = jnp.concatenate([kk_ref[0, pl.ds(k0, SEL_KEYS), 0:KV_WIDTH], blk_ref[pl.ds(k0, SEL_KEYS), :]],
                               axis=1)
        s_ref[slot, :, unit_cols(unit)] = _dot(keys, q_sel[:, unit_cols(unit)])

    def sel_update(c, s, unit=range(NSA_HEADS)):
        for lo_head in range(unit[0], unit[-1] + 1, min(len(unit), NSA_REP)):
            g = lo_head // NSA_REP
            n_heads = min(len(unit), NSA_REP)
            sl = slice((lo_head % NSA_REP) * NSA_Q, (lo_head % NSA_REP + n_heads) * NSA_Q)
            gl = slice(lo_head * NSA_Q, (lo_head + n_heads) * NSA_Q)
            m, l, acc = _online_update((m_ref[g, :, sl], l_ref[g, :, sl], acc_ref[g, :, sl]), s[:, gl],
                                       vst_ref[0, d_rows[g], pl.ds(tile_start(c), SEL_KEYS)])
            m_ref[g, :, sl], l_ref[g, :, sl], acc_ref[g, :, sl] = m, l, acc

    n_past = qi // (SEL_KEYS // NSA_Q)
    m_ref[...] = jnp.full(m_ref.shape, NEG, F32)
    l_ref[...] = jnp.zeros(l_ref.shape, F32)
    acc_ref[...] = jnp.zeros(acc_ref.shape, F32)
    sel_scores(0, 0)

    kpos, tq = key_and_query_pos(n_win, w0)
    s_win = _mask_keys(s_win, (kpos <= tq) & (tq - kpos < WINDOW))
    win_ref[...] = s_win
    m_win = jnp.max(s_win, axis=0, keepdims=True)

    for c in range(seq // SEL_KEYS - 1):
        @pl.when(c < n_past)
        def _():
            for unit in units:
                sel_scores(c + 1, (c + 1) % 2, unit)
                sel_update(c, s_ref[c % 2], unit)

    kpos, tq = key_and_query_pos(SEL_KEYS, tile_start(n_past))
    s_diag = _mask_keys(s_ref[n_past % 2], kpos <= tq)

    ones_rows = jnp.where(lax.broadcasted_iota(jnp.int32, (BF16_ROWS, n_win), 0) == 0, 1.0, 0.0).astype(BF16)
    o_win = []
    for unit in units:
        sel_update(n_past, s_diag, unit)
        p_win = jnp.exp2(win_ref[:, unit_cols(unit)] - m_win[:, unit_cols(unit)]).astype(BF16)
        pv = _dot(jnp.concatenate([vwt_ref[0, d_rows[unit[0] // NSA_REP], pl.ds(w0, n_win)], ones_rows], axis=0),
                  p_win)
        o_win.append(pv[:HEAD_DIM] / jnp.maximum(pv[HEAD_DIM:HEAD_DIM + 1], TINY))
    o_sel = [acc_ref[g] / jnp.maximum(l_ref[g], TINY) for g in groups]

    outs = []
    for n in range(NSA_HEADS):
        g, r = divmod(n, NSA_REP)
        sl = slice(r * NSA_Q, (r + 1) * NSA_Q)
        ul = slice((n % SEL_UNIT) * NSA_Q, (n % SEL_UNIT + 1) * NSA_Q)
        outs.append(gst[3 * n:3 * n + 1, :] * o_cmp[g][:, sl] + gst[3 * n + 1:3 * n + 2, :] * o_sel[g][:, sl]
                    + gst[3 * n + 2:3 * n + 3, :] * o_win[n // SEL_UNIT][:, ul])
    o_ref[0] = jnp.transpose(jnp.concatenate(outs, axis=0)).astype(BF16)


def _nsa(qat, gst, kcb, vcbt, kk, vvt, ovt, blk):
    b, _, seq = qat.shape
    n_chunks = kcb.shape[1]
    return pl.pallas_call(
        functools.partial(_nsa_kernel, seq=seq),
        grid=(b, seq // NSA_Q),
        in_specs=[pl.BlockSpec((1, NSA_WIDTH, NSA_Q), lambda i, j: (i, 0, j)),
                  pl.BlockSpec((1, LANES, NSA_Q), lambda i, j: (i, 0, j)),
                  pl.BlockSpec((1, n_chunks, KV_WIDTH), lambda i, j: (i, 0, 0)),
                  pl.BlockSpec((1, KV_WIDTH, n_chunks), lambda i, j: (i, 0, 0)),
                  pl.BlockSpec((1, seq, 2 * KV_WIDTH), lambda i, j: (i, 0, 0)),
                  pl.BlockSpec((1, KV_WIDTH, seq), lambda i, j: (i, 0, 0)),
                  pl.BlockSpec((1, KV_WIDTH, seq), lambda i, j: (i, 1, 0)),
                  pl.BlockSpec(ovt.shape, lambda i, j: (0, 0)),
                  pl.BlockSpec(blk.shape, lambda i, j: (0, 0))],
        out_specs=pl.BlockSpec((1, NSA_Q, NSA_WIDTH), lambda i, j: (i, j, 0)),
        out_shape=jax.ShapeDtypeStruct((b, seq, NSA_WIDTH), BF16),
        scratch_shapes=[pltpu.VMEM((2, SEL_KEYS, NSA_HEADS * NSA_Q), F32),
                        pltpu.VMEM((WINDOW + NSA_Q, NSA_HEADS * NSA_Q), F32),
                        pltpu.VMEM((NSA_GROUPS, 1, NSA_REP * NSA_Q), F32),
                        pltpu.VMEM((NSA_GROUPS, 1, NSA_REP * NSA_Q), F32),
                        pltpu.VMEM((NSA_GROUPS, HEAD_DIM, NSA_REP * NSA_Q), F32)],
        compiler_params=pltpu.CompilerParams(dimension_semantics=("parallel", "parallel"),
                                             vmem_limit_bytes=VMEM_LIMIT),
        name="nsa",
    )(qat, gst, kcb, vcbt, kk, vvt, vvt, ovt, blk)


def _moba_kernel(qt_ref, k_ref, vt_ref, kmean_ref, o_ref, s_ref, m_ref, l_ref, acc_ref):
    c = pl.program_id(1)
    nb = kmean_ref.shape[1]
    k_own = pl.multiple_of(c * MOBA_BLOCK, MOBA_BLOCK)
    jj = lax.broadcasted_iota(jnp.int32, (BF16_ROWS, MOBA_BLOCK), 0)
    past = jj < c
    pad = jnp.zeros((BF16_ROWS - nb, LANES), F32)
    zeros_d = jnp.zeros((HEAD_DIM, MOBA_BLOCK), BF16)
    ones_rows = jnp.where(lax.broadcasted_iota(jnp.int32, (BF16_ROWS, MOBA_BLOCK), 0) == 0, 1.0, 0.0).astype(BF16)

    def pair(h):
        return slice((h // 2) * LANES, (h // 2 + 1) * LANES)

    qs, gscs = [], []
    for h in range(MOBA_HEADS):
        x = qt_ref[0, h * HEAD_DIM:(h + 1) * HEAD_DIM, :]
        q = jnp.concatenate([x, zeros_d] if h % 2 == 0 else [zeros_d, x], axis=0)
        km = jnp.concatenate([kmean_ref[0, :, pair(h)], pad], axis=0).astype(BF16)
        qs.append(q)
        gscs.append(_dot(km, q))
    chosen = [jnp.where(_rank_select(jnp.where(past, g, NEG), nb, MOBA_TOPK) & past, 1.0, 0.0) for g in gscs]

    m_ref[...] = jnp.full(m_ref.shape, NEG, F32)
    l_ref[...] = jnp.zeros(l_ref.shape, F32)
    acc_ref[...] = jnp.zeros(acc_ref.shape, F32)

    def block_start(j):
        return pl.multiple_of(j * MOBA_BLOCK, MOBA_BLOCK)

    head_pairs = tuple(range(h, h + 2) for h in range(0, MOBA_HEADS, 2))

    def scores(j, slot, heads=range(MOBA_HEADS)):
        for h in heads:
            s_ref[slot, h] = _dot(k_ref[0, pl.ds(block_start(j), MOBA_BLOCK), pair(h)], qs[h])

    def past_update(j, slot, heads=range(MOBA_HEADS)):
        for h in heads:
            rows = slice(h * HEAD_DIM, (h + 1) * HEAD_DIM)
            s = s_ref[slot, h]
            pick = jnp.sum(jnp.where(jj == j, chosen[h], 0.0), axis=0, keepdims=True) > 0.5
            m = m_ref[h]
            m_new = jnp.maximum(m, jnp.where(pick, jnp.max(s, axis=0, keepdims=True), NEG))
            alpha = jnp.exp2(m - m_new)
            p = jnp.exp2(s - jnp.where(pick, m_new, jnp.inf))
            m_ref[h] = m_new
            pv = _dot(jnp.concatenate([vt_ref[0, rows, pl.ds(block_start(j), MOBA_BLOCK)], ones_rows], axis=0),
                      p.astype(BF16))
            l_ref[h] = alpha * l_ref[h] + pv[HEAD_DIM:HEAD_DIM + 1]
            acc_ref[rows, :] = alpha * acc_ref[rows, :] + pv[:HEAD_DIM]

    scores(0, 0)

    def two_blocks(i, _):
        j = 2 * i
        for heads in head_pairs:
            scores(j + 1, 1, heads)
            past_update(j, 0, heads)
        for heads in head_pairs:
            scores(j + 2, 0, heads)
            past_update(j + 1, 1, heads)
        return 0

    lax.fori_loop(0, c // 2, two_blocks, 0)

    @pl.when(c % 2 == 1)
    def _():
        for heads in head_pairs:
            scores(c, 1, heads)
            past_update(c - 1, 0, heads)

    kq = (MOBA_BLOCK, MOBA_BLOCK)
    causal = lax.broadcasted_iota(jnp.int32, kq, 0) <= lax.broadcasted_iota(jnp.int32, kq, 1)
    for h in range(MOBA_HEADS):
        rows = slice(h * HEAD_DIM, (h + 1) * HEAD_DIM)
        carry = (m_ref[h], l_ref[h], acc_ref[rows, :])
        _, l, acc = _online_update(carry, jnp.where(causal, s_ref[c % 2, h], NEG),
                                   vt_ref[0, rows, pl.ds(k_own, MOBA_BLOCK)])
        acc_ref[rows, :] = acc / jnp.maximum(l, TINY)
    o_ref[0] = jnp.transpose(acc_ref[...]).astype(BF16)


def _moba(qbt, kb, vbt, kmean):
    b, _, seq = qbt.shape
    nb = seq // MOBA_BLOCK
    return pl.pallas_call(
        _moba_kernel,
        grid=(b, nb),
        in_specs=[pl.BlockSpec((1, MOBA_WIDTH, MOBA_BLOCK), lambda i, j: (i, 0, j)),
                  pl.BlockSpec((1, seq, MOBA_WIDTH), lambda i, j: (i, 0, 0)),
                  pl.BlockSpec((1, MOBA_WIDTH, seq), lambda i, j: (i, 0, 0)),
                  pl.BlockSpec((1, nb, MOBA_WIDTH), lambda i, j: (i, 0, 0))],
        out_specs=pl.BlockSpec((1, MOBA_BLOCK, MOBA_WIDTH), lambda i, j: (i, j, 0)),
        out_shape=jax.ShapeDtypeStruct((b, seq, MOBA_WIDTH), BF16),
        scratch_shapes=[pltpu.VMEM((2, MOBA_HEADS, MOBA_BLOCK, MOBA_BLOCK), F32),
                        pltpu.VMEM((MOBA_HEADS, 1, MOBA_BLOCK), F32), pltpu.VMEM((MOBA_HEADS, 1, MOBA_BLOCK), F32),
                        pltpu.VMEM((MOBA_WIDTH, MOBA_BLOCK), F32)],
        compiler_params=pltpu.CompilerParams(dimension_semantics=("parallel", "parallel"),
                                             vmem_limit_bytes=VMEM_LIMIT),
        name="moba",
    )(qbt, kb, vbt, kmean)


def _merge_kernel(x_ref, g_ref, a_ref, b_ref, wgate_ref, pa_ref, pb_ref, wo_ref, o_ref):
    x = x_ref[...]
    h = _rms(x, g_ref[...]).astype(BF16)
    ya = _dot(a_ref[0], pa_ref[...])
    yb = _dot(b_ref[0], pb_ref[...])
    merged = (jax.nn.sigmoid(_dot(h, wgate_ref[:, :D_MODEL])) * ya
              + jax.nn.sigmoid(_dot(h, wgate_ref[:, D_MODEL:])) * yb)
    o_ref[...] = x + _dot(merged.astype(BF16), wo_ref[...])


def _merge(x, layer, g, a, bm, wgate, pa, pb, wo):
    t = x.shape[0]
    tiles_per_seq = a.shape[1] // TOKEN_TILE

    def row(n):
        return pl.BlockSpec((TOKEN_TILE, n), lambda i: (i, 0))

    def seq_row(n):
        return pl.BlockSpec((1, TOKEN_TILE, n), lambda i: (i // tiles_per_seq, i % tiles_per_seq, 0))

    return pl.pallas_call(
        _merge_kernel,
        grid=(t // TOKEN_TILE,),
        in_specs=[row(D_MODEL), _layer_resident(g, layer), seq_row(NSA_WIDTH), seq_row(MOBA_WIDTH)]
                 + [_layer_resident(w, layer) for w in (wgate, pa, pb, wo)],
        out_specs=row(D_MODEL),
        out_shape=jax.ShapeDtypeStruct(x.shape, F32),
        compiler_params=pltpu.CompilerParams(dimension_semantics=("parallel",),
                                             vmem_limit_bytes=VMEM_LIMIT),
        name="merge",
    )(x, g, a, bm, wgate, pa, pb, wo)


def _rope_tables(pos):
    inv = ROPE_THETA ** (-jnp.arange(0, HEAD_DIM, 2, dtype=F32) / HEAD_DIM)
    ang = pos.astype(F32)[:, None] * inv[None, :]
    cos, sin = jnp.cos(ang), jnp.sin(ang)
    reps = LANES // HEAD_DIM
    return (jnp.tile(jnp.concatenate([cos, cos], axis=1), (1, reps)),
            jnp.tile(jnp.concatenate([-sin, sin], axis=1), (1, reps)))


def _overlap_matrix_t(seq):
    n_chunks = seq // CMP_STRIDE
    ci = np.arange(n_chunks)[None, :] * CMP_STRIDE
    sj = np.arange(seq // SEL_BLOCK)[:, None] * SEL_BLOCK
    ov = (ci < sj + SEL_BLOCK) & (ci + CMP_LEN > sj) & (np.arange(n_chunks)[None, :] < n_chunks - 1)
    return jnp.asarray(ov, dtype=BF16)


def _block_onehot(seq):
    return jnp.asarray(np.arange(seq)[:, None] // SEL_BLOCK == np.arange(LANES)[None, :], dtype=BF16)


def kernel(x, ffn1_norm, ffn1_wg, ffn1_wu, ffn1_wd, mix_norm, w_in, cmpk_pos, cmpk_w1, cmpk_w2, cmpv_pos, cmpv_w1, cmpv_w2, w_branch_nsa, w_branch_moba, w_out, ffn2_norm, ffn2_wg, ffn2_wu, ffn2_wd, final_norm):
    b, seq, d = x.shape
    depth = w_in.shape[0]
    assert d == D_MODEL and seq % TOKEN_TILE == 0 and seq // CMP_STRIDE == LANES
    t = b * seq
    n_chunks = seq // CMP_STRIDE
    cos_t, sin_t = _rope_tables(jnp.arange(seq))
    cos_c, sin_c = _rope_tables(jnp.arange(n_chunks) * CMP_STRIDE + (CMP_LEN - 1))
    ovt = _overlap_matrix_t(seq)
    blk = _block_onehot(seq)

    def gain(gs):
        return gs.reshape(depth, 1, d)

    ffn1 = (gain(ffn1_norm), ffn1_wg.astype(BF16), ffn1_wu.astype(BF16), ffn1_wd.astype(BF16))
    ffn2 = (gain(ffn2_norm), ffn2_wg.astype(BF16), ffn2_wu.astype(BF16), ffn2_wd.astype(BF16))
    w_attn, w_gate = _split_w_in(w_in)
    w_merge = (w_gate, w_branch_nsa.astype(BF16), w_branch_moba.astype(BF16), w_out.astype(BF16))
    fin = final_norm.reshape(1, d)

    xf = x.reshape(t, d)
    for l in range(depth):
        xf = _ffn(xf, l, *ffn1, fin, False)
        x_mix = xf
        qat, kc, vc, kk, vvt, qbt, kb, vbt, gst, kmean = _proj(xf, l, gain(mix_norm), w_attn, cos_t, sin_t, seq)
        kcb, vcb = _compress(kc, vc,
                             (cmpk_pos[l], cmpk_w1[l], cmpk_w2[l]),
                             (cmpv_pos[l], cmpv_w1[l], cmpv_w2[l]), cos_c, sin_c)
        a = _nsa(qat, gst, kcb, vcb.transpose(0, 2, 1), kk, vvt, ovt, blk)
        bm = _moba(qbt, kb, vbt, kmean.reshape(b, seq // MOBA_BLOCK, MOBA_WIDTH))
        xf = _merge(x_mix, l, gain(mix_norm), a, bm, *w_merge)
        xf = _ffn(xf, l, *ffn2, fin, l == depth - 1)
    return xf.reshape(b, seq, d)
```

```python
import functools

import jax
import jax.numpy as jnp
import numpy as np
from jax import lax
from jax.experimental import pallas as pl
from jax.experimental.pallas import tpu as pltpu

F32 = jnp.float32
BF16 = jnp.bfloat16

D_MODEL = 1024
HEAD_DIM = 64
HALF = HEAD_DIM // 2
NSA_HEADS = 8
NSA_GROUPS = 2
NSA_REP = NSA_HEADS // NSA_GROUPS
CMP_LEN = 32
CMP_STRIDE = 16
CMP_HIDDEN = 128
SEL_BLOCK = 64
SEL_TOPK = 16
WINDOW = 512
MOBA_HEADS = 8
MOBA_BLOCK = 256
MOBA_TOPK = 3
D_FF = 2816
ROPE_THETA = 10000.0
EPS = 1e-6
NEG = -1e30
TINY = 1e-30
FORCE_BONUS = 1e4
Q_SCALE = HEAD_DIM ** -0.5 * float(np.log2(np.e))

NSA_WIDTH = NSA_HEADS * HEAD_DIM
KV_WIDTH = NSA_GROUPS * HEAD_DIM
MOBA_WIDTH = MOBA_HEADS * HEAD_DIM
N_GATE_COLS = 3 * NSA_HEADS

LANES = 128
BF16_ROWS = 16
TOKEN_TILE = 1024
FFN_TILE = 1024
FF_CHUNK = 256
NSA_Q = 256
SEL_KEYS = 512
SEL_UNIT = 2
VMEM_LIMIT = 56 * 1024 * 1024

COL_QA = 0
COL_KC = COL_QA + NSA_WIDTH
COL_VC = COL_KC + KV_WIDTH
COL_KV4 = COL_VC + KV_WIDTH
COL_QB = COL_KV4 + 4 * KV_WIDTH
COL_KB = COL_QB + MOBA_WIDTH
COL_VB = COL_KB + MOBA_WIDTH
COL_GS = COL_VB + MOBA_WIDTH


def _rms(x, g):
    return x * lax.rsqrt(jnp.mean(x * x, axis=-1, keepdims=True) + EPS) * g


def _dot(a, b):
    return jnp.dot(a, b, preferred_element_type=F32)


def _swap_halves(z):
    lane = lax.broadcasted_iota(jnp.int32, z.shape, 1)
    lo = (lane & (HEAD_DIM - 1)) < HALF
    return jnp.where(lo, pltpu.roll(z, LANES - HALF, 1), pltpu.roll(z, HALF, 1))


def _rope_slab(z, cos, sin):
    parts = []
    for c in range(z.shape[1] // LANES):
        zc = z[:, c * LANES:(c + 1) * LANES]
        parts.append(zc * cos + _swap_halves(zc) * sin)
    return parts[0] if len(parts) == 1 else jnp.concatenate(parts, axis=1)


def _slabs(s, n):
    return [s[:, i * n:(i + 1) * n] for i in range(s.shape[1] // n)]


def _masked_softmax_keys(s, mask):
    out = []
    for x in _slabs(s, mask.shape[1]):
        x = jnp.where(mask, x, NEG)
        m = jnp.max(x, axis=0, keepdims=True)
        p = jnp.where(mask, jnp.exp2(x - m), 0.0)
        out.append(p / jnp.maximum(jnp.sum(p, axis=0, keepdims=True), TINY))
    return jnp.concatenate(out, axis=1)


def _mask_keys(s, mask):
    return jnp.concatenate([jnp.where(mask, x, NEG) for x in _slabs(s, mask.shape[1])], axis=1)


def _online_update(carry, s, vt):
    m, l, acc = carry
    m_new = jnp.maximum(m, jnp.max(s, axis=0, keepdims=True))
    alpha = jnp.exp2(m - m_new)
    p = jnp.exp2(s - m_new).astype(BF16)
    ones_rows = jnp.where(lax.broadcasted_iota(jnp.int32, (BF16_ROWS, vt.shape[1]), 0) == 0, 1.0, 0.0).astype(BF16)
    pv = _dot(jnp.concatenate([vt, ones_rows], axis=0), p)
    return m_new, alpha * l + pv[HEAD_DIM:HEAD_DIM + 1], alpha * acc + pv[:HEAD_DIM]


def _rank_select(v, n_candidates, k):
    row = lax.broadcasted_iota(jnp.int32, v.shape, 0)
    cnt = jnp.zeros(v.shape, F32)
    for j in range(n_candidates):
        r = v[j:j + 1, :]
        cnt = cnt + jnp.where(r > v, 1.0, jnp.where(r == v, jnp.where(row > j, 1.0, 0.0), 0.0))
    return cnt < float(k)


def _ffn_kernel(x_ref, g_ref, wg_ref, wu_ref, wd_ref, fin_ref, o_ref, t_ref, *, final_norm):
    x = x_ref[...]
    h = (x * g_ref[...]).astype(BF16)
    inv = lax.rsqrt(jnp.mean(x * x, axis=-1, keepdims=True) + EPS)
    for f in range(D_FF // FF_CHUNK):
        c = slice(f * FF_CHUNK, (f + 1) * FF_CHUNK)
        a = _dot(h, wg_ref[:, c]) * inv
        b = _dot(h, wu_ref[:, c]) * inv
        t_ref[:, c] = (a * jax.nn.sigmoid(a) * b).astype(BF16)
    y = x + 0.5 * _dot(t_ref[...], wd_ref[...])
    if final_norm:
        y = _rms(y, fin_ref[...])
    o_ref[...] = y


def _resident(shape):
    nd = len(shape)
    return pl.BlockSpec(shape, lambda *_: (0,) * nd, pipeline_mode=pl.Buffered(1))


def _layer_resident(stacked, layer):
    return pl.BlockSpec((None,) + stacked.shape[1:], lambda *_: (layer, 0, 0), pipeline_mode=pl.Buffered(1))


def _ffn(x, layer, g, wg, wu, wd, fin, final_norm):
    t = x.shape[0]
    row = pl.BlockSpec((FFN_TILE, D_MODEL), lambda i: (i, 0))
    return pl.pallas_call(
        functools.partial(_ffn_kernel, final_norm=final_norm),
        grid=(t // FFN_TILE,),
        in_specs=[row, _layer_resident(g, layer), _layer_resident(wg, layer), _layer_resident(wu, layer),
                  _layer_resident(wd, layer), _resident((1, D_MODEL))],
        out_specs=row,
        out_shape=jax.ShapeDtypeStruct(x.shape, F32),
        scratch_shapes=[pltpu.VMEM((FFN_TILE, D_FF), BF16)],
        compiler_params=pltpu.CompilerParams(dimension_semantics=("parallel",),
                                             vmem_limit_bytes=VMEM_LIMIT),
        name="ffn",
    )(x, g, wg, wu, wd, fin)


def _proj_kernel(x_ref, g_ref, w_ref, cos_ref, sin_ref,
                 qat_ref, kc_ref, vc_ref, kk_ref, vvt_ref, qbt_ref, kb_ref, vbt_ref, gst_ref, kmean_ref):
    h = _rms(x_ref[...], g_ref[...]).astype(BF16)
    cos = cos_ref[...]
    sin = sin_ref[...]

    def mm(c0, n):
        return _dot(h, w_ref[:, c0:c0 + n])

    def feature_major(z, dtype=BF16):
        return jnp.transpose(z).astype(dtype)

    qat_ref[0] = feature_major(_rope_slab(mm(COL_QA, NSA_WIDTH), cos, sin) * Q_SCALE)
    kcvc = mm(COL_KC, 2 * KV_WIDTH)
    kc_ref[0] = kcvc[:, :KV_WIDTH]
    vc_ref[0] = kcvc[:, KV_WIDTH:]
    kv4 = mm(COL_KV4, 4 * KV_WIDTH)
    kk_ref[0] = jnp.concatenate(
        [_rope_slab(kv4[:, 0:KV_WIDTH], cos, sin), _rope_slab(kv4[:, 2 * KV_WIDTH:3 * KV_WIDTH], cos, sin)],
        axis=1).astype(BF16)
    vvt_ref[0] = feature_major(jnp.concatenate([kv4[:, KV_WIDTH:2 * KV_WIDTH], kv4[:, 3 * KV_WIDTH:]], axis=1))
    qbt_ref[0] = feature_major(_rope_slab(mm(COL_QB, MOBA_WIDTH), cos, sin) * Q_SCALE)
    kb = _rope_slab(mm(COL_KB, MOBA_WIDTH), cos, sin)
    kb_ref[0] = kb.astype(BF16)
    means = [jnp.mean(kb[i * MOBA_BLOCK:(i + 1) * MOBA_BLOCK], axis=0, keepdims=True)
             for i in range(TOKEN_TILE // MOBA_BLOCK)]
    kmean_ref[0] = jnp.concatenate(means, axis=0)
    vbt_ref[0] = feature_major(mm(COL_VB, MOBA_WIDTH))
    gst_ref[0] = feature_major(jax.nn.sigmoid(mm(COL_GS, LANES)), F32)


W_IN_ROWS = 256


def _split_w_in_kernel(wt_ref, attn_ref, gate_ref):
    off_ga = NSA_WIDTH
    off_rest = off_ga + N_GATE_COLS
    off_gates = off_rest + 6 * KV_WIDTH + 3 * MOBA_WIDTH

    def rows_major(a, b):
        return jnp.transpose(wt_ref[a:b, :]).astype(BF16)

    attn_ref[:, :off_ga] = rows_major(0, off_ga)
    attn_ref[:, off_ga:COL_GS] = rows_major(off_rest, off_gates)
    pad = jnp.zeros((LANES - N_GATE_COLS, wt_ref.shape[1]), F32)
    attn_ref[:, COL_GS:] = jnp.transpose(jnp.concatenate([wt_ref[off_ga:off_rest, :], pad], axis=0)).astype(BF16)
    gate_ref[...] = rows_major(off_gates, wt_ref.shape[0])


def _split_w_in(w_in):
    depth, d, n = w_in.shape
    steps = d // W_IN_ROWS
    return pl.pallas_call(
        _split_w_in_kernel,
        grid=(depth, steps),
        in_specs=[pl.BlockSpec((None, n, W_IN_ROWS), lambda l, i: (l, 0, i))],
        out_specs=(pl.BlockSpec((None, W_IN_ROWS, COL_GS + LANES), lambda l, i: (l, i, 0)),
                   pl.BlockSpec((None, W_IN_ROWS, 2 * D_MODEL), lambda l, i: (l, i, 0))),
        out_shape=(jax.ShapeDtypeStruct((depth, d, COL_GS + LANES), BF16),
                   jax.ShapeDtypeStruct((depth, d, 2 * D_MODEL), BF16)),
        compiler_params=pltpu.CompilerParams(dimension_semantics=("parallel", "parallel"),
                                             vmem_limit_bytes=VMEM_LIMIT),
        name="split_w_in",
    )(jnp.swapaxes(w_in, 1, 2))


def _proj(x, layer, g, w, cos_t, sin_t, seq):
    t = x.shape[0]
    tiles_per_seq = seq // TOKEN_TILE
    blocks_per_tile = TOKEN_TILE // MOBA_BLOCK

    def row(n):
        return pl.BlockSpec((TOKEN_TILE, n), lambda i: (i, 0))

    def token_major(n, dtype=BF16):
        return (jax.ShapeDtypeStruct((t // seq, seq, n), dtype),
                pl.BlockSpec((1, TOKEN_TILE, n), lambda i: (i // tiles_per_seq, i % tiles_per_seq, 0)))

    def feature_major(n, dtype=BF16):
        return (jax.ShapeDtypeStruct((t // seq, n, seq), dtype),
                pl.BlockSpec((1, n, TOKEN_TILE), lambda i: (i // tiles_per_seq, 0, i % tiles_per_seq)))

    tab = pl.BlockSpec((TOKEN_TILE, LANES), lambda i: (i % tiles_per_seq, 0))
    outs = (feature_major(NSA_WIDTH), token_major(KV_WIDTH, F32), token_major(KV_WIDTH, F32),
            token_major(2 * KV_WIDTH),
            feature_major(2 * KV_WIDTH), feature_major(MOBA_WIDTH), token_major(MOBA_WIDTH),
            feature_major(MOBA_WIDTH), feature_major(LANES, F32),
            (jax.ShapeDtypeStruct((t // TOKEN_TILE, blocks_per_tile, MOBA_WIDTH), F32),
             pl.BlockSpec((1, blocks_per_tile, MOBA_WIDTH), lambda i: (i, 0, 0))))
    out_shape, out_specs = zip(*outs)
    return pl.pallas_call(
        _proj_kernel,
        grid=(t // TOKEN_TILE,),
        in_specs=[row(D_MODEL), _layer_resident(g, layer), _layer_resident(w, layer), tab, tab],
        out_specs=out_specs,
        out_shape=out_shape,
        compiler_params=pltpu.CompilerParams(dimension_semantics=("parallel",),
                                             vmem_limit_bytes=VMEM_LIMIT),
        name="proj",
    )(x, g, w, cos_t, sin_t)


def _gelu_tanh(x):
    return 0.5 * x * (1.0 + jnp.tanh(np.sqrt(2.0 / np.pi).astype(np.float32) * (x + 0.044715 * (x * x * x))))


def _compress_kernel(kc_ref, vc_ref, kw1_ref, kw1f_ref, kpos_ref, kw2_ref,
                     vw1_ref, vw1f_ref, vpos_ref, vw2_ref, cos_ref, sin_ref, kcb_ref, vcb_ref):
    n_chunks = kc_ref.shape[1] // CMP_STRIDE

    def compress(x_ref, w1_ref, w1f_ref, pos_ref, w2_ref):
        u = jnp.zeros((n_chunks, 4 * CMP_HIDDEN), F32)
        for p in range(CMP_STRIDE):
            rows = x_ref[0, pl.ds(p, n_chunks, stride=CMP_STRIDE), :].astype(BF16)
            u = u + _dot(rows, w1_ref[p * KV_WIDTH:(p + 1) * KV_WIDTH, :])
        first = u[:, :2 * CMP_HIDDEN]
        second = pltpu.roll(u[:, 2 * CMP_HIDDEN:], n_chunks - 1, 0)
        pos = jnp.broadcast_to(pos_ref[...], (8, CMP_LEN * HEAD_DIM)).astype(BF16)
        bias = _dot(pos, w1f_ref[...])[0:1]
        pre = first + second + jnp.concatenate([bias, bias], axis=1)
        return _dot(_gelu_tanh(pre).astype(BF16), w2_ref[...])

    k = compress(kc_ref, kw1_ref, kw1f_ref, kpos_ref, kw2_ref)
    kcb_ref[0] = _rope_slab(k, cos_ref[...], sin_ref[...]).astype(BF16)
    vcb_ref[0] = compress(vc_ref, vw1_ref, vw1f_ref, vpos_ref, vw2_ref).astype(BF16)


def _compress_weights(pos, w1, w2):
    assert NSA_GROUPS == 2
    half = CMP_STRIDE * HEAD_DIM
    a0 = w1[:half].reshape(CMP_STRIDE, HEAD_DIM, CMP_HIDDEN)
    a1 = w1[half:].reshape(CMP_STRIDE, HEAD_DIM, CMP_HIDDEN)
    z = jnp.zeros_like(a0)
    big = jnp.concatenate([jnp.concatenate([a0, z, a1, z], axis=2), jnp.concatenate([z, a0, z, a1], axis=2)], axis=1)
    big = big.reshape(CMP_STRIDE * KV_WIDTH, 2 * NSA_GROUPS * CMP_HIDDEN).astype(BF16)
    z2 = jnp.zeros_like(w2)
    w2big = jnp.concatenate([jnp.concatenate([w2, z2], axis=1), jnp.concatenate([z2, w2], axis=1)], axis=0)
    return big, w1.astype(BF16), pos.reshape(1, CMP_LEN * HEAD_DIM), w2big.astype(BF16)


def _compress(kc, vc, kparams, vparams, cos_c, sin_c):
    b, seq, width = kc.shape
    n_chunks = seq // CMP_STRIDE
    args = (kc, vc) + _compress_weights(*kparams) + _compress_weights(*vparams) + (cos_c, sin_c)
    x_spec = pl.BlockSpec((1, seq, width), lambda i: (i, 0, 0))
    in_specs = [x_spec, x_spec] + [_resident(a.shape) for a in args[2:]]
    o_spec = pl.BlockSpec((1, n_chunks, KV_WIDTH), lambda i: (i, 0, 0))
    o_shape = jax.ShapeDtypeStruct((b, n_chunks, KV_WIDTH), BF16)
    return pl.pallas_call(
        _compress_kernel,
        grid=(b,),
        in_specs=in_specs,
        out_specs=(o_spec, o_spec),
        out_shape=(o_shape, o_shape),
        compiler_params=pltpu.CompilerParams(dimension_semantics=("parallel",),
                                             vmem_limit_bytes=VMEM_LIMIT),
        name="compress",
    )(*args)


def _nsa_kernel(qat_ref, gst_ref, kcb_ref, vcbt_ref, kk_ref, vst_ref, vwt_ref, ovt_ref, blk_ref, o_ref,
                s_ref, win_ref, m_ref, l_ref, acc_ref, *, seq):
    qi = pl.program_id(1)
    t0 = qi * NSA_Q
    cols = NSA_REP * NSA_Q
    n_cmp = seq // CMP_STRIDE - 1
    n_sel = seq // SEL_BLOCK
    gst = gst_ref[0]
    ovt = ovt_ref[...]

    def key_and_query_pos(n, k0):
        shape = (n, NSA_Q)
        return k0 + lax.broadcasted_iota(jnp.int32, shape, 0), t0 + lax.broadcasted_iota(jnp.int32, shape, 1)

    groups = range(NSA_GROUPS)
    d_rows = [slice(g * HEAD_DIM, (g + 1) * HEAD_DIM) for g in groups]
    g_cols = [slice(g * cols, (g + 1) * cols) for g in groups]

    zeros_d = jnp.zeros((HEAD_DIM, NSA_Q), BF16)
    heads = []
    for n in range(NSA_HEADS):
        x = qat_ref[0, n * HEAD_DIM:(n + 1) * HEAD_DIM, :]
        heads.append(jnp.concatenate([x, zeros_d] if n < NSA_REP else [zeros_d, x], axis=0))
    q = jnp.concatenate(heads, axis=1)

    n_win = WINDOW + NSA_Q
    w0 = pl.multiple_of(jnp.maximum(t0 - WINDOW, 0), NSA_Q)
    s = _dot(kcb_ref[0], q)
    k_win = kk_ref[0, pl.ds(w0, n_win), KV_WIDTH:2 * KV_WIDTH]
    win_cols = [slice(n * NSA_Q, (n + SEL_UNIT) * NSA_Q) for n in range(0, NSA_HEADS, SEL_UNIT)]
    s_win_parts = [_dot(k_win, q[:, win_cols[0]])]

    n_idx, tq = key_and_query_pos(s.shape[0], 0)
    cmask = (n_idx * CMP_STRIDE + (CMP_LEN - 1) <= tq) & (n_idx < n_cmp)
    p_cmp = _masked_softmax_keys(s, cmask).astype(BF16)
    o_cmp = [_dot(vcbt_ref[0, d_rows[g], :], p_cmp[:, g_cols[g]]) for g in groups]

    pi = _dot(ovt, p_cmp)
    s_win_parts += [_dot(k_win, q[:, c]) for c in win_cols[1:]]
    s_win = jnp.concatenate(s_win_parts, axis=1)
    jj, tsel = key_and_query_pos(n_sel, 0)
    tblk = tsel >> 6
    bonus = jnp.where((jj == 0) | (jj == tblk) | (jj == tblk - 1), FORCE_BONUS, 0.0)
    pad = jnp.zeros((LANES - n_sel, NSA_Q), BF16)
    biases = []
    for g in groups:
        imp = pi[:, g * cols:g * cols + NSA_Q]
        for r in range(1, NSA_REP):
            imp = imp + pi[:, g * cols + r * NSA_Q:g * cols + (r + 1) * NSA_Q]
        imp = jnp.where(jj <= tblk, imp + bonus, NEG)
        bias = jnp.where(_rank_select(imp, n_sel, SEL_TOPK) & (jj <= tblk), 0.0, NEG)
        biases.append(jnp.concatenate([bias.astype(BF16), pad], axis=0))
    q_sel = jnp.concatenate(
        [jnp.concatenate([heads[n], biases[n // NSA_REP]], axis=0) for n in range(NSA_HEADS)], axis=1)

    def tile_start(c):
        return pl.multiple_of(c * SEL_KEYS, SEL_KEYS)

    units = tuple(range(n, n + SEL_UNIT) for n in range(0, NSA_HEADS, SEL_UNIT))

    def unit_cols(unit):
        return slice(unit[0] * NSA_Q, (unit[-1] + 1) * NSA_Q)

    def sel_scores(c, slot, unit=range(NSA_HEADS)):
        k0 = tile_start(c)
        keys = jnp.concatenate([kk_ref[0, pl.ds(k0, SEL_KEYS), 0:KV_WIDTH], blk_ref[pl.ds(k0, SEL_KEYS), :]],
                               axis=1)
        s_ref[slot, :, unit_cols(unit)] = _dot(keys, q_sel[:, unit_cols(unit)])

    def sel_update(c, s, unit=range(NSA_HEADS)):
        for lo_head in range(unit[0], unit[-1] + 1, min(len(unit), NSA_REP)):
            g = lo_head // NSA_REP
            n_heads = min(len(unit), NSA_REP)
            sl = slice((lo_head % NSA_REP) * NSA_Q, (lo_head % NSA_REP + n_heads) * NSA_Q)
            gl = slice(lo_head * NSA_Q, (lo_head + n_heads) * NSA_Q)
            m, l, acc = _online_update((m_ref[g, :, sl], l_ref[g, :, sl], acc_ref[g, :, sl]), s[:, gl],
                                       vst_ref[0, d_rows[g], pl.ds(tile_start(c), SEL_KEYS)])
            m_ref[g, :, sl], l_ref[g, :, sl], acc_ref[g, :, sl] = m, l, acc

    n_past = qi // (SEL_KEYS // NSA_Q)
    m_ref[...] = jnp.full(m_ref.shape, NEG, F32)
    l_ref[...] = jnp.zeros(l_ref.shape, F32)
    acc_ref[...] = jnp.zeros(acc_ref.shape, F32)
    sel_scores(0, 0)

    kpos, tq = key_and_query_pos(n_win, w0)
    s_win = _mask_keys(s_win, (kpos <= tq) & (tq - kpos < WINDOW))
    win_ref[...] = s_win
    m_win = jnp.max(s_win, axis=0, keepdims=True)

    for c in range(seq // SEL_KEYS - 1):
        @pl.when(c < n_past)
        def _():
            for unit in units:
                sel_scores(c + 1, (c + 1) % 2, unit)
                sel_update(c, s_ref[c % 2], unit)

    kpos, tq = key_and_query_pos(SEL_KEYS, tile_start(n_past))
    s_diag = _mask_keys(s_ref[n_past % 2], kpos <= tq)

    ones_rows = jnp.where(lax.broadcasted_iota(jnp.int32, (BF16_ROWS, n_win), 0) == 0, 1.0, 0.0).astype(BF16)
    o_win = []
    for unit in units:
        sel_update(n_past, s_diag, unit)
        p_win = jnp.exp2(win_ref[:, unit_cols(unit)] - m_win[:, unit_cols(unit)]).astype(BF16)
        pv = _dot(jnp.concatenate([vwt_ref[0, d_rows[unit[0] // NSA_REP], pl.ds(w0, n_win)], ones_rows], axis=0),
                  p_win)
        o_win.append(pv[:HEAD_DIM] / jnp.maximum(pv[HEAD_DIM:HEAD_DIM + 1], TINY))
    o_sel = [acc_ref[g] / jnp.maximum(l_ref[g], TINY) for g in groups]

    outs = []
    for n in range(NSA_HEADS):
        g, r = divmod(n, NSA_REP)
        sl = slice(r * NSA_Q, (r + 1) * NSA_Q)
        ul = slice((n % SEL_UNIT) * NSA_Q, (n % SEL_UNIT + 1) * NSA_Q)
        outs.append(gst[3 * n:3 * n + 1, :] * o_cmp[g][:, sl] + gst[3 * n + 1:3 * n + 2, :] * o_sel[g][:, sl]
                    + gst[3 * n + 2:3 * n + 3, :] * o_win[n // SEL_UNIT][:, ul])
    o_ref[0] = jnp.transpose(jnp.concatenate(outs, axis=0)).astype(BF16)


def _nsa(qat, gst, kcb, vcbt, kk, vvt, ovt, blk):
    b, _, seq = qat.shape
    n_chunks = kcb.shape[1]
    return pl.pallas_call(
        functools.partial(_nsa_kernel, seq=seq),
        grid=(b, seq // NSA_Q),
        in_specs=[pl.BlockSpec((1, NSA_WIDTH, NSA_Q), lambda i, j: (i, 0, j)),
                  pl.BlockSpec((1, LANES, NSA_Q), lambda i, j: (i, 0, j)),
                  pl.BlockSpec((1, n_chunks, KV_WIDTH), lambda i, j: (i, 0, 0)),
                  pl.BlockSpec((1, KV_WIDTH, n_chunks), lambda i, j: (i, 0, 0)),
                  pl.BlockSpec((1, seq, 2 * KV_WIDTH), lambda i, j: (i, 0, 0)),
                  pl.BlockSpec((1, KV_WIDTH, seq), lambda i, j: (i, 0, 0)),
                  pl.BlockSpec((1, KV_WIDTH, seq), lambda i, j: (i, 1, 0)),
                  pl.BlockSpec(ovt.shape, lambda i, j: (0, 0)),
                  pl.BlockSpec(blk.shape, lambda i, j: (0, 0))],
        out_specs=pl.BlockSpec((1, NSA_Q, NSA_WIDTH), lambda i, j: (i, j, 0)),
        out_shape=jax.ShapeDtypeStruct((b, seq, NSA_WIDTH), BF16),
        scratch_shapes=[pltpu.VMEM((2, SEL_KEYS, NSA_HEADS * NSA_Q), F32),
                        pltpu.VMEM((WINDOW + NSA_Q, NSA_HEADS * NSA_Q), F32),
                        pltpu.VMEM((NSA_GROUPS, 1, NSA_REP * NSA_Q), F32),
                        pltpu.VMEM((NSA_GROUPS, 1, NSA_REP * NSA_Q), F32),
                        pltpu.VMEM((NSA_GROUPS, HEAD_DIM, NSA_REP * NSA_Q), F32)],
        compiler_params=pltpu.CompilerParams(dimension_semantics=("parallel", "parallel"),
                                             vmem_limit_bytes=VMEM_LIMIT),
        name="nsa",
    )(qat, gst, kcb, vcbt, kk, vvt, vvt, ovt, blk)


def _moba_kernel(qt_ref, k_ref, vt_ref, kmean_ref, o_ref, s_ref, m_ref, l_ref, acc_ref):
    c = pl.program_id(1)
    nb = kmean_ref.shape[1]
    k_own = pl.multiple_of(c * MOBA_BLOCK, MOBA_BLOCK)
    jj = lax.broadcasted_iota(jnp.int32, (BF16_ROWS, MOBA_BLOCK), 0)
    past = jj < c
    pad = jnp.zeros((BF16_ROWS - nb, LANES), F32)
    zeros_d = jnp.zeros((HEAD_DIM, MOBA_BLOCK), BF16)
    ones_rows = jnp.where(lax.broadcasted_iota(jnp.int32, (BF16_ROWS, MOBA_BLOCK), 0) == 0, 1.0, 0.0).astype(BF16)

    def pair(h):
        return slice((h // 2) * LANES, (h // 2 + 1) * LANES)

    qs, gscs = [], []
    for h in range(MOBA_HEADS):
        x = qt_ref[0, h * HEAD_DIM:(h + 1) * HEAD_DIM, :]
        q = jnp.concatenate([x, zeros_d] if h % 2 == 0 else [zeros_d, x], axis=0)
        km = jnp.concatenate([kmean_ref[0, :, pair(h)], pad], axis=0).astype(BF16)
        qs.append(q)
        gscs.append(_dot(km, q))
    chosen = [jnp.where(_rank_select(jnp.where(past, g, NEG), nb, MOBA_TOPK) & past, 1.0, 0.0) for g in gscs]

    m_ref[...] = jnp.full(m_ref.shape, NEG, F32)
    l_ref[...] = jnp.zeros(l_ref.shape, F32)
    acc_ref[...] = jnp.zeros(acc_ref.shape, F32)

    def block_start(j):
        return pl.multiple_of(j * MOBA_BLOCK, MOBA_BLOCK)

    head_pairs = tuple(range(h, h + 2) for h in range(0, MOBA_HEADS, 2))

    def scores(j, slot, heads=range(MOBA_HEADS)):
        for h in heads:
            s_ref[slot, h] = _dot(k_ref[0, pl.ds(block_start(j), MOBA_BLOCK), pair(h)], qs[h])

    def past_update(j, slot, heads=range(MOBA_HEADS)):
        for h in heads:
            rows = slice(h * HEAD_DIM, (h + 1) * HEAD_DIM)
            s = s_ref[slot, h]
            pick = jnp.sum(jnp.where(jj == j, chosen[h], 0.0), axis=0, keepdims=True) > 0.5
            m = m_ref[h]
            m_new = jnp.maximum(m, jnp.where(pick, jnp.max(s, axis=0, keepdims=True), NEG))
            alpha = jnp.exp2(m - m_new)
            p = jnp.exp2(s - jnp.where(pick, m_new, jnp.inf))
            m_ref[h] = m_new
            pv = _dot(jnp.concatenate([vt_ref[0, rows, pl.ds(block_start(j), MOBA_BLOCK)], ones_rows], axis=0),
                      p.astype(BF16))
            l_ref[h] = alpha * l_ref[h] + pv[HEAD_DIM:HEAD_DIM + 1]
            acc_ref[rows, :] = alpha * acc_ref[rows, :] + pv[:HEAD_DIM]

    scores(0, 0)

    def two_blocks(i, _):
        j = 2 * i
        for heads in head_pairs:
            scores(j + 1, 1, heads)
            past_update(j, 0, heads)
        for heads in head_pairs:
            scores(j + 2, 0, heads)
            past_update(j + 1, 1, heads)
        return 0

    lax.fori_loop(0, c // 2, two_blocks, 0)

    @pl.when(c % 2 == 1)
    def _():
        for heads in head_pairs:
            scores(c, 1, heads)
            past_update(c - 1, 0, heads)

    kq = (MOBA_BLOCK, MOBA_BLOCK)
    causal = lax.broadcasted_iota(jnp.int32, kq, 0) <= lax.broadcasted_iota(jnp.int32, kq, 1)
    for h in range(MOBA_HEADS):
        rows = slice(h * HEAD_DIM, (h + 1) * HEAD_DIM)
        carry = (m_ref[h], l_ref[h], acc_ref[rows, :])
        _, l, acc = _online_update(carry, jnp.where(causal, s_ref[c % 2, h], NEG),
                                   vt_ref[0, rows, pl.ds(k_own, MOBA_BLOCK)])
        acc_ref[rows, :] = acc / jnp.maximum(l, TINY)
    o_ref[0] = jnp.transpose(acc_ref[...]).astype(BF16)


def _moba(qbt, kb, vbt, kmean):
    b, _, seq = qbt.shape
    nb = seq // MOBA_BLOCK
    return pl.pallas_call(
        _moba_kernel,
        grid=(b, nb),
        in_specs=[pl.BlockSpec((1, MOBA_WIDTH, MOBA_BLOCK), lambda i, j: (i, 0, j)),
                  pl.BlockSpec((1, seq, MOBA_WIDTH), lambda i, j: (i, 0, 0)),
                  pl.BlockSpec((1, MOBA_WIDTH, seq), lambda i, j: (i, 0, 0)),
                  pl.BlockSpec((1, nb, MOBA_WIDTH), lambda i, j: (i, 0, 0))],
        out_specs=pl.BlockSpec((1, MOBA_BLOCK, MOBA_WIDTH), lambda i, j: (i, j, 0)),
        out_shape=jax.ShapeDtypeStruct((b, seq, MOBA_WIDTH), BF16),
        scratch_shapes=[pltpu.VMEM((2, MOBA_HEADS, MOBA_BLOCK, MOBA_BLOCK), F32),
                        pltpu.VMEM((MOBA_HEADS, 1, MOBA_BLOCK), F32), pltpu.VMEM((MOBA_HEADS, 1, MOBA_BLOCK), F32),
                        pltpu.VMEM((MOBA_WIDTH, MOBA_BLOCK), F32)],
        compiler_params=pltpu.CompilerParams(dimension_semantics=("parallel", "parallel"),
                                             vmem_limit_bytes=VMEM_LIMIT),
        name="moba",
    )(qbt, kb, vbt, kmean)


def _merge_kernel(x_ref, g_ref, a_ref, b_ref, wgate_ref, pa_ref, pb_ref, wo_ref, o_ref):
    x = x_ref[...]
    h = _rms(x, g_ref[...]).astype(BF16)
    ya = _dot(a_ref[0], pa_ref[...])
    yb = _dot(b_ref[0], pb_ref[...])
    merged = (jax.nn.sigmoid(_dot(h, wgate_ref[:, :D_MODEL])) * ya
              + jax.nn.sigmoid(_dot(h, wgate_ref[:, D_MODEL:])) * yb)
    o_ref[...] = x + _dot(merged.astype(BF16), wo_ref[...])


def _merge(x, layer, g, a, bm, wgate, pa, pb, wo):
    t = x.shape[0]
    tiles_per_seq = a.shape[1] // TOKEN_TILE

    def row(n):
        return pl.BlockSpec((TOKEN_TILE, n), lambda i: (i, 0))

    def seq_row(n):
        return pl.BlockSpec((1, TOKEN_TILE, n), lambda i: (i // tiles_per_seq, i % tiles_per_seq, 0))

    return pl.pallas_call(
        _merge_kernel,
        grid=(t // TOKEN_TILE,),
        in_specs=[row(D_MODEL), _layer_resident(g, layer), seq_row(NSA_WIDTH), seq_row(MOBA_WIDTH)]
                 + [_layer_resident(w, layer) for w in (wgate, pa, pb, wo)],
        out_specs=row(D_MODEL),
        out_shape=jax.ShapeDtypeStruct(x.shape, F32),
        compiler_params=pltpu.CompilerParams(dimension_semantics=("parallel",),
                                             vmem_limit_bytes=VMEM_LIMIT),
        name="merge",
    )(x, g, a, bm, wgate, pa, pb, wo)


def _rope_tables(pos):
    inv = ROPE_THETA ** (-jnp.arange(0, HEAD_DIM, 2, dtype=F32) / HEAD_DIM)
    ang = pos.astype(F32)[:, None] * inv[None, :]
    cos, sin = jnp.cos(ang), jnp.sin(ang)
    reps = LANES // HEAD_DIM
    return (jnp.tile(jnp.concatenate([cos, cos], axis=1), (1, reps)),
            jnp.tile(jnp.concatenate([-sin, sin], axis=1), (1, reps)))


def _overlap_matrix_t(seq):
    n_chunks = seq // CMP_STRIDE
    ci = np.arange(n_chunks)[None, :] * CMP_STRIDE
    sj = np.arange(seq // SEL_BLOCK)[:, None] * SEL_BLOCK
    ov = (ci < sj + SEL_BLOCK) & (ci + CMP_LEN > sj) & (np.arange(n_chunks)[None, :] < n_chunks - 1)
    return jnp.asarray(ov, dtype=BF16)


def _block_onehot(seq):
    return jnp.asarray(np.arange(seq)[:, None] // SEL_BLOCK == np.arange(LANES)[None, :], dtype=BF16)


def kernel(x, ffn1_norm, ffn1_wg, ffn1_wu, ffn1_wd, mix_norm, w_in, cmpk_pos, cmpk_w1, cmpk_w2, cmpv_pos, cmpv_w1, cmpv_w2, w_branch_nsa, w_branch_moba, w_out, ffn2_norm, ffn2_wg, ffn2_wu, ffn2_wd, final_norm):
    b, seq, d = x.shape
    depth = w_in.shape[0]
    assert d == D_MODEL and seq % TOKEN_TILE == 0 and seq // CMP_STRIDE == LANES
    t = b * seq
    n_chunks = seq // CMP_STRIDE
    cos_t, sin_t = _rope_tables(jnp.arange(seq))
    cos_c, sin_c = _rope_tables(jnp.arange(n_chunks) * CMP_STRIDE + (CMP_LEN - 1))
    ovt = _overlap_matrix_t(seq)
    blk = _block_onehot(seq)

    def gain(gs):
        return gs.reshape(depth, 1, d)

    ffn1 = (gain(ffn1_norm), ffn1_wg.astype(BF16), ffn1_wu.astype(BF16), ffn1_wd.astype(BF16))
    ffn2 = (gain(ffn2_norm), ffn2_wg.astype(BF16), ffn2_wu.astype(BF16), ffn2_wd.astype(BF16))
    w_attn, w_gate = _split_w_in(w_in)
    w_merge = (w_gate, w_branch_nsa.astype(BF16), w_branch_moba.astype(BF16), w_out.astype(BF16))
    fin = final_norm.reshape(1, d)

    xf = x.reshape(t, d)
    for l in range(depth):
        xf = _ffn(xf, l, *ffn1, fin, False)
        x_mix = xf
        qat, kc, vc, kk, vvt, qbt, kb, vbt, gst, kmean = _proj(xf, l, gain(mix_norm), w_attn, cos_t, sin_t, seq)
        kcb, vcb = _compress(kc, vc,
                             (cmpk_pos[l], cmpk_w1[l], cmpk_w2[l]),
                             (cmpv_pos[l], cmpv_w1[l], cmpv_w2[l]), cos_c, sin_c)
        a = _nsa(qat, gst, kcb, vcb.transpose(0, 2, 1), kk, vvt, ovt, blk)
        bm = _moba(qbt, kb, vbt, kmean.reshape(b, seq // MOBA_BLOCK, MOBA_WIDTH))
        xf = _merge(x_mix, l, gain(mix_norm), a, bm, *w_merge)
        xf = _ffn(xf, l, *ffn2, fin, l == depth - 1)
    return xf.reshape(b, seq, d)
```

```python
import functools

import jax
import jax.numpy as jnp
import numpy as np
from jax import lax
from jax.experimental import pallas as pl
from jax.experimental.pallas import tpu as pltpu

F32 = jnp.float32
BF16 = jnp.bfloat16

D_MODEL = 1024
HEAD_DIM = 64
HALF = HEAD_DIM // 2
NSA_HEADS = 8
NSA_GROUPS = 2
NSA_REP = NSA_HEADS // NSA_GROUPS
CMP_LEN = 32
CMP_STRIDE = 16
CMP_HIDDEN = 128
SEL_BLOCK = 64
SEL_TOPK = 16
WINDOW = 512
MOBA_HEADS = 8
MOBA_BLOCK = 256
MOBA_TOPK = 3
D_FF = 2816
ROPE_THETA = 10000.0
EPS = 1e-6
NEG = -1e30
TINY = 1e-30
FORCE_BONUS = 1e4
Q_SCALE = HEAD_DIM ** -0.5 * float(np.log2(np.e))

NSA_WIDTH = NSA_HEADS * HEAD_DIM
KV_WIDTH = NSA_GROUPS * HEAD_DIM
MOBA_WIDTH = MOBA_HEADS * HEAD_DIM
N_GATE_COLS = 3 * NSA_HEADS

LANES = 128
BF16_ROWS = 16
TOKEN_TILE = 1024
FFN_TILE = 1024
FF_CHUNK = 256
NSA_Q = 256
SEL_KEYS = 512
MERGE_COLS = 512
SEL_UNIT = 2
VMEM_LIMIT = 56 * 1024 * 1024

COL_QA = 0
COL_KC = COL_QA + NSA_WIDTH
COL_VC = COL_KC + KV_WIDTH
COL_KV4 = COL_VC + KV_WIDTH
COL_QB = COL_KV4 + 4 * KV_WIDTH
COL_KB = COL_QB + MOBA_WIDTH
COL_VB = COL_KB + MOBA_WIDTH
COL_GS = COL_VB + MOBA_WIDTH


def _rms(x, g):
    return x * lax.rsqrt(jnp.mean(x * x, axis=-1, keepdims=True) + EPS) * g


def _dot(a, b):
    return jnp.dot(a, b, preferred_element_type=F32)


def _swap_halves(z):
    lane = lax.broadcasted_iota(jnp.int32, z.shape, 1)
    lo = (lane & (HEAD_DIM - 1)) < HALF
    return jnp.where(lo, pltpu.roll(z, LANES - HALF, 1), pltpu.roll(z, HALF, 1))


def _rope_slab(z, cos, sin):
    parts = []
    for c in range(z.shape[1] // LANES):
        zc = z[:, c * LANES:(c + 1) * LANES]
        parts.append(zc * cos + _swap_halves(zc) * sin)
    return parts[0] if len(parts) == 1 else jnp.concatenate(parts, axis=1)


def _slabs(s, n):
    return [s[:, i * n:(i + 1) * n] for i in range(s.shape[1] // n)]


def _masked_softmax_keys(s, mask):
    out = []
    for x in _slabs(s, mask.shape[1]):
        x = jnp.where(mask, x, NEG)
        m = jnp.max(x, axis=0, keepdims=True)
        p = jnp.where(mask, jnp.exp2(x - m), 0.0)
        out.append(p / jnp.maximum(jnp.sum(p, axis=0, keepdims=True), TINY))
    return jnp.concatenate(out, axis=1)


def _mask_keys(s, mask):
    return jnp.concatenate([jnp.where(mask, x, NEG) for x in _slabs(s, mask.shape[1])], axis=1)


def _online_update(carry, s, vt):
    m, l, acc = carry
    m_new = jnp.maximum(m, jnp.max(s, axis=0, keepdims=True))
    alpha = jnp.exp2(m - m_new)
    p = jnp.exp2(s - m_new).astype(BF16)
    ones_rows = jnp.where(lax.broadcasted_iota(jnp.int32, (BF16_ROWS, vt.shape[1]), 0) == 0, 1.0, 0.0).astype(BF16)
    pv = _dot(jnp.concatenate([vt, ones_rows], axis=0), p)
    return m_new, alpha * l + pv[HEAD_DIM:HEAD_DIM + 1], alpha * acc + pv[:HEAD_DIM]


def _rank_select(v, n_candidates, k):
    row = lax.broadcasted_iota(jnp.int32, v.shape, 0)
    cnt = jnp.zeros(v.shape, F32)
    for j in range(n_candidates):
        r = v[j:j + 1, :]
        cnt = cnt + jnp.where(r > v, 1.0, jnp.where(r == v, jnp.where(row > j, 1.0, 0.0), 0.0))
    return cnt < float(k)


def _ffn_kernel(x_ref, g_ref, wg_ref, wu_ref, wd_ref, fin_ref, o_ref, t_ref, *, final_norm):
    x = x_ref[...]
    h = (x * g_ref[...]).astype(BF16)
    inv = lax.rsqrt(jnp.mean(x * x, axis=-1, keepdims=True) + EPS)
    for f in range(D_FF // FF_CHUNK):
        c = slice(f * FF_CHUNK, (f + 1) * FF_CHUNK)
        a = _dot(h, wg_ref[:, c]) * inv
        b = _dot(h, wu_ref[:, c]) * inv
        t_ref[:, c] = (a * jax.nn.sigmoid(a) * b).astype(BF16)
    y = x + 0.5 * _dot(t_ref[...], wd_ref[...])
    if final_norm:
        y = _rms(y, fin_ref[...])
    o_ref[...] = y


def _resident(shape):
    nd = len(shape)
    return pl.BlockSpec(shape, lambda *_: (0,) * nd, pipeline_mode=pl.Buffered(1))


def _layer_resident(stacked, layer):
    return pl.BlockSpec((None,) + stacked.shape[1:], lambda *_: (layer, 0, 0), pipeline_mode=pl.Buffered(1))


def _ffn(x, layer, g, wg, wu, wd, fin, final_norm):
    t = x.shape[0]
    row = pl.BlockSpec((FFN_TILE, D_MODEL), lambda i: (i, 0))
    return pl.pallas_call(
        functools.partial(_ffn_kernel, final_norm=final_norm),
        grid=(t // FFN_TILE,),
        in_specs=[row, _layer_resident(g, layer), _layer_resident(wg, layer), _layer_resident(wu, layer),
                  _layer_resident(wd, layer), _resident((1, D_MODEL))],
        out_specs=row,
        out_shape=jax.ShapeDtypeStruct(x.shape, F32),
        scratch_shapes=[pltpu.VMEM((FFN_TILE, D_FF), BF16)],
        compiler_params=pltpu.CompilerParams(dimension_semantics=("parallel",),
                                             vmem_limit_bytes=VMEM_LIMIT),
        name="ffn",
    )(x, g, wg, wu, wd, fin)


def _proj_kernel(x_ref, g_ref, w_ref, cos_ref, sin_ref,
                 qat_ref, kc_ref, vc_ref, kk_ref, vvt_ref, qbt_ref, kb_ref, vbt_ref, gst_ref, kmean_ref):
    h = _rms(x_ref[...], g_ref[...]).astype(BF16)
    cos = cos_ref[...]
    sin = sin_ref[...]

    def mm(c0, n):
        return _dot(h, w_ref[:, c0:c0 + n])

    def feature_major(z, dtype=BF16):
        return jnp.transpose(z).astype(dtype)

    qat_ref[0] = feature_major(_rope_slab(mm(COL_QA, NSA_WIDTH), cos, sin) * Q_SCALE)
    kcvc = mm(COL_KC, 2 * KV_WIDTH)
    kc_ref[0] = kcvc[:, :KV_WIDTH]
    vc_ref[0] = kcvc[:, KV_WIDTH:]
    kv4 = mm(COL_KV4, 4 * KV_WIDTH)
    kk_ref[0] = jnp.concatenate(
        [_rope_slab(kv4[:, 0:KV_WIDTH], cos, sin), _rope_slab(kv4[:, 2 * KV_WIDTH:3 * KV_WIDTH], cos, sin)],
        axis=1).astype(BF16)
    vvt_ref[0] = feature_major(jnp.concatenate([kv4[:, KV_WIDTH:2 * KV_WIDTH], kv4[:, 3 * KV_WIDTH:]], axis=1))
    qbt_ref[0] = feature_major(_rope_slab(mm(COL_QB, MOBA_WIDTH), cos, sin) * Q_SCALE)
    kb = _rope_slab(mm(COL_KB, MOBA_WIDTH), cos, sin)
    kb_ref[0] = kb.astype(BF16)
    means = [jnp.mean(kb[i * MOBA_BLOCK:(i + 1) * MOBA_BLOCK], axis=0, keepdims=True)
             for i in range(TOKEN_TILE // MOBA_BLOCK)]
    kmean_ref[0] = jnp.concatenate(means, axis=0)
    vbt_ref[0] = feature_major(mm(COL_VB, MOBA_WIDTH))
    gst_ref[0] = feature_major(jax.nn.sigmoid(mm(COL_GS, LANES)), F32)


W_IN_ROWS = 256


def _split_w_in_kernel(wt_ref, attn_ref, gate_ref):
    off_ga = NSA_WIDTH
    off_rest = off_ga + N_GATE_COLS
    off_gates = off_rest + 6 * KV_WIDTH + 3 * MOBA_WIDTH

    def rows_major(a, b):
        return jnp.transpose(wt_ref[a:b, :]).astype(BF16)

    attn_ref[:, :off_ga] = rows_major(0, off_ga)
    attn_ref[:, off_ga:COL_GS] = rows_major(off_rest, off_gates)
    pad = jnp.zeros((LANES - N_GATE_COLS, wt_ref.shape[1]), F32)
    attn_ref[:, COL_GS:] = jnp.transpose(jnp.concatenate([wt_ref[off_ga:off_rest, :], pad], axis=0)).astype(BF16)
    gate_ref[...] = rows_major(off_gates, wt_ref.shape[0])


def _split_w_in(w_in):
    depth, d, n = w_in.shape
    steps = d // W_IN_ROWS
    return pl.pallas_call(
        _split_w_in_kernel,
        grid=(depth, steps),
        in_specs=[pl.BlockSpec((None, n, W_IN_ROWS), lambda l, i: (l, 0, i))],
        out_specs=(pl.BlockSpec((None, W_IN_ROWS, COL_GS + LANES), lambda l, i: (l, i, 0)),
                   pl.BlockSpec((None, W_IN_ROWS, 2 * D_MODEL), lambda l, i: (l, i, 0))),
        out_shape=(jax.ShapeDtypeStruct((depth, d, COL_GS + LANES), BF16),
                   jax.ShapeDtypeStruct((depth, d, 2 * D_MODEL), BF16)),
        compiler_params=pltpu.CompilerParams(dimension_semantics=("parallel", "parallel"),
                                             vmem_limit_bytes=VMEM_LIMIT),
        name="split_w_in",
    )(jnp.swapaxes(w_in, 1, 2))


def _proj(x, layer, g, w, cos_t, sin_t, seq):
    t = x.shape[0]
    tiles_per_seq = seq // TOKEN_TILE
    blocks_per_tile = TOKEN_TILE // MOBA_BLOCK

    def row(n):
        return pl.BlockSpec((TOKEN_TILE, n), lambda i: (i, 0))

    def token_major(n, dtype=BF16):
        return (jax.ShapeDtypeStruct((t // seq, seq, n), dtype),
                pl.BlockSpec((1, TOKEN_TILE, n), lambda i: (i // tiles_per_seq, i % tiles_per_seq, 0)))

    def feature_major(n, dtype=BF16):
        return (jax.ShapeDtypeStruct((t // seq, n, seq), dtype),
                pl.BlockSpec((1, n, TOKEN_TILE), lambda i: (i // tiles_per_seq, 0, i % tiles_per_seq)))

    tab = pl.BlockSpec((TOKEN_TILE, LANES), lambda i: (i % tiles_per_seq, 0))
    outs = (feature_major(NSA_WIDTH), token_major(KV_WIDTH, F32), token_major(KV_WIDTH, F32),
            token_major(2 * KV_WIDTH),
            feature_major(2 * KV_WIDTH), feature_major(MOBA_WIDTH), token_major(MOBA_WIDTH),
            feature_major(MOBA_WIDTH), feature_major(LANES, F32),
            (jax.ShapeDtypeStruct((t // TOKEN_TILE, blocks_per_tile, MOBA_WIDTH), F32),
             pl.BlockSpec((1, blocks_per_tile, MOBA_WIDTH), lambda i: (i, 0, 0))))
    out_shape, out_specs = zip(*outs)
    return pl.pallas_call(
        _proj_kernel,
        grid=(t // TOKEN_TILE,),
        in_specs=[row(D_MODEL), _layer_resident(g, layer), _layer_resident(w, layer), tab, tab],
        out_specs=out_specs,
        out_shape=out_shape,
        compiler_params=pltpu.CompilerParams(dimension_semantics=("parallel",),
                                             vmem_limit_bytes=VMEM_LIMIT),
        name="proj",
    )(x, g, w, cos_t, sin_t)


def _gelu_tanh(x):
    return 0.5 * x * (1.0 + jnp.tanh(np.sqrt(2.0 / np.pi).astype(np.float32) * (x + 0.044715 * (x * x * x))))


def _compress_kernel(kc_ref, vc_ref, kw1_ref, kw1f_ref, kpos_ref, kw2_ref,
                     vw1_ref, vw1f_ref, vpos_ref, vw2_ref, cos_ref, sin_ref, kcb_ref, vcb_ref):
    n_chunks = kc_ref.shape[1] // CMP_STRIDE

    def compress(x_ref, w1_ref, w1f_ref, pos_ref, w2_ref):
        u = jnp.zeros((n_chunks, 4 * CMP_HIDDEN), F32)
        for p in range(CMP_STRIDE):
            rows = x_ref[0, pl.ds(p, n_chunks, stride=CMP_STRIDE), :].astype(BF16)
            u = u + _dot(rows, w1_ref[p * KV_WIDTH:(p + 1) * KV_WIDTH, :])
        first = u[:, :2 * CMP_HIDDEN]
        second = pltpu.roll(u[:, 2 * CMP_HIDDEN:], n_chunks - 1, 0)
        pos = jnp.broadcast_to(pos_ref[...], (8, CMP_LEN * HEAD_DIM)).astype(BF16)
        bias = _dot(pos, w1f_ref[...])[0:1]
        pre = first + second + jnp.concatenate([bias, bias], axis=1)
        return _dot(_gelu_tanh(pre).astype(BF16), w2_ref[...])

    k = compress(kc_ref, kw1_ref, kw1f_ref, kpos_ref, kw2_ref)
    kcb_ref[0] = _rope_slab(k, cos_ref[...], sin_ref[...]).astype(BF16)
    vcb_ref[0] = compress(vc_ref, vw1_ref, vw1f_ref, vpos_ref, vw2_ref).astype(BF16)


def _compress_weights(pos, w1, w2):
    assert NSA_GROUPS == 2
    half = CMP_STRIDE * HEAD_DIM
    a0 = w1[:half].reshape(CMP_STRIDE, HEAD_DIM, CMP_HIDDEN)
    a1 = w1[half:].reshape(CMP_STRIDE, HEAD_DIM, CMP_HIDDEN)
    z = jnp.zeros_like(a0)
    big = jnp.concatenate([jnp.concatenate([a0, z, a1, z], axis=2), jnp.concatenate([z, a0, z, a1], axis=2)], axis=1)
    big = big.reshape(CMP_STRIDE * KV_WIDTH, 2 * NSA_GROUPS * CMP_HIDDEN).astype(BF16)
    z2 = jnp.zeros_like(w2)
    w2big = jnp.concatenate([jnp.concatenate([w2, z2], axis=1), jnp.concatenate([z2, w2], axis=1)], axis=0)
    return big, w1.astype(BF16), pos.reshape(1, CMP_LEN * HEAD_DIM), w2big.astype(BF16)


def _compress(kc, vc, kparams, vparams, cos_c, sin_c):
    b, seq, width = kc.shape
    n_chunks = seq // CMP_STRIDE
    args = (kc, vc) + _compress_weights(*kparams) + _compress_weights(*vparams) + (cos_c, sin_c)
    x_spec = pl.BlockSpec((1, seq, width), lambda i: (i, 0, 0))
    in_specs = [x_spec, x_spec] + [_resident(a.shape) for a in args[2:]]
    o_spec = pl.BlockSpec((1, n_chunks, KV_WIDTH), lambda i: (i, 0, 0))
    o_shape = jax.ShapeDtypeStruct((b, n_chunks, KV_WIDTH), BF16)
    return pl.pallas_call(
        _compress_kernel,
        grid=(b,),
        in_specs=in_specs,
        out_specs=(o_spec, o_spec),
        out_shape=(o_shape, o_shape),
        compiler_params=pltpu.CompilerParams(dimension_semantics=("parallel",),
                                             vmem_limit_bytes=VMEM_LIMIT),
        name="compress",
    )(*args)


def _nsa_kernel(qat_ref, gst_ref, kcb_ref, vcbt_ref, kk_ref, vst_ref, vwt_ref, ovt_ref, blk_ref, o_ref,
                s_ref, win_ref, m_ref, l_ref, acc_ref, *, seq):
    qi = pl.program_id(1)
    t0 = qi * NSA_Q
    cols = NSA_REP * NSA_Q
    n_cmp = seq // CMP_STRIDE - 1
    n_sel = seq // SEL_BLOCK
    gst = gst_ref[0]
    ovt = ovt_ref[...]

    def key_and_query_pos(n, k0):
        shape = (n, NSA_Q)
        return k0 + lax.broadcasted_iota(jnp.int32, shape, 0), t0 + lax.broadcasted_iota(jnp.int32, shape, 1)

    groups = range(NSA_GROUPS)
    d_rows = [slice(g * HEAD_DIM, (g + 1) * HEAD_DIM) for g in groups]
    g_cols = [slice(g * cols, (g + 1) * cols) for g in groups]

    zeros_d = jnp.zeros((HEAD_DIM, NSA_Q), BF16)
    heads = []
    for n in range(NSA_HEADS):
        x = qat_ref[0, n * HEAD_DIM:(n + 1) * HEAD_DIM, :]
        heads.append(jnp.concatenate([x, zeros_d] if n < NSA_REP else [zeros_d, x], axis=0))
    q = jnp.concatenate(heads, axis=1)

    n_win = WINDOW + NSA_Q
    w0 = pl.multiple_of(jnp.maximum(t0 - WINDOW, 0), NSA_Q)
    s = _dot(kcb_ref[0], q)
    k_win = kk_ref[0, pl.ds(w0, n_win), KV_WIDTH:2 * KV_WIDTH]
    win_cols = [slice(n * NSA_Q, (n + SEL_UNIT) * NSA_Q) for n in range(0, NSA_HEADS, SEL_UNIT)]
    s_win_parts = [_dot(k_win, q[:, win_cols[0]])]

    n_idx, tq = key_and_query_pos(s.shape[0], 0)
    cmask = (n_idx * CMP_STRIDE + (CMP_LEN - 1) <= tq) & (n_idx < n_cmp)
    p_cmp = _masked_softmax_keys(s, cmask).astype(BF16)
    o_cmp = [_dot(vcbt_ref[0, d_rows[g], :], p_cmp[:, g_cols[g]]) for g in groups]

    pi = _dot(ovt, p_cmp)
    s_win_parts += [_dot(k_win, q[:, c]) for c in win_cols[1:]]
    s_win = jnp.concatenate(s_win_parts, axis=1)
    jj, tsel = key_and_query_pos(n_sel, 0)
    tblk = tsel >> 6
    bonus = jnp.where((jj == 0) | (jj == tblk) | (jj == tblk - 1), FORCE_BONUS, 0.0)
    pad = jnp.zeros((LANES - n_sel, NSA_Q), BF16)
    biases = []
    for g in groups:
        imp = pi[:, g * cols:g * cols + NSA_Q]
        for r in range(1, NSA_REP):
            imp = imp + pi[:, g * cols + r * NSA_Q:g * cols + (r + 1) * NSA_Q]
        imp = jnp.where(jj <= tblk, imp + bonus, NEG)
        bias = jnp.where(_rank_select(imp, n_sel, SEL_TOPK) & (jj <= tblk), 0.0, NEG)
        biases.append(jnp.concatenate([bias.astype(BF16), pad], axis=0))
    q_sel = jnp.concatenate(
        [jnp.concatenate([heads[n], biases[n // NSA_REP]], axis=0) for n in range(NSA_HEADS)], axis=1)

    def tile_start(c):
        return pl.multiple_of(c * SEL_KEYS, SEL_KEYS)

    units = tuple(range(n, n + SEL_UNIT) for n in range(0, NSA_HEADS, SEL_UNIT))

    def unit_cols(unit):
        return slice(unit[0] * NSA_Q, (unit[-1] + 1) * NSA_Q)

    def sel_scores(c, slot, unit=range(NSA_HEADS)):
        k0 = tile_start(c)
        keys = jnp.concatenate([kk_ref[0, pl.ds(k0, SEL_KEYS), 0:KV_WIDTH], blk_ref[pl.ds(k0, SEL_KEYS), :]],
                               axis=1)
        s_ref[slot, :, unit_cols(unit)] = _dot(keys, q_sel[:, unit_cols(unit)])

    def sel_update(c, s, unit=range(NSA_HEADS)):
        for lo_head in range(unit[0], unit[-1] + 1, min(len(unit), NSA_REP)):
            g = lo_head // NSA_REP
            n_heads = min(len(unit), NSA_REP)
            sl = slice((lo_head % NSA_REP) * NSA_Q, (lo_head % NSA_REP + n_heads) * NSA_Q)
            gl = slice(lo_head * NSA_Q, (lo_head + n_heads) * NSA_Q)
            m, l, acc = _online_update((m_ref[g, :, sl], l_ref[g, :, sl], acc_ref[g, :, sl]), s[:, gl],
                                       vst_ref[0, d_rows[g], pl.ds(tile_start(c), SEL_KEYS)])
            m_ref[g, :, sl], l_ref[g, :, sl], acc_ref[g, :, sl] = m, l, acc

    n_past = qi // (SEL_KEYS // NSA_Q)
    m_ref[...] = jnp.full(m_ref.shape, NEG, F32)
    l_ref[...] = jnp.zeros(l_ref.shape, F32)
    acc_ref[...] = jnp.zeros(acc_ref.shape, F32)
    sel_scores(0, 0)

    kpos, tq = key_and_query_pos(n_win, w0)
    s_win = _mask_keys(s_win, (kpos <= tq) & (tq - kpos < WINDOW))
    win_ref[...] = s_win
    m_win = jnp.max(s_win, axis=0, keepdims=True)

    for c in range(seq // SEL_KEYS - 1):
        @pl.when(c < n_past)
        def _():
            for unit in units:
                sel_scores(c + 1, (c + 1) % 2, unit)
                sel_update(c, s_ref[c % 2], unit)

    kpos, tq = key_and_query_pos(SEL_KEYS, tile_start(n_past))
    s_diag = _mask_keys(s_ref[n_past % 2], kpos <= tq)

    ones_rows = jnp.where(lax.broadcasted_iota(jnp.int32, (BF16_ROWS, n_win), 0) == 0, 1.0, 0.0).astype(BF16)
    o_win = []
    for unit in units:
        sel_update(n_past, s_diag, unit)
        p_win = jnp.exp2(win_ref[:, unit_cols(unit)] - m_win[:, unit_cols(unit)]).astype(BF16)
        pv = _dot(jnp.concatenate([vwt_ref[0, d_rows[unit[0] // NSA_REP], pl.ds(w0, n_win)], ones_rows], axis=0),
                  p_win)
        o_win.append(pv[:HEAD_DIM] / jnp.maximum(pv[HEAD_DIM:HEAD_DIM + 1], TINY))
    o_sel = [acc_ref[g] / jnp.maximum(l_ref[g], TINY) for g in groups]

    outs = []
    for n in range(NSA_HEADS):
        g, r = divmod(n, NSA_REP)
        sl = slice(r * NSA_Q, (r + 1) * NSA_Q)
        ul = slice((n % SEL_UNIT) * NSA_Q, (n % SEL_UNIT + 1) * NSA_Q)
        outs.append(gst[3 * n:3 * n + 1, :] * o_cmp[g][:, sl] + gst[3 * n + 1:3 * n + 2, :] * o_sel[g][:, sl]
                    + gst[3 * n + 2:3 * n + 3, :] * o_win[n // SEL_UNIT][:, ul])
    o_ref[0] = jnp.transpose(jnp.concatenate(outs, axis=0)).astype(BF16)


def _nsa(qat, gst, kcb, vcbt, kk, vvt, ovt, blk):
    b, _, seq = qat.shape
    n_chunks = kcb.shape[1]
    return pl.pallas_call(
        functools.partial(_nsa_kernel, seq=seq),
        grid=(b, seq // NSA_Q),
        in_specs=[pl.BlockSpec((1, NSA_WIDTH, NSA_Q), lambda i, j: (i, 0, j)),
                  pl.BlockSpec((1, LANES, NSA_Q), lambda i, j: (i, 0, j)),
                  pl.BlockSpec((1, n_chunks, KV_WIDTH), lambda i, j: (i, 0, 0)),
                  pl.BlockSpec((1, KV_WIDTH, n_chunks), lambda i, j: (i, 0, 0)),
                  pl.BlockSpec((1, seq, 2 * KV_WIDTH), lambda i, j: (i, 0, 0)),
                  pl.BlockSpec((1, KV_WIDTH, seq), lambda i, j: (i, 0, 0)),
                  pl.BlockSpec((1, KV_WIDTH, seq), lambda i, j: (i, 1, 0)),
                  pl.BlockSpec(ovt.shape, lambda i, j: (0, 0)),
                  pl.BlockSpec(blk.shape, lambda i, j: (0, 0))],
        out_specs=pl.BlockSpec((1, NSA_Q, NSA_WIDTH), lambda i, j: (i, j, 0)),
        out_shape=jax.ShapeDtypeStruct((b, seq, NSA_WIDTH), BF16),
        scratch_shapes=[pltpu.VMEM((2, SEL_KEYS, NSA_HEADS * NSA_Q), F32),
                        pltpu.VMEM((WINDOW + NSA_Q, NSA_HEADS * NSA_Q), F32),
                        pltpu.VMEM((NSA_GROUPS, 1, NSA_REP * NSA_Q), F32),
                        pltpu.VMEM((NSA_GROUPS, 1, NSA_REP * NSA_Q), F32),
                        pltpu.VMEM((NSA_GROUPS, HEAD_DIM, NSA_REP * NSA_Q), F32)],
        compiler_params=pltpu.CompilerParams(dimension_semantics=("parallel", "parallel"),
                                             vmem_limit_bytes=VMEM_LIMIT),
        name="nsa",
    )(qat, gst, kcb, vcbt, kk, vvt, vvt, ovt, blk)


def _moba_kernel(qt_ref, k_ref, vt_ref, kmean_ref, o_ref, s_ref, m_ref, l_ref, acc_ref):
    c = pl.program_id(1)
    nb = kmean_ref.shape[1]
    k_own = pl.multiple_of(c * MOBA_BLOCK, MOBA_BLOCK)
    jj = lax.broadcasted_iota(jnp.int32, (BF16_ROWS, MOBA_BLOCK), 0)
    past = jj < c
    pad = jnp.zeros((BF16_ROWS - nb, LANES), F32)
    zeros_d = jnp.zeros((HEAD_DIM, MOBA_BLOCK), BF16)
    ones_rows = jnp.where(lax.broadcasted_iota(jnp.int32, (BF16_ROWS, MOBA_BLOCK), 0) == 0, 1.0, 0.0).astype(BF16)

    def pair(h):
        return slice((h // 2) * LANES, (h // 2 + 1) * LANES)

    qs, gscs = [], []
    for h in range(MOBA_HEADS):
        x = qt_ref[0, h * HEAD_DIM:(h + 1) * HEAD_DIM, :]
        q = jnp.concatenate([x, zeros_d] if h % 2 == 0 else [zeros_d, x], axis=0)
        km = jnp.concatenate([kmean_ref[0, :, pair(h)], pad], axis=0).astype(BF16)
        qs.append(q)
        gscs.append(_dot(km, q))
    chosen = [jnp.where(_rank_select(jnp.where(past, g, NEG), nb, MOBA_TOPK) & past, 1.0, 0.0) for g in gscs]

    m_ref[...] = jnp.full(m_ref.shape, NEG, F32)
    l_ref[...] = jnp.zeros(l_ref.shape, F32)
    acc_ref[...] = jnp.zeros(acc_ref.shape, F32)

    def block_start(j):
        return pl.multiple_of(j * MOBA_BLOCK, MOBA_BLOCK)

    head_pairs = tuple(range(h, h + 2) for h in range(0, MOBA_HEADS, 2))

    def scores(j, slot, heads=range(MOBA_HEADS)):
        for h in heads:
            s_ref[slot, h] = _dot(k_ref[0, pl.ds(block_start(j), MOBA_BLOCK), pair(h)], qs[h])

    def past_update(j, slot, heads=range(MOBA_HEADS)):
        for h in heads:
            rows = slice(h * HEAD_DIM, (h + 1) * HEAD_DIM)
            s = s_ref[slot, h]
            pick = jnp.sum(jnp.where(jj == j, chosen[h], 0.0), axis=0, keepdims=True) > 0.5
            m = m_ref[h]
            m_new = jnp.maximum(m, jnp.where(pick, jnp.max(s, axis=0, keepdims=True), NEG))
            alpha = jnp.exp2(m - m_new)
            p = jnp.exp2(s - jnp.where(pick, m_new, jnp.inf))
            m_ref[h] = m_new
            pv = _dot(jnp.concatenate([vt_ref[0, rows, pl.ds(block_start(j), MOBA_BLOCK)], ones_rows], axis=0),
                      p.astype(BF16))
            l_ref[h] = alpha * l_ref[h] + pv[HEAD_DIM:HEAD_DIM + 1]
            acc_ref[rows, :] = alpha * acc_ref[rows, :] + pv[:HEAD_DIM]

    scores(0, 0)

    def two_blocks(i, _):
        j = 2 * i
        for heads in head_pairs:
            scores(j + 1, 1, heads)
            past_update(j, 0, heads)
        for heads in head_pairs:
            scores(j + 2, 0, heads)
            past_update(j + 1, 1, heads)
        return 0

    lax.fori_loop(0, c // 2, two_blocks, 0)

    @pl.when(c % 2 == 1)
    def _():
        for heads in head_pairs:
            scores(c, 1, heads)
            past_update(c - 1, 0, heads)

    kq = (MOBA_BLOCK, MOBA_BLOCK)
    causal = lax.broadcasted_iota(jnp.int32, kq, 0) <= lax.broadcasted_iota(jnp.int32, kq, 1)
    for h in range(MOBA_HEADS):
        rows = slice(h * HEAD_DIM, (h + 1) * HEAD_DIM)
        carry = (m_ref[h], l_ref[h], acc_ref[rows, :])
        _, l, acc = _online_update(carry, jnp.where(causal, s_ref[c % 2, h], NEG),
                                   vt_ref[0, rows, pl.ds(k_own, MOBA_BLOCK)])
        acc_ref[rows, :] = acc / jnp.maximum(l, TINY)
    o_ref[0] = jnp.transpose(acc_ref[...]).astype(BF16)


def _moba(qbt, kb, vbt, kmean):
    b, _, seq = qbt.shape
    nb = seq // MOBA_BLOCK
    return pl.pallas_call(
        _moba_kernel,
        grid=(b, nb),
        in_specs=[pl.BlockSpec((1, MOBA_WIDTH, MOBA_BLOCK), lambda i, j: (i, 0, j)),
                  pl.BlockSpec((1, seq, MOBA_WIDTH), lambda i, j: (i, 0, 0)),
                  pl.BlockSpec((1, MOBA_WIDTH, seq), lambda i, j: (i, 0, 0)),
                  pl.BlockSpec((1, nb, MOBA_WIDTH), lambda i, j: (i, 0, 0))],
        out_specs=pl.BlockSpec((1, MOBA_BLOCK, MOBA_WIDTH), lambda i, j: (i, j, 0)),
        out_shape=jax.ShapeDtypeStruct((b, seq, MOBA_WIDTH), BF16),
        scratch_shapes=[pltpu.VMEM((2, MOBA_HEADS, MOBA_BLOCK, MOBA_BLOCK), F32),
                        pltpu.VMEM((MOBA_HEADS, 1, MOBA_BLOCK), F32), pltpu.VMEM((MOBA_HEADS, 1, MOBA_BLOCK), F32),
                        pltpu.VMEM((MOBA_WIDTH, MOBA_BLOCK), F32)],
        compiler_params=pltpu.CompilerParams(dimension_semantics=("parallel", "parallel"),
                                             vmem_limit_bytes=VMEM_LIMIT),
        name="moba",
    )(qbt, kb, vbt, kmean)


def _merge_kernel(x_ref, g_ref, a_ref, b_ref, wgate_ref, pa_ref, pb_ref, wo_ref, o_ref):
    x = x_ref[...]
    h = _rms(x, g_ref[...]).astype(BF16)
    a, b = a_ref[0], b_ref[0]
    y = x
    for c0 in range(0, D_MODEL, MERGE_COLS):
        c = slice(c0, c0 + MERGE_COLS)
        merged = (jax.nn.sigmoid(_dot(h, wgate_ref[:, c])) * _dot(a, pa_ref[:, c])
                  + jax.nn.sigmoid(_dot(h, wgate_ref[:, D_MODEL + c0:D_MODEL + c0 + MERGE_COLS])) * _dot(b, pb_ref[:, c]))
        y = y + _dot(merged.astype(BF16), wo_ref[c, :])
    o_ref[...] = y


def _merge(x, layer, g, a, bm, wgate, pa, pb, wo):
    t = x.shape[0]
    tiles_per_seq = a.shape[1] // TOKEN_TILE

    def row(n):
        return pl.BlockSpec((TOKEN_TILE, n), lambda i: (i, 0))

    def seq_row(n):
        return pl.BlockSpec((1, TOKEN_TILE, n), lambda i: (i // tiles_per_seq, i % tiles_per_seq, 0))

    return pl.pallas_call(
        _merge_kernel,
        grid=(t // TOKEN_TILE,),
        in_specs=[row(D_MODEL), _layer_resident(g, layer), seq_row(NSA_WIDTH), seq_row(MOBA_WIDTH)]
                 + [_layer_resident(w, layer) for w in (wgate, pa, pb, wo)],
        out_specs=row(D_MODEL),
        out_shape=jax.ShapeDtypeStruct(x.shape, F32),
        compiler_params=pltpu.CompilerParams(dimension_semantics=("parallel",),
                                             vmem_limit_bytes=VMEM_LIMIT),
        name="merge",
    )(x, g, a, bm, wgate, pa, pb, wo)


def _rope_tables(pos):
    inv = ROPE_THETA ** (-jnp.arange(0, HEAD_DIM, 2, dtype=F32) / HEAD_DIM)
    ang = pos.astype(F32)[:, None] * inv[None, :]
    cos, sin = jnp.cos(ang), jnp.sin(ang)
    reps = LANES // HEAD_DIM
    return (jnp.tile(jnp.concatenate([cos, cos], axis=1), (1, reps)),
            jnp.tile(jnp.concatenate([-sin, sin], axis=1), (1, reps)))


def _overlap_matrix_t(seq):
    n_chunks = seq // CMP_STRIDE
    ci = np.arange(n_chunks)[None, :] * CMP_STRIDE
    sj = np.arange(seq // SEL_BLOCK)[:, None] * SEL_BLOCK
    ov = (ci < sj + SEL_BLOCK) & (ci + CMP_LEN > sj) & (np.arange(n_chunks)[None, :] < n_chunks - 1)
    return jnp.asarray(ov, dtype=BF16)


def _block_onehot(seq):
    return jnp.asarray(np.arange(seq)[:, None] // SEL_BLOCK == np.arange(LANES)[None, :], dtype=BF16)


def kernel(x, ffn1_norm, ffn1_wg, ffn1_wu, ffn1_wd, mix_norm, w_in, cmpk_pos, cmpk_w1, cmpk_w2, cmpv_pos, cmpv_w1, cmpv_w2, w_branch_nsa, w_branch_moba, w_out, ffn2_norm, ffn2_wg, ffn2_wu, ffn2_wd, final_norm):
    b, seq, d = x.shape
    depth = w_in.shape[0]
    assert d == D_MODEL and seq % TOKEN_TILE == 0 and seq // CMP_STRIDE == LANES
    t = b * seq
    n_chunks = seq // CMP_STRIDE
    cos_t, sin_t = _rope_tables(jnp.arange(seq))
    cos_c, sin_c = _rope_tables(jnp.arange(n_chunks) * CMP_STRIDE + (CMP_LEN - 1))
    ovt = _overlap_matrix_t(seq)
    blk = _block_onehot(seq)

    def gain(gs):
        return gs.reshape(depth, 1, d)

    ffn1 = (gain(ffn1_norm), ffn1_wg.astype(BF16), ffn1_wu.astype(BF16), ffn1_wd.astype(BF16))
    ffn2 = (gain(ffn2_norm), ffn2_wg.astype(BF16), ffn2_wu.astype(BF16), ffn2_wd.astype(BF16))
    w_attn, w_gate = _split_w_in(w_in)
    w_merge = (w_gate, w_branch_nsa.astype(BF16), w_branch_moba.astype(BF16), w_out.astype(BF16))
    fin = final_norm.reshape(1, d)

    xf = x.reshape(t, d)
    for l in range(depth):
        xf = _ffn(xf, l, *ffn1, fin, False)
        x_mix = xf
        qat, kc, vc, kk, vvt, qbt, kb, vbt, gst, kmean = _proj(xf, l, gain(mix_norm), w_attn, cos_t, sin_t, seq)
        kcb, vcb = _compress(kc, vc,
                             (cmpk_pos[l], cmpk_w1[l], cmpk_w2[l]),
                             (cmpv_pos[l], cmpv_w1[l], cmpv_w2[l]), cos_c, sin_c)
        a = _nsa(qat, gst, kcb, vcb.transpose(0, 2, 1), kk, vvt, ovt, blk)
        bm = _moba(qbt, kb, vbt, kmean.reshape(b, seq // MOBA_BLOCK, MOBA_WIDTH))
        xf = _merge(x_mix, l, gain(mix_norm), a, bm, *w_merge)
        xf = _ffn(xf, l, *ffn2, fin, l == depth - 1)
    return xf.reshape(b, seq, d)
```
